```python
import math
import jax, jax.numpy as jnp
from jax import lax
import numpy as np

D_MODEL = 2048
BATCH = 16
SEQ = 2048
DEPTH = 4

MEM_LEN = 256
XA_HEADS = 4
XA_HEAD_DIM = D_MODEL // XA_HEADS
MIX_WIDTH = D_MODEL
POOL_WIDTH = MIX_WIDTH // 2
POOL_WINDOWS = (2, 4, 8, 16)
POOL_GROUP = POOL_WIDTH // len(POOL_WINDOWS)
DN_HEAD_DIM = 128
DN_WIDTH = MIX_WIDTH - POOL_WIDTH
DN_HEADS = DN_WIDTH // DN_HEAD_DIM
DN_CONV = 4
DN_CHUNK = 64
IN_WIDTH = POOL_WIDTH + 4 * DN_WIDTH + 2 * DN_HEADS
D_FF = 256 * ((8 * D_MODEL // 3 + 255) // 256)
FFN_CONV = 3
EPS = 1e-6

kernel_name = "hybrid_pool_deltanet_memxattn_convglu"


def rms_norm(x, g):
    xf = x.astype(jnp.float32)
    y = xf * lax.rsqrt(jnp.mean(xf * xf, axis=-1, keepdims=True) + EPS)
    return (y * g.astype(jnp.float32)).astype(x.dtype)


def l2_normalize(t):
    return t * lax.rsqrt(jnp.sum(t * t, axis=-1, keepdims=True) + EPS)


def causal_dwconv(x, w, b=None):
    K = w.shape[0]
    S = x.shape[1]
    xp = jnp.pad(x, ((0, 0), (K - 1, 0), (0, 0)))
    y = sum(xp[:, k:k + S] * w[k] for k in range(K))
    if b is not None:
        y = y + b
    return y


def pool_mixer(u, w_pool, pool_scale):
    B, S, _ = u.shape
    uf = u.astype(jnp.float32)
    cs = jnp.pad(jnp.cumsum(uf, axis=1), ((0, 0), (1, 0), (0, 0)))
    pos = jnp.arange(1, S + 1, dtype=jnp.float32)
    groups = []
    for i, w in enumerate(POOL_WINDOWS):
        sl = slice(i * POOL_GROUP, (i + 1) * POOL_GROUP)
        c = cs[:, :, sl]
        lagged = jnp.pad(c, ((0, 0), (w - 1, 0), (0, 0)))[:, :S]
        count = jnp.minimum(pos, float(w))
        mean = (c[:, 1:] - lagged) / count[None, :, None]
        groups.append(mean - uf[:, :, sl])
    mixed = jnp.stack(groups, axis=2).astype(u.dtype)
    y = jnp.einsum('bsng,ngh->bsnh', mixed, w_pool).reshape(B, S, POOL_WIDTH)
    return y * pool_scale


def gated_delta_net(q, k, v, z, b_logit, a_logit, conv_w, a_log, dt_bias, norm_g):
    B, S, _ = q.shape
    H, Dh, C = DN_HEADS, DN_HEAD_DIM, DN_CHUNK
    N = S // C
    f32 = jnp.float32
    qkv = jax.nn.silu(causal_dwconv(jnp.concatenate([q, k, v], axis=-1), conv_w)).astype(f32)
    q, k, v = [t.reshape(B, S, H, Dh) for t in jnp.split(qkv, 3, axis=-1)]
    q = l2_normalize(q) * (Dh ** -0.5)
    k = l2_normalize(k)
    beta = jax.nn.sigmoid(b_logit.astype(f32))
    g = -jnp.exp(a_log.astype(f32)) * jax.nn.softplus(a_logit.astype(f32) + dt_bias.astype(f32))

    def to_chunks(t):
        return t.reshape(B, N, C, H, -1).transpose(0, 3, 1, 2, 4)

    q, k, v = to_chunks(q), to_chunks(k), to_chunks(v)
    beta = to_chunks(beta[..., None])[..., 0]
    gc = jnp.cumsum(to_chunks(g[..., None])[..., 0], axis=-1)
    idx = jnp.arange(C)
    causal = idx[:, None] >= idx[None, :]
    strict = idx[:, None] > idx[None, :]
    decay = jnp.exp(jnp.where(causal, gc[..., :, None] - gc[..., None, :], -jnp.inf))
    kk = jnp.einsum('bhncd,bhnjd->bhncj', k, k)
    l_mat = jnp.where(strict, beta[..., :, None] * kk * decay, 0.0)
    rhs = jnp.concatenate([v * beta[..., None], k * (beta * jnp.exp(gc))[..., None]], axis=-1)
    sol = lax.linalg.triangular_solve(l_mat, rhs, left_side=True, lower=True, unit_diagonal=True)
    u_c, w_c = sol[..., :Dh], sol[..., Dh:]
    attn = jnp.einsum('bhncd,bhnjd->bhncj', q, k) * decay
    q_dec = q * jnp.exp(gc)[..., None]
    k_dec = k * jnp.exp(gc[..., -1:] - gc)[..., None]
    chunk_decay = jnp.exp(gc[..., -1])
    xs = tuple(jnp.moveaxis(t, 2, 0) for t in (u_c, w_c, q_dec, k_dec, attn, chunk_decay))

    def step(state, inp):
        u_i, w_i, qd_i, kd_i, a_i, dec_i = inp
        v_new = u_i - jnp.einsum('bhcd,bhde->bhce', w_i, state)
        o_i = jnp.einsum('bhcd,bhde->bhce', qd_i, state) + jnp.einsum('bhcj,bhje->bhce', a_i, v_new)
        state = state * dec_i[..., None, None] + jnp.einsum('bhcd,bhce->bhde', kd_i, v_new)
        return state, o_i

    _, o = lax.scan(step, jnp.zeros((B, H, Dh, Dh), f32), xs)
    o = o.transpose(1, 0, 3, 2, 4).reshape(B, S, H, Dh)
    o = rms_norm(o, norm_g) * jax.nn.silu(z.astype(f32).reshape(B, S, H, Dh))
    return o.reshape(B, S, DN_WIDTH).astype(z.dtype)


def hybrid_mixer(h, w_in, w_pool, pool_scale, dn_conv_w, dn_a_log, dn_dt_bias, dn_norm_g, w_mix_out):
    p = h @ w_in
    o0 = POOL_WIDTH
    u = p[..., :o0]
    q = p[..., o0:o0 + DN_WIDTH]
    k = p[..., o0 + DN_WIDTH:o0 + 2 * DN_WIDTH]
    v = p[..., o0 + 2 * DN_WIDTH:o0 + 3 * DN_WIDTH]
    z = p[..., o0 + 3 * DN_WIDTH:o0 + 4 * DN_WIDTH]
    o1 = o0 + 4 * DN_WIDTH
    b_logit = p[..., o1:o1 + DN_HEADS]
    a_logit = p[..., o1 + DN_HEADS:o1 + 2 * DN_HEADS]
    y_pool = pool_mixer(u, w_pool, pool_scale)
    y_dn = gated_delta_net(q, k, v, z, b_logit, a_logit, dn_conv_w, dn_a_log, dn_dt_bias, dn_norm_g)
    return jnp.concatenate([y_pool, y_dn], axis=-1) @ w_mix_out


def memory_cross_attention(h, mem_h, w_xq, w_xkv, w_xo):
    B, S, _ = h.shape
    M = mem_h.shape[1]
    q = (h @ w_xq).reshape(B, S, XA_HEADS, XA_HEAD_DIM)
    kv = mem_h @ w_xkv
    k = kv[..., :D_MODEL].reshape(B, M, XA_HEADS, XA_HEAD_DIM)
    v = kv[..., D_MODEL:].reshape(B, M, XA_HEADS, XA_HEAD_DIM)
    s = jnp.einsum('bshd,bmhd->bhsm', q, k).astype(jnp.float32) * (XA_HEAD_DIM ** -0.5)
    pr = jax.nn.softmax(s, axis=-1).astype(v.dtype)
    o = jnp.einsum('bhsm,bmhd->bshd', pr, v).reshape(B, S, D_MODEL)
    return o @ w_xo


def conv_glu_ffn(h, w_gate, w_up, conv_w, conv_b, w_down):
    gate = causal_dwconv(h @ w_gate, conv_w, conv_b)
    return (jax.nn.silu(gate) * (h @ w_up)) @ w_down


def _fwd_setup_inputs(seed: int = 0) -> dict:
    key = jax.random.key(seed)
    ks = jax.random.split(key, 32)
    f32 = jnp.float32

    def nrm(i, shape, scale):
        return jax.random.normal(ks[i], shape, f32) * scale

    def gain(i, shape):
        return 1.0 + 0.02 * jax.random.normal(ks[i], shape, f32)

    dt = jnp.exp(jax.random.uniform(ks[7], (DEPTH, DN_HEADS), f32) * (math.log(0.1) - math.log(0.001)) + math.log(0.001))
    return {
        "x": nrm(0, (BATCH, SEQ, D_MODEL), 1.0),
        "mem": nrm(1, (BATCH, MEM_LEN, D_MODEL), 1.0),
        "mix_norm_g": gain(2, (DEPTH, D_MODEL)),
        "w_in": nrm(3, (DEPTH, D_MODEL, IN_WIDTH), D_MODEL ** -0.5),
        "w_pool": nrm(4, (DEPTH, len(POOL_WINDOWS), POOL_GROUP, POOL_GROUP), POOL_GROUP ** -0.5),
        "pool_scale": gain(5, (DEPTH, POOL_WIDTH)),
        "dn_conv_w": nrm(6, (DEPTH, DN_CONV, 3 * DN_WIDTH), DN_CONV ** -0.5),
        "dn_a_log": jnp.log(jax.random.uniform(ks[8], (DEPTH, DN_HEADS), f32, 1.0, 16.0)),
        "dn_dt_bias": dt + jnp.log(-jnp.expm1(-dt)),
        "dn_norm_g": gain(9, (DEPTH, DN_HEAD_DIM)),
        "w_mix_out": nrm(10, (DEPTH, MIX_WIDTH, D_MODEL), MIX_WIDTH ** -0.5),
        "xa_norm_g": gain(11, (DEPTH, D_MODEL)),
        "mem_norm_g": gain(12, (D_MODEL,)),
        "w_xq": nrm(13, (DEPTH, D_MODEL, D_MODEL), D_MODEL ** -0.5),
        "w_xkv": nrm(14, (DEPTH, D_MODEL, 2 * D_MODEL), D_MODEL ** -0.5),
        "w_xo": nrm(15, (DEPTH, D_MODEL, D_MODEL), D_MODEL ** -0.5),
        "ffn_norm_g": gain(16, (DEPTH, D_MODEL)),
        "w_gate": nrm(17, (DEPTH, D_MODEL, D_FF), D_MODEL ** -0.5),
        "w_up": nrm(18, (DEPTH, D_MODEL, D_FF), D_MODEL ** -0.5),
        "ffn_conv_w": nrm(19, (DEPTH, FFN_CONV, D_FF), FFN_CONV ** -0.5),
        "ffn_conv_b": nrm(20, (DEPTH, D_FF), 0.01),
        "w_down": nrm(21, (DEPTH, D_FF, D_MODEL), D_FF ** -0.5),
        "final_norm_g": gain(22, (D_MODEL,)),
    }


def _fwd_reference(x, mem, mix_norm_g, w_in, w_pool, pool_scale, dn_conv_w, dn_a_log, dn_dt_bias, dn_norm_g,
              w_mix_out, xa_norm_g, mem_norm_g, w_xq, w_xkv, w_xo, ffn_norm_g, w_gate, w_up,
              ffn_conv_w, ffn_conv_b, w_down, final_norm_g):
    mem_h = rms_norm(mem, mem_norm_g)
    for l in range(DEPTH):
        x = x + hybrid_mixer(rms_norm(x, mix_norm_g[l]), w_in[l], w_pool[l], pool_scale[l], dn_conv_w[l],
                             dn_a_log[l], dn_dt_bias[l], dn_norm_g[l], w_mix_out[l])
        x = x + memory_cross_attention(rms_norm(x, xa_norm_g[l]), mem_h, w_xq[l], w_xkv[l], w_xo[l])
        x = x + conv_glu_ffn(rms_norm(x, ffn_norm_g[l]), w_gate[l], w_up[l], ffn_conv_w[l], ffn_conv_b[l], w_down[l])
    return rms_norm(x, final_norm_g)


import jax as _jax
import jax.numpy as _jnp

TWIN_FORMAT = 'train_step'
FWD_PARAMS = ['x', 'mem', 'mix_norm_g', 'w_in', 'w_pool', 'pool_scale', 'dn_conv_w', 'dn_a_log', 'dn_dt_bias', 'dn_norm_g', 'w_mix_out', 'xa_norm_g', 'mem_norm_g', 'w_xq', 'w_xkv', 'w_xo', 'ffn_norm_g', 'w_gate', 'w_up', 'ffn_conv_w', 'ffn_conv_b', 'w_down', 'final_norm_g']
TWIN_WEIGHTS = ['mix_norm_g', 'w_in', 'w_pool', 'pool_scale', 'dn_conv_w', 'dn_a_log', 'dn_dt_bias', 'dn_norm_g', 'w_mix_out', 'xa_norm_g', 'mem_norm_g', 'w_xq', 'w_xkv', 'w_xo', 'ffn_norm_g', 'w_gate', 'w_up', 'ffn_conv_w', 'ffn_conv_b', 'w_down', 'final_norm_g']
TWIN_DIFF_INPUT = 'x'
TWIN_INPUTS = ['x', 'mem', 'mix_norm_g', 'w_in', 'w_pool', 'pool_scale', 'dn_conv_w', 'dn_a_log', 'dn_dt_bias', 'dn_norm_g', 'w_mix_out', 'xa_norm_g', 'mem_norm_g', 'w_xq', 'w_xkv', 'w_xo', 'ffn_norm_g', 'w_gate', 'w_up', 'ffn_conv_w', 'ffn_conv_b', 'w_down', 'final_norm_g', 'loss_target', 'm_mix_norm_g', 'm_w_in', 'm_w_pool', 'm_pool_scale', 'm_dn_conv_w', 'm_dn_a_log', 'm_dn_dt_bias', 'm_dn_norm_g', 'm_w_mix_out', 'm_xa_norm_g', 'm_mem_norm_g', 'm_w_xq', 'm_w_xkv', 'm_w_xo', 'm_ffn_norm_g', 'm_w_gate', 'm_w_up', 'm_ffn_conv_w', 'm_ffn_conv_b', 'm_w_down', 'm_final_norm_g', 'v_mix_norm_g', 'v_w_in', 'v_w_pool', 'v_pool_scale', 'v_dn_conv_w', 'v_dn_a_log', 'v_dn_dt_bias', 'v_dn_norm_g', 'v_w_mix_out', 'v_xa_norm_g', 'v_mem_norm_g', 'v_w_xq', 'v_w_xkv', 'v_w_xo', 'v_ffn_norm_g', 'v_w_gate', 'v_w_up', 'v_ffn_conv_w', 'v_ffn_conv_b', 'v_w_down', 'v_final_norm_g']
TWIN_OUTPUTS = ['loss', 'grad_x', 'grad_mix_norm_g', 'grad_w_in', 'grad_w_pool', 'grad_pool_scale', 'grad_dn_conv_w', 'grad_dn_a_log', 'grad_dn_dt_bias', 'grad_dn_norm_g', 'grad_w_mix_out', 'grad_xa_norm_g', 'grad_mem_norm_g', 'grad_w_xq', 'grad_w_xkv', 'grad_w_xo', 'grad_ffn_norm_g', 'grad_w_gate', 'grad_w_up', 'grad_ffn_conv_w', 'grad_ffn_conv_b', 'grad_w_down', 'grad_final_norm_g', 'delta_mix_norm_g', 'delta_w_in', 'delta_w_pool', 'delta_pool_scale', 'delta_dn_conv_w', 'delta_dn_a_log', 'delta_dn_dt_bias', 'delta_dn_norm_g', 'delta_w_mix_out', 'delta_xa_norm_g', 'delta_mem_norm_g', 'delta_w_xq', 'delta_w_xkv', 'delta_w_xo', 'delta_ffn_norm_g', 'delta_w_gate', 'delta_w_up', 'delta_ffn_conv_w', 'delta_ffn_conv_b', 'delta_w_down', 'delta_final_norm_g', 'new_m_mix_norm_g', 'new_m_w_in', 'new_m_w_pool', 'new_m_pool_scale', 'new_m_dn_conv_w', 'new_m_dn_a_log', 'new_m_dn_dt_bias', 'new_m_dn_norm_g', 'new_m_w_mix_out', 'new_m_xa_norm_g', 'new_m_mem_norm_g', 'new_m_w_xq', 'new_m_w_xkv', 'new_m_w_xo', 'new_m_ffn_norm_g', 'new_m_w_gate', 'new_m_w_up', 'new_m_ffn_conv_w', 'new_m_ffn_conv_b', 'new_m_w_down', 'new_m_final_norm_g', 'new_v_mix_norm_g', 'new_v_w_in', 'new_v_w_pool', 'new_v_pool_scale', 'new_v_dn_conv_w', 'new_v_dn_a_log', 'new_v_dn_dt_bias', 'new_v_dn_norm_g', 'new_v_w_mix_out', 'new_v_xa_norm_g', 'new_v_mem_norm_g', 'new_v_w_xq', 'new_v_w_xkv', 'new_v_w_xo', 'new_v_ffn_norm_g', 'new_v_w_gate', 'new_v_w_up', 'new_v_ffn_conv_w', 'new_v_ffn_conv_b', 'new_v_w_down', 'new_v_final_norm_g']
TWIN_LEAF_KINDS = {'loss': 'loss', 'grad_x': 'grad_x', 'grad_mix_norm_g': 'grad_w', 'grad_w_in': 'grad_w', 'grad_w_pool': 'grad_w', 'grad_pool_scale': 'grad_w', 'grad_dn_conv_w': 'grad_w', 'grad_dn_a_log': 'grad_w', 'grad_dn_dt_bias': 'grad_w', 'grad_dn_norm_g': 'grad_w', 'grad_w_mix_out': 'grad_w', 'grad_xa_norm_g': 'grad_w', 'grad_mem_norm_g': 'grad_w', 'grad_w_xq': 'grad_w', 'grad_w_xkv': 'grad_w', 'grad_w_xo': 'grad_w', 'grad_ffn_norm_g': 'grad_w', 'grad_w_gate': 'grad_w', 'grad_w_up': 'grad_w', 'grad_ffn_conv_w': 'grad_w', 'grad_ffn_conv_b': 'grad_w', 'grad_w_down': 'grad_w', 'grad_final_norm_g': 'grad_w', 'delta_mix_norm_g': 'delta_w', 'delta_w_in': 'delta_w', 'delta_w_pool': 'delta_w', 'delta_pool_scale': 'delta_w', 'delta_dn_conv_w': 'delta_w', 'delta_dn_a_log': 'delta_w', 'delta_dn_dt_bias': 'delta_w', 'delta_dn_norm_g': 'delta_w', 'delta_w_mix_out': 'delta_w', 'delta_xa_norm_g': 'delta_w', 'delta_mem_norm_g': 'delta_w', 'delta_w_xq': 'delta_w', 'delta_w_xkv': 'delta_w', 'delta_w_xo': 'delta_w', 'delta_ffn_norm_g': 'delta_w', 'delta_w_gate': 'delta_w', 'delta_w_up': 'delta_w', 'delta_ffn_conv_w': 'delta_w', 'delta_ffn_conv_b': 'delta_w', 'delta_w_down': 'delta_w', 'delta_final_norm_g': 'delta_w', 'new_m_mix_norm_g': 'new_m', 'new_m_w_in': 'new_m', 'new_m_w_pool': 'new_m', 'new_m_pool_scale': 'new_m', 'new_m_dn_conv_w': 'new_m', 'new_m_dn_a_log': 'new_m', 'new_m_dn_dt_bias': 'new_m', 'new_m_dn_norm_g': 'new_m', 'new_m_w_mix_out': 'new_m', 'new_m_xa_norm_g': 'new_m', 'new_m_mem_norm_g': 'new_m', 'new_m_w_xq': 'new_m', 'new_m_w_xkv': 'new_m', 'new_m_w_xo': 'new_m', 'new_m_ffn_norm_g': 'new_m', 'new_m_w_gate': 'new_m', 'new_m_w_up': 'new_m', 'new_m_ffn_conv_w': 'new_m', 'new_m_ffn_conv_b': 'new_m', 'new_m_w_down': 'new_m', 'new_m_final_norm_g': 'new_m', 'new_v_mix_norm_g': 'new_v', 'new_v_w_in': 'new_v', 'new_v_w_pool': 'new_v', 'new_v_pool_scale': 'new_v', 'new_v_dn_conv_w': 'new_v', 'new_v_dn_a_log': 'new_v', 'new_v_dn_dt_bias': 'new_v', 'new_v_dn_norm_g': 'new_v', 'new_v_w_mix_out': 'new_v', 'new_v_xa_norm_g': 'new_v', 'new_v_mem_norm_g': 'new_v', 'new_v_w_xq': 'new_v', 'new_v_w_xkv': 'new_v', 'new_v_w_xo': 'new_v', 'new_v_ffn_norm_g': 'new_v', 'new_v_w_gate': 'new_v', 'new_v_w_up': 'new_v', 'new_v_ffn_conv_w': 'new_v', 'new_v_ffn_conv_b': 'new_v', 'new_v_w_down': 'new_v', 'new_v_final_norm_g': 'new_v'}


def _forward(args):
    return _fwd_reference(*[args[k] for k in FWD_PARAMS])


def _output_shape():
    out = _jax.eval_shape(lambda: _forward(_fwd_setup_inputs(0)))
    return out.shape, out.dtype

N_MICROBATCH = 1
ADAM_LR = 0.001
ADAM_B1 = 0.9
ADAM_B2 = 0.999
ADAM_EPS = 1e-08
ADAM_WD = 0.01
ADAM_STEP = 10
PER_EXAMPLE_BATCH_AXIS = {'x': 0, 'mem': 0, 'loss_target': 0}
SHARED_INPUTS = []
_WEIGHT_DTYPES = {'mix_norm_g': _jnp.float32, 'w_in': _jnp.float32, 'w_pool': _jnp.float32, 'pool_scale': _jnp.float32, 'dn_conv_w': _jnp.float32, 'dn_a_log': _jnp.float32, 'dn_dt_bias': _jnp.float32, 'dn_norm_g': _jnp.float32, 'w_mix_out': _jnp.float32, 'xa_norm_g': _jnp.float32, 'mem_norm_g': _jnp.float32, 'w_xq': _jnp.float32, 'w_xkv': _jnp.float32, 'w_xo': _jnp.float32, 'ffn_norm_g': _jnp.float32, 'w_gate': _jnp.float32, 'w_up': _jnp.float32, 'ffn_conv_w': _jnp.float32, 'ffn_conv_b': _jnp.float32, 'w_down': _jnp.float32, 'final_norm_g': _jnp.float32}
MOMENT_SCALE = {'mix_norm_g': 6.693577e-02, 'w_in': 4.261533e-02, 'w_pool': 6.352470e-02, 'pool_scale': 6.528134e-02, 'dn_conv_w': 3.299701e-02, 'dn_a_log': 2.328651e-01, 'dn_dt_bias': 2.253117e-01, 'dn_norm_g': 1.252060e-01, 'w_mix_out': 5.415218e-02, 'xa_norm_g': 7.503717e-03, 'mem_norm_g': 2.096823e-02, 'w_xq': 7.395743e-03, 'w_xkv': 7.418630e-03, 'w_xo': 7.422450e-03, 'ffn_norm_g': 5.326310e-02, 'w_gate': 2.334168e-02, 'w_up': 2.259991e-02, 'ffn_conv_w': 2.364260e-02, 'ffn_conv_b': 2.215069e-02, 'w_down': 3.748743e-02, 'final_norm_g': 1.597947e+01}


def _to_microbatches(a, axis):
    t = _jnp.moveaxis(a, axis, 0)
    t = t.reshape((N_MICROBATCH, t.shape[0] // N_MICROBATCH) + t.shape[1:])
    return _jnp.moveaxis(t, 1, axis + 1)


def setup_inputs(seed: int = 0) -> dict:
    inp = _fwd_setup_inputs(seed)
    key = _jax.random.fold_in(_jax.random.key(seed), 7919)
    shape, _ = _output_shape()
    out = dict(inp)
    out["loss_target"] = _jax.random.normal(_jax.random.fold_in(key, 0), shape, _jnp.float32)
    for i, name in enumerate(TWIN_WEIGHTS):
        w = inp[name].astype(_jnp.float32)
        if MOMENT_SCALE is None:
            s = _jnp.sqrt(_jnp.mean(_jnp.square(w)) + 1e-30)
        else:
            s = MOMENT_SCALE[name]
        km, kv = _jax.random.split(_jax.random.fold_in(key, i + 1))
        out[name] = w
        out["m_" + name] = s * _jax.random.normal(km, w.shape, _jnp.float32)
        out["v_" + name] = (s * s) * _jax.random.uniform(kv, w.shape, _jnp.float32, 0.5, 1.5)
    if N_MICROBATCH > 1:
        for name, axis in PER_EXAMPLE_BATCH_AXIS.items():
            out[name] = _to_microbatches(out[name], axis)
    return {'x': out['x'], 'mem': out['mem'], 'mix_norm_g': out['mix_norm_g'], 'w_in': out['w_in'], 'w_pool': out['w_pool'], 'pool_scale': out['pool_scale'], 'dn_conv_w': out['dn_conv_w'], 'dn_a_log': out['dn_a_log'], 'dn_dt_bias': out['dn_dt_bias'], 'dn_norm_g': out['dn_norm_g'], 'w_mix_out': out['w_mix_out'], 'xa_norm_g': out['xa_norm_g'], 'mem_norm_g': out['mem_norm_g'], 'w_xq': out['w_xq'], 'w_xkv': out['w_xkv'], 'w_xo': out['w_xo'], 'ffn_norm_g': out['ffn_norm_g'], 'w_gate': out['w_gate'], 'w_up': out['w_up'], 'ffn_conv_w': out['ffn_conv_w'], 'ffn_conv_b': out['ffn_conv_b'], 'w_down': out['w_down'], 'final_norm_g': out['final_norm_g'], 'loss_target': out['loss_target'], 'm_mix_norm_g': out['m_mix_norm_g'], 'm_w_in': out['m_w_in'], 'm_w_pool': out['m_w_pool'], 'm_pool_scale': out['m_pool_scale'], 'm_dn_conv_w': out['m_dn_conv_w'], 'm_dn_a_log': out['m_dn_a_log'], 'm_dn_dt_bias': out['m_dn_dt_bias'], 'm_dn_norm_g': out['m_dn_norm_g'], 'm_w_mix_out': out['m_w_mix_out'], 'm_xa_norm_g': out['m_xa_norm_g'], 'm_mem_norm_g': out['m_mem_norm_g'], 'm_w_xq': out['m_w_xq'], 'm_w_xkv': out['m_w_xkv'], 'm_w_xo': out['m_w_xo'], 'm_ffn_norm_g': out['m_ffn_norm_g'], 'm_w_gate': out['m_w_gate'], 'm_w_up': out['m_w_up'], 'm_ffn_conv_w': out['m_ffn_conv_w'], 'm_ffn_conv_b': out['m_ffn_conv_b'], 'm_w_down': out['m_w_down'], 'm_final_norm_g': out['m_final_norm_g'], 'v_mix_norm_g': out['v_mix_norm_g'], 'v_w_in': out['v_w_in'], 'v_w_pool': out['v_w_pool'], 'v_pool_scale': out['v_pool_scale'], 'v_dn_conv_w': out['v_dn_conv_w'], 'v_dn_a_log': out['v_dn_a_log'], 'v_dn_dt_bias': out['v_dn_dt_bias'], 'v_dn_norm_g': out['v_dn_norm_g'], 'v_w_mix_out': out['v_w_mix_out'], 'v_xa_norm_g': out['v_xa_norm_g'], 'v_mem_norm_g': out['v_mem_norm_g'], 'v_w_xq': out['v_w_xq'], 'v_w_xkv': out['v_w_xkv'], 'v_w_xo': out['v_w_xo'], 'v_ffn_norm_g': out['v_ffn_norm_g'], 'v_w_gate': out['v_w_gate'], 'v_w_up': out['v_w_up'], 'v_ffn_conv_w': out['v_ffn_conv_w'], 'v_ffn_conv_b': out['v_ffn_conv_b'], 'v_w_down': out['v_w_down'], 'v_final_norm_g': out['v_final_norm_g']}


def _loss(weights, diff, rest, loss_target):
    with _jax.named_scope("forward"):
        args = {**rest, TWIN_DIFF_INPUT: diff, **{k: w.astype(_WEIGHT_DTYPES[k]) for k, w in weights.items()}}
        y = _forward(args)
    with _jax.named_scope("loss_head"):
        err = _jnp.square(y.astype(_jnp.float32) - loss_target)
        return 0.5 * _jnp.sum(_jnp.mean(err, axis=-1)) if err.ndim else 0.5 * err


def _adamw(w, g, m, v):
    m = ADAM_B1 * m + (1.0 - ADAM_B1) * g
    v = ADAM_B2 * v + (1.0 - ADAM_B2) * _jnp.square(g)
    m_hat = m / (1.0 - ADAM_B1 ** ADAM_STEP)
    v_hat = v / (1.0 - ADAM_B2 ** ADAM_STEP)
    delta = -ADAM_LR * (m_hat / (_jnp.sqrt(v_hat) + ADAM_EPS) + ADAM_WD * w)
    return delta, m, v


def reference(x, mem, mix_norm_g, w_in, w_pool, pool_scale, dn_conv_w, dn_a_log, dn_dt_bias, dn_norm_g, w_mix_out, xa_norm_g, mem_norm_g, w_xq, w_xkv, w_xo, ffn_norm_g, w_gate, w_up, ffn_conv_w, ffn_conv_b, w_down, final_norm_g, loss_target, m_mix_norm_g, m_w_in, m_w_pool, m_pool_scale, m_dn_conv_w, m_dn_a_log, m_dn_dt_bias, m_dn_norm_g, m_w_mix_out, m_xa_norm_g, m_mem_norm_g, m_w_xq, m_w_xkv, m_w_xo, m_ffn_norm_g, m_w_gate, m_w_up, m_ffn_conv_w, m_ffn_conv_b, m_w_down, m_final_norm_g, v_mix_norm_g, v_w_in, v_w_pool, v_pool_scale, v_dn_conv_w, v_dn_a_log, v_dn_dt_bias, v_dn_norm_g, v_w_mix_out, v_xa_norm_g, v_mem_norm_g, v_w_xq, v_w_xkv, v_w_xo, v_ffn_norm_g, v_w_gate, v_w_up, v_ffn_conv_w, v_ffn_conv_b, v_w_down, v_final_norm_g):
    given = dict(x=x, mem=mem, mix_norm_g=mix_norm_g, w_in=w_in, w_pool=w_pool, pool_scale=pool_scale, dn_conv_w=dn_conv_w, dn_a_log=dn_a_log, dn_dt_bias=dn_dt_bias, dn_norm_g=dn_norm_g, w_mix_out=w_mix_out, xa_norm_g=xa_norm_g, mem_norm_g=mem_norm_g, w_xq=w_xq, w_xkv=w_xkv, w_xo=w_xo, ffn_norm_g=ffn_norm_g, w_gate=w_gate, w_up=w_up, ffn_conv_w=ffn_conv_w, ffn_conv_b=ffn_conv_b, w_down=w_down, final_norm_g=final_norm_g, loss_target=loss_target, m_mix_norm_g=m_mix_norm_g, m_w_in=m_w_in, m_w_pool=m_w_pool, m_pool_scale=m_pool_scale, m_dn_conv_w=m_dn_conv_w, m_dn_a_log=m_dn_a_log, m_dn_dt_bias=m_dn_dt_bias, m_dn_norm_g=m_dn_norm_g, m_w_mix_out=m_w_mix_out, m_xa_norm_g=m_xa_norm_g, m_mem_norm_g=m_mem_norm_g, m_w_xq=m_w_xq, m_w_xkv=m_w_xkv, m_w_xo=m_w_xo, m_ffn_norm_g=m_ffn_norm_g, m_w_gate=m_w_gate, m_w_up=m_w_up, m_ffn_conv_w=m_ffn_conv_w, m_ffn_conv_b=m_ffn_conv_b, m_w_down=m_w_down, m_final_norm_g=m_final_norm_g, v_mix_norm_g=v_mix_norm_g, v_w_in=v_w_in, v_w_pool=v_w_pool, v_pool_scale=v_pool_scale, v_dn_conv_w=v_dn_conv_w, v_dn_a_log=v_dn_a_log, v_dn_dt_bias=v_dn_dt_bias, v_dn_norm_g=v_dn_norm_g, v_w_mix_out=v_w_mix_out, v_xa_norm_g=v_xa_norm_g, v_mem_norm_g=v_mem_norm_g, v_w_xq=v_w_xq, v_w_xkv=v_w_xkv, v_w_xo=v_w_xo, v_ffn_norm_g=v_ffn_norm_g, v_w_gate=v_w_gate, v_w_up=v_w_up, v_ffn_conv_w=v_ffn_conv_w, v_ffn_conv_b=v_ffn_conv_b, v_w_down=v_w_down, v_final_norm_g=v_final_norm_g)
    weights = {n: given[n] for n in TWIN_WEIGHTS}
    shared = {n: given[n] for n in SHARED_INPUTS}
    per_example = {n: given[n] for n in ['x', 'mem']}
    grad_fn = _jax.value_and_grad(_loss, argnums=(0, 1))

    def one_microbatch(ex, loss_target):
        ex = dict(ex)
        diff = ex.pop(TWIN_DIFF_INPUT)
        return grad_fn(weights, diff, {**shared, **ex}, loss_target)

    if N_MICROBATCH == 1:
        loss, (grad_w, grad_x) = one_microbatch(per_example, given["loss_target"])
    else:
        def body(carry, xs):
            loss_sum, grad_sum = carry
            l_k, (gw_k, gx_k) = one_microbatch(xs[0], xs[1])
            with _jax.named_scope("update"):
                return (loss_sum + l_k, _jax.tree.map(_jnp.add, grad_sum, gw_k)), gx_k

        init = (_jnp.zeros((), _jnp.float32), _jax.tree.map(_jnp.zeros_like, weights))
        (loss, grad_w), grad_x = _jax.lax.scan(body, init, (per_example, given["loss_target"]))
    with _jax.named_scope("update"):
        delta_w, new_m, new_v = {}, {}, {}
        for n in TWIN_WEIGHTS:
            delta_w[n], new_m[n], new_v[n] = _adamw(weights[n], grad_w[n], given["m_" + n], given["v_" + n])
    return (loss, grad_x, *[grad_w[n] for n in TWIN_WEIGHTS], *[delta_w[n] for n in TWIN_WEIGHTS],
            *[new_m[n] for n in TWIN_WEIGHTS], *[new_v[n] for n in TWIN_WEIGHTS])
```

```python
import functools

import jax
import jax.numpy as jnp
from jax import lax
from jax.experimental import pallas as pl
from jax.experimental.pallas import tpu as pltpu

F32 = jnp.float32
BF16 = jnp.bfloat16
MESH = pl.DeviceIdType.MESH

N_DEV = 8
EPS = 1e-6
POOL_WINDOWS = (2, 4, 8, 16)
DN_HEAD_DIM = 128
DN_CHUNK = 64
DN_TAPS = 4
XA_HEADS = 4
ADAM_LR = 0.001
ADAM_B1 = 0.9
ADAM_B2 = 0.999
ADAM_EPS = 1e-08
ADAM_WD = 0.01
ADAM_STEP = 10

LANES = 128
VMEM_LIMIT = 48 * 1024 * 1024
UT_ROWS = 256

WEIGHTS = ['mix_norm_g', 'w_in', 'w_pool', 'pool_scale', 'dn_conv_w', 'dn_a_log', 'dn_dt_bias', 'dn_norm_g',
           'w_mix_out', 'xa_norm_g', 'mem_norm_g', 'w_xq', 'w_xkv', 'w_xo', 'ffn_norm_g', 'w_gate', 'w_up',
           'ffn_conv_w', 'ffn_conv_b', 'w_down', 'final_norm_g']
SHARD_AXIS = {'w_in': 2, 'w_pool': 2, 'dn_conv_w': 2, 'w_mix_out': 1, 'w_xq': 1, 'w_xkv': 2, 'w_xo': 1,
              'w_gate': 2, 'w_up': 2, 'ffn_conv_w': 2, 'w_down': 1}
F32_TRAVEL = ('dn_conv_w', 'ffn_conv_w')
REPLICATED = [n for n in WEIGHTS if n not in SHARD_AXIS]
PER_LAYER_REPLICATED = ('mix_norm_g', 'pool_scale', 'dn_a_log', 'dn_dt_bias', 'dn_norm_g', 'xa_norm_g',
                        'ffn_norm_g', 'ffn_conv_b')


def _params(*semantics):
    return pltpu.CompilerParams(dimension_semantics=semantics, vmem_limit_bytes=VMEM_LIMIT)


def _tile(dim, pref, unit=LANES):
    if dim <= pref:
        return dim
    t = (pref // unit) * unit
    while t >= unit:
        if dim % t == 0:
            return t
        t -= unit
    return dim


def matmul(a, b, mode, out_dtype, res=None):
    assert a.dtype == BF16 and b.dtype == BF16, (a.dtype, b.dtype)
    if mode == 'nn':
        (m, c), (c2, n) = a.shape, b.shape
    elif mode == 'nt':
        (m, c), (n, c2) = a.shape, b.shape
    else:
        (c, m), (c2, n) = a.shape, b.shape
    assert c == c2, (mode, a.shape, b.shape)
    tm, tn, tc = _tile(m, 1024), _tile(n, 512), _tile(c, 2048)
    nc = c // tc
    if mode == 'nn':
        a_spec = pl.BlockSpec((tm, tc), lambda i, j, k: (i, k))
        b_spec = pl.BlockSpec((tc, tn), lambda i, j, k: (k, j))
        dims = (((1,), (0,)), ((), ()))
    elif mode == 'nt':
        a_spec = pl.BlockSpec((tm, tc), lambda i, j, k: (i, k))
        b_spec = pl.BlockSpec((tn, tc), lambda i, j, k: (j, k))
        dims = (((1,), (1,)), ((), ()))
    else:
        a_spec = pl.BlockSpec((tc, tm), lambda i, j, k: (k, i))
        b_spec = pl.BlockSpec((tc, tn), lambda i, j, k: (k, j))
        dims = (((0,), (0,)), ((), ()))
    out_spec = pl.BlockSpec((tm, tn), lambda i, j, k: (i, j))
    has_res = res is not None

    def body(a_ref, b_ref, *rest):
        rest = list(rest)
        r_ref = rest.pop(0) if has_res else None
        o_ref = rest.pop(0)
        prod = lax.dot_general(a_ref[...], b_ref[...], dims, preferred_element_type=F32)

        def finish(total):
            if has_res:
                total = total + r_ref[...]
            o_ref[...] = total.astype(o_ref.dtype)

        if nc == 1:
            finish(prod)
            return
        acc_ref, = rest
        k = pl.program_id(2)

        @pl.when(k == 0)
        def _():
            acc_ref[...] = prod

        @pl.when(k > 0)
        def _():
            acc_ref[...] += prod

        @pl.when(k == nc - 1)
        def _():
            finish(acc_ref[...])

    return pl.pallas_call(
        body,
        name=f"mm_{mode}_{m}x{c}x{n}" + ("_res" if has_res else ""),
        grid=(m // tm, n // tn, nc),
        in_specs=[a_spec, b_spec] + ([out_spec] if has_res else []),
        out_specs=out_spec,
        out_shape=jax.ShapeDtypeStruct((m, n), out_dtype),
        scratch_shapes=[] if nc == 1 else [pltpu.VMEM((tm, tn), F32)],
        compiler_params=_params("parallel", "parallel", "arbitrary"),
    )(a, b, *([res] if has_res else []))


def _position():
    return lax.axis_index("x"), lax.axis_index("y"), lax.axis_index("c")


def _flip(v, bit):
    return 1 - v if bit else v


def _dev_index(px, py, pc):
    return 4 * px + 2 * py + pc


def all_gather(shard):
    def body(x_ref, out_ref, send_sems, recv_sems, local_sem):
        x, y, c = _position()
        me, sibling = (x, y, c), (x, y, 1 - c)
        chips = [(1 - x, y), (x, 1 - y), (1 - x, 1 - y)]

        def slot(px, py, pc):
            return out_ref.at[_dev_index(px, py, pc)]

        def copy(k, block, to, src=None):
            return pltpu.make_async_remote_copy(
                src_ref=slot(*block) if src is None else src, dst_ref=slot(*block),
                send_sem=send_sems.at[k], recv_sem=recv_sems.at[k], device_id=to, device_id_type=MESH)

        mine = pltpu.make_async_copy(x_ref, slot(*me), local_sem)
        mine.start()
        first = [copy(0, me, sibling, src=x_ref)]
        first += [copy(1 + j, me, (*chip, c), src=x_ref) for j, chip in enumerate(chips)]
        for cp in first:
            cp.start()
        passed = [copy(4 + j, (*chip, c), sibling) for j, chip in enumerate(chips)]
        for j, chip in enumerate(chips):
            copy(1 + j, (*chip, c), me).wait_recv()
            passed[j].start()
        copy(0, sibling, me).wait_recv()
        for j, chip in enumerate(chips):
            copy(4 + j, (*chip, 1 - c), me).wait_recv()
        for cp in first + passed:
            cp.wait_send()
        mine.wait()

    return pl.pallas_call(
        body,
        name="all_gather_" + "x".join(map(str, shard.shape)) + "_" + shard.dtype.name,
        out_shape=jax.ShapeDtypeStruct((N_DEV,) + shard.shape, shard.dtype),
        in_specs=[pl.BlockSpec(memory_space=pl.ANY)],
        out_specs=pl.BlockSpec(memory_space=pl.ANY),
        scratch_shapes=[pltpu.SemaphoreType.DMA((7,)), pltpu.SemaphoreType.DMA((7,)), pltpu.SemaphoreType.DMA],
    )(shard)


def shard_exchange(parts):
    def body(g_ref, out_ref, send_sems, recv_sems, local_sem):
        x, y, c = _position()
        me = _dev_index(x, y, c)
        mine = pltpu.make_async_copy(g_ref.at[me], out_ref.at[me], local_sem)
        mine.start()
        copies = []
        for k in range(1, N_DEV):
            px, py, pc = _flip(x, k & 4), _flip(y, k & 2), _flip(c, k & 1)
            peer = _dev_index(px, py, pc)
            copies.append(pltpu.make_async_remote_copy(
                src_ref=g_ref.at[peer], dst_ref=out_ref.at[me], send_sem=send_sems.at[k - 1],
                recv_sem=recv_sems.at[k - 1], device_id=(px, py, pc), device_id_type=MESH))
        for cp in copies:
            cp.start()
        for cp in copies:
            cp.wait_recv()
        for cp in copies:
            cp.wait_send()
        mine.wait()

    return pl.pallas_call(
        body,
        name="shard_exchange_" + "x".join(map(str, parts.shape[1:])) + "_" + parts.dtype.name,
        out_shape=jax.ShapeDtypeStruct(parts.shape, parts.dtype),
        in_specs=[pl.BlockSpec(memory_space=pl.ANY)],
        out_specs=pl.BlockSpec(memory_space=pl.ANY),
        scratch_shapes=[pltpu.SemaphoreType.DMA((7,)), pltpu.SemaphoreType.DMA((7,)), pltpu.SemaphoreType.DMA],
    )(parts)


def adamw(parts, w, m, v):
    _, r, c = parts.shape
    unit = 16 if parts.dtype == BF16 else 8
    tr = _tile(r, max(unit, (256 * 1024 // c) // unit * unit), unit)

    def body(p_ref, w_ref, m_ref, v_ref, g_ref, d_ref, nm_ref, nv_ref):
        g = p_ref[0].astype(F32)
        for j in range(1, N_DEV):
            g = g + p_ref[j].astype(F32)
        nm = ADAM_B1 * m_ref[...] + (1.0 - ADAM_B1) * g
        nv = ADAM_B2 * v_ref[...] + (1.0 - ADAM_B2) * jnp.square(g)
        m_hat = nm / (1.0 - ADAM_B1 ** ADAM_STEP)
        v_hat = nv / (1.0 - ADAM_B2 ** ADAM_STEP)
        g_ref[...] = g
        d_ref[...] = -ADAM_LR * (m_hat / (jnp.sqrt(v_hat) + ADAM_EPS) + ADAM_WD * w_ref[...])
        nm_ref[...] = nm
        nv_ref[...] = nv

    spec = pl.BlockSpec((tr, c), lambda i: (i, 0))
    out = jax.ShapeDtypeStruct((r, c), F32)
    return pl.pallas_call(
        body,
        name=f"adamw_{r}x{c}_{parts.dtype.name}",
        grid=(r // tr,),
        in_specs=[pl.BlockSpec((N_DEV, tr, c), lambda i: (0, i, 0)), spec, spec, spec],
        out_specs=[spec, spec, spec, spec],
        out_shape=[out, out, out, out],
        compiler_params=_params("parallel"),
    )(parts, w, m, v)


_NN = (((1,), (0,)), ((), ()))
_NT = (((1,), (1,)), ((), ()))
_TN = (((0,), (0,)), ((), ()))


def _dot_bf16(a, b, dims):
    return lax.dot_general(a.astype(BF16), b.astype(BF16), dims, preferred_element_type=F32)


def _dot_split(a, b, dims):
    a_hi, b_hi = a.astype(BF16), b.astype(BF16)
    a_lo = (a - a_hi.astype(F32)).astype(BF16)
    b_lo = (b - b_hi.astype(F32)).astype(BF16)
    dot = functools.partial(lax.dot_general, dimension_numbers=dims, preferred_element_type=F32)
    return dot(a_hi, b_hi) + (dot(a_hi, b_lo) + dot(a_lo, b_hi))


def _matmul_family(dot):
    @jax.custom_vjp
    def nn(a, b):
        return dot(a, b, _NN)

    @jax.custom_vjp
    def nt(a, b):
        return dot(a, b, _NT)

    @jax.custom_vjp
    def tn(a, b):
        return dot(a, b, _TN)

    nn.defvjp(lambda a, b: (nn(a, b), (a, b)), lambda r, g: (nt(g, r[1]), tn(r[0], g)))
    nt.defvjp(lambda a, b: (nt(a, b), (a, b)), lambda r, g: (nn(g, r[1]), tn(g, r[0])))
    tn.defvjp(lambda a, b: (tn(a, b), (a, b)), lambda r, g: (nt(r[1], g), nn(r[0], g)))
    return nn, nt, tn


mm, mm_nt, mm_tn = _matmul_family(_dot_bf16)
mms, _, _ = _matmul_family(_dot_split)


def _shift_rows(x, k, down):
    n = x.shape[0]
    rows = lax.broadcasted_iota(jnp.int32, x.shape, 0)
    if down:
        return jnp.where(rows >= k, pltpu.roll(x, k, 0), 0.0)
    return jnp.where(rows < n - k, pltpu.roll(x, n - k, 0), 0.0)


@functools.partial(jax.custom_vjp, nondiff_argnums=(1,))
def delay(x, k):
    return _shift_rows(x, k, True) if k else x


delay.defvjp(lambda x, k: (delay(x, k), None), lambda k, _, g: ((_shift_rows(g, k, False) if k else g),))


def _silu(x):
    return x * jax.nn.sigmoid(x)


def _rms(x, g):
    return x * lax.rsqrt(jnp.mean(x * x, axis=-1, keepdims=True) + EPS) * g


def _l2n(t):
    return t * lax.rsqrt(jnp.sum(t * t, axis=-1, keepdims=True) + EPS)


def _causal_conv(x, taps):
    k_taps = len(taps)
    y = delay(x, k_taps - 1) * taps[0]
    for k in range(1, k_taps):
        y = y + delay(x, k_taps - 1 - k) * taps[k]
    return y


def _pool_block(u, w, scale, group):
    sums, acc, width = [], u, 1
    while width < POOL_WINDOWS[-1]:
        acc = acc + delay(acc, width)
        width *= 2
        sums.append(acc)
    picked = sums[-1]
    for i in range(len(POOL_WINDOWS) - 2, -1, -1):
        picked = jnp.where(group == i, sums[i], picked)
    rows = lax.broadcasted_iota(jnp.int32, u.shape, 0)
    count = jnp.minimum(rows + 1, jnp.left_shift(2, group)).astype(F32)
    return mm(picked / count - u, w) * scale


def _unit_lower_inverse(l_mat):
    n = l_mat.shape[0]
    eye = (lax.broadcasted_iota(jnp.int32, (n, n), 0) == lax.broadcasted_iota(jnp.int32, (n, n), 1)).astype(F32)
    m1 = -l_mat
    m2 = mms(m1, m1)
    m4 = mms(m2, m2)
    m8 = mms(m4, m4)
    m16 = mms(m8, m8)
    m32 = mms(m16, m16)
    low = mms(eye + m1, eye + m2)
    mid = mms(eye + m4, eye + m8)
    high = mms(eye + m16, eye + m32)
    return mms(mms(low, mid), high)


def _ut_chain(k, v, beta, gc_c, gc_r):
    r = k.shape[0]
    kn = _l2n(k)
    row = lax.broadcasted_iota(jnp.int32, (r, r), 0)
    col = lax.broadcasted_iota(jnp.int32, (r, r), 1)
    strict = (row // DN_CHUNK == col // DN_CHUNK) & (row > col)
    decay = jnp.exp(jnp.where(strict, gc_c - gc_r, -1e30))
    l_mat = jnp.where(strict, beta * mm_nt(kn, kn) * decay, 0.0)
    inv = _unit_lower_inverse(l_mat)
    return mms(inv, v * beta), mms(inv, kn * (beta * jnp.exp(gc_c)))


def _rec_chain(q, k, gc_c, gc_r, u, w, state):
    ch, dh = q.shape
    qn = _l2n(q) * (dh ** -0.5)
    kn = _l2n(k)
    row = lax.broadcasted_iota(jnp.int32, (ch, ch), 0)
    col = lax.broadcasted_iota(jnp.int32, (ch, ch), 1)
    decay = jnp.exp(jnp.where(row >= col, gc_c - gc_r, -1e30))
    attn = mm_nt(qn, kn) * decay
    is_last = lax.broadcasted_iota(jnp.int32, gc_r.shape, 1) == ch - 1
    last = jnp.sum(jnp.where(is_last, gc_r, 0.0), axis=1, keepdims=True)
    v_new = u - mm(w, state)
    out = mm(qn * jnp.exp(gc_c), state) + mm(attn, v_new)
    return out, state * jnp.exp(last) + mm_tn(kn * jnp.exp(last - gc_c), v_new)


def _gated_norm(o, z, g):
    return _rms(o, g) * _silu(z)


def _attn_block(q, k, v):
    s = mm_nt(q, k) * (q.shape[-1] ** -0.5)
    p = jnp.exp(s - lax.stop_gradient(jnp.max(s, axis=-1, keepdims=True)))
    return mm(p / jnp.sum(p, axis=-1, keepdims=True), v)


def _ffn_block(gate_pre, up, taps, bias):
    return _silu(_causal_conv(gate_pre, taps) + bias) * up


def _accumulate(ref, value, first):
    @pl.when(first)
    def _():
        ref[...] = value

    @pl.when(jnp.logical_not(first))
    def _():
        ref[...] += value


def norm_fwd(x, g):
    t, d = x.shape
    tr = _tile(t, 256, 8)

    def body(x_ref, g_ref, h_ref):
        h_ref[...] = _rms(x_ref[...], g_ref[...]).astype(BF16)

    return pl.pallas_call(
        body, name=f"norm_fwd_{t}", grid=(t // tr,),
        in_specs=[pl.BlockSpec((tr, d), lambda i: (i, 0)), pl.BlockSpec((1, d), lambda i: (0, 0))],
        out_specs=pl.BlockSpec((tr, d), lambda i: (i, 0)),
        out_shape=jax.ShapeDtypeStruct((t, d), BF16), compiler_params=_params("parallel"),
    )(x, g.reshape(1, d))


def norm_bwd(x, g, dh, dres):
    t, d = x.shape
    tr = _tile(t, 256, 8)
    has_res = dres is not None

    def body(x_ref, g_ref, dh_ref, *rest):
        rest = list(rest)
        r_ref = rest.pop(0) if has_res else None
        dx_ref, dxb_ref, dg_ref = rest
        _, pull = jax.vjp(_rms, x_ref[...], g_ref[...])
        dx, dg = pull(dh_ref[...])
        if has_res:
            dx = dx + r_ref[...]
        dx_ref[...] = dx
        dxb_ref[...] = dx.astype(BF16)
        _accumulate(dg_ref, dg, pl.program_id(0) == 0)

    row = pl.BlockSpec((tr, d), lambda i: (i, 0))
    vec = pl.BlockSpec((1, d), lambda i: (0, 0))
    dx, dxb, dg = pl.pallas_call(
        body, name=f"norm_bwd_{t}" + ("_res" if has_res else ""), grid=(t // tr,),
        in_specs=[row, vec, row] + ([row] if has_res else []),
        out_specs=[row, row, vec],
        out_shape=[jax.ShapeDtypeStruct((t, d), F32), jax.ShapeDtypeStruct((t, d), BF16),
                   jax.ShapeDtypeStruct((1, d), F32)],
        compiler_params=_params("arbitrary"),
    )(x, g.reshape(1, d), dh, *([dres] if has_res else []))
    return dx, dxb, dg.reshape(d)


def loss_head(x, target, g):
    t, d = x.shape
    tr = _tile(t, 256, 8)

    def body(x_ref, t_ref, g_ref, l_ref, dx_ref, dxb_ref, dg_ref):
        tgt = t_ref[...]

        def block_loss(xv, gv):
            return 0.5 * jnp.sum(jnp.mean(jnp.square(_rms(xv, gv) - tgt), axis=-1))

        val, pull = jax.vjp(block_loss, x_ref[...], g_ref[...])
        dx, dg = pull(jnp.ones((), F32))
        dx_ref[...] = dx
        dxb_ref[...] = dx.astype(BF16)
        first = pl.program_id(0) == 0
        _accumulate(dg_ref, dg, first)
        _accumulate(l_ref, jnp.full((1, LANES), val, F32), first)

    row = pl.BlockSpec((tr, d), lambda i: (i, 0))
    vec = pl.BlockSpec((1, d), lambda i: (0, 0))
    loss, dx, dxb, dg = pl.pallas_call(
        body, name="loss_head", grid=(t // tr,),
        in_specs=[row, row, vec],
        out_specs=[pl.BlockSpec((1, LANES), lambda i: (0, 0)), row, row, vec],
        out_shape=[jax.ShapeDtypeStruct((1, LANES), F32), jax.ShapeDtypeStruct((t, d), F32),
                   jax.ShapeDtypeStruct((t, d), BF16), jax.ShapeDtypeStruct((1, d), F32)],
        compiler_params=_params("arbitrary"),
    )(x, target, g.reshape(1, d))
    return loss[0, 0], dx, dxb, dg.reshape(d)


def pool_fwd(proj, w_pool, scale, b, s):
    n_g, grp = w_pool.shape[0], w_pool.shape[-1]

    def body(u_ref, w_ref, s_ref, y_ref):
        y_ref[...] = _pool_block(u_ref[...], w_ref[...], s_ref[...], pl.program_id(1)).astype(BF16)

    blk = pl.BlockSpec((s, grp), lambda i, j: (i, j))
    return pl.pallas_call(
        body, name="pool_fwd", grid=(b, n_g),
        in_specs=[blk, pl.BlockSpec((None, grp, grp), lambda i, j: (j, 0, 0)),
                  pl.BlockSpec((1, grp), lambda i, j: (0, j))],
        out_specs=blk,
        out_shape=jax.ShapeDtypeStruct((b * s, n_g * grp), BF16), compiler_params=_params("parallel", "parallel"),
    )(proj, w_pool, scale.reshape(1, -1))


def pool_bwd(proj, w_pool, scale, dmixed, b, s):
    n_g, grp = w_pool.shape[0], w_pool.shape[-1]

    def body(u_ref, w_ref, s_ref, dy_ref, du_ref, dw_ref, ds_ref):
        group = pl.program_id(0)
        _, pull = jax.vjp(lambda u, w, sc: _pool_block(u, w, sc, group), u_ref[...], w_ref[...].astype(F32),
                          s_ref[...])
        du, dw, ds = pull(dy_ref[...])
        du_ref[...] = du.astype(BF16)
        first = pl.program_id(1) == 0
        _accumulate(dw_ref, dw, first)
        _accumulate(ds_ref, ds, first)

    blk = pl.BlockSpec((s, grp), lambda j, i: (i, j))
    w_spec = pl.BlockSpec((None, grp, grp), lambda j, i: (j, 0, 0))
    s_spec = pl.BlockSpec((1, grp), lambda j, i: (0, j))
    du, dw, ds = pl.pallas_call(
        body, name="pool_bwd", grid=(n_g, b),
        in_specs=[blk, w_spec, s_spec, blk],
        out_specs=[blk, w_spec, s_spec],
        out_shape=[jax.ShapeDtypeStruct((b * s, n_g * grp), BF16), jax.ShapeDtypeStruct(w_pool.shape, F32),
                   jax.ShapeDtypeStruct((1, n_g * grp), F32)],
        compiler_params=_params("arbitrary", "arbitrary"),
    )(proj, w_pool, scale.reshape(1, -1), dmixed)
    return du, dw, ds.reshape(-1)


def conv_silu_fwd(proj, conv_w, part, col0, width, b, s):
    x_off, w_off = (col0 + part * width) // LANES, part * width // LANES

    def body(x_ref, w_ref, y_ref):
        taps = [w_ref[k:k + 1, :] for k in range(DN_TAPS)]
        y_ref[...] = _silu(_causal_conv(x_ref[...], taps))

    return pl.pallas_call(
        body, name=f"conv_silu_fwd_{part}", grid=(width // LANES, b),
        in_specs=[pl.BlockSpec((s, LANES), lambda j, i: (i, x_off + j)),
                  pl.BlockSpec((DN_TAPS, LANES), lambda j, i: (0, w_off + j))],
        out_specs=pl.BlockSpec((s, LANES), lambda j, i: (i, j)),
        out_shape=jax.ShapeDtypeStruct((b * s, width), F32), compiler_params=_params("parallel", "parallel"),
    )(proj, conv_w)


def conv_silu_bwd(proj, conv_w, dact, part, col0, width, b, s):
    x_off, w_off = (col0 + part * width) // LANES, part * width // LANES

    def body(x_ref, w_ref, dy_ref, dx_ref, dw_ref):
        taps = [w_ref[k:k + 1, :] for k in range(DN_TAPS)]
        _, pull = jax.vjp(lambda x, *tp: _silu(_causal_conv(x, tp)), x_ref[...], *taps)
        dx, *dtaps = pull(dy_ref[...])
        dx_ref[...] = dx.astype(BF16)
        first = pl.program_id(1) == 0
        for k in range(DN_TAPS):
            _accumulate(dw_ref.at[k:k + 1, :], dtaps[k], first)

    out_blk = pl.BlockSpec((s, LANES), lambda j, i: (i, j))
    return pl.pallas_call(
        body, name=f"conv_silu_bwd_{part}", grid=(width // LANES, b),
        in_specs=[pl.BlockSpec((s, LANES), lambda j, i: (i, x_off + j)),
                  pl.BlockSpec((DN_TAPS, LANES), lambda j, i: (0, w_off + j)), out_blk],
        out_specs=[out_blk, pl.BlockSpec((DN_TAPS, LANES), lambda j, i: (0, j))],
        out_shape=[jax.ShapeDtypeStruct((b * s, width), BF16), jax.ShapeDtypeStruct((DN_TAPS, width), F32)],
        compiler_params=_params("arbitrary", "arbitrary"),
    )(proj, conv_w, dact)


def _ut_specs(b, s, heads, width):
    r = min(UT_ROWS, s)
    ns = s // r
    tok = pl.BlockSpec((r, width), lambda i, n: (i * ns + n, 0))
    col = pl.BlockSpec((None, heads, r, 1), lambda i, n: (i, 0, n, 0))
    row = pl.BlockSpec((None, heads, None, 1, r), lambda i, n: (i, 0, n, 0, 0))
    return r, ns, tok, col, row


def ut_fwd(act_k, act_v, beta_col, gc_col, gc_row, b, s):
    width = act_k.shape[1]
    heads = width // DN_HEAD_DIM
    r, ns, tok, col, row = _ut_specs(b, s, heads, width)

    def body(k_ref, v_ref, beta_ref, gcc_ref, gcr_ref, u_ref, w_ref):
        for h in range(heads):
            sl = slice(h * DN_HEAD_DIM, (h + 1) * DN_HEAD_DIM)
            u, w = _ut_chain(k_ref[:, sl], v_ref[:, sl], beta_ref[h], gcc_ref[h], gcr_ref[h])
            u_ref[:, sl] = u
            w_ref[:, sl] = w

    out = jax.ShapeDtypeStruct((b * s, width), F32)
    return pl.pallas_call(
        body, name="ut_fwd", grid=(b, ns), in_specs=[tok, tok, col, col, row], out_specs=[tok, tok],
        out_shape=[out, out], compiler_params=_params("parallel", "parallel"),
    )(act_k, act_v, beta_col, gc_col, gc_row)


def ut_bwd(act_k, act_v, beta_col, gc_col, gc_row, du, dw, dk_more, b, s):
    width = act_k.shape[1]
    heads = width // DN_HEAD_DIM
    r, ns, tok, col, row = _ut_specs(b, s, heads, width)

    def body(k_ref, v_ref, beta_ref, gcc_ref, gcr_ref, du_ref, dw_ref, dkm_ref,
             dk_ref, dv_ref, dbeta_ref, dgcc_ref, dgcr_ref):
        for h in range(heads):
            sl = slice(h * DN_HEAD_DIM, (h + 1) * DN_HEAD_DIM)
            _, pull = jax.vjp(_ut_chain, k_ref[:, sl], v_ref[:, sl], beta_ref[h], gcc_ref[h], gcr_ref[h])
            dk, dv, dbeta, dgcc, dgcr = pull((du_ref[:, sl], dw_ref[:, sl]))
            dk_ref[:, sl] = dk + dkm_ref[:, sl]
            dv_ref[:, sl] = dv
            dbeta_ref[h] = dbeta
            dgcc_ref[h] = dgcc
            dgcr_ref[h] = dgcr

    out = jax.ShapeDtypeStruct((b * s, width), F32)
    return pl.pallas_call(
        body, name="ut_bwd", grid=(b, ns), in_specs=[tok, tok, col, col, row, tok, tok, tok],
        out_specs=[tok, tok, col, col, row],
        out_shape=[out, out, jax.ShapeDtypeStruct(beta_col.shape, F32), jax.ShapeDtypeStruct(gc_col.shape, F32),
                   jax.ShapeDtypeStruct(gc_row.shape, F32)],
        compiler_params=_params("parallel", "parallel"),
    )(act_k, act_v, beta_col, gc_col, gc_row, du, dw, dk_more)


def _rec_specs(b, n, heads, width, chunk_of):
    ch = DN_CHUNK
    tok = pl.BlockSpec((b, None, ch, width), lambda i: (0, chunk_of(i), 0, 0))
    col = pl.BlockSpec((b, heads, None, ch, 1), lambda i: (0, 0, chunk_of(i), 0, 0))
    row = pl.BlockSpec((b, heads, None, 1, ch), lambda i: (0, 0, chunk_of(i), 0, 0))
    st = pl.BlockSpec((None, b, heads, DN_HEAD_DIM, DN_HEAD_DIM), lambda i: (chunk_of(i), 0, 0, 0, 0))
    return tok, col, row, st


def rec_fwd(act_q, act_k, gc_col, gc_row, u, w, b, s):
    width = act_q.shape[1]
    heads, n = width // DN_HEAD_DIM, s // DN_CHUNK
    tok, col, row, st = _rec_specs(b, n, heads, width, lambda i: i)
    shape4 = (b, n, DN_CHUNK, width)

    def body(q_ref, k_ref, gcc_ref, gcr_ref, u_ref, w_ref, o_ref, st_ref, state):
        @pl.when(pl.program_id(0) == 0)
        def _():
            state[...] = jnp.zeros_like(state)

        for i in range(b):
            for h in range(heads):
                sl = slice(h * DN_HEAD_DIM, (h + 1) * DN_HEAD_DIM)
                s_in = state[i, h]
                st_ref[i, h] = s_in
                o, s_out = _rec_chain(q_ref[i, :, sl], k_ref[i, :, sl], gcc_ref[i, h], gcr_ref[i, h],
                                      u_ref[i, :, sl], w_ref[i, :, sl], s_in)
                o_ref[i, :, sl] = o
                state[i, h] = s_out

    o, states = pl.pallas_call(
        body, name="rec_fwd", grid=(n,), in_specs=[tok, tok, col, row, tok, tok], out_specs=[tok, st],
        out_shape=[jax.ShapeDtypeStruct(shape4, F32),
                   jax.ShapeDtypeStruct((n, b, heads, DN_HEAD_DIM, DN_HEAD_DIM), F32)],
        scratch_shapes=[pltpu.VMEM((b, heads, DN_HEAD_DIM, DN_HEAD_DIM), F32)],
        compiler_params=_params("arbitrary"),
    )(act_q.reshape(shape4), act_k.reshape(shape4), gc_col.reshape(b, heads, n, DN_CHUNK, 1), gc_row,
      u.reshape(shape4), w.reshape(shape4))
    return o.reshape(b * s, width), states


def rec_bwd(act_q, act_k, gc_col, gc_row, u, w, states, do, b, s):
    width = act_q.shape[1]
    heads, n = width // DN_HEAD_DIM, s // DN_CHUNK
    tok, col, row, st = _rec_specs(b, n, heads, width, lambda i: n - 1 - i)
    shape4 = (b, n, DN_CHUNK, width)

    def body(q_ref, k_ref, gcc_ref, gcr_ref, u_ref, w_ref, st_ref, do_ref,
             dq_ref, dk_ref, du_ref, dw_ref, dgcc_ref, dgcr_ref, dstate):
        @pl.when(pl.program_id(0) == 0)
        def _():
            dstate[...] = jnp.zeros_like(dstate)

        for i in range(b):
            for h in range(heads):
                sl = slice(h * DN_HEAD_DIM, (h + 1) * DN_HEAD_DIM)
                _, pull = jax.vjp(_rec_chain, q_ref[i, :, sl], k_ref[i, :, sl], gcc_ref[i, h], gcr_ref[i, h],
                                  u_ref[i, :, sl], w_ref[i, :, sl], st_ref[i, h])
                dq, dk, dgcc, dgcr, du, dw, ds = pull((do_ref[i, :, sl], dstate[i, h]))
                dq_ref[i, :, sl] = dq
                dk_ref[i, :, sl] = dk
                du_ref[i, :, sl] = du
                dw_ref[i, :, sl] = dw
                dgcc_ref[i, h] = dgcc
                dgcr_ref[i, h] = dgcr
                dstate[i, h] = ds

    tok_out = jax.ShapeDtypeStruct(shape4, F32)
    dq, dk, du, dw, dgcc, dgcr = pl.pallas_call(
        body, name="rec_bwd", grid=(n,), in_specs=[tok, tok, col, row, tok, tok, st, tok],
        out_specs=[tok, tok, tok, tok, col, row],
        out_shape=[tok_out, tok_out, tok_out, tok_out,
                   jax.ShapeDtypeStruct((b, heads, n, DN_CHUNK, 1), F32), jax.ShapeDtypeStruct(gc_row.shape, F32)],
        scratch_shapes=[pltpu.VMEM((b, heads, DN_HEAD_DIM, DN_HEAD_DIM), F32)],
        compiler_params=_params("arbitrary"),
    )(act_q.reshape(shape4), act_k.reshape(shape4), gc_col.reshape(b, heads, n, DN_CHUNK, 1), gc_row,
      u.reshape(shape4), w.reshape(shape4), states, do.reshape(shape4))
    flat = lambda a: a.reshape(b * s, width)
    return flat(dq), flat(dk), flat(du), flat(dw), dgcc.reshape(b, heads, s, 1), dgcr


def dn_norm_fwd(o, proj, g, z_col0):
    t, width = o.shape
    heads = width // DN_HEAD_DIM
    tr = _tile(t, 512, 8)
    z_off = z_col0 // LANES

    def body(o_ref, z_ref, g_ref, y_ref):
        y_ref[...] = _gated_norm(o_ref[...], z_ref[...], g_ref[...]).astype(BF16)

    blk = pl.BlockSpec((tr, DN_HEAD_DIM), lambda i, h: (i, h))
    return pl.pallas_call(
        body, name="dn_norm_fwd", grid=(t // tr, heads),
        in_specs=[blk, pl.BlockSpec((tr, DN_HEAD_DIM), lambda i, h: (i, z_off + h)),
                  pl.BlockSpec((1, DN_HEAD_DIM), lambda i, h: (0, 0))],
        out_specs=blk, out_shape=jax.ShapeDtypeStruct((t, width), BF16),
        compiler_params=_params("parallel", "parallel"),
    )(o, proj, g.reshape(1, -1))


def dn_norm_bwd(o, proj, g, dmixed, z_col0, dy_col0):
    t, width = o.shape
    heads = width // DN_HEAD_DIM
    tr = _tile(t, 512, 8)
    z_off, dy_off = z_col0 // LANES, dy_col0 // LANES

    def body(o_ref, z_ref, g_ref, dy_ref, do_ref, dz_ref, dg_ref):
        _, pull = jax.vjp(_gated_norm, o_ref[...], z_ref[...], g_ref[...])
        do, dz, dg = pull(dy_ref[...])
        do_ref[...] = do
        dz_ref[...] = dz.astype(BF16)
        _accumulate(dg_ref, dg, (pl.program_id(0) == 0) & (pl.program_id(1) == 0))

    blk = pl.BlockSpec((tr, DN_HEAD_DIM), lambda i, h: (i, h))
    vec = pl.BlockSpec((1, DN_HEAD_DIM), lambda i, h: (0, 0))
    do, dz, dg = pl.pallas_call(
        body, name="dn_norm_bwd", grid=(t // tr, heads),
        in_specs=[blk, pl.BlockSpec((tr, DN_HEAD_DIM), lambda i, h: (i, z_off + h)), vec,
                  pl.BlockSpec((tr, DN_HEAD_DIM), lambda i, h: (i, dy_off + h))],
        out_specs=[blk, blk, vec],
        out_shape=[jax.ShapeDtypeStruct((t, width), F32), jax.ShapeDtypeStruct((t, width), BF16),
                   jax.ShapeDtypeStruct((1, DN_HEAD_DIM), F32)],
        compiler_params=_params("arbitrary", "arbitrary"),
    )(o, proj, g.reshape(1, -1), dmixed)
    return do, dz, dg.reshape(-1)


def _attn_specs(b, s, mem_len, d):
    hd = d // XA_HEADS
    tq = _tile(s, 512, 8)
    nq = s // tq
    q_spec = pl.BlockSpec((tq, hd), lambda i, h, j: (i * nq + j, h))
    k_spec = pl.BlockSpec((mem_len, hd), lambda i, h, j: (i, h))
    v_spec = pl.BlockSpec((mem_len, hd), lambda i, h, j: (i, XA_HEADS + h))
    return nq, q_spec, k_spec, v_spec


def attn_fwd(q, kv, b, s):
    d = q.shape[1]
    nq, q_spec, k_spec, v_spec = _attn_specs(b, s, kv.shape[0] // b, d)

    def body(q_ref, k_ref, v_ref, o_ref):
        o_ref[...] = _attn_block(q_ref[...], k_ref[...], v_ref[...]).astype(BF16)

    return pl.pallas_call(
        body, name="attn_fwd", grid=(b, XA_HEADS, nq), in_specs=[q_spec, k_spec, v_spec], out_specs=q_spec,
        out_shape=jax.ShapeDtypeStruct(q.shape, BF16), compiler_params=_params("parallel", "parallel", "parallel"),
    )(q, kv, kv)


def attn_bwd(q, kv, do, b, s):
    d = q.shape[1]
    rows = kv.shape[0]
    nq, q_spec, k_spec, v_spec = _attn_specs(b, s, rows // b, d)

    def body(q_ref, k_ref, v_ref, do_ref, dq_ref, dk_ref, dv_ref):
        _, pull = jax.vjp(_attn_block, q_ref[...], k_ref[...], v_ref[...])
        dq, dk, dv = pull(do_ref[...])
        dq_ref[...] = dq.astype(BF16)
        first = pl.program_id(2) == 0
        _accumulate(dk_ref, dk, first)
        _accumulate(dv_ref, dv, first)

    kv_out = jax.ShapeDtypeStruct((rows, d), F32)
    return pl.pallas_call(
        body, name="attn_bwd", grid=(b, XA_HEADS, nq), in_specs=[q_spec, k_spec, v_spec, q_spec],
        out_specs=[q_spec, k_spec, k_spec], out_shape=[jax.ShapeDtypeStruct(q.shape, BF16), kv_out, kv_out],
        compiler_params=_params("parallel", "parallel", "arbitrary"),
    )(q, kv, kv, do)


def ffn_fwd(gu, conv_w, conv_b, b, s):
    taps_n, f = conv_w.shape

    def body(gu_ref, w_ref, b_ref, y_ref):
        taps = [w_ref[k:k + 1, :] for k in range(taps_n)]
        y_ref[...] = _ffn_block(gu_ref[:, :LANES], gu_ref[:, LANES:], taps, b_ref[...]).astype(BF16)

    return pl.pallas_call(
        body, name="ffn_fwd", grid=(f // LANES, b),
        in_specs=[pl.BlockSpec((s, 2 * LANES), lambda j, i: (i, j)),
                  pl.BlockSpec((taps_n, LANES), lambda j, i: (0, j)), pl.BlockSpec((1, LANES), lambda j, i: (0, j))],
        out_specs=pl.BlockSpec((s, LANES), lambda j, i: (i, j)),
        out_shape=jax.ShapeDtypeStruct((b * s, f), BF16), compiler_params=_params("parallel", "parallel"),
    )(gu, conv_w, conv_b.reshape(1, f))


def ffn_bwd(gu, conv_w, conv_b, dact, b, s):
    taps_n, f = conv_w.shape

    def body(gu_ref, w_ref, b_ref, dy_ref, dgu_ref, dw_ref, db_ref):
        taps = [w_ref[k:k + 1, :] for k in range(taps_n)]
        _, pull = jax.vjp(lambda gt, up, bias, *tp: _ffn_block(gt, up, tp, bias), gu_ref[:, :LANES],
                          gu_ref[:, LANES:], b_ref[...], *taps)
        dgate, dup, dbias, *dtaps = pull(dy_ref[...])
        dgu_ref[:, :LANES] = dgate.astype(BF16)
        dgu_ref[:, LANES:] = dup.astype(BF16)
        first = pl.program_id(1) == 0
        _accumulate(db_ref, dbias, first)
        for k in range(taps_n):
            _accumulate(dw_ref.at[k:k + 1, :], dtaps[k], first)

    gu_spec = pl.BlockSpec((s, 2 * LANES), lambda j, i: (i, j))
    w_spec = pl.BlockSpec((taps_n, LANES), lambda j, i: (0, j))
    b_spec = pl.BlockSpec((1, LANES), lambda j, i: (0, j))
    dgu, dw, db = pl.pallas_call(
        body, name="ffn_bwd", grid=(f // LANES, b),
        in_specs=[gu_spec, w_spec, b_spec, pl.BlockSpec((s, LANES), lambda j, i: (i, j))],
        out_specs=[gu_spec, w_spec, b_spec],
        out_shape=[jax.ShapeDtypeStruct(gu.shape, BF16), jax.ShapeDtypeStruct((taps_n, f), F32),
                   jax.ShapeDtypeStruct((1, f), F32)],
        compiler_params=_params("arbitrary", "arbitrary"),
    )(gu, conv_w, conv_b.reshape(1, f), dact)
    return dgu, dw, db.reshape(f)


def gate_arrays(logits, a_log, dt_bias, b, s, heads):
    n, r = s // DN_CHUNK, min(UT_ROWS, s)
    lg = logits.reshape(b, s, -1)
    beta = jax.nn.sigmoid(lg[..., :heads])
    g = -jnp.exp(a_log) * jax.nn.softplus(lg[..., heads:2 * heads] + dt_bias)
    gc = jnp.cumsum(g.reshape(b, n, DN_CHUNK, heads), axis=2).transpose(0, 3, 1, 2)
    return (beta.transpose(0, 2, 1)[..., None], gc.reshape(b, heads, s, 1), gc[:, :, :, None, :],
            gc.reshape(b, heads, s // r, 1, r))


def forward_layer(x, mem_hb, p, b, s):
    d = x.shape[1]
    pw = d // 2
    dn = d - pw
    heads = dn // DN_HEAD_DIM
    sv = {'x0': x}
    sv['h1'] = h1 = norm_fwd(x, p['mix_norm_g'])
    sv['proj'] = proj = matmul(h1, p['w_in_main'], 'nn', F32)
    logits = matmul(h1, p['w_in_logits'], 'nn', F32)
    y_pool = pool_fwd(proj, p['w_pool'], p['pool_scale'], b, s)
    gates, sv['gates_pull'] = jax.vjp(lambda lg, al, dtb: gate_arrays(lg, al, dtb, b, s, heads), logits,
                                      p['dn_a_log'], p['dn_dt_bias'])
    sv['gates'] = beta_col, gc_col, gc_row, gc_row_ut = gates
    sv['act'] = aq, ak, av = [conv_silu_fwd(proj, p['dn_conv_w'], part, pw, dn, b, s) for part in range(3)]
    sv['u'], sv['w'] = u, w = ut_fwd(ak, av, beta_col, gc_col, gc_row_ut, b, s)
    sv['o_dn'], sv['states'] = o_dn, _ = rec_fwd(aq, ak, gc_col, gc_row, u, w, b, s)
    y_dn = dn_norm_fwd(o_dn, proj, p['dn_norm_g'], pw + 3 * dn)
    sv['mixed'] = mixed = jnp.concatenate([y_pool, y_dn], axis=1)
    sv['x1'] = x1 = matmul(mixed, p['w_mix_out'], 'nn', F32, res=x)

    sv['h2'] = h2 = norm_fwd(x1, p['xa_norm_g'])
    sv['q'] = q = matmul(h2, p['w_xq'], 'nn', F32)
    sv['kv'] = kv = matmul(mem_hb, p['w_xkv'], 'nn', F32)
    sv['o_at'] = o_at = attn_fwd(q, kv, b, s)
    sv['x2'] = x2 = matmul(o_at, p['w_xo'], 'nn', F32, res=x1)

    sv['h3'] = h3 = norm_fwd(x2, p['ffn_norm_g'])
    sv['gu'] = gu = matmul(h3, p['w_gate_up'], 'nn', F32)
    sv['a_ffn'] = a_ffn = ffn_fwd(gu, p['ffn_conv_w'], p['ffn_conv_b'], b, s)
    return matmul(a_ffn, p['w_down'], 'nn', F32, res=x2), sv


def backward_layer(dx, dxb, sv, mem_hb, p, b, s):
    d = dx.shape[1]
    pw = d // 2
    dn = d - pw
    g = {}
    da = matmul(dxb, p['w_down'], 'nt', F32)
    g['w_down'] = matmul(sv['a_ffn'], dxb, 'tn', BF16)
    dgu, g['ffn_conv_w'], g['ffn_conv_b'] = ffn_bwd(sv['gu'], p['ffn_conv_w'], p['ffn_conv_b'], da, b, s)
    dh = matmul(dgu, p['w_gate_up'], 'nt', F32)
    g['w_gate_up'] = matmul(sv['h3'], dgu, 'tn', BF16)
    dx, dxb, g['ffn_norm_g'] = norm_bwd(sv['x2'], p['ffn_norm_g'], dh, dx)

    do = matmul(dxb, p['w_xo'], 'nt', F32)
    g['w_xo'] = matmul(sv['o_at'], dxb, 'tn', BF16)
    dq, dk, dv = attn_bwd(sv['q'], sv['kv'], do, b, s)
    dkv = jnp.concatenate([dk, dv], axis=1).astype(BF16)
    dh = matmul(dq, p['w_xq'], 'nt', F32)
    g['w_xq'] = matmul(sv['h2'], dq, 'tn', BF16)
    g['w_xkv'] = matmul(mem_hb, dkv, 'tn', BF16)
    dmem_h = matmul(dkv, p['w_xkv'], 'nt', F32)
    dx, dxb, g['xa_norm_g'] = norm_bwd(sv['x1'], p['xa_norm_g'], dh, dx)

    dmixed = matmul(dxb, p['w_mix_out'], 'nt', F32)
    g['w_mix_out'] = matmul(sv['mixed'], dxb, 'tn', BF16)
    proj = sv['proj']
    du_pool, g['w_pool'], g['pool_scale'] = pool_bwd(proj, p['w_pool'], p['pool_scale'], dmixed, b, s)
    do_dn, dz, g['dn_norm_g'] = dn_norm_bwd(sv['o_dn'], proj, p['dn_norm_g'], dmixed, pw + 3 * dn, pw)
    beta_col, gc_col, gc_row, gc_row_ut = sv['gates']
    aq, ak, av = sv['act']
    daq, dak_rec, du, dw, dgcc_rec, dgcr = rec_bwd(aq, ak, gc_col, gc_row, sv['u'], sv['w'], sv['states'],
                                                   do_dn, b, s)
    dak, dav, dbeta, dgcc_ut, dgcr_ut = ut_bwd(ak, av, beta_col, gc_col, gc_row_ut, du, dw, dak_rec, b, s)
    dparts, dtaps = zip(*[conv_silu_bwd(proj, p['dn_conv_w'], dact, part, pw, dn, b, s)
                          for part, dact in enumerate((daq, dak, dav))])
    g['dn_conv_w'] = jnp.concatenate(dtaps, axis=1)
    dlogits, g['dn_a_log'], g['dn_dt_bias'] = sv['gates_pull']((dbeta, dgcc_rec + dgcc_ut, dgcr, dgcr_ut))
    dproj = jnp.concatenate([du_pool, *dparts, dz], axis=1)
    dlogits = dlogits.astype(BF16)
    dh = matmul(dproj, p['w_in_main'], 'nt', F32, res=matmul(dlogits, p['w_in_logits'], 'nt', F32))
    g['w_in_main'] = matmul(sv['h1'], dproj, 'tn', BF16)
    g['w_in_logits'] = matmul(sv['h1'], dlogits, 'tn', BF16)
    dx, dxb, g['mix_norm_g'] = norm_bwd(sv['x0'], p['mix_norm_g'], dh, dx)
    return dx, dxb, dmem_h, g


def _to_full(gathered, axis):
    moved = jnp.moveaxis(gathered, 0, axis)
    shape = list(gathered.shape[1:])
    shape[axis] *= N_DEV
    return moved.reshape(shape)


def _to_parts(full, axis):
    shape = list(full.shape)
    shape[axis:axis + 1] = [N_DEV, shape[axis] // N_DEV]
    return jnp.moveaxis(full.reshape(shape), axis, 0)


def _interleave(gate, up):
    lead, f = gate.shape[:-1], gate.shape[-1]
    pair = jnp.stack([gate.reshape(*lead, f // LANES, LANES), up.reshape(*lead, f // LANES, LANES)], axis=-2)
    return pair.reshape(*lead, 2 * f)


def _deinterleave(both):
    lead, f2 = both.shape[:-1], both.shape[-1]
    pair = both.reshape(*lead, f2 // (2 * LANES), 2, LANES)
    return pair[..., 0, :].reshape(*lead, f2 // 2), pair[..., 1, :].reshape(*lead, f2 // 2)


def _pad_rows(n):
    return -(-n // (8 * LANES)) * 8


def kernel(x, mem, mix_norm_g, w_in, w_pool, pool_scale, dn_conv_w, dn_a_log, dn_dt_bias, dn_norm_g, w_mix_out, xa_norm_g, mem_norm_g, w_xq, w_xkv, w_xo, ffn_norm_g, w_gate, w_up, ffn_conv_w, ffn_conv_b, w_down, final_norm_g, loss_target, m_mix_norm_g, m_w_in, m_w_pool, m_pool_scale, m_dn_conv_w, m_dn_a_log, m_dn_dt_bias, m_dn_norm_g, m_w_mix_out, m_xa_norm_g, m_mem_norm_g, m_w_xq, m_w_xkv, m_w_xo, m_ffn_norm_g, m_w_gate, m_w_up, m_ffn_conv_w, m_ffn_conv_b, m_w_down, m_final_norm_g, v_mix_norm_g, v_w_in, v_w_pool, v_pool_scale, v_dn_conv_w, v_dn_a_log, v_dn_dt_bias, v_dn_norm_g, v_w_mix_out, v_xa_norm_g, v_mem_norm_g, v_w_xq, v_w_xkv, v_w_xo, v_ffn_norm_g, v_w_gate, v_w_up, v_ffn_conv_w, v_ffn_conv_b, v_w_down, v_final_norm_g):
    given = dict(locals())
    w = {n: given[n] for n in WEIGHTS}
    mom = {n: given['m_' + n] for n in WEIGHTS}
    var = {n: given['v_' + n] for n in WEIGHTS}
    b, s, d = x.shape
    depth = w_in.shape[0]
    main = 5 * (d // 2)
    n_logits = w_in.shape[-1] * N_DEV - main

    full = {}
    for n, axis in SHARD_AXIS.items():
        travel = w[n] if n in F32_TRAVEL else w[n].astype(BF16)
        full[n] = _to_full(all_gather(travel), axis)
    stacked = {n: full[n] for n in ('w_pool', 'dn_conv_w', 'w_mix_out', 'w_xq', 'w_xkv', 'w_xo', 'ffn_conv_w',
                                    'w_down')}
    stacked['w_in_main'] = full['w_in'][..., :main]
    stacked['w_in_logits'] = jnp.pad(full['w_in'][..., main:], ((0, 0), (0, 0), (0, LANES - n_logits)))
    stacked['w_gate_up'] = _interleave(full['w_gate'], full['w_up'])
    for n in PER_LAYER_REPLICATED:
        stacked[n] = w[n]
    layers = [{n: a[l] for n, a in stacked.items()} for l in range(depth)]

    mem2 = mem.reshape(-1, d)
    mem_hb = norm_fwd(mem2, mem_norm_g)
    xc, saved = x.reshape(b * s, d), []
    for p in layers:
        xc, sv = forward_layer(xc, mem_hb, p, b, s)
        saved.append(sv)
    loss, dx, dxb, g_final = loss_head(xc, loss_target.reshape(b * s, d), final_norm_g)
    g_layers, dmem_h = [None] * depth, None
    for l in reversed(range(depth)):
        dx, dxb, dm, g_layers[l] = backward_layer(dx, dxb, saved[l], mem_hb, layers[l], b, s)
        dmem_h = dm if dmem_h is None else dmem_h + dm
    _, _, g_mem = norm_bwd(mem2, mem_norm_g, dmem_h, None)
    grad_x = dx.reshape(b, s, d)

    g_st = {n: jnp.stack([g[n] for g in g_layers]) for n in g_layers[0]}
    g_full = {n: g_st[n] for n in ('w_pool', 'dn_conv_w', 'w_mix_out', 'w_xq', 'w_xkv', 'w_xo', 'ffn_conv_w',
                                   'w_down')}
    g_full['w_in'] = jnp.concatenate([g_st['w_in_main'], g_st['w_in_logits'][..., :n_logits]], axis=-1)
    g_full['w_gate'], g_full['w_up'] = _deinterleave(g_st['w_gate_up'])

    grad, delta, new_m, new_v = {}, {}, {}, {}

    for n, axis in SHARD_AXIS.items():
        got = shard_exchange(_to_parts(g_full[n], axis))
        cols = w[n].shape[-1]
        outs = adamw(got.reshape(N_DEV, -1, cols), w[n].reshape(-1, cols), mom[n].reshape(-1, cols),
                     var[n].reshape(-1, cols))
        grad[n], delta[n], new_m[n], new_v[n] = [o.reshape(w[n].shape) for o in outs]

    g_rep = {n: g_st[n] for n in PER_LAYER_REPLICATED}
    g_rep['mem_norm_g'], g_rep['final_norm_g'] = g_mem, g_final
    sizes = [w[n].size for n in REPLICATED]
    rows = _pad_rows(sum(sizes) + 1)

    def pack(parts, first):
        flat = jnp.concatenate([jnp.reshape(first, (1,))] + [parts[n].reshape(-1) for n in REPLICATED])
        return jnp.pad(flat, (0, rows * LANES - flat.size)).reshape(rows, LANES)

    zero = jnp.zeros((), F32)
    outs = adamw(all_gather(pack(g_rep, loss)), pack(w, zero), pack(mom, zero), pack(var, zero))
    flat_outs = [o.reshape(-1) for o in outs]
    loss_total = flat_outs[0][0]
    offset = 1
    for n, size in zip(REPLICATED, sizes):
        grad[n], delta[n], new_m[n], new_v[n] = [o[offset:offset + size].reshape(w[n].shape) for o in flat_outs]
        offset += size

    return (loss_total, grad_x, *[grad[n] for n in WEIGHTS], *[delta[n] for n in WEIGHTS],
            *[new_m[n] for n in WEIGHTS], *[new_v[n] for n in WEIGHTS])
```

```python
import functools

import jax
import jax.numpy as jnp
from jax import lax
from jax.experimental import pallas as pl
from jax.experimental.pallas import tpu as pltpu

F32 = jnp.float32
BF16 = jnp.bfloat16
MESH = pl.DeviceIdType.MESH

N_DEV = 8
EPS = 1e-6
POOL_WINDOWS = (2, 4, 8, 16)
DN_HEAD_DIM = 128
DN_CHUNK = 64
DN_TAPS = 4
XA_HEADS = 4
ADAM_LR = 0.001
ADAM_B1 = 0.9
ADAM_B2 = 0.999
ADAM_EPS = 1e-08
ADAM_WD = 0.01
ADAM_STEP = 10

LANES = 128
VMEM_LIMIT = 48 * 1024 * 1024
UT_ROWS = 256

WEIGHTS = ['mix_norm_g', 'w_in', 'w_pool', 'pool_scale', 'dn_conv_w', 'dn_a_log', 'dn_dt_bias', 'dn_norm_g',
           'w_mix_out', 'xa_norm_g', 'mem_norm_g', 'w_xq', 'w_xkv', 'w_xo', 'ffn_norm_g', 'w_gate', 'w_up',
           'ffn_conv_w', 'ffn_conv_b', 'w_down', 'final_norm_g']
SHARD_AXIS = {'w_in': 2, 'w_pool': 2, 'dn_conv_w': 2, 'w_mix_out': 1, 'w_xq': 1, 'w_xkv': 2, 'w_xo': 1,
              'w_gate': 2, 'w_up': 2, 'ffn_conv_w': 2, 'w_down': 1}
MATRICES = ('w_in', 'w_pool', 'w_mix_out', 'w_xq', 'w_xkv', 'w_xo', 'w_gate', 'w_up', 'w_down')
TAPS = ('dn_conv_w', 'ffn_conv_w')
REPLICATED = [n for n in WEIGHTS if n not in SHARD_AXIS]
PER_LAYER_REPLICATED = ('mix_norm_g', 'pool_scale', 'dn_a_log', 'dn_dt_bias', 'dn_norm_g', 'xa_norm_g',
                        'ffn_norm_g', 'ffn_conv_b')


def _params(*semantics):
    return pltpu.CompilerParams(dimension_semantics=semantics, vmem_limit_bytes=VMEM_LIMIT)


def _tile(dim, pref, unit=LANES):
    if dim <= pref:
        return dim
    t = (pref // unit) * unit
    while t >= unit:
        if dim % t == 0:
            return t
        t -= unit
    return dim


def matmul(a, b, mode, out_dtype, res=None):
    assert a.dtype == BF16 and b.dtype == BF16, (a.dtype, b.dtype)
    if mode == 'nn':
        (m, c), (c2, n) = a.shape, b.shape
    elif mode == 'nt':
        (m, c), (n, c2) = a.shape, b.shape
    else:
        (c, m), (c2, n) = a.shape, b.shape
    assert c == c2, (mode, a.shape, b.shape)
    tm, tn, tc = _tile(m, 1024), _tile(n, 512), _tile(c, 2048)
    nc = c // tc
    if mode == 'nn':
        a_spec = pl.BlockSpec((tm, tc), lambda i, j, k: (i, k))
        b_spec = pl.BlockSpec((tc, tn), lambda i, j, k: (k, j))
        dims = (((1,), (0,)), ((), ()))
    elif mode == 'nt':
        a_spec = pl.BlockSpec((tm, tc), lambda i, j, k: (i, k))
        b_spec = pl.BlockSpec((tn, tc), lambda i, j, k: (j, k))
        dims = (((1,), (1,)), ((), ()))
    else:
        a_spec = pl.BlockSpec((tc, tm), lambda i, j, k: (k, i))
        b_spec = pl.BlockSpec((tc, tn), lambda i, j, k: (k, j))
        dims = (((0,), (0,)), ((), ()))
    out_spec = pl.BlockSpec((tm, tn), lambda i, j, k: (i, j))
    has_res = res is not None

    def body(a_ref, b_ref, *rest):
        rest = list(rest)
        r_ref = rest.pop(0) if has_res else None
        o_ref = rest.pop(0)
        prod = lax.dot_general(a_ref[...], b_ref[...], dims, preferred_element_type=F32)

        def finish(total):
            if has_res:
                total = total + r_ref[...]
            o_ref[...] = total.astype(o_ref.dtype)

        if nc == 1:
            finish(prod)
            return
        acc_ref, = rest
        k = pl.program_id(2)

        @pl.when(k == 0)
        def _():
            acc_ref[...] = prod

        @pl.when(k > 0)
        def _():
            acc_ref[...] += prod

        @pl.when(k == nc - 1)
        def _():
            finish(acc_ref[...])

    return pl.pallas_call(
        body,
        name=f"mm_{mode}_{m}x{c}x{n}" + ("_res" if has_res else ""),
        grid=(m // tm, n // tn, nc),
        in_specs=[a_spec, b_spec] + ([out_spec] if has_res else []),
        out_specs=out_spec,
        out_shape=jax.ShapeDtypeStruct((m, n), out_dtype),
        scratch_shapes=[] if nc == 1 else [pltpu.VMEM((tm, tn), F32)],
        compiler_params=_params("parallel", "parallel", "arbitrary"),
    )(a, b, *([res] if has_res else []))


def _position():
    return lax.axis_index("x"), lax.axis_index("y"), lax.axis_index("c")


def _flip(v, bit):
    return 1 - v if bit else v


def _dev_index(px, py, pc):
    return 4 * px + 2 * py + pc


def _hbm_call(body, name, arrays, out_shapes, n_sems, n_local):
    hbm = pl.BlockSpec(memory_space=pl.ANY)
    return pl.pallas_call(
        body, name=name, out_shape=out_shapes, in_specs=[hbm] * len(arrays), out_specs=[hbm] * len(out_shapes),
        scratch_shapes=[pltpu.SemaphoreType.DMA((n_sems,)), pltpu.SemaphoreType.DMA((n_sems,)),
                        pltpu.SemaphoreType.DMA((n_local,))],
    )(*arrays)


def all_gather(shards, tag):
    n = len(shards)

    def body(*refs):
        x_refs, out_refs = refs[:n], refs[n:2 * n]
        send_sems, recv_sems, local_sems = refs[2 * n:]
        x, y, c = _position()
        me, sibling = (x, y, c), (x, y, 1 - c)
        chips = [(1 - x, y), (x, 1 - y), (1 - x, 1 - y)]

        def copy(t, k, block, to, own=False):
            slot = out_refs[t].at[_dev_index(*block)]
            return pltpu.make_async_remote_copy(
                src_ref=x_refs[t] if own else slot, dst_ref=slot, send_sem=send_sems.at[7 * t + k],
                recv_sem=recv_sems.at[7 * t + k], device_id=to, device_id_type=MESH)

        local = [pltpu.make_async_copy(x_refs[t], out_refs[t].at[_dev_index(*me)], local_sems.at[t])
                 for t in range(n)]
        sent = []
        for t in range(n):
            local[t].start()
            sent.append(copy(t, 0, me, sibling, own=True))
            sent += [copy(t, 1 + j, me, (*chip, c), own=True) for j, chip in enumerate(chips)]
        for cp in sent:
            cp.start()
        for j, chip in enumerate(chips):
            for t in range(n):
                copy(t, 1 + j, (*chip, c), me).wait_recv()
                passed = copy(t, 4 + j, (*chip, c), sibling)
                passed.start()
                sent.append(passed)
        for t in range(n):
            copy(t, 0, sibling, me).wait_recv()
            for j, chip in enumerate(chips):
                copy(t, 4 + j, (*chip, 1 - c), me).wait_recv()
        for cp in sent:
            cp.wait_send()
        for cp in local:
            cp.wait()

    return _hbm_call(body, "all_gather_" + tag, shards,
                     [jax.ShapeDtypeStruct((N_DEV,) + a.shape, a.dtype) for a in shards], 7 * n, n)


def sibling_swap(parts, tag):
    n = len(parts)

    def body(*refs):
        p_refs, out_refs = refs[:n], refs[n:2 * n]
        send_sems, recv_sems, _ = refs[2 * n:]
        x, y, c = _position()
        copies = [pltpu.make_async_remote_copy(
            src_ref=p_refs[t].at[2 * k + (1 - c)], dst_ref=out_refs[t].at[k], send_sem=send_sems.at[4 * t + k],
            recv_sem=recv_sems.at[4 * t + k], device_id=(x, y, 1 - c), device_id_type=MESH)
            for t in range(n) for k in range(N_DEV // 2)]
        for cp in copies:
            cp.start()
        for cp in copies:
            cp.wait()

    return _hbm_call(body, "sibling_swap_" + tag, parts,
                     [jax.ShapeDtypeStruct((N_DEV // 2,) + a.shape[1:], a.dtype) for a in parts], 4 * n, 1)


def chip_sum(parts, from_sibling, core):
    _, r, c = parts.shape
    unit = 16 if parts.dtype == BF16 else 8
    tr = _tile(r, max(unit, (512 * 1024 // c) // unit * unit), unit)

    def body(core_ref, a_ref, b_ref, o_ref):
        o_ref[...] = (a_ref[...].astype(F32) + b_ref[...].astype(F32)).astype(o_ref.dtype)

    blk = pl.BlockSpec((None, tr, c), lambda k, i, core_ref: (k, i, 0))
    return pl.pallas_call(
        body, name=f"chip_sum_{r}x{c}_{parts.dtype.name}",
        grid_spec=pltpu.PrefetchScalarGridSpec(
            num_scalar_prefetch=1, grid=(N_DEV // 2, r // tr),
            in_specs=[pl.BlockSpec((None, tr, c), lambda k, i, core_ref: (2 * k + core_ref[0], i, 0)), blk],
            out_specs=blk),
        out_shape=jax.ShapeDtypeStruct(from_sibling.shape, from_sibling.dtype),
        compiler_params=_params("parallel", "parallel"),
    )(core, parts, from_sibling)


def chip_exchange(parts, tag):
    n = len(parts)

    def body(*refs):
        p_refs, out_refs = refs[:n], refs[n:2 * n]
        send_sems, recv_sems, local_sems = refs[2 * n:]
        x, y, c = _position()
        my_chip = 2 * x + y
        local = [pltpu.make_async_copy(p_refs[t].at[my_chip], out_refs[t].at[my_chip], local_sems.at[t])
                 for t in range(n)]
        copies = []
        for t in range(n):
            local[t].start()
            for k in (1, 2, 3):
                px, py = _flip(x, k & 2), _flip(y, k & 1)
                copies.append(pltpu.make_async_remote_copy(
                    src_ref=p_refs[t].at[2 * px + py], dst_ref=out_refs[t].at[my_chip],
                    send_sem=send_sems.at[3 * t + k - 1], recv_sem=recv_sems.at[3 * t + k - 1],
                    device_id=(px, py, c), device_id_type=MESH))
        for cp in copies:
            cp.start()
        for cp in copies:
            cp.wait_recv()
        for cp in copies:
            cp.wait_send()
        for cp in local:
            cp.wait()

    return _hbm_call(body, "chip_exchange_" + tag, parts,
                     [jax.ShapeDtypeStruct(a.shape, a.dtype) for a in parts], 3 * n, n)


def adamw(parts, w, m, v, layer=0, prev=None):
    n_parts, r, c = parts.shape
    depth = w.shape[0]
    unit = 16 if parts.dtype == BF16 else 8
    tr = _tile(r, max(unit, (256 * 1024 // c) // unit * unit), unit)

    def body(p_ref, w_ref, m_ref, v_ref, *rest):
        g_ref, d_ref, nm_ref, nv_ref = rest[-4:]
        g = p_ref[0].astype(F32)
        for j in range(1, n_parts):
            g = g + p_ref[j].astype(F32)
        nm = ADAM_B1 * m_ref[...] + (1.0 - ADAM_B1) * g
        nv = ADAM_B2 * v_ref[...] + (1.0 - ADAM_B2) * jnp.square(g)
        m_hat = nm / (1.0 - ADAM_B1 ** ADAM_STEP)
        v_hat = nv / (1.0 - ADAM_B2 ** ADAM_STEP)
        g_ref[...] = g
        d_ref[...] = -ADAM_LR * (m_hat / (jnp.sqrt(v_hat) + ADAM_EPS) + ADAM_WD * w_ref[...])
        nm_ref[...] = nm
        nv_ref[...] = nv

    spec = pl.BlockSpec((None, tr, c), lambda i: (layer, i, 0))
    out = jax.ShapeDtypeStruct((depth, r, c), F32)
    carried = [] if prev is None else list(prev)
    return pl.pallas_call(
        body,
        name=f"adamw_{n_parts}x{r}x{c}_{parts.dtype.name}_layer{layer}",
        grid=(r // tr,),
        in_specs=[pl.BlockSpec((n_parts, tr, c), lambda i: (0, i, 0)), spec, spec, spec]
        + [pl.BlockSpec(memory_space=pl.ANY)] * len(carried),
        out_specs=[spec, spec, spec, spec],
        out_shape=[out, out, out, out],
        input_output_aliases={4 + k: k for k in range(len(carried))},
        compiler_params=_params("parallel"),
    )(parts, w, m, v, *carried)


_NN = (((1,), (0,)), ((), ()))
_NT = (((1,), (1,)), ((), ()))
_TN = (((0,), (0,)), ((), ()))


def _dot_bf16(a, b, dims):
    return lax.dot_general(a.astype(BF16), b.astype(BF16), dims, preferred_element_type=F32)


def _dot_split(a, b, dims):
    a_hi, b_hi = a.astype(BF16), b.astype(BF16)
    a_lo = (a - a_hi.astype(F32)).astype(BF16)
    b_lo = (b - b_hi.astype(F32)).astype(BF16)
    dot = functools.partial(lax.dot_general, dimension_numbers=dims, preferred_element_type=F32)
    return dot(a_hi, b_hi) + (dot(a_hi, b_lo) + dot(a_lo, b_hi))


def _matmul_family(dot):
    @jax.custom_vjp
    def nn(a, b):
        return dot(a, b, _NN)

    @jax.custom_vjp
    def nt(a, b):
        return dot(a, b, _NT)

    @jax.custom_vjp
    def tn(a, b):
        return dot(a, b, _TN)

    nn.defvjp(lambda a, b: (nn(a, b), (a, b)), lambda r, g: (nt(g, r[1]), tn(r[0], g)))
    nt.defvjp(lambda a, b: (nt(a, b), (a, b)), lambda r, g: (nn(g, r[1]), tn(g, r[0])))
    tn.defvjp(lambda a, b: (tn(a, b), (a, b)), lambda r, g: (nt(r[1], g), nn(r[0], g)))
    return nn, nt, tn


mm, mm_nt, mm_tn = _matmul_family(_dot_bf16)
mms, mms_nt, mms_tn = _matmul_family(_dot_split)


def _shift_rows(x, k, down):
    n = x.shape[0]
    rows = lax.broadcasted_iota(jnp.int32, x.shape, 0)
    if down:
        return jnp.where(rows >= k, pltpu.roll(x, k, 0), 0.0)
    return jnp.where(rows < n - k, pltpu.roll(x, n - k, 0), 0.0)


@functools.partial(jax.custom_vjp, nondiff_argnums=(1,))
def delay(x, k):
    return _shift_rows(x, k, True) if k else x


delay.defvjp(lambda x, k: (delay(x, k), None), lambda k, _, g: ((_shift_rows(g, k, False) if k else g),))


def _silu(x):
    return x * jax.nn.sigmoid(x)


def _rms(x, g):
    return x * lax.rsqrt(jnp.mean(x * x, axis=-1, keepdims=True) + EPS) * g


def _l2n(t):
    return t * lax.rsqrt(jnp.sum(t * t, axis=-1, keepdims=True) + EPS)


def _causal_conv(x, taps):
    k_taps = len(taps)
    y = delay(x, k_taps - 1) * taps[0]
    for k in range(1, k_taps):
        y = y + delay(x, k_taps - 1 - k) * taps[k]
    return y


def _pool_block(u, w, scale, group):
    sums, acc, width = [], u, 1
    while width < POOL_WINDOWS[-1]:
        acc = acc + delay(acc, width)
        width *= 2
        sums.append(acc)
    picked = sums[-1]
    for i in range(len(POOL_WINDOWS) - 2, -1, -1):
        picked = jnp.where(group == i, sums[i], picked)
    rows = lax.broadcasted_iota(jnp.int32, u.shape, 0)
    count = jnp.minimum(rows + 1, jnp.left_shift(2, group)).astype(F32)
    return mm(picked / count - u, w) * scale


def _unit_lower_inverse(l_mat):
    n = l_mat.shape[0]
    eye = (lax.broadcasted_iota(jnp.int32, (n, n), 0) == lax.broadcasted_iota(jnp.int32, (n, n), 1)).astype(F32)
    m1 = -l_mat
    m2 = mms(m1, m1)
    m4 = mms(m2, m2)
    m8 = mms(m4, m4)
    m16 = mms(m8, m8)
    m32 = mms(m16, m16)
    low = mms(eye + m1, eye + m2)
    mid = mms(eye + m4, eye + m8)
    high = mms(eye + m16, eye + m32)
    return mms(mms(low, mid), high)


@jax.custom_vjp
def _known_inverse(l_mat, inv):
    return inv


_known_inverse.defvjp(lambda l_mat, inv: (inv, inv),
                      lambda inv, g: (-mms_tn(inv, mms_nt(g, inv)), jnp.zeros_like(inv)))


def _ut_chain(k, v, beta, gc_c, gc_r, inv=None):
    r = k.shape[0]
    kn = _l2n(k)
    row = lax.broadcasted_iota(jnp.int32, (r, r), 0)
    col = lax.broadcasted_iota(jnp.int32, (r, r), 1)
    strict = (row // DN_CHUNK == col // DN_CHUNK) & (row > col)
    decay = jnp.exp(jnp.where(strict, gc_c - gc_r, -1e30))
    l_mat = jnp.where(strict, beta * mm_nt(kn, kn) * decay, 0.0)
    inv = _unit_lower_inverse(l_mat) if inv is None else _known_inverse(l_mat, inv)
    return mms(inv, v * beta), mms(inv, kn * (beta * jnp.exp(gc_c))), inv


def _rec_chain(q, k, gc_c, gc_r, u, w, state):
    ch, dh = q.shape
    qn = _l2n(q) * (dh ** -0.5)
    kn = _l2n(k)
    row = lax.broadcasted_iota(jnp.int32, (ch, ch), 0)
    col = lax.broadcasted_iota(jnp.int32, (ch, ch), 1)
    decay = jnp.exp(jnp.where(row >= col, gc_c - gc_r, -1e30))
    attn = mm_nt(qn, kn) * decay
    is_last = lax.broadcasted_iota(jnp.int32, gc_r.shape, 1) == ch - 1
    last = jnp.sum(jnp.where(is_last, gc_r, 0.0), axis=1, keepdims=True)
    v_new = u - mm(w, state)
    out = mm(qn * jnp.exp(gc_c), state) + mm(attn, v_new)
    return out, state * jnp.exp(last) + mm_tn(kn * jnp.exp(last - gc_c), v_new)


def _gated_norm(o, z, g):
    return _rms(o, g) * _silu(z)


def _attn_block(q, k, v):
    s = mm_nt(q, k) * (q.shape[-1] ** -0.5)
    p = jnp.exp(s - lax.stop_gradient(jnp.max(s, axis=-1, keepdims=True)))
    return mm(p / jnp.sum(p, axis=-1, keepdims=True), v)


def _ffn_block(gate_pre, up, taps, bias):
    return _silu(_causal_conv(gate_pre, taps) + bias) * up


def _accumulate(ref, value, first):
    @pl.when(first)
    def _():
        ref[...] = value

    @pl.when(jnp.logical_not(first))
    def _():
        ref[...] += value


def norm_fwd(x, g):
    t, d = x.shape
    tr = _tile(t, 256, 8)

    def body(x_ref, g_ref, h_ref):
        h_ref[...] = _rms(x_ref[...], g_ref[...]).astype(BF16)

    return pl.pallas_call(
        body, name=f"norm_fwd_{t}", grid=(t // tr,),
        in_specs=[pl.BlockSpec((tr, d), lambda i: (i, 0)), pl.BlockSpec((1, d), lambda i: (0, 0))],
        out_specs=pl.BlockSpec((tr, d), lambda i: (i, 0)),
        out_shape=jax.ShapeDtypeStruct((t, d), BF16), compiler_params=_params("parallel"),
    )(x, g.reshape(1, d))


def norm_bwd(x, g, dh, dres):
    t, d = x.shape
    tr = _tile(t, 256, 8)
    has_res = dres is not None

    def body(x_ref, g_ref, dh_ref, *rest):
        rest = list(rest)
        r_ref = rest.pop(0) if has_res else None
        dx_ref, dxb_ref, dg_ref = rest
        _, pull = jax.vjp(_rms, x_ref[...], g_ref[...])
        dx, dg = pull(dh_ref[...])
        if has_res:
            dx = dx + r_ref[...]
        dx_ref[...] = dx
        dxb_ref[...] = dx.astype(BF16)
        _accumulate(dg_ref, dg, pl.program_id(0) == 0)

    row = pl.BlockSpec((tr, d), lambda i: (i, 0))
    vec = pl.BlockSpec((1, d), lambda i: (0, 0))
    dx, dxb, dg = pl.pallas_call(
        body, name=f"norm_bwd_{t}" + ("_res" if has_res else ""), grid=(t // tr,),
        in_specs=[row, vec, row] + ([row] if has_res else []),
        out_specs=[row, row, vec],
        out_shape=[jax.ShapeDtypeStruct((t, d), F32), jax.ShapeDtypeStruct((t, d), BF16),
                   jax.ShapeDtypeStruct((1, d), F32)],
        compiler_params=_params("arbitrary"),
    )(x, g.reshape(1, d), dh, *([dres] if has_res else []))
    return dx, dxb, dg.reshape(d)


def loss_head(x, target, g):
    t, d = x.shape
    tr = _tile(t, 256, 8)

    def body(x_ref, t_ref, g_ref, l_ref, dx_ref, dxb_ref, dg_ref):
        tgt = t_ref[...]

        def block_loss(xv, gv):
            return 0.5 * jnp.sum(jnp.mean(jnp.square(_rms(xv, gv) - tgt), axis=-1))

        val, pull = jax.vjp(block_loss, x_ref[...], g_ref[...])
        dx, dg = pull(jnp.ones((), F32))
        dx_ref[...] = dx
        dxb_ref[...] = dx.astype(BF16)
        first = pl.program_id(0) == 0
        _accumulate(dg_ref, dg, first)
        _accumulate(l_ref, jnp.full((1, LANES), val, F32), first)

    row = pl.BlockSpec((tr, d), lambda i: (i, 0))
    vec = pl.BlockSpec((1, d), lambda i: (0, 0))
    loss, dx, dxb, dg = pl.pallas_call(
        body, name="loss_head", grid=(t // tr,),
        in_specs=[row, row, vec],
        out_specs=[pl.BlockSpec((1, LANES), lambda i: (0, 0)), row, row, vec],
        out_shape=[jax.ShapeDtypeStruct((1, LANES), F32), jax.ShapeDtypeStruct((t, d), F32),
                   jax.ShapeDtypeStruct((t, d), BF16), jax.ShapeDtypeStruct((1, d), F32)],
        compiler_params=_params("arbitrary"),
    )(x, target, g.reshape(1, d))
    return loss[0, 0], dx, dxb, dg.reshape(d)


def pool_fwd(proj, w_pool, scale, b, s):
    n_g, grp = w_pool.shape[0], w_pool.shape[-1]

    def body(u_ref, w_ref, s_ref, y_ref):
        y_ref[...] = _pool_block(u_ref[...], w_ref[...], s_ref[...], pl.program_id(1)).astype(BF16)

    blk = pl.BlockSpec((s, grp), lambda i, j: (i, j))
    return pl.pallas_call(
        body, name="pool_fwd", grid=(b, n_g),
        in_specs=[blk, pl.BlockSpec((None, grp, grp), lambda i, j: (j, 0, 0)),
                  pl.BlockSpec((1, grp), lambda i, j: (0, j))],
        out_specs=blk,
        out_shape=jax.ShapeDtypeStruct((b * s, n_g * grp), BF16), compiler_params=_params("parallel", "parallel"),
    )(proj, w_pool, scale.reshape(1, -1))


def pool_bwd(proj, w_pool, scale, dmixed, b, s):
    n_g, grp = w_pool.shape[0], w_pool.shape[-1]

    def body(u_ref, w_ref, s_ref, dy_ref, du_ref, dw_ref, ds_ref):
        group = pl.program_id(0)
        _, pull = jax.vjp(lambda u, w, sc: _pool_block(u, w, sc, group), u_ref[...], w_ref[...].astype(F32),
                          s_ref[...])
        du, dw, ds = pull(dy_ref[...])
        du_ref[...] = du.astype(BF16)
        first = pl.program_id(1) == 0
        _accumulate(dw_ref, dw, first)
        _accumulate(ds_ref, ds, first)

    blk = pl.BlockSpec((s, grp), lambda j, i: (i, j))
    w_spec = pl.BlockSpec((None, grp, grp), lambda j, i: (j, 0, 0))
    s_spec = pl.BlockSpec((1, grp), lambda j, i: (0, j))
    du, dw, ds = pl.pallas_call(
        body, name="pool_bwd", grid=(n_g, b),
        in_specs=[blk, w_spec, s_spec, blk],
        out_specs=[blk, w_spec, s_spec],
        out_shape=[jax.ShapeDtypeStruct((b * s, n_g * grp), BF16), jax.ShapeDtypeStruct(w_pool.shape, F32),
                   jax.ShapeDtypeStruct((1, n_g * grp), F32)],
        compiler_params=_params("arbitrary", "arbitrary"),
    )(proj, w_pool, scale.reshape(1, -1), dmixed)
    return du, dw, ds.reshape(-1)


def conv_silu_fwd(proj, conv_w, part, col0, width, b, s):
    x_off, w_off = (col0 + part * width) // LANES, part * width // LANES

    def body(x_ref, w_ref, y_ref):
        taps = [w_ref[k:k + 1, :] for k in range(DN_TAPS)]
        y_ref[...] = _silu(_causal_conv(x_ref[...], taps))

    return pl.pallas_call(
        body, name=f"conv_silu_fwd_{part}", grid=(width // LANES, b),
        in_specs=[pl.BlockSpec((s, LANES), lambda j, i: (i, x_off + j)),
                  pl.BlockSpec((DN_TAPS, LANES), lambda j, i: (0, w_off + j))],
        out_specs=pl.BlockSpec((s, LANES), lambda j, i: (i, j)),
        out_shape=jax.ShapeDtypeStruct((b * s, width), F32), compiler_params=_params("parallel", "parallel"),
    )(proj, conv_w)


def conv_silu_bwd(proj, conv_w, dact, part, col0, width, b, s):
    x_off, w_off = (col0 + part * width) // LANES, part * width // LANES

    def body(x_ref, w_ref, dy_ref, dx_ref, dw_ref):
        taps = [w_ref[k:k + 1, :] for k in range(DN_TAPS)]
        _, pull = jax.vjp(lambda x, *tp: _silu(_causal_conv(x, tp)), x_ref[...], *taps)
        dx, *dtaps = pull(dy_ref[...])
        dx_ref[...] = dx.astype(BF16)
        first = pl.program_id(1) == 0
        for k in range(DN_TAPS):
            _accumulate(dw_ref.at[k:k + 1, :], dtaps[k], first)

    out_blk = pl.BlockSpec((s, LANES), lambda j, i: (i, j))
    return pl.pallas_call(
        body, name=f"conv_silu_bwd_{part}", grid=(width // LANES, b),
        in_specs=[pl.BlockSpec((s, LANES), lambda j, i: (i, x_off + j)),
                  pl.BlockSpec((DN_TAPS, LANES), lambda j, i: (0, w_off + j)), out_blk],
        out_specs=[out_blk, pl.BlockSpec((DN_TAPS, LANES), lambda j, i: (0, j))],
        out_shape=[jax.ShapeDtypeStruct((b * s, width), BF16), jax.ShapeDtypeStruct((DN_TAPS, width), F32)],
        compiler_params=_params("arbitrary", "arbitrary"),
    )(proj, conv_w, dact)


def _ut_specs(b, s, heads, width):
    r = min(UT_ROWS, s)
    ns = s // r
    tok = pl.BlockSpec((r, width), lambda i, n: (i * ns + n, 0))
    col = pl.BlockSpec((None, heads, r, 1), lambda i, n: (i, 0, n, 0))
    row = pl.BlockSpec((None, heads, None, 1, r), lambda i, n: (i, 0, n, 0, 0))
    return r, ns, tok, col, row


def ut_fwd(act_k, act_v, beta_col, gc_col, gc_row, b, s):
    width = act_k.shape[1]
    heads = width // DN_HEAD_DIM
    r, ns, tok, col, row = _ut_specs(b, s, heads, width)
    inv_spec = pl.BlockSpec((r, heads * r), lambda i, n: (i * ns + n, 0))

    def body(k_ref, v_ref, beta_ref, gcc_ref, gcr_ref, u_ref, w_ref, inv_ref):
        for h in range(heads):
            sl = slice(h * DN_HEAD_DIM, (h + 1) * DN_HEAD_DIM)
            u, w, inv = _ut_chain(k_ref[:, sl], v_ref[:, sl], beta_ref[h], gcc_ref[h], gcr_ref[h])
            u_ref[:, sl] = u
            w_ref[:, sl] = w
            inv_ref[:, h * r:(h + 1) * r] = inv

    out = jax.ShapeDtypeStruct((b * s, width), F32)
    return pl.pallas_call(
        body, name="ut_fwd", grid=(b, ns), in_specs=[tok, tok, col, col, row], out_specs=[tok, tok, inv_spec],
        out_shape=[out, out, jax.ShapeDtypeStruct((b * s, heads * r), F32)],
        compiler_params=_params("parallel", "parallel"),
    )(act_k, act_v, beta_col, gc_col, gc_row)


def ut_bwd(act_k, act_v, beta_col, gc_col, gc_row, inv, du, dw, dk_more, b, s):
    width = act_k.shape[1]
    heads = width // DN_HEAD_DIM
    r, ns, tok, col, row = _ut_specs(b, s, heads, width)
    inv_spec = pl.BlockSpec((r, heads * r), lambda i, n: (i * ns + n, 0))

    def body(k_ref, v_ref, beta_ref, gcc_ref, gcr_ref, inv_ref, du_ref, dw_ref, dkm_ref,
             dk_ref, dv_ref, dbeta_ref, dgcc_ref, dgcr_ref):
        for h in range(heads):
            sl = slice(h * DN_HEAD_DIM, (h + 1) * DN_HEAD_DIM)
            inv = inv_ref[:, h * r:(h + 1) * r]
            _, pull = jax.vjp(lambda *a: _ut_chain(*a, inv=inv)[:2], k_ref[:, sl], v_ref[:, sl], beta_ref[h],
                              gcc_ref[h], gcr_ref[h])
            dk, dv, dbeta, dgcc, dgcr = pull((du_ref[:, sl], dw_ref[:, sl]))
            dk_ref[:, sl] = dk + dkm_ref[:, sl]
            dv_ref[:, sl] = dv
            dbeta_ref[h] = dbeta
            dgcc_ref[h] = dgcc
            dgcr_ref[h] = dgcr

    out = jax.ShapeDtypeStruct((b * s, width), F32)
    return pl.pallas_call(
        body, name="ut_bwd", grid=(b, ns), in_specs=[tok, tok, col, col, row, inv_spec, tok, tok, tok],
        out_specs=[tok, tok, col, col, row],
        out_shape=[out, out, jax.ShapeDtypeStruct(beta_col.shape, F32), jax.ShapeDtypeStruct(gc_col.shape, F32),
                   jax.ShapeDtypeStruct(gc_row.shape, F32)],
        compiler_params=_params("parallel", "parallel"),
    )(act_k, act_v, beta_col, gc_col, gc_row, inv, du, dw, dk_more)


def _rec_specs(b, n, heads, width, chunk_of):
    ch = DN_CHUNK
    tok = pl.BlockSpec((b, None, ch, width), lambda i: (0, chunk_of(i), 0, 0))
    col = pl.BlockSpec((b, heads, None, ch, 1), lambda i: (0, 0, chunk_of(i), 0, 0))
    row = pl.BlockSpec((b, heads, None, 1, ch), lambda i: (0, 0, chunk_of(i), 0, 0))
    st = pl.BlockSpec((None, b, heads, DN_HEAD_DIM, DN_HEAD_DIM), lambda i: (chunk_of(i), 0, 0, 0, 0))
    return tok, col, row, st


def rec_fwd(act_q, act_k, gc_col, gc_row, u, w, b, s):
    width = act_q.shape[1]
    heads, n = width // DN_HEAD_DIM, s // DN_CHUNK
    tok, col, row, st = _rec_specs(b, n, heads, width, lambda i: i)
    shape4 = (b, n, DN_CHUNK, width)

    def body(q_ref, k_ref, gcc_ref, gcr_ref, u_ref, w_ref, o_ref, st_ref, state):
        @pl.when(pl.program_id(0) == 0)
        def _():
            state[...] = jnp.zeros_like(state)

        for i in range(b):
            for h in range(heads):
                sl = slice(h * DN_HEAD_DIM, (h + 1) * DN_HEAD_DIM)
                s_in = state[i, h]
                st_ref[i, h] = s_in
                o, s_out = _rec_chain(q_ref[i, :, sl], k_ref[i, :, sl], gcc_ref[i, h], gcr_ref[i, h],
                                      u_ref[i, :, sl], w_ref[i, :, sl], s_in)
                o_ref[i, :, sl] = o
                state[i, h] = s_out

    o, states = pl.pallas_call(
        body, name="rec_fwd", grid=(n,), in_specs=[tok, tok, col, row, tok, tok], out_specs=[tok, st],
        out_shape=[jax.ShapeDtypeStruct(shape4, F32),
                   jax.ShapeDtypeStruct((n, b, heads, DN_HEAD_DIM, DN_HEAD_DIM), F32)],
        scratch_shapes=[pltpu.VMEM((b, heads, DN_HEAD_DIM, DN_HEAD_DIM), F32)],
        compiler_params=_params("arbitrary"),
    )(act_q.reshape(shape4), act_k.reshape(shape4), gc_col.reshape(b, heads, n, DN_CHUNK, 1), gc_row,
      u.reshape(shape4), w.reshape(shape4))
    return o.reshape(b * s, width), states


def rec_bwd(act_q, act_k, gc_col, gc_row, u, w, states, do, b, s):
    width = act_q.shape[1]
    heads, n = width // DN_HEAD_DIM, s // DN_CHUNK
    tok, col, row, st = _rec_specs(b, n, heads, width, lambda i: n - 1 - i)
    shape4 = (b, n, DN_CHUNK, width)

    def body(q_ref, k_ref, gcc_ref, gcr_ref, u_ref, w_ref, st_ref, do_ref,
             dq_ref, dk_ref, du_ref, dw_ref, dgcc_ref, dgcr_ref, dstate):
        @pl.when(pl.program_id(0) == 0)
        def _():
            dstate[...] = jnp.zeros_like(dstate)

        for i in range(b):
            for h in range(heads):
                sl = slice(h * DN_HEAD_DIM, (h + 1) * DN_HEAD_DIM)
                _, pull = jax.vjp(_rec_chain, q_ref[i, :, sl], k_ref[i, :, sl], gcc_ref[i, h], gcr_ref[i, h],
                                  u_ref[i, :, sl], w_ref[i, :, sl], st_ref[i, h])
                dq, dk, dgcc, dgcr, du, dw, ds = pull((do_ref[i, :, sl], dstate[i, h]))
                dq_ref[i, :, sl] = dq
                dk_ref[i, :, sl] = dk
                du_ref[i, :, sl] = du
                dw_ref[i, :, sl] = dw
                dgcc_ref[i, h] = dgcc
                dgcr_ref[i, h] = dgcr
                dstate[i, h] = ds

    tok_out = jax.ShapeDtypeStruct(shape4, F32)
    dq, dk, du, dw, dgcc, dgcr = pl.pallas_call(
        body, name="rec_bwd", grid=(n,), in_specs=[tok, tok, col, row, tok, tok, st, tok],
        out_specs=[tok, tok, tok, tok, col, row],
        out_shape=[tok_out, tok_out, tok_out, tok_out,
                   jax.ShapeDtypeStruct((b, heads, n, DN_CHUNK, 1), F32), jax.ShapeDtypeStruct(gc_row.shape, F32)],
        scratch_shapes=[pltpu.VMEM((b, heads, DN_HEAD_DIM, DN_HEAD_DIM), F32)],
        compiler_params=_params("arbitrary"),
    )(act_q.reshape(shape4), act_k.reshape(shape4), gc_col.reshape(b, heads, n, DN_CHUNK, 1), gc_row,
      u.reshape(shape4), w.reshape(shape4), states, do.reshape(shape4))
    flat = lambda a: a.reshape(b * s, width)
    return flat(dq), flat(dk), flat(du), flat(dw), dgcc.reshape(b, heads, s, 1), dgcr


def dn_norm_fwd(o, proj, g, z_col0):
    t, width = o.shape
    heads = width // DN_HEAD_DIM
    tr = _tile(t, 512, 8)
    z_off = z_col0 // LANES

    def body(o_ref, z_ref, g_ref, y_ref):
        y_ref[...] = _gated_norm(o_ref[...], z_ref[...], g_ref[...]).astype(BF16)

    blk = pl.BlockSpec((tr, DN_HEAD_DIM), lambda i, h: (i, h))
    return pl.pallas_call(
        body, name="dn_norm_fwd", grid=(t // tr, heads),
        in_specs=[blk, pl.BlockSpec((tr, DN_HEAD_DIM), lambda i, h: (i, z_off + h)),
                  pl.BlockSpec((1, DN_HEAD_DIM), lambda i, h: (0, 0))],
        out_specs=blk, out_shape=jax.ShapeDtypeStruct((t, width), BF16),
        compiler_params=_params("parallel", "parallel"),
    )(o, proj, g.reshape(1, -1))


def dn_norm_bwd(o, proj, g, dmixed, z_col0, dy_col0):
    t, width = o.shape
    heads = width // DN_HEAD_DIM
    tr = _tile(t, 512, 8)
    z_off, dy_off = z_col0 // LANES, dy_col0 // LANES

    def body(o_ref, z_ref, g_ref, dy_ref, do_ref, dz_ref, dg_ref):
        _, pull = jax.vjp(_gated_norm, o_ref[...], z_ref[...], g_ref[...])
        do, dz, dg = pull(dy_ref[...])
        do_ref[...] = do
        dz_ref[...] = dz.astype(BF16)
        _accumulate(dg_ref, dg, (pl.program_id(0) == 0) & (pl.program_id(1) == 0))

    blk = pl.BlockSpec((tr, DN_HEAD_DIM), lambda i, h: (i, h))
    vec = pl.BlockSpec((1, DN_HEAD_DIM), lambda i, h: (0, 0))
    do, dz, dg = pl.pallas_call(
        body, name="dn_norm_bwd", grid=(t // tr, heads),
        in_specs=[blk, pl.BlockSpec((tr, DN_HEAD_DIM), lambda i, h: (i, z_off + h)), vec,
                  pl.BlockSpec((tr, DN_HEAD_DIM), lambda i, h: (i, dy_off + h))],
        out_specs=[blk, blk, vec],
        out_shape=[jax.ShapeDtypeStruct((t, width), F32), jax.ShapeDtypeStruct((t, width), BF16),
                   jax.ShapeDtypeStruct((1, DN_HEAD_DIM), F32)],
        compiler_params=_params("arbitrary", "arbitrary"),
    )(o, proj, g.reshape(1, -1), dmixed)
    return do, dz, dg.reshape(-1)


def _attn_specs(b, s, mem_len, d):
    hd = d // XA_HEADS
    tq = _tile(s, 512, 8)
    nq = s // tq
    q_spec = pl.BlockSpec((tq, hd), lambda i, h, j: (i * nq + j, h))
    k_spec = pl.BlockSpec((mem_len, hd), lambda i, h, j: (i, h))
    v_spec = pl.BlockSpec((mem_len, hd), lambda i, h, j: (i, XA_HEADS + h))
    return nq, q_spec, k_spec, v_spec


def attn_fwd(q, kv, b, s):
    d = q.shape[1]
    nq, q_spec, k_spec, v_spec = _attn_specs(b, s, kv.shape[0] // b, d)

    def body(q_ref, k_ref, v_ref, o_ref):
        o_ref[...] = _attn_block(q_ref[...], k_ref[...], v_ref[...]).astype(BF16)

    return pl.pallas_call(
        body, name="attn_fwd", grid=(b, XA_HEADS, nq), in_specs=[q_spec, k_spec, v_spec], out_specs=q_spec,
        out_shape=jax.ShapeDtypeStruct(q.shape, BF16), compiler_params=_params("parallel", "parallel", "parallel"),
    )(q, kv, kv)


def attn_bwd(q, kv, do, b, s):
    d = q.shape[1]
    rows = kv.shape[0]
    nq, q_spec, k_spec, v_spec = _attn_specs(b, s, rows // b, d)

    def body(q_ref, k_ref, v_ref, do_ref, dq_ref, dk_ref, dv_ref):
        _, pull = jax.vjp(_attn_block, q_ref[...], k_ref[...], v_ref[...])
        dq, dk, dv = pull(do_ref[...])
        dq_ref[...] = dq.astype(BF16)
        first = pl.program_id(2) == 0
        _accumulate(dk_ref, dk, first)
        _accumulate(dv_ref, dv, first)

    kv_out = jax.ShapeDtypeStruct((rows, d), F32)
    return pl.pallas_call(
        body, name="attn_bwd", grid=(b, XA_HEADS, nq), in_specs=[q_spec, k_spec, v_spec, q_spec],
        out_specs=[q_spec, k_spec, k_spec], out_shape=[jax.ShapeDtypeStruct(q.shape, BF16), kv_out, kv_out],
        compiler_params=_params("parallel", "parallel", "arbitrary"),
    )(q, kv, kv, do)


def ffn_fwd(gu, conv_w, conv_b, b, s):
    taps_n, f = conv_w.shape

    def body(gu_ref, w_ref, b_ref, y_ref):
        taps = [w_ref[k:k + 1, :] for k in range(taps_n)]
        y_ref[...] = _ffn_block(gu_ref[:, :LANES], gu_ref[:, LANES:], taps, b_ref[...]).astype(BF16)

    return pl.pallas_call(
        body, name="ffn_fwd", grid=(f // LANES, b),
        in_specs=[pl.BlockSpec((s, 2 * LANES), lambda j, i: (i, j)),
                  pl.BlockSpec((taps_n, LANES), lambda j, i: (0, j)), pl.BlockSpec((1, LANES), lambda j, i: (0, j))],
        out_specs=pl.BlockSpec((s, LANES), lambda j, i: (i, j)),
        out_shape=jax.ShapeDtypeStruct((b * s, f), BF16), compiler_params=_params("parallel", "parallel"),
    )(gu, conv_w, conv_b.reshape(1, f))


def ffn_bwd(gu, conv_w, conv_b, dact, b, s):
    taps_n, f = conv_w.shape

    def body(gu_ref, w_ref, b_ref, dy_ref, dgu_ref, dw_ref, db_ref):
        taps = [w_ref[k:k + 1, :] for k in range(taps_n)]
        _, pull = jax.vjp(lambda gt, up, bias, *tp: _ffn_block(gt, up, tp, bias), gu_ref[:, :LANES],
                          gu_ref[:, LANES:], b_ref[...], *taps)
        dgate, dup, dbias, *dtaps = pull(dy_ref[...])
        dgu_ref[:, :LANES] = dgate.astype(BF16)
        dgu_ref[:, LANES:] = dup.astype(BF16)
        first = pl.program_id(1) == 0
        _accumulate(db_ref, dbias, first)
        for k in range(taps_n):
            _accumulate(dw_ref.at[k:k + 1, :], dtaps[k], first)

    gu_spec = pl.BlockSpec((s, 2 * LANES), lambda j, i: (i, j))
    w_spec = pl.BlockSpec((taps_n, LANES), lambda j, i: (0, j))
    b_spec = pl.BlockSpec((1, LANES), lambda j, i: (0, j))
    dgu, dw, db = pl.pallas_call(
        body, name="ffn_bwd", grid=(f // LANES, b),
        in_specs=[gu_spec, w_spec, b_spec, pl.BlockSpec((s, LANES), lambda j, i: (i, j))],
        out_specs=[gu_spec, w_spec, b_spec],
        out_shape=[jax.ShapeDtypeStruct(gu.shape, BF16), jax.ShapeDtypeStruct((taps_n, f), F32),
                   jax.ShapeDtypeStruct((1, f), F32)],
        compiler_params=_params("arbitrary", "arbitrary"),
    )(gu, conv_w, conv_b.reshape(1, f), dact)
    return dgu, dw, db.reshape(f)


def gate_arrays(logits, a_log, dt_bias, b, s, heads):
    n, r = s // DN_CHUNK, min(UT_ROWS, s)
    lg = logits.reshape(b, s, -1)
    beta = jax.nn.sigmoid(lg[..., :heads])
    g = -jnp.exp(a_log) * jax.nn.softplus(lg[..., heads:2 * heads] + dt_bias)
    gc = jnp.cumsum(g.reshape(b, n, DN_CHUNK, heads), axis=2).transpose(0, 3, 1, 2)
    return (beta.transpose(0, 2, 1)[..., None], gc.reshape(b, heads, s, 1), gc[:, :, :, None, :],
            gc.reshape(b, heads, s // r, 1, r))


def forward_layer(x, mem_hb, p, b, s):
    d = x.shape[1]
    pw = d // 2
    dn = d - pw
    heads = dn // DN_HEAD_DIM
    sv = {'x0': x}
    sv['h1'] = h1 = norm_fwd(x, p['mix_norm_g'])
    sv['proj'] = proj = matmul(h1, p['w_in_main'], 'nn', F32)
    logits = matmul(h1, p['w_in_logits'], 'nn', F32)
    y_pool = pool_fwd(proj, p['w_pool'], p['pool_scale'], b, s)
    gates, sv['gates_pull'] = jax.vjp(lambda lg, al, dtb: gate_arrays(lg, al, dtb, b, s, heads), logits,
                                      p['dn_a_log'], p['dn_dt_bias'])
    sv['gates'] = beta_col, gc_col, gc_row, gc_row_ut = gates
    sv['act'] = aq, ak, av = [conv_silu_fwd(proj, p['dn_conv_w'], part, pw, dn, b, s) for part in range(3)]
    sv['u'], sv['w'], sv['inv'] = u, w, _ = ut_fwd(ak, av, beta_col, gc_col, gc_row_ut, b, s)
    sv['o_dn'], sv['states'] = o_dn, _ = rec_fwd(aq, ak, gc_col, gc_row, u, w, b, s)
    y_dn = dn_norm_fwd(o_dn, proj, p['dn_norm_g'], pw + 3 * dn)
    sv['mixed'] = mixed = jnp.concatenate([y_pool, y_dn], axis=1)
    sv['x1'] = x1 = matmul(mixed, p['w_mix_out'], 'nn', F32, res=x)

    sv['h2'] = h2 = norm_fwd(x1, p['xa_norm_g'])
    sv['q'] = q = matmul(h2, p['w_xq'], 'nn', F32)
    sv['kv'] = kv = matmul(mem_hb, p['w_xkv'], 'nn', F32)
    sv['o_at'] = o_at = attn_fwd(q, kv, b, s)
    sv['x2'] = x2 = matmul(o_at, p['w_xo'], 'nn', F32, res=x1)

    sv['h3'] = h3 = norm_fwd(x2, p['ffn_norm_g'])
    sv['gu'] = gu = matmul(h3, p['w_gate_up'], 'nn', F32)
    sv['a_ffn'] = a_ffn = ffn_fwd(gu, p['ffn_conv_w'], p['ffn_conv_b'], b, s)
    return matmul(a_ffn, p['w_down'], 'nn', F32, res=x2), sv


def backward_layer(dx, dxb, sv, mem_hb, p, b, s):
    d = dx.shape[1]
    pw = d // 2
    dn = d - pw
    g = {}
    da = matmul(dxb, p['w_down'], 'nt', F32)
    g['w_down'] = matmul(sv['a_ffn'], dxb, 'tn', BF16)
    dgu, g['ffn_conv_w'], g['ffn_conv_b'] = ffn_bwd(sv['gu'], p['ffn_conv_w'], p['ffn_conv_b'], da, b, s)
    dh = matmul(dgu, p['w_gate_up'], 'nt', F32)
    g['w_gate_up'] = matmul(sv['h3'], dgu, 'tn', BF16)
    dx, dxb, g['ffn_norm_g'] = norm_bwd(sv['x2'], p['ffn_norm_g'], dh, dx)

    do = matmul(dxb, p['w_xo'], 'nt', F32)
    g['w_xo'] = matmul(sv['o_at'], dxb, 'tn', BF16)
    dq, dk, dv = attn_bwd(sv['q'], sv['kv'], do, b, s)
    dkv = jnp.concatenate([dk, dv], axis=1).astype(BF16)
    dh = matmul(dq, p['w_xq'], 'nt', F32)
    g['w_xq'] = matmul(sv['h2'], dq, 'tn', BF16)
    g['w_xkv'] = matmul(mem_hb, dkv, 'tn', BF16)
    dmem_h = matmul(dkv, p['w_xkv'], 'nt', F32)
    dx, dxb, g['xa_norm_g'] = norm_bwd(sv['x1'], p['xa_norm_g'], dh, dx)

    dmixed = matmul(dxb, p['w_mix_out'], 'nt', F32)
    g['w_mix_out'] = matmul(sv['mixed'], dxb, 'tn', BF16)
    proj = sv['proj']
    du_pool, g['w_pool'], g['pool_scale'] = pool_bwd(proj, p['w_pool'], p['pool_scale'], dmixed, b, s)
    do_dn, dz, g['dn_norm_g'] = dn_norm_bwd(sv['o_dn'], proj, p['dn_norm_g'], dmixed, pw + 3 * dn, pw)
    beta_col, gc_col, gc_row, gc_row_ut = sv['gates']
    aq, ak, av = sv['act']
    daq, dak_rec, du, dw, dgcc_rec, dgcr = rec_bwd(aq, ak, gc_col, gc_row, sv['u'], sv['w'], sv['states'],
                                                   do_dn, b, s)
    dak, dav, dbeta, dgcc_ut, dgcr_ut = ut_bwd(ak, av, beta_col, gc_col, gc_row_ut, sv['inv'], du, dw, dak_rec, b, s)
    dparts, dtaps = zip(*[conv_silu_bwd(proj, p['dn_conv_w'], dact, part, pw, dn, b, s)
                          for part, dact in enumerate((daq, dak, dav))])
    g['dn_conv_w'] = jnp.concatenate(dtaps, axis=1)
    dlogits, g['dn_a_log'], g['dn_dt_bias'] = sv['gates_pull']((dbeta, dgcc_rec + dgcc_ut, dgcr, dgcr_ut))
    dproj = jnp.concatenate([du_pool, *dparts, dz], axis=1)
    dlogits = dlogits.astype(BF16)
    dh = matmul(dproj, p['w_in_main'], 'nt', F32, res=matmul(dlogits, p['w_in_logits'], 'nt', F32))
    g['w_in_main'] = matmul(sv['h1'], dproj, 'tn', BF16)
    g['w_in_logits'] = matmul(sv['h1'], dlogits, 'tn', BF16)
    dx, dxb, g['mix_norm_g'] = norm_bwd(sv['x0'], p['mix_norm_g'], dh, dx)
    return dx, dxb, dmem_h, g


def _to_full(gathered, axis):
    moved = jnp.moveaxis(gathered, 0, axis)
    shape = list(gathered.shape[1:])
    shape[axis] *= N_DEV
    return moved.reshape(shape)


def _to_parts(full, axis):
    shape = list(full.shape)
    shape[axis:axis + 1] = [N_DEV, shape[axis] // N_DEV]
    return jnp.moveaxis(full.reshape(shape), axis, 0)


def _interleave(gate, up):
    lead, f = gate.shape[:-1], gate.shape[-1]
    pair = jnp.stack([gate.reshape(*lead, f // LANES, LANES), up.reshape(*lead, f // LANES, LANES)], axis=-2)
    return pair.reshape(*lead, 2 * f)


def _deinterleave(both):
    lead, f2 = both.shape[:-1], both.shape[-1]
    pair = both.reshape(*lead, f2 // (2 * LANES), 2, LANES)
    return pair[..., 0, :].reshape(*lead, f2 // 2), pair[..., 1, :].reshape(*lead, f2 // 2)


def _as_pack_rows(a, cols):
    rows = -(-a.shape[0] // (8 * cols)) * 8
    return jnp.pad(a, (0, rows * cols - a.shape[0])).reshape(rows, cols)


def kernel(x, mem, mix_norm_g, w_in, w_pool, pool_scale, dn_conv_w, dn_a_log, dn_dt_bias, dn_norm_g, w_mix_out, xa_norm_g, mem_norm_g, w_xq, w_xkv, w_xo, ffn_norm_g, w_gate, w_up, ffn_conv_w, ffn_conv_b, w_down, final_norm_g, loss_target, m_mix_norm_g, m_w_in, m_w_pool, m_pool_scale, m_dn_conv_w, m_dn_a_log, m_dn_dt_bias, m_dn_norm_g, m_w_mix_out, m_xa_norm_g, m_mem_norm_g, m_w_xq, m_w_xkv, m_w_xo, m_ffn_norm_g, m_w_gate, m_w_up, m_ffn_conv_w, m_ffn_conv_b, m_w_down, m_final_norm_g, v_mix_norm_g, v_w_in, v_w_pool, v_pool_scale, v_dn_conv_w, v_dn_a_log, v_dn_dt_bias, v_dn_norm_g, v_w_mix_out, v_xa_norm_g, v_mem_norm_g, v_w_xq, v_w_xkv, v_w_xo, v_ffn_norm_g, v_w_gate, v_w_up, v_ffn_conv_w, v_ffn_conv_b, v_w_down, v_final_norm_g):
    given = dict(locals())
    w = {n: given[n] for n in WEIGHTS}
    mom = {n: given['m_' + n] for n in WEIGHTS}
    var = {n: given['v_' + n] for n in WEIGHTS}
    b, s, d = x.shape
    depth = w_in.shape[0]
    main = 5 * (d // 2)
    n_logits = w_in.shape[-1] * N_DEV - main
    core = jnp.reshape(lax.axis_index("c"), (1,)).astype(jnp.int32)

    def layer_weights(l):
        names = MATRICES + TAPS
        gathered = all_gather([w[n][l] if n in TAPS else w[n][l].astype(BF16) for n in names], f"layer{l}")
        full = {n: _to_full(a, SHARD_AXIS[n] - 1) for n, a in zip(names, gathered)}
        p = {n: full[n] for n in ('w_pool', 'dn_conv_w', 'w_mix_out', 'w_xq', 'w_xkv', 'w_xo', 'ffn_conv_w',
                                  'w_down')}
        p['w_in_main'] = full['w_in'][:, :main]
        p['w_in_logits'] = jnp.pad(full['w_in'][:, main:], ((0, 0), (0, LANES - n_logits)))
        p['w_gate_up'] = _interleave(full['w_gate'], full['w_up'])
        for n in PER_LAYER_REPLICATED:
            p[n] = w[n][l]
        return p

    layers = [layer_weights(l) for l in range(depth)]

    mem2 = mem.reshape(-1, d)
    mem_hb = norm_fwd(mem2, mem_norm_g)
    xc, saved = x.reshape(b * s, d), []
    for p in layers:
        xc, sv = forward_layer(xc, mem_hb, p, b, s)
        saved.append(sv)
    loss, dx, dxb, g_final = loss_head(xc, loss_target.reshape(b * s, d), final_norm_g)

    def as_layers(a, cols):
        return a.reshape(depth, -1, cols)

    names = MATRICES + TAPS
    results = {n: None for n in names}
    g_layers, dmem_h = [None] * depth, None
    for l in reversed(range(depth)):
        dx, dxb, dm, g = backward_layer(dx, dxb, saved[l], mem_hb, layers[l], b, s)
        g_layers[l] = g
        dmem_h = dm if dmem_h is None else dmem_h + dm
        g_full = {n: g[n] for n in ('w_mix_out', 'w_xq', 'w_xkv', 'w_xo', 'w_down', 'dn_conv_w', 'ffn_conv_w')}
        g_full['w_pool'] = g['w_pool'].astype(BF16)
        g_full['w_in'] = jnp.concatenate([g['w_in_main'], g['w_in_logits'][:, :n_logits]], axis=1)
        g_full['w_gate'], g_full['w_up'] = _deinterleave(g['w_gate_up'])
        parts = []
        for n in names:
            by_dev = _to_parts(g_full[n], SHARD_AXIS[n] - 1)
            parts.append(by_dev.reshape(N_DEV, -1, by_dev.shape[-1]))
        from_sibling = sibling_swap(parts, f"layer{l}")
        got = chip_exchange([chip_sum(a, f, core) for a, f in zip(parts, from_sibling)], f"layer{l}")
        for n, a in zip(names, got):
            cols = w[n].shape[-1]
            results[n] = adamw(a, as_layers(w[n], cols), as_layers(mom[n], cols), as_layers(var[n], cols),
                               layer=l, prev=results[n])
    _, _, g_mem = norm_bwd(mem2, mem_norm_g, dmem_h, None)
    grad_x = dx.reshape(b, s, d)

    grad, delta, new_m, new_v = {}, {}, {}, {}
    for n in names:
        grad[n], delta[n], new_m[n], new_v[n] = [o.reshape(w[n].shape) for o in results[n]]

    g_small = {n: jnp.stack([g[n] for g in g_layers]) for n in PER_LAYER_REPLICATED}
    g_small['mem_norm_g'], g_small['final_norm_g'] = g_mem, g_final
    sizes = [w[n].size for n in REPLICATED]

    def pack(parts, first):
        flat = jnp.concatenate([jnp.reshape(first, (1,))] + [parts[n].reshape(-1) for n in REPLICATED])
        return _as_pack_rows(flat, LANES)

    zero = jnp.zeros((), F32)
    everyone, = all_gather([pack(g_small, loss)], "replicated")
    outs = adamw(everyone, pack(w, zero)[None], pack(mom, zero)[None], pack(var, zero)[None])
    flat_outs = [o.reshape(-1) for o in outs]
    loss_total = flat_outs[0][0]
    offset = 1
    for n, size in zip(REPLICATED, sizes):
        grad[n], delta[n], new_m[n], new_v[n] = [o[offset:offset + size].reshape(w[n].shape) for o in flat_outs]
        offset += size

    return (loss_total, grad_x, *[grad[n] for n in WEIGHTS], *[delta[n] for n in WEIGHTS],
            *[new_m[n] for n in WEIGHTS], *[new_v[n] for n in WEIGHTS])
```

```python
import functools

import jax
import jax.numpy as jnp
from jax import lax
from jax.experimental import pallas as pl
from jax.experimental.pallas import tpu as pltpu

F32 = jnp.float32
BF16 = jnp.bfloat16
MESH = pl.DeviceIdType.MESH

N_DEV = 8
EPS = 1e-6
POOL_WINDOWS = (2, 4, 8, 16)
DN_HEAD_DIM = 128
DN_CHUNK = 64
DN_TAPS = 4
XA_HEADS = 4
ADAM_LR = 0.001
ADAM_B1 = 0.9
ADAM_B2 = 0.999
ADAM_EPS = 1e-08
ADAM_WD = 0.01
ADAM_STEP = 10

LANES = 128
VMEM_LIMIT = 48 * 1024 * 1024
UT_ROWS = 256

WEIGHTS = ['mix_norm_g', 'w_in', 'w_pool', 'pool_scale', 'dn_conv_w', 'dn_a_log', 'dn_dt_bias', 'dn_norm_g',
           'w_mix_out', 'xa_norm_g', 'mem_norm_g', 'w_xq', 'w_xkv', 'w_xo', 'ffn_norm_g', 'w_gate', 'w_up',
           'ffn_conv_w', 'ffn_conv_b', 'w_down', 'final_norm_g']
SHARD_AXIS = {'w_in': 2, 'w_pool': 2, 'dn_conv_w': 2, 'w_mix_out': 1, 'w_xq': 1, 'w_xkv': 2, 'w_xo': 1,
              'w_gate': 2, 'w_up': 2, 'ffn_conv_w': 2, 'w_down': 1}
MATRICES = ('w_in', 'w_pool', 'w_mix_out', 'w_xq', 'w_xkv', 'w_xo', 'w_gate', 'w_up', 'w_down')
TAPS = ('dn_conv_w', 'ffn_conv_w')
REPLICATED = [n for n in WEIGHTS if n not in SHARD_AXIS]
PER_LAYER_REPLICATED = ('mix_norm_g', 'pool_scale', 'dn_a_log', 'dn_dt_bias', 'dn_norm_g', 'xa_norm_g',
                        'ffn_norm_g', 'ffn_conv_b')


def _params(*semantics):
    return pltpu.CompilerParams(dimension_semantics=semantics, vmem_limit_bytes=VMEM_LIMIT)


def _tile(dim, pref, unit=LANES):
    if dim <= pref:
        return dim
    t = (pref // unit) * unit
    while t >= unit:
        if dim % t == 0:
            return t
        t -= unit
    return dim


def matmul(a, b, mode, out_dtype, res=None):
    assert a.dtype == BF16 and b.dtype == BF16, (a.dtype, b.dtype)
    if mode == 'nn':
        (m, c), (c2, n) = a.shape, b.shape
    elif mode == 'nt':
        (m, c), (n, c2) = a.shape, b.shape
    else:
        (c, m), (c2, n) = a.shape, b.shape
    assert c == c2, (mode, a.shape, b.shape)
    tm, tn, tc = _tile(m, 1024), _tile(n, 512), _tile(c, 2048)
    nc = c // tc
    if mode == 'nn':
        a_spec = pl.BlockSpec((tm, tc), lambda i, j, k: (i, k))
        b_spec = pl.BlockSpec((tc, tn), lambda i, j, k: (k, j))
        dims = (((1,), (0,)), ((), ()))
    elif mode == 'nt':
        a_spec = pl.BlockSpec((tm, tc), lambda i, j, k: (i, k))
        b_spec = pl.BlockSpec((tn, tc), lambda i, j, k: (j, k))
        dims = (((1,), (1,)), ((), ()))
    else:
        a_spec = pl.BlockSpec((tc, tm), lambda i, j, k: (k, i))
        b_spec = pl.BlockSpec((tc, tn), lambda i, j, k: (k, j))
        dims = (((0,), (0,)), ((), ()))
    out_spec = pl.BlockSpec((tm, tn), lambda i, j, k: (i, j))
    has_res = res is not None

    def body(a_ref, b_ref, *rest):
        rest = list(rest)
        r_ref = rest.pop(0) if has_res else None
        o_ref = rest.pop(0)
        prod = lax.dot_general(a_ref[...], b_ref[...], dims, preferred_element_type=F32)

        def finish(total):
            if has_res:
                total = total + r_ref[...]
            o_ref[...] = total.astype(o_ref.dtype)

        if nc == 1:
            finish(prod)
            return
        acc_ref, = rest
        k = pl.program_id(2)

        @pl.when(k == 0)
        def _():
            acc_ref[...] = prod

        @pl.when(k > 0)
        def _():
            acc_ref[...] += prod

        @pl.when(k == nc - 1)
        def _():
            finish(acc_ref[...])

    return pl.pallas_call(
        body,
        name=f"mm_{mode}_{m}x{c}x{n}" + ("_res" if has_res else ""),
        grid=(m // tm, n // tn, nc),
        in_specs=[a_spec, b_spec] + ([out_spec] if has_res else []),
        out_specs=out_spec,
        out_shape=jax.ShapeDtypeStruct((m, n), out_dtype),
        scratch_shapes=[] if nc == 1 else [pltpu.VMEM((tm, tn), F32)],
        compiler_params=_params("parallel", "parallel", "arbitrary"),
    )(a, b, *([res] if has_res else []))


def _position():
    return lax.axis_index("x"), lax.axis_index("y"), lax.axis_index("c")


def _flip(v, bit):
    return 1 - v if bit else v


def _dev_index(px, py, pc):
    return 4 * px + 2 * py + pc


def _hbm_call(body, name, arrays, out_shapes, n_sems, n_local):
    hbm = pl.BlockSpec(memory_space=pl.ANY)
    return pl.pallas_call(
        body, name=name, out_shape=out_shapes, in_specs=[hbm] * len(arrays), out_specs=[hbm] * len(out_shapes),
        scratch_shapes=[pltpu.SemaphoreType.DMA((n_sems,)), pltpu.SemaphoreType.DMA((n_sems,)),
                        pltpu.SemaphoreType.DMA((n_local,))],
    )(*arrays)


def all_gather(shards, tag):
    n = len(shards)

    def body(*refs):
        x_refs, out_refs = refs[:n], refs[n:2 * n]
        send_sems, recv_sems, local_sems = refs[2 * n:]
        x, y, c = _position()
        me, sibling = (x, y, c), (x, y, 1 - c)
        chips = [(1 - x, y), (x, 1 - y), (1 - x, 1 - y)]

        def copy(t, k, block, to, own=False):
            slot = out_refs[t].at[_dev_index(*block)]
            return pltpu.make_async_remote_copy(
                src_ref=x_refs[t] if own else slot, dst_ref=slot, send_sem=send_sems.at[7 * t + k],
                recv_sem=recv_sems.at[7 * t + k], device_id=to, device_id_type=MESH)

        local = [pltpu.make_async_copy(x_refs[t], out_refs[t].at[_dev_index(*me)], local_sems.at[t])
                 for t in range(n)]
        sent = []
        for t in range(n):
            local[t].start()
            sent.append(copy(t, 0, me, sibling, own=True))
            sent += [copy(t, 1 + j, me, (*chip, c), own=True) for j, chip in enumerate(chips)]
        for cp in sent:
            cp.start()
        for j, chip in enumerate(chips):
            for t in range(n):
                copy(t, 1 + j, (*chip, c), me).wait_recv()
                passed = copy(t, 4 + j, (*chip, c), sibling)
                passed.start()
                sent.append(passed)
        for t in range(n):
            copy(t, 0, sibling, me).wait_recv()
            for j, chip in enumerate(chips):
                copy(t, 4 + j, (*chip, 1 - c), me).wait_recv()
        for cp in sent:
            cp.wait_send()
        for cp in local:
            cp.wait()

    return _hbm_call(body, "all_gather_" + tag, shards,
                     [jax.ShapeDtypeStruct((N_DEV,) + a.shape, a.dtype) for a in shards], 7 * n, n)


def sibling_swap(parts, tag):
    n = len(parts)

    def body(*refs):
        p_refs, out_refs = refs[:n], refs[n:2 * n]
        send_sems, recv_sems, _ = refs[2 * n:]
        x, y, c = _position()
        copies = [pltpu.make_async_remote_copy(
            src_ref=p_refs[t].at[2 * k + (1 - c)], dst_ref=out_refs[t].at[k], send_sem=send_sems.at[4 * t + k],
            recv_sem=recv_sems.at[4 * t + k], device_id=(x, y, 1 - c), device_id_type=MESH)
            for t in range(n) for k in range(N_DEV // 2)]
        for cp in copies:
            cp.start()
        for cp in copies:
            cp.wait()

    return _hbm_call(body, "sibling_swap_" + tag, parts,
                     [jax.ShapeDtypeStruct((N_DEV // 2,) + a.shape[1:], a.dtype) for a in parts], 4 * n, 1)


def chip_sum(parts, from_sibling, core):
    _, r, c = parts.shape
    unit = 16 if parts.dtype == BF16 else 8
    tr = _tile(r, max(unit, (512 * 1024 // c) // unit * unit), unit)

    def body(core_ref, a_ref, b_ref, o_ref):
        o_ref[...] = (a_ref[...].astype(F32) + b_ref[...].astype(F32)).astype(o_ref.dtype)

    blk = pl.BlockSpec((None, tr, c), lambda k, i, core_ref: (k, i, 0))
    return pl.pallas_call(
        body, name=f"chip_sum_{r}x{c}_{parts.dtype.name}",
        grid_spec=pltpu.PrefetchScalarGridSpec(
            num_scalar_prefetch=1, grid=(N_DEV // 2, r // tr),
            in_specs=[pl.BlockSpec((None, tr, c), lambda k, i, core_ref: (2 * k + core_ref[0], i, 0)), blk],
            out_specs=blk),
        out_shape=jax.ShapeDtypeStruct(from_sibling.shape, from_sibling.dtype),
        compiler_params=_params("parallel", "parallel"),
    )(core, parts, from_sibling)


def chip_exchange(parts, tag):
    n = len(parts)

    def body(*refs):
        p_refs, out_refs = refs[:n], refs[n:2 * n]
        send_sems, recv_sems, local_sems = refs[2 * n:]
        x, y, c = _position()
        my_chip = 2 * x + y
        local = [pltpu.make_async_copy(p_refs[t].at[my_chip], out_refs[t].at[my_chip], local_sems.at[t])
                 for t in range(n)]
        copies = []
        for t in range(n):
            local[t].start()
            for k in (1, 2, 3):
                px, py = _flip(x, k & 2), _flip(y, k & 1)
                copies.append(pltpu.make_async_remote_copy(
                    src_ref=p_refs[t].at[2 * px + py], dst_ref=out_refs[t].at[my_chip],
                    send_sem=send_sems.at[3 * t + k - 1], recv_sem=recv_sems.at[3 * t + k - 1],
                    device_id=(px, py, c), device_id_type=MESH))
        for cp in copies:
            cp.start()
        for cp in copies:
            cp.wait_recv()
        for cp in copies:
            cp.wait_send()
        for cp in local:
            cp.wait()

    return _hbm_call(body, "chip_exchange_" + tag, parts,
                     [jax.ShapeDtypeStruct(a.shape, a.dtype) for a in parts], 3 * n, n)


def adamw(parts, w, m, v, layer=0, prev=None):
    n_parts, r, c = parts.shape
    depth = w.shape[0]
    unit = 16 if parts.dtype == BF16 else 8
    tr = _tile(r, max(unit, (256 * 1024 // c) // unit * unit), unit)

    def body(p_ref, w_ref, m_ref, v_ref, *rest):
        g_ref, d_ref, nm_ref, nv_ref = rest[-4:]
        g = p_ref[0].astype(F32)
        for j in range(1, n_parts):
            g = g + p_ref[j].astype(F32)
        nm = ADAM_B1 * m_ref[...] + (1.0 - ADAM_B1) * g
        nv = ADAM_B2 * v_ref[...] + (1.0 - ADAM_B2) * jnp.square(g)
        m_hat = nm / (1.0 - ADAM_B1 ** ADAM_STEP)
        v_hat = nv / (1.0 - ADAM_B2 ** ADAM_STEP)
        g_ref[...] = g
        d_ref[...] = -ADAM_LR * (m_hat / (jnp.sqrt(v_hat) + ADAM_EPS) + ADAM_WD * w_ref[...])
        nm_ref[...] = nm
        nv_ref[...] = nv

    spec = pl.BlockSpec((None, tr, c), lambda i: (layer, i, 0))
    out = jax.ShapeDtypeStruct((depth, r, c), F32)
    carried = [] if prev is None else list(prev)
    return pl.pallas_call(
        body,
        name=f"adamw_{n_parts}x{r}x{c}_{parts.dtype.name}_layer{layer}",
        grid=(r // tr,),
        in_specs=[pl.BlockSpec((n_parts, tr, c), lambda i: (0, i, 0)), spec, spec, spec]
        + [pl.BlockSpec(memory_space=pl.ANY)] * len(carried),
        out_specs=[spec, spec, spec, spec],
        out_shape=[out, out, out, out],
        input_output_aliases={4 + k: k for k in range(len(carried))},
        compiler_params=_params("parallel"),
    )(parts, w, m, v, *carried)


_NN = (((1,), (0,)), ((), ()))
_NT = (((1,), (1,)), ((), ()))
_TN = (((0,), (0,)), ((), ()))


def _dot_bf16(a, b, dims):
    return lax.dot_general(a.astype(BF16), b.astype(BF16), dims, preferred_element_type=F32)


def _dot_split(a, b, dims):
    a_hi, b_hi = a.astype(BF16), b.astype(BF16)
    a_lo = (a - a_hi.astype(F32)).astype(BF16)
    b_lo = (b - b_hi.astype(F32)).astype(BF16)
    dot = functools.partial(lax.dot_general, dimension_numbers=dims, preferred_element_type=F32)
    return dot(a_hi, b_hi) + (dot(a_hi, b_lo) + dot(a_lo, b_hi))


def _matmul_family(dot):
    @jax.custom_vjp
    def nn(a, b):
        return dot(a, b, _NN)

    @jax.custom_vjp
    def nt(a, b):
        return dot(a, b, _NT)

    @jax.custom_vjp
    def tn(a, b):
        return dot(a, b, _TN)

    nn.defvjp(lambda a, b: (nn(a, b), (a, b)), lambda r, g: (nt(g, r[1]), tn(r[0], g)))
    nt.defvjp(lambda a, b: (nt(a, b), (a, b)), lambda r, g: (nn(g, r[1]), tn(g, r[0])))
    tn.defvjp(lambda a, b: (tn(a, b), (a, b)), lambda r, g: (nt(r[1], g), nn(r[0], g)))
    return nn, nt, tn


mm, mm_nt, mm_tn = _matmul_family(_dot_bf16)
mms, mms_nt, mms_tn = _matmul_family(_dot_split)


def _shift_rows(x, k, down):
    n = x.shape[0]
    rows = lax.broadcasted_iota(jnp.int32, x.shape, 0)
    if down:
        return jnp.where(rows >= k, pltpu.roll(x, k, 0), 0.0)
    return jnp.where(rows < n - k, pltpu.roll(x, n - k, 0), 0.0)


@functools.partial(jax.custom_vjp, nondiff_argnums=(1,))
def delay(x, k):
    return _shift_rows(x, k, True) if k else x


delay.defvjp(lambda x, k: (delay(x, k), None), lambda k, _, g: ((_shift_rows(g, k, False) if k else g),))


def _silu(x):
    return x * jax.nn.sigmoid(x)


def _rms(x, g):
    return x * lax.rsqrt(jnp.mean(x * x, axis=-1, keepdims=True) + EPS) * g


def _l2n(t):
    return t * lax.rsqrt(jnp.sum(t * t, axis=-1, keepdims=True) + EPS)


def _causal_conv(x, taps):
    k_taps = len(taps)
    y = delay(x, k_taps - 1) * taps[0]
    for k in range(1, k_taps):
        y = y + delay(x, k_taps - 1 - k) * taps[k]
    return y


def _pool_block(u, w, scale, group):
    sums, acc, width = [], u, 1
    while width < POOL_WINDOWS[-1]:
        acc = acc + delay(acc, width)
        width *= 2
        sums.append(acc)
    picked = sums[-1]
    for i in range(len(POOL_WINDOWS) - 2, -1, -1):
        picked = jnp.where(group == i, sums[i], picked)
    rows = lax.broadcasted_iota(jnp.int32, u.shape, 0)
    count = jnp.minimum(rows + 1, jnp.left_shift(2, group)).astype(F32)
    return mm(picked / count - u, w) * scale


def _unit_lower_inverse(l_mat):
    n = l_mat.shape[0]
    eye = (lax.broadcasted_iota(jnp.int32, (n, n), 0) == lax.broadcasted_iota(jnp.int32, (n, n), 1)).astype(F32)
    m1 = -l_mat
    m2 = mms(m1, m1)
    m4 = mms(m2, m2)
    m8 = mms(m4, m4)
    m16 = mms(m8, m8)
    m32 = mms(m16, m16)
    low = mms(eye + m1, eye + m2)
    mid = mms(eye + m4, eye + m8)
    high = mms(eye + m16, eye + m32)
    return mms(mms(low, mid), high)


@jax.custom_vjp
def _known_inverse(l_mat, inv):
    return inv


_known_inverse.defvjp(lambda l_mat, inv: (inv, inv),
                      lambda inv, g: (-mms_tn(inv, mms_nt(g, inv)), jnp.zeros_like(inv)))


def _ut_chain(k, v, beta, gc_c, gc_r, inv=None):
    r = k.shape[0]
    kn = _l2n(k)
    row = lax.broadcasted_iota(jnp.int32, (r, r), 0)
    col = lax.broadcasted_iota(jnp.int32, (r, r), 1)
    strict = (row // DN_CHUNK == col // DN_CHUNK) & (row > col)
    decay = jnp.exp(jnp.where(strict, gc_c - gc_r, -1e30))
    l_mat = jnp.where(strict, beta * mm_nt(kn, kn) * decay, 0.0)
    inv = _unit_lower_inverse(l_mat) if inv is None else _known_inverse(l_mat, inv)
    return mms(inv, v * beta), mms(inv, kn * (beta * jnp.exp(gc_c))), inv


def _rec_chain(q, k, gc_c, gc_r, u, w, state):
    ch, dh = q.shape
    qn = _l2n(q) * (dh ** -0.5)
    kn = _l2n(k)
    row = lax.broadcasted_iota(jnp.int32, (ch, ch), 0)
    col = lax.broadcasted_iota(jnp.int32, (ch, ch), 1)
    decay = jnp.exp(jnp.where(row >= col, gc_c - gc_r, -1e30))
    attn = mm_nt(qn, kn) * decay
    is_last = lax.broadcasted_iota(jnp.int32, gc_r.shape, 1) == ch - 1
    last = jnp.sum(jnp.where(is_last, gc_r, 0.0), axis=1, keepdims=True)
    v_new = u - mm(w, state)
    out = mm(qn * jnp.exp(gc_c), state) + mm(attn, v_new)
    return out, state * jnp.exp(last) + mm_tn(kn * jnp.exp(last - gc_c), v_new)


def _gated_norm(o, z, g):
    return _rms(o, g) * _silu(z)


def _attn_block(q, k, v):
    s = mm_nt(q, k) * (q.shape[-1] ** -0.5)
    p = jnp.exp(s - lax.stop_gradient(jnp.max(s, axis=-1, keepdims=True)))
    return mm(p / jnp.sum(p, axis=-1, keepdims=True), v)


def _ffn_block(gate_pre, up, taps, bias):
    return _silu(_causal_conv(gate_pre, taps) + bias) * up


def _accumulate(ref, value, first):
    @pl.when(first)
    def _():
        ref[...] = value

    @pl.when(jnp.logical_not(first))
    def _():
        ref[...] += value


def norm_fwd(x, g):
    t, d = x.shape
    tr = _tile(t, 256, 8)

    def body(x_ref, g_ref, h_ref):
        h_ref[...] = _rms(x_ref[...], g_ref[...]).astype(BF16)

    return pl.pallas_call(
        body, name=f"norm_fwd_{t}", grid=(t // tr,),
        in_specs=[pl.BlockSpec((tr, d), lambda i: (i, 0)), pl.BlockSpec((1, d), lambda i: (0, 0))],
        out_specs=pl.BlockSpec((tr, d), lambda i: (i, 0)),
        out_shape=jax.ShapeDtypeStruct((t, d), BF16), compiler_params=_params("parallel"),
    )(x, g.reshape(1, d))


def norm_bwd(x, g, dh, dres):
    t, d = x.shape
    tr = _tile(t, 256, 8)
    has_res = dres is not None

    def body(x_ref, g_ref, dh_ref, *rest):
        rest = list(rest)
        r_ref = rest.pop(0) if has_res else None
        dx_ref, dxb_ref, dg_ref = rest
        _, pull = jax.vjp(_rms, x_ref[...], g_ref[...])
        dx, dg = pull(dh_ref[...])
        if has_res:
            dx = dx + r_ref[...]
        dx_ref[...] = dx
        dxb_ref[...] = dx.astype(BF16)
        _accumulate(dg_ref, dg, pl.program_id(0) == 0)

    row = pl.BlockSpec((tr, d), lambda i: (i, 0))
    vec = pl.BlockSpec((1, d), lambda i: (0, 0))
    dx, dxb, dg = pl.pallas_call(
        body, name=f"norm_bwd_{t}" + ("_res" if has_res else ""), grid=(t // tr,),
        in_specs=[row, vec, row] + ([row] if has_res else []),
        out_specs=[row, row, vec],
        out_shape=[jax.ShapeDtypeStruct((t, d), F32), jax.ShapeDtypeStruct((t, d), BF16),
                   jax.ShapeDtypeStruct((1, d), F32)],
        compiler_params=_params("arbitrary"),
    )(x, g.reshape(1, d), dh, *([dres] if has_res else []))
    return dx, dxb, dg.reshape(d)


def loss_head(x, target, g):
    t, d = x.shape
    tr = _tile(t, 256, 8)

    def body(x_ref, t_ref, g_ref, l_ref, dx_ref, dxb_ref, dg_ref):
        tgt = t_ref[...]

        def block_loss(xv, gv):
            return 0.5 * jnp.sum(jnp.mean(jnp.square(_rms(xv, gv) - tgt), axis=-1))

        val, pull = jax.vjp(block_loss, x_ref[...], g_ref[...])
        dx, dg = pull(jnp.ones((), F32))
        dx_ref[...] = dx
        dxb_ref[...] = dx.astype(BF16)
        first = pl.program_id(0) == 0
        _accumulate(dg_ref, dg, first)
        _accumulate(l_ref, jnp.full((1, LANES), val, F32), first)

    row = pl.BlockSpec((tr, d), lambda i: (i, 0))
    vec = pl.BlockSpec((1, d), lambda i: (0, 0))
    loss, dx, dxb, dg = pl.pallas_call(
        body, name="loss_head", grid=(t // tr,),
        in_specs=[row, row, vec],
        out_specs=[pl.BlockSpec((1, LANES), lambda i: (0, 0)), row, row, vec],
        out_shape=[jax.ShapeDtypeStruct((1, LANES), F32), jax.ShapeDtypeStruct((t, d), F32),
                   jax.ShapeDtypeStruct((t, d), BF16), jax.ShapeDtypeStruct((1, d), F32)],
        compiler_params=_params("arbitrary"),
    )(x, target, g.reshape(1, d))
    return loss[0, 0], dx, dxb, dg.reshape(d)


def pool_fwd(proj, w_pool, scale, b, s):
    n_g, grp = w_pool.shape[0], w_pool.shape[-1]

    def body(u_ref, w_ref, s_ref, y_ref):
        y_ref[...] = _pool_block(u_ref[...], w_ref[...], s_ref[...], pl.program_id(1)).astype(BF16)

    blk = pl.BlockSpec((s, grp), lambda i, j: (i, j))
    return pl.pallas_call(
        body, name="pool_fwd", grid=(b, n_g),
        in_specs=[blk, pl.BlockSpec((None, grp, grp), lambda i, j: (j, 0, 0)),
                  pl.BlockSpec((1, grp), lambda i, j: (0, j))],
        out_specs=blk,
        out_shape=jax.ShapeDtypeStruct((b * s, n_g * grp), BF16), compiler_params=_params("parallel", "parallel"),
    )(proj, w_pool, scale.reshape(1, -1))


def pool_bwd(proj, w_pool, scale, dmixed, b, s):
    n_g, grp = w_pool.shape[0], w_pool.shape[-1]

    def body(u_ref, w_ref, s_ref, dy_ref, du_ref, dw_ref, ds_ref):
        group = pl.program_id(0)
        _, pull = jax.vjp(lambda u, w, sc: _pool_block(u, w, sc, group), u_ref[...], w_ref[...].astype(F32),
                          s_ref[...])
        du, dw, ds = pull(dy_ref[...])
        du_ref[...] = du.astype(BF16)
        first = pl.program_id(1) == 0
        _accumulate(dw_ref, dw, first)
        _accumulate(ds_ref, ds, first)

    blk = pl.BlockSpec((s, grp), lambda j, i: (i, j))
    w_spec = pl.BlockSpec((None, grp, grp), lambda j, i: (j, 0, 0))
    s_spec = pl.BlockSpec((1, grp), lambda j, i: (0, j))
    du, dw, ds = pl.pallas_call(
        body, name="pool_bwd", grid=(n_g, b),
        in_specs=[blk, w_spec, s_spec, blk],
        out_specs=[blk, w_spec, s_spec],
        out_shape=[jax.ShapeDtypeStruct((b * s, n_g * grp), BF16), jax.ShapeDtypeStruct(w_pool.shape, F32),
                   jax.ShapeDtypeStruct((1, n_g * grp), F32)],
        compiler_params=_params("arbitrary", "arbitrary"),
    )(proj, w_pool, scale.reshape(1, -1), dmixed)
    return du, dw, ds.reshape(-1)


def conv_silu_fwd(proj, conv_w, part, col0, width, b, s):
    x_off, w_off = (col0 + part * width) // LANES, part * width // LANES

    def body(x_ref, w_ref, y_ref):
        taps = [w_ref[k:k + 1, :] for k in range(DN_TAPS)]
        y_ref[...] = _silu(_causal_conv(x_ref[...], taps))

    return pl.pallas_call(
        body, name=f"conv_silu_fwd_{part}", grid=(width // LANES, b),
        in_specs=[pl.BlockSpec((s, LANES), lambda j, i: (i, x_off + j)),
                  pl.BlockSpec((DN_TAPS, LANES), lambda j, i: (0, w_off + j))],
        out_specs=pl.BlockSpec((s, LANES), lambda j, i: (i, j)),
        out_shape=jax.ShapeDtypeStruct((b * s, width), F32), compiler_params=_params("parallel", "parallel"),
    )(proj, conv_w)


def conv_silu_bwd(proj, conv_w, dact, part, col0, width, b, s):
    x_off, w_off = (col0 + part * width) // LANES, part * width // LANES

    def body(x_ref, w_ref, dy_ref, dx_ref, dw_ref):
        taps = [w_ref[k:k + 1, :] for k in range(DN_TAPS)]
        _, pull = jax.vjp(lambda x, *tp: _silu(_causal_conv(x, tp)), x_ref[...], *taps)
        dx, *dtaps = pull(dy_ref[...])
        dx_ref[...] = dx.astype(BF16)
        first = pl.program_id(1) == 0
        for k in range(DN_TAPS):
            _accumulate(dw_ref.at[k:k + 1, :], dtaps[k], first)

    out_blk = pl.BlockSpec((s, LANES), lambda j, i: (i, j))
    return pl.pallas_call(
        body, name=f"conv_silu_bwd_{part}", grid=(width // LANES, b),
        in_specs=[pl.BlockSpec((s, LANES), lambda j, i: (i, x_off + j)),
                  pl.BlockSpec((DN_TAPS, LANES), lambda j, i: (0, w_off + j)), out_blk],
        out_specs=[out_blk, pl.BlockSpec((DN_TAPS, LANES), lambda j, i: (0, j))],
        out_shape=[jax.ShapeDtypeStruct((b * s, width), BF16), jax.ShapeDtypeStruct((DN_TAPS, width), F32)],
        compiler_params=_params("arbitrary", "arbitrary"),
    )(proj, conv_w, dact)


def _ut_specs(b, s, heads, width):
    r = min(UT_ROWS, s)
    ns = s // r
    tok = pl.BlockSpec((r, width), lambda i, n: (i * ns + n, 0))
    col = pl.BlockSpec((None, heads, r, 1), lambda i, n: (i, 0, n, 0))
    row = pl.BlockSpec((None, heads, None, 1, r), lambda i, n: (i, 0, n, 0, 0))
    return r, ns, tok, col, row


def ut_fwd(act_k, act_v, beta_col, gc_col, gc_row, b, s):
    width = act_k.shape[1]
    heads = width // DN_HEAD_DIM
    r, ns, tok, col, row = _ut_specs(b, s, heads, width)
    inv_spec = pl.BlockSpec((r, heads * r), lambda i, n: (i * ns + n, 0))

    def body(k_ref, v_ref, beta_ref, gcc_ref, gcr_ref, u_ref, w_ref, inv_ref):
        for h in range(heads):
            sl = slice(h * DN_HEAD_DIM, (h + 1) * DN_HEAD_DIM)
            u, w, inv = _ut_chain(k_ref[:, sl], v_ref[:, sl], beta_ref[h], gcc_ref[h], gcr_ref[h])
            u_ref[:, sl] = u
            w_ref[:, sl] = w
            inv_ref[:, h * r:(h + 1) * r] = inv

    out = jax.ShapeDtypeStruct((b * s, width), F32)
    return pl.pallas_call(
        body, name="ut_fwd", grid=(b, ns), in_specs=[tok, tok, col, col, row], out_specs=[tok, tok, inv_spec],
        out_shape=[out, out, jax.ShapeDtypeStruct((b * s, heads * r), F32)],
        compiler_params=_params("parallel", "parallel"),
    )(act_k, act_v, beta_col, gc_col, gc_row)


def ut_bwd(act_k, act_v, beta_col, gc_col, gc_row, inv, du, dw, dk_more, b, s):
    width = act_k.shape[1]
    heads = width // DN_HEAD_DIM
    r, ns, tok, col, row = _ut_specs(b, s, heads, width)
    inv_spec = pl.BlockSpec((r, heads * r), lambda i, n: (i * ns + n, 0))

    def body(k_ref, v_ref, beta_ref, gcc_ref, gcr_ref, inv_ref, du_ref, dw_ref, dkm_ref,
             dk_ref, dv_ref, dbeta_ref, dgcc_ref, dgcr_ref):
        for h in range(heads):
            sl = slice(h * DN_HEAD_DIM, (h + 1) * DN_HEAD_DIM)
            inv = inv_ref[:, h * r:(h + 1) * r]
            _, pull = jax.vjp(lambda *a: _ut_chain(*a, inv=inv)[:2], k_ref[:, sl], v_ref[:, sl], beta_ref[h],
                              gcc_ref[h], gcr_ref[h])
            dk, dv, dbeta, dgcc, dgcr = pull((du_ref[:, sl], dw_ref[:, sl]))
            dk_ref[:, sl] = dk + dkm_ref[:, sl]
            dv_ref[:, sl] = dv
            dbeta_ref[h] = dbeta
            dgcc_ref[h] = dgcc
            dgcr_ref[h] = dgcr

    out = jax.ShapeDtypeStruct((b * s, width), F32)
    return pl.pallas_call(
        body, name="ut_bwd", grid=(b, ns), in_specs=[tok, tok, col, col, row, inv_spec, tok, tok, tok],
        out_specs=[tok, tok, col, col, row],
        out_shape=[out, out, jax.ShapeDtypeStruct(beta_col.shape, F32), jax.ShapeDtypeStruct(gc_col.shape, F32),
                   jax.ShapeDtypeStruct(gc_row.shape, F32)],
        compiler_params=_params("parallel", "parallel"),
    )(act_k, act_v, beta_col, gc_col, gc_row, inv, du, dw, dk_more)


def _rec_specs(b, n, heads, width, chunk_of):
    ch = DN_CHUNK
    tok = pl.BlockSpec((b, None, ch, width), lambda i: (0, chunk_of(i), 0, 0))
    col = pl.BlockSpec((b, heads, None, ch, 1), lambda i: (0, 0, chunk_of(i), 0, 0))
    row = pl.BlockSpec((b, heads, None, 1, ch), lambda i: (0, 0, chunk_of(i), 0, 0))
    st = pl.BlockSpec((None, b, heads, DN_HEAD_DIM, DN_HEAD_DIM), lambda i: (chunk_of(i), 0, 0, 0, 0))
    return tok, col, row, st


def rec_fwd(act_q, act_k, gc_col, gc_row, u, w, b, s):
    width = act_q.shape[1]
    heads, n = width // DN_HEAD_DIM, s // DN_CHUNK
    tok, col, row, st = _rec_specs(b, n, heads, width, lambda i: i)
    shape4 = (b, n, DN_CHUNK, width)

    def body(q_ref, k_ref, gcc_ref, gcr_ref, u_ref, w_ref, o_ref, st_ref, state):
        @pl.when(pl.program_id(0) == 0)
        def _():
            state[...] = jnp.zeros_like(state)

        for i in range(b):
            for h in range(heads):
                sl = slice(h * DN_HEAD_DIM, (h + 1) * DN_HEAD_DIM)
                s_in = state[i, h]
                st_ref[i, h] = s_in
                o, s_out = _rec_chain(q_ref[i, :, sl], k_ref[i, :, sl], gcc_ref[i, h], gcr_ref[i, h],
                                      u_ref[i, :, sl], w_ref[i, :, sl], s_in)
                o_ref[i, :, sl] = o
                state[i, h] = s_out

    o, states = pl.pallas_call(
        body, name="rec_fwd", grid=(n,), in_specs=[tok, tok, col, row, tok, tok], out_specs=[tok, st],
        out_shape=[jax.ShapeDtypeStruct(shape4, F32),
                   jax.ShapeDtypeStruct((n, b, heads, DN_HEAD_DIM, DN_HEAD_DIM), F32)],
        scratch_shapes=[pltpu.VMEM((b, heads, DN_HEAD_DIM, DN_HEAD_DIM), F32)],
        compiler_params=_params("arbitrary"),
    )(act_q.reshape(shape4), act_k.reshape(shape4), gc_col.reshape(b, heads, n, DN_CHUNK, 1), gc_row,
      u.reshape(shape4), w.reshape(shape4))
    return o.reshape(b * s, width), states


def rec_bwd(act_q, act_k, gc_col, gc_row, u, w, states, do, b, s):
    width = act_q.shape[1]
    heads, n = width // DN_HEAD_DIM, s // DN_CHUNK
    tok, col, row, st = _rec_specs(b, n, heads, width, lambda i: n - 1 - i)
    shape4 = (b, n, DN_CHUNK, width)

    def body(q_ref, k_ref, gcc_ref, gcr_ref, u_ref, w_ref, st_ref, do_ref,
             dq_ref, dk_ref, du_ref, dw_ref, dgcc_ref, dgcr_ref, dstate):
        @pl.when(pl.program_id(0) == 0)
        def _():
            dstate[...] = jnp.zeros_like(dstate)

        for i in range(b):
            for h in range(heads):
                sl = slice(h * DN_HEAD_DIM, (h + 1) * DN_HEAD_DIM)
                _, pull = jax.vjp(_rec_chain, q_ref[i, :, sl], k_ref[i, :, sl], gcc_ref[i, h], gcr_ref[i, h],
                                  u_ref[i, :, sl], w_ref[i, :, sl], st_ref[i, h])
                dq, dk, dgcc, dgcr, du, dw, ds = pull((do_ref[i, :, sl], dstate[i, h]))
                dq_ref[i, :, sl] = dq
                dk_ref[i, :, sl] = dk
                du_ref[i, :, sl] = du
                dw_ref[i, :, sl] = dw
                dgcc_ref[i, h] = dgcc
                dgcr_ref[i, h] = dgcr
                dstate[i, h] = ds

    tok_out = jax.ShapeDtypeStruct(shape4, F32)
    dq, dk, du, dw, dgcc, dgcr = pl.pallas_call(
        body, name="rec_bwd", grid=(n,), in_specs=[tok, tok, col, row, tok, tok, st, tok],
        out_specs=[tok, tok, tok, tok, col, row],
        out_shape=[tok_out, tok_out, tok_out, tok_out,
                   jax.ShapeDtypeStruct((b, heads, n, DN_CHUNK, 1), F32), jax.ShapeDtypeStruct(gc_row.shape, F32)],
        scratch_shapes=[pltpu.VMEM((b, heads, DN_HEAD_DIM, DN_HEAD_DIM), F32)],
        compiler_params=_params("arbitrary"),
    )(act_q.reshape(shape4), act_k.reshape(shape4), gc_col.reshape(b, heads, n, DN_CHUNK, 1), gc_row,
      u.reshape(shape4), w.reshape(shape4), states, do.reshape(shape4))
    flat = lambda a: a.reshape(b * s, width)
    return flat(dq), flat(dk), flat(du), flat(dw), dgcc.reshape(b, heads, s, 1), dgcr


def dn_norm_fwd(o, proj, g, z_col0):
    t, width = o.shape
    heads = width // DN_HEAD_DIM
    tr = _tile(t, 512, 8)
    z_off = z_col0 // LANES

    def body(o_ref, z_ref, g_ref, y_ref):
        y_ref[...] = _gated_norm(o_ref[...], z_ref[...], g_ref[...]).astype(BF16)

    blk = pl.BlockSpec((tr, DN_HEAD_DIM), lambda i, h: (i, h))
    return pl.pallas_call(
        body, name="dn_norm_fwd", grid=(t // tr, heads),
        in_specs=[blk, pl.BlockSpec((tr, DN_HEAD_DIM), lambda i, h: (i, z_off + h)),
                  pl.BlockSpec((1, DN_HEAD_DIM), lambda i, h: (0, 0))],
        out_specs=blk, out_shape=jax.ShapeDtypeStruct((t, width), BF16),
        compiler_params=_params("parallel", "parallel"),
    )(o, proj, g.reshape(1, -1))


def dn_norm_bwd(o, proj, g, dmixed, z_col0, dy_col0):
    t, width = o.shape
    heads = width // DN_HEAD_DIM
    tr = _tile(t, 512, 8)
    z_off, dy_off = z_col0 // LANES, dy_col0 // LANES

    def body(o_ref, z_ref, g_ref, dy_ref, do_ref, dz_ref, dg_ref):
        _, pull = jax.vjp(_gated_norm, o_ref[...], z_ref[...], g_ref[...])
        do, dz, dg = pull(dy_ref[...])
        do_ref[...] = do
        dz_ref[...] = dz.astype(BF16)
        _accumulate(dg_ref, dg, (pl.program_id(0) == 0) & (pl.program_id(1) == 0))

    blk = pl.BlockSpec((tr, DN_HEAD_DIM), lambda i, h: (i, h))
    vec = pl.BlockSpec((1, DN_HEAD_DIM), lambda i, h: (0, 0))
    do, dz, dg = pl.pallas_call(
        body, name="dn_norm_bwd", grid=(t // tr, heads),
        in_specs=[blk, pl.BlockSpec((tr, DN_HEAD_DIM), lambda i, h: (i, z_off + h)), vec,
                  pl.BlockSpec((tr, DN_HEAD_DIM), lambda i, h: (i, dy_off + h))],
        out_specs=[blk, blk, vec],
        out_shape=[jax.ShapeDtypeStruct((t, width), F32), jax.ShapeDtypeStruct((t, width), BF16),
                   jax.ShapeDtypeStruct((1, DN_HEAD_DIM), F32)],
        compiler_params=_params("arbitrary", "arbitrary"),
    )(o, proj, g.reshape(1, -1), dmixed)
    return do, dz, dg.reshape(-1)


def _attn_specs(b, s, mem_len, d):
    hd = d // XA_HEADS
    tq = _tile(s, 512, 8)
    nq = s // tq
    q_spec = pl.BlockSpec((tq, hd), lambda i, h, j: (i * nq + j, h))
    k_spec = pl.BlockSpec((mem_len, hd), lambda i, h, j: (i, h))
    v_spec = pl.BlockSpec((mem_len, hd), lambda i, h, j: (i, XA_HEADS + h))
    return nq, q_spec, k_spec, v_spec


def attn_fwd(q, kv, b, s):
    d = q.shape[1]
    nq, q_spec, k_spec, v_spec = _attn_specs(b, s, kv.shape[0] // b, d)

    def body(q_ref, k_ref, v_ref, o_ref):
        o_ref[...] = _attn_block(q_ref[...], k_ref[...], v_ref[...]).astype(BF16)

    return pl.pallas_call(
        body, name="attn_fwd", grid=(b, XA_HEADS, nq), in_specs=[q_spec, k_spec, v_spec], out_specs=q_spec,
        out_shape=jax.ShapeDtypeStruct(q.shape, BF16), compiler_params=_params("parallel", "parallel", "parallel"),
    )(q, kv, kv)


def attn_bwd(q, kv, do, b, s):
    d = q.shape[1]
    rows = kv.shape[0]
    nq, q_spec, k_spec, v_spec = _attn_specs(b, s, rows // b, d)

    def body(q_ref, k_ref, v_ref, do_ref, dq_ref, dk_ref, dv_ref):
        _, pull = jax.vjp(_attn_block, q_ref[...], k_ref[...], v_ref[...])
        dq, dk, dv = pull(do_ref[...])
        dq_ref[...] = dq.astype(BF16)
        first = pl.program_id(2) == 0
        _accumulate(dk_ref, dk, first)
        _accumulate(dv_ref, dv, first)

    kv_out = jax.ShapeDtypeStruct((rows, d), F32)
    return pl.pallas_call(
        body, name="attn_bwd", grid=(b, XA_HEADS, nq), in_specs=[q_spec, k_spec, v_spec, q_spec],
        out_specs=[q_spec, k_spec, k_spec], out_shape=[jax.ShapeDtypeStruct(q.shape, BF16), kv_out, kv_out],
        compiler_params=_params("parallel", "parallel", "arbitrary"),
    )(q, kv, kv, do)


def ffn_fwd(gate_pre, up, conv_w, conv_b, b, s):
    taps_n, f = conv_w.shape

    def body(g_ref, u_ref, w_ref, b_ref, y_ref):
        taps = [w_ref[k:k + 1, :] for k in range(taps_n)]
        y_ref[...] = _ffn_block(g_ref[...], u_ref[...], taps, b_ref[...]).astype(BF16)

    blk = pl.BlockSpec((s, LANES), lambda j, i: (i, j))
    return pl.pallas_call(
        body, name="ffn_fwd", grid=(f // LANES, b),
        in_specs=[blk, blk, pl.BlockSpec((taps_n, LANES), lambda j, i: (0, j)),
                  pl.BlockSpec((1, LANES), lambda j, i: (0, j))],
        out_specs=blk,
        out_shape=jax.ShapeDtypeStruct((b * s, f), BF16), compiler_params=_params("parallel", "parallel"),
    )(gate_pre, up, conv_w, conv_b.reshape(1, f))


def ffn_bwd(gate_pre, up, conv_w, conv_b, dact, b, s):
    taps_n, f = conv_w.shape

    def body(g_ref, u_ref, w_ref, b_ref, dy_ref, dg_ref, du_ref, dw_ref, db_ref):
        taps = [w_ref[k:k + 1, :] for k in range(taps_n)]
        _, pull = jax.vjp(lambda gt, up_, bias, *tp: _ffn_block(gt, up_, tp, bias), g_ref[...], u_ref[...],
                          b_ref[...], *taps)
        dgate, dup, dbias, *dtaps = pull(dy_ref[...])
        dg_ref[...] = dgate.astype(BF16)
        du_ref[...] = dup.astype(BF16)
        first = pl.program_id(1) == 0
        _accumulate(db_ref, dbias, first)
        for k in range(taps_n):
            _accumulate(dw_ref.at[k:k + 1, :], dtaps[k], first)

    blk = pl.BlockSpec((s, LANES), lambda j, i: (i, j))
    w_spec = pl.BlockSpec((taps_n, LANES), lambda j, i: (0, j))
    b_spec = pl.BlockSpec((1, LANES), lambda j, i: (0, j))
    half = jax.ShapeDtypeStruct(gate_pre.shape, BF16)
    dgate, dup, dw, db = pl.pallas_call(
        body, name="ffn_bwd", grid=(f // LANES, b),
        in_specs=[blk, blk, w_spec, b_spec, blk],
        out_specs=[blk, blk, w_spec, b_spec],
        out_shape=[half, half, jax.ShapeDtypeStruct((taps_n, f), F32), jax.ShapeDtypeStruct((1, f), F32)],
        compiler_params=_params("arbitrary", "arbitrary"),
    )(gate_pre, up, conv_w, conv_b.reshape(1, f), dact)
    return dgate, dup, dw, db.reshape(f)


def gate_arrays(logits, a_log, dt_bias, b, s, heads):
    n, r = s // DN_CHUNK, min(UT_ROWS, s)
    lg = logits.reshape(b, s, -1)
    beta = jax.nn.sigmoid(lg[..., :heads])
    g = -jnp.exp(a_log) * jax.nn.softplus(lg[..., heads:2 * heads] + dt_bias)
    gc = jnp.cumsum(g.reshape(b, n, DN_CHUNK, heads), axis=2).transpose(0, 3, 1, 2)
    return (beta.transpose(0, 2, 1)[..., None], gc.reshape(b, heads, s, 1), gc[:, :, :, None, :],
            gc.reshape(b, heads, s // r, 1, r))


def forward_layer(x, mem_hb, p, b, s):
    d = x.shape[1]
    pw = d // 2
    dn = d - pw
    heads = dn // DN_HEAD_DIM
    sv = {'x0': x}
    sv['h1'] = h1 = norm_fwd(x, p['mix_norm_g'])
    sv['proj'] = proj = matmul(h1, p['w_in_main'], 'nn', F32)
    logits = matmul(h1, p['w_in_logits'], 'nn', F32)
    y_pool = pool_fwd(proj, p['w_pool'], p['pool_scale'], b, s)
    gates, sv['gates_pull'] = jax.vjp(lambda lg, al, dtb: gate_arrays(lg, al, dtb, b, s, heads), logits,
                                      p['dn_a_log'], p['dn_dt_bias'])
    sv['gates'] = beta_col, gc_col, gc_row, gc_row_ut = gates
    sv['act'] = aq, ak, av = [conv_silu_fwd(proj, p['dn_conv_w'], part, pw, dn, b, s) for part in range(3)]
    sv['u'], sv['w'], sv['inv'] = u, w, _ = ut_fwd(ak, av, beta_col, gc_col, gc_row_ut, b, s)
    sv['o_dn'], sv['states'] = o_dn, _ = rec_fwd(aq, ak, gc_col, gc_row, u, w, b, s)
    y_dn = dn_norm_fwd(o_dn, proj, p['dn_norm_g'], pw + 3 * dn)
    sv['mixed'] = mixed = jnp.concatenate([y_pool, y_dn], axis=1)
    sv['x1'] = x1 = matmul(mixed, p['w_mix_out'], 'nn', F32, res=x)

    sv['h2'] = h2 = norm_fwd(x1, p['xa_norm_g'])
    sv['q'] = q = matmul(h2, p['w_xq'], 'nn', F32)
    sv['kv'] = kv = matmul(mem_hb, p['w_xkv'], 'nn', F32)
    sv['o_at'] = o_at = attn_fwd(q, kv, b, s)
    sv['x2'] = x2 = matmul(o_at, p['w_xo'], 'nn', F32, res=x1)

    sv['h3'] = h3 = norm_fwd(x2, p['ffn_norm_g'])
    sv['gate_pre'] = gate_pre = matmul(h3, p['w_gate'], 'nn', F32)
    sv['up'] = up = matmul(h3, p['w_up'], 'nn', F32)
    sv['a_ffn'] = a_ffn = ffn_fwd(gate_pre, up, p['ffn_conv_w'], p['ffn_conv_b'], b, s)
    return matmul(a_ffn, p['w_down'], 'nn', F32, res=x2), sv


def backward_layer(dx, dxb, sv, mem_hb, p, b, s):
    d = dx.shape[1]
    pw = d // 2
    dn = d - pw
    g = {}
    da = matmul(dxb, p['w_down'], 'nt', F32)
    g['w_down'] = matmul(sv['a_ffn'], dxb, 'tn', BF16)
    dgate, dup, g['ffn_conv_w'], g['ffn_conv_b'] = ffn_bwd(sv['gate_pre'], sv['up'], p['ffn_conv_w'],
                                                           p['ffn_conv_b'], da, b, s)
    dh = matmul(dup, p['w_up'], 'nt', F32, res=matmul(dgate, p['w_gate'], 'nt', F32))
    g['w_gate'] = matmul(sv['h3'], dgate, 'tn', BF16)
    g['w_up'] = matmul(sv['h3'], dup, 'tn', BF16)
    dx, dxb, g['ffn_norm_g'] = norm_bwd(sv['x2'], p['ffn_norm_g'], dh, dx)

    do = matmul(dxb, p['w_xo'], 'nt', F32)
    g['w_xo'] = matmul(sv['o_at'], dxb, 'tn', BF16)
    dq, dk, dv = attn_bwd(sv['q'], sv['kv'], do, b, s)
    dkv = jnp.concatenate([dk, dv], axis=1).astype(BF16)
    dh = matmul(dq, p['w_xq'], 'nt', F32)
    g['w_xq'] = matmul(sv['h2'], dq, 'tn', BF16)
    g['w_xkv'] = matmul(mem_hb, dkv, 'tn', BF16)
    dmem_h = matmul(dkv, p['w_xkv'], 'nt', F32)
    dx, dxb, g['xa_norm_g'] = norm_bwd(sv['x1'], p['xa_norm_g'], dh, dx)

    dmixed = matmul(dxb, p['w_mix_out'], 'nt', F32)
    g['w_mix_out'] = matmul(sv['mixed'], dxb, 'tn', BF16)
    proj = sv['proj']
    du_pool, g['w_pool'], g['pool_scale'] = pool_bwd(proj, p['w_pool'], p['pool_scale'], dmixed, b, s)
    do_dn, dz, g['dn_norm_g'] = dn_norm_bwd(sv['o_dn'], proj, p['dn_norm_g'], dmixed, pw + 3 * dn, pw)
    beta_col, gc_col, gc_row, gc_row_ut = sv['gates']
    aq, ak, av = sv['act']
    daq, dak_rec, du, dw, dgcc_rec, dgcr = rec_bwd(aq, ak, gc_col, gc_row, sv['u'], sv['w'], sv['states'],
                                                   do_dn, b, s)
    dak, dav, dbeta, dgcc_ut, dgcr_ut = ut_bwd(ak, av, beta_col, gc_col, gc_row_ut, sv['inv'], du, dw, dak_rec, b, s)
    dparts, dtaps = zip(*[conv_silu_bwd(proj, p['dn_conv_w'], dact, part, pw, dn, b, s)
                          for part, dact in enumerate((daq, dak, dav))])
    g['dn_conv_w'] = jnp.concatenate(dtaps, axis=1)
    dlogits, g['dn_a_log'], g['dn_dt_bias'] = sv['gates_pull']((dbeta, dgcc_rec + dgcc_ut, dgcr, dgcr_ut))
    dproj = jnp.concatenate([du_pool, *dparts, dz], axis=1)
    dlogits = dlogits.astype(BF16)
    dh = matmul(dproj, p['w_in_main'], 'nt', F32, res=matmul(dlogits, p['w_in_logits'], 'nt', F32))
    g['w_in_main'] = matmul(sv['h1'], dproj, 'tn', BF16)
    g['w_in_logits'] = matmul(sv['h1'], dlogits, 'tn', BF16)
    dx, dxb, g['mix_norm_g'] = norm_bwd(sv['x0'], p['mix_norm_g'], dh, dx)
    return dx, dxb, dmem_h, g


def _to_full(gathered, axis):
    moved = jnp.moveaxis(gathered, 0, axis)
    shape = list(gathered.shape[1:])
    shape[axis] *= N_DEV
    return moved.reshape(shape)


def _to_parts(full, axis):
    shape = list(full.shape)
    shape[axis:axis + 1] = [N_DEV, shape[axis] // N_DEV]
    return jnp.moveaxis(full.reshape(shape), axis, 0)


def _as_pack_rows(a, cols):
    rows = -(-a.shape[0] // (8 * cols)) * 8
    return jnp.pad(a, (0, rows * cols - a.shape[0])).reshape(rows, cols)


def kernel(x, mem, mix_norm_g, w_in, w_pool, pool_scale, dn_conv_w, dn_a_log, dn_dt_bias, dn_norm_g, w_mix_out, xa_norm_g, mem_norm_g, w_xq, w_xkv, w_xo, ffn_norm_g, w_gate, w_up, ffn_conv_w, ffn_conv_b, w_down, final_norm_g, loss_target, m_mix_norm_g, m_w_in, m_w_pool, m_pool_scale, m_dn_conv_w, m_dn_a_log, m_dn_dt_bias, m_dn_norm_g, m_w_mix_out, m_xa_norm_g, m_mem_norm_g, m_w_xq, m_w_xkv, m_w_xo, m_ffn_norm_g, m_w_gate, m_w_up, m_ffn_conv_w, m_ffn_conv_b, m_w_down, m_final_norm_g, v_mix_norm_g, v_w_in, v_w_pool, v_pool_scale, v_dn_conv_w, v_dn_a_log, v_dn_dt_bias, v_dn_norm_g, v_w_mix_out, v_xa_norm_g, v_mem_norm_g, v_w_xq, v_w_xkv, v_w_xo, v_ffn_norm_g, v_w_gate, v_w_up, v_ffn_conv_w, v_ffn_conv_b, v_w_down, v_final_norm_g):
    given = dict(locals())
    w = {n: given[n] for n in WEIGHTS}
    mom = {n: given['m_' + n] for n in WEIGHTS}
    var = {n: given['v_' + n] for n in WEIGHTS}
    b, s, d = x.shape
    depth = w_in.shape[0]
    main = 5 * (d // 2)
    n_logits = w_in.shape[-1] * N_DEV - main
    core = jnp.reshape(lax.axis_index("c"), (1,)).astype(jnp.int32)

    def layer_weights(l):
        names = MATRICES + TAPS
        gathered = all_gather([w[n][l] if n in TAPS else w[n][l].astype(BF16) for n in names], f"layer{l}")
        full = {n: _to_full(a, SHARD_AXIS[n] - 1) for n, a in zip(names, gathered)}
        p = {n: full[n] for n in names if n != 'w_in'}
        p['w_in_main'] = full['w_in'][:, :main]
        p['w_in_logits'] = jnp.pad(full['w_in'][:, main:], ((0, 0), (0, LANES - n_logits)))
        for n in PER_LAYER_REPLICATED:
            p[n] = w[n][l]
        return p

    layers = [layer_weights(l) for l in range(depth)]

    mem2 = mem.reshape(-1, d)
    mem_hb = norm_fwd(mem2, mem_norm_g)
    xc, saved = x.reshape(b * s, d), []
    for p in layers:
        xc, sv = forward_layer(xc, mem_hb, p, b, s)
        saved.append(sv)
    loss, dx, dxb, g_final = loss_head(xc, loss_target.reshape(b * s, d), final_norm_g)

    def as_layers(a, cols):
        return a.reshape(depth, -1, cols)

    names = MATRICES + TAPS
    results = {n: None for n in names}
    g_layers, dmem_h = [None] * depth, None
    for l in reversed(range(depth)):
        dx, dxb, dm, g = backward_layer(dx, dxb, saved[l], mem_hb, layers[l], b, s)
        g_layers[l] = g
        dmem_h = dm if dmem_h is None else dmem_h + dm
        g_full = {n: g[n] for n in names if n not in ('w_in', 'w_pool')}
        g_full['w_pool'] = g['w_pool'].astype(BF16)
        g_full['w_in'] = jnp.concatenate([g['w_in_main'], g['w_in_logits'][:, :n_logits]], axis=1)
        parts = []
        for n in names:
            by_dev = _to_parts(g_full[n], SHARD_AXIS[n] - 1)
            parts.append(by_dev.reshape(N_DEV, -1, by_dev.shape[-1]))
        from_sibling = sibling_swap(parts, f"layer{l}")
        got = chip_exchange([chip_sum(a, f, core) for a, f in zip(parts, from_sibling)], f"layer{l}")
        for n, a in zip(names, got):
            cols = w[n].shape[-1]
            results[n] = adamw(a, as_layers(w[n], cols), as_layers(mom[n], cols), as_layers(var[n], cols),
                               layer=l, prev=results[n])
    _, _, g_mem = norm_bwd(mem2, mem_norm_g, dmem_h, None)
    grad_x = dx.reshape(b, s, d)

    grad, delta, new_m, new_v = {}, {}, {}, {}
    for n in names:
        grad[n], delta[n], new_m[n], new_v[n] = [o.reshape(w[n].shape) for o in results[n]]

    g_small = {n: jnp.stack([g[n] for g in g_layers]) for n in PER_LAYER_REPLICATED}
    g_small['mem_norm_g'], g_small['final_norm_g'] = g_mem, g_final
    sizes = [w[n].size for n in REPLICATED]

    def pack(parts, first):
        flat = jnp.concatenate([jnp.reshape(first, (1,))] + [parts[n].reshape(-1) for n in REPLICATED])
        return _as_pack_rows(flat, LANES)

    zero = jnp.zeros((), F32)
    everyone, = all_gather([pack(g_small, loss)], "replicated")
    outs = adamw(everyone, pack(w, zero)[None], pack(mom, zero)[None], pack(var, zero)[None])
    flat_outs = [o.reshape(-1) for o in outs]
    loss_total = flat_outs[0][0]
    offset = 1
    for n, size in zip(REPLICATED, sizes):
        grad[n], delta[n], new_m[n], new_v[n] = [o[offset:offset + size].reshape(w[n].shape) for o in flat_outs]
        offset += size

    return (loss_total, grad_x, *[grad[n] for n in WEIGHTS], *[delta[n] for n in WEIGHTS],
            *[new_m[n] for n in WEIGHTS], *[new_v[n] for n in WEIGHTS])
```

```python
import functools

import jax
import jax.numpy as jnp
from jax import lax
from jax.experimental import pallas as pl
from jax.experimental.pallas import tpu as pltpu

F32 = jnp.float32
BF16 = jnp.bfloat16
MESH = pl.DeviceIdType.MESH

N_DEV = 8
EPS = 1e-6
POOL_WINDOWS = (2, 4, 8, 16)
DN_HEAD_DIM = 128
DN_CHUNK = 64
DN_TAPS = 4
XA_HEADS = 4
ADAM_LR = 0.001
ADAM_B1 = 0.9
ADAM_B2 = 0.999
ADAM_EPS = 1e-08
ADAM_WD = 0.01
ADAM_STEP = 10

LANES = 128
VMEM_LIMIT = 48 * 1024 * 1024
UT_ROWS = 256

WEIGHTS = ['mix_norm_g', 'w_in', 'w_pool', 'pool_scale', 'dn_conv_w', 'dn_a_log', 'dn_dt_bias', 'dn_norm_g',
           'w_mix_out', 'xa_norm_g', 'mem_norm_g', 'w_xq', 'w_xkv', 'w_xo', 'ffn_norm_g', 'w_gate', 'w_up',
           'ffn_conv_w', 'ffn_conv_b', 'w_down', 'final_norm_g']
SHARD_AXIS = {'w_in': 2, 'w_pool': 2, 'dn_conv_w': 2, 'w_mix_out': 1, 'w_xq': 1, 'w_xkv': 2, 'w_xo': 1,
              'w_gate': 2, 'w_up': 2, 'ffn_conv_w': 2, 'w_down': 1}
MATRICES = ('w_in', 'w_pool', 'w_mix_out', 'w_xq', 'w_xkv', 'w_xo', 'w_gate', 'w_up', 'w_down')
TAPS = ('dn_conv_w', 'ffn_conv_w')
REPLICATED = [n for n in WEIGHTS if n not in SHARD_AXIS]
PER_LAYER_REPLICATED = ('mix_norm_g', 'pool_scale', 'dn_a_log', 'dn_dt_bias', 'dn_norm_g', 'xa_norm_g',
                        'ffn_norm_g', 'ffn_conv_b')


def _params(*semantics):
    return pltpu.CompilerParams(dimension_semantics=semantics, vmem_limit_bytes=VMEM_LIMIT)


def _tile(dim, pref, unit=LANES):
    if dim <= pref:
        return dim
    t = (pref // unit) * unit
    while t >= unit:
        if dim % t == 0:
            return t
        t -= unit
    return dim


class Ride:
    def __init__(self, tag, arrays, out_shapes, n_sems, n_local, start, finish):
        self.tag, self.arrays, self.out_shapes = tag, list(arrays), list(out_shapes)
        self.start, self.finish = start, finish
        self.scratch = [pltpu.SemaphoreType.DMA((n_sems,)), pltpu.SemaphoreType.DMA((n_sems,)),
                        pltpu.SemaphoreType.DMA((n_local,))]

    @property
    def specs_in(self):
        return [pl.BlockSpec(memory_space=pl.ANY)] * len(self.arrays)

    @property
    def specs_out(self):
        return [pl.BlockSpec(memory_space=pl.ANY)] * len(self.out_shapes)

    def alone(self):
        n = len(self.arrays)

        def body(*refs):
            ins, outs, sems = refs[:n], refs[n:n + len(self.out_shapes)], refs[n + len(self.out_shapes):]
            self.start(ins, outs, *sems)
            self.finish(ins, outs, *sems)

        return pl.pallas_call(body, name=self.tag, out_shape=self.out_shapes, in_specs=self.specs_in,
                              out_specs=self.specs_out, scratch_shapes=self.scratch)(*self.arrays)


def _first_and_last_step(grid):
    ids = [pl.program_id(axis) for axis in range(len(grid))]
    first = functools.reduce(jnp.logical_and, [i == 0 for i in ids])
    last = functools.reduce(jnp.logical_and, [i == n - 1 for i, n in zip(ids, grid)])
    return first, last


def matmul(a, b, mode, out_dtype, res=None, ride=None):
    assert a.dtype == BF16 and b.dtype == BF16, (a.dtype, b.dtype)
    if mode == 'nn':
        (m, c), (c2, n) = a.shape, b.shape
    elif mode == 'nt':
        (m, c), (n, c2) = a.shape, b.shape
    else:
        (c, m), (c2, n) = a.shape, b.shape
    assert c == c2, (mode, a.shape, b.shape)
    tm, tn, tc = _tile(m, 1024), _tile(n, 512), _tile(c, 2048)
    nc = c // tc
    if mode == 'nn':
        a_spec = pl.BlockSpec((tm, tc), lambda i, j, k: (i, k))
        b_spec = pl.BlockSpec((tc, tn), lambda i, j, k: (k, j))
        dims = (((1,), (0,)), ((), ()))
    elif mode == 'nt':
        a_spec = pl.BlockSpec((tm, tc), lambda i, j, k: (i, k))
        b_spec = pl.BlockSpec((tn, tc), lambda i, j, k: (j, k))
        dims = (((1,), (1,)), ((), ()))
    else:
        a_spec = pl.BlockSpec((tc, tm), lambda i, j, k: (k, i))
        b_spec = pl.BlockSpec((tc, tn), lambda i, j, k: (k, j))
        dims = (((0,), (0,)), ((), ()))
    out_spec = pl.BlockSpec((tm, tn), lambda i, j, k: (i, j))
    has_res = res is not None
    grid = (m // tm, n // tn, nc)
    n_ride_in = len(ride.arrays) if ride else 0
    n_ride_out = len(ride.out_shapes) if ride else 0

    def body(a_ref, b_ref, *rest):
        rest = list(rest)
        r_ref = rest.pop(0) if has_res else None
        ride_in = [rest.pop(0) for _ in range(n_ride_in)]
        o_ref = rest.pop(0)
        ride_out = [rest.pop(0) for _ in range(n_ride_out)]
        acc_ref = rest.pop(0) if nc > 1 else None
        if ride:
            first, last = _first_and_last_step(grid)
            pl.when(first)(lambda: ride.start(ride_in, ride_out, *rest))
        prod = lax.dot_general(a_ref[...], b_ref[...], dims, preferred_element_type=F32)

        def finish(total):
            if has_res:
                total = total + r_ref[...]
            o_ref[...] = total.astype(o_ref.dtype)

        if nc == 1:
            finish(prod)
        else:
            k = pl.program_id(2)

            @pl.when(k == 0)
            def _():
                acc_ref[...] = prod

            @pl.when(k > 0)
            def _():
                acc_ref[...] += prod

            @pl.when(k == nc - 1)
            def _():
                finish(acc_ref[...])

        if ride:
            pl.when(last)(lambda: ride.finish(ride_in, ride_out, *rest))

    outs = pl.pallas_call(
        body,
        name=f"mm_{mode}_{m}x{c}x{n}" + ("_res" if has_res else "") + ("_with_" + ride.tag if ride else ""),
        grid=grid,
        in_specs=[a_spec, b_spec] + ([out_spec] if has_res else []) + (ride.specs_in if ride else []),
        out_specs=[out_spec] + (ride.specs_out if ride else []),
        out_shape=[jax.ShapeDtypeStruct((m, n), out_dtype)] + (ride.out_shapes if ride else []),
        scratch_shapes=([] if nc == 1 else [pltpu.VMEM((tm, tn), F32)]) + (ride.scratch if ride else []),
        compiler_params=(_params("arbitrary", "arbitrary", "arbitrary") if ride
                         else _params("parallel", "parallel", "arbitrary")),
    )(a, b, *([res] if has_res else []), *(ride.arrays if ride else []))
    return (outs[0], outs[1:]) if ride else outs[0]


def _position():
    return lax.axis_index("x"), lax.axis_index("y"), lax.axis_index("c")


def _flip(v, bit):
    return 1 - v if bit else v


def _dev_index(px, py, pc):
    return 4 * px + 2 * py + pc


def _hbm_call(body, name, arrays, out_shapes, n_sems, n_local):
    hbm = pl.BlockSpec(memory_space=pl.ANY)
    return pl.pallas_call(
        body, name=name, out_shape=out_shapes, in_specs=[hbm] * len(arrays), out_specs=[hbm] * len(out_shapes),
        scratch_shapes=[pltpu.SemaphoreType.DMA((n_sems,)), pltpu.SemaphoreType.DMA((n_sems,)),
                        pltpu.SemaphoreType.DMA((n_local,))],
    )(*arrays)


def gather_ride(shards, tag):
    n = len(shards)

    def plan(x_refs, out_refs, send_sems, recv_sems, local_sems):
        x, y, c = _position()
        me = (x, y, c)

        def copy(t, k, block, to, own=False):
            slot = out_refs[t].at[_dev_index(*block)]
            return pltpu.make_async_remote_copy(
                src_ref=x_refs[t] if own else slot, dst_ref=slot, send_sem=send_sems.at[7 * t + k],
                recv_sem=recv_sems.at[7 * t + k], device_id=to, device_id_type=MESH)

        def local(t):
            return pltpu.make_async_copy(x_refs[t], out_refs[t].at[_dev_index(*me)], local_sems.at[t])

        return me, (x, y, 1 - c), [(1 - x, y), (x, 1 - y), (1 - x, 1 - y)], copy, local

    def start(x_refs, out_refs, *sems):
        me, sibling, chips, copy, local = plan(x_refs, out_refs, *sems)
        for t in range(n):
            local(t).start()
            copy(t, 0, me, sibling, own=True).start()
            for j, chip in enumerate(chips):
                copy(t, 1 + j, me, (*chip, me[2]), own=True).start()

    def finish(x_refs, out_refs, *sems):
        me, sibling, chips, copy, local = plan(x_refs, out_refs, *sems)
        c = me[2]
        for j, chip in enumerate(chips):
            for t in range(n):
                copy(t, 1 + j, (*chip, c), me).wait_recv()
                copy(t, 4 + j, (*chip, c), sibling).start()
        for t in range(n):
            copy(t, 0, sibling, me).wait_recv()
            for j, chip in enumerate(chips):
                copy(t, 4 + j, (*chip, 1 - c), me).wait_recv()
        for t in range(n):
            copy(t, 0, me, sibling, own=True).wait_send()
            for j, chip in enumerate(chips):
                copy(t, 1 + j, me, (*chip, c), own=True).wait_send()
                copy(t, 4 + j, (*chip, c), sibling).wait_send()
            local(t).wait()

    return Ride("all_gather_" + tag, shards, [jax.ShapeDtypeStruct((N_DEV,) + a.shape, a.dtype) for a in shards],
                7 * n, n, start, finish)


def all_gather(shards, tag):
    return gather_ride(shards, tag).alone()


def sibling_swap(parts, tag):
    n = len(parts)

    def body(*refs):
        p_refs, out_refs = refs[:n], refs[n:2 * n]
        send_sems, recv_sems, _ = refs[2 * n:]
        x, y, c = _position()
        copies = [pltpu.make_async_remote_copy(
            src_ref=p_refs[t].at[2 * k + (1 - c)], dst_ref=out_refs[t].at[k], send_sem=send_sems.at[4 * t + k],
            recv_sem=recv_sems.at[4 * t + k], device_id=(x, y, 1 - c), device_id_type=MESH)
            for t in range(n) for k in range(N_DEV // 2)]
        for cp in copies:
            cp.start()
        for cp in copies:
            cp.wait()

    return _hbm_call(body, "sibling_swap_" + tag, parts,
                     [jax.ShapeDtypeStruct((N_DEV // 2,) + a.shape[1:], a.dtype) for a in parts], 4 * n, 1)


def chip_sum(parts, from_sibling, core):
    _, r, c = parts.shape
    unit = 16 if parts.dtype == BF16 else 8
    tr = _tile(r, max(unit, (512 * 1024 // c) // unit * unit), unit)

    def body(core_ref, a_ref, b_ref, o_ref):
        o_ref[...] = (a_ref[...].astype(F32) + b_ref[...].astype(F32)).astype(o_ref.dtype)

    blk = pl.BlockSpec((None, tr, c), lambda k, i, core_ref: (k, i, 0))
    return pl.pallas_call(
        body, name=f"chip_sum_{r}x{c}_{parts.dtype.name}",
        grid_spec=pltpu.PrefetchScalarGridSpec(
            num_scalar_prefetch=1, grid=(N_DEV // 2, r // tr),
            in_specs=[pl.BlockSpec((None, tr, c), lambda k, i, core_ref: (2 * k + core_ref[0], i, 0)), blk],
            out_specs=blk),
        out_shape=jax.ShapeDtypeStruct(from_sibling.shape, from_sibling.dtype),
        compiler_params=_params("parallel", "parallel"),
    )(core, parts, from_sibling)


def chip_exchange_ride(parts, tag):
    n = len(parts)

    def plan(p_refs, out_refs, send_sems, recv_sems, local_sems):
        x, y, c = _position()
        my_chip = 2 * x + y
        local = [pltpu.make_async_copy(p_refs[t].at[my_chip], out_refs[t].at[my_chip], local_sems.at[t])
                 for t in range(n)]
        copies = []
        for t in range(n):
            for k in (1, 2, 3):
                px, py = _flip(x, k & 2), _flip(y, k & 1)
                copies.append(pltpu.make_async_remote_copy(
                    src_ref=p_refs[t].at[2 * px + py], dst_ref=out_refs[t].at[my_chip],
                    send_sem=send_sems.at[3 * t + k - 1], recv_sem=recv_sems.at[3 * t + k - 1],
                    device_id=(px, py, c), device_id_type=MESH))
        return local, copies

    def start(p_refs, out_refs, *sems):
        local, copies = plan(p_refs, out_refs, *sems)
        for cp in local + copies:
            cp.start()

    def finish(p_refs, out_refs, *sems):
        local, copies = plan(p_refs, out_refs, *sems)
        for cp in copies:
            cp.wait_recv()
        for cp in copies:
            cp.wait_send()
        for cp in local:
            cp.wait()

    return Ride("chip_exchange_" + tag, parts, [jax.ShapeDtypeStruct(a.shape, a.dtype) for a in parts],
                3 * n, n, start, finish)


def adamw(parts, w, m, v, layer=0, prev=None):
    n_parts, r, c = parts.shape
    depth = w.shape[0]
    unit = 16 if parts.dtype == BF16 else 8
    tr = _tile(r, max(unit, (256 * 1024 // c) // unit * unit), unit)

    def body(p_ref, w_ref, m_ref, v_ref, *rest):
        g_ref, d_ref, nm_ref, nv_ref = rest[-4:]
        g = p_ref[0].astype(F32)
        for j in range(1, n_parts):
            g = g + p_ref[j].astype(F32)
        nm = ADAM_B1 * m_ref[...] + (1.0 - ADAM_B1) * g
        nv = ADAM_B2 * v_ref[...] + (1.0 - ADAM_B2) * jnp.square(g)
        m_hat = nm / (1.0 - ADAM_B1 ** ADAM_STEP)
        v_hat = nv / (1.0 - ADAM_B2 ** ADAM_STEP)
        g_ref[...] = g
        d_ref[...] = -ADAM_LR * (m_hat / (jnp.sqrt(v_hat) + ADAM_EPS) + ADAM_WD * w_ref[...])
        nm_ref[...] = nm
        nv_ref[...] = nv

    spec = pl.BlockSpec((None, tr, c), lambda i: (layer, i, 0))
    out = jax.ShapeDtypeStruct((depth, r, c), F32)
    carried = [] if prev is None else list(prev)
    return pl.pallas_call(
        body,
        name=f"adamw_{n_parts}x{r}x{c}_{parts.dtype.name}_layer{layer}",
        grid=(r // tr,),
        in_specs=[pl.BlockSpec((n_parts, tr, c), lambda i: (0, i, 0)), spec, spec, spec]
        + [pl.BlockSpec(memory_space=pl.ANY)] * len(carried),
        out_specs=[spec, spec, spec, spec],
        out_shape=[out, out, out, out],
        input_output_aliases={4 + k: k for k in range(len(carried))},
        compiler_params=_params("parallel"),
    )(parts, w, m, v, *carried)


_NN = (((1,), (0,)), ((), ()))
_NT = (((1,), (1,)), ((), ()))
_TN = (((0,), (0,)), ((), ()))


def _dot_bf16(a, b, dims):
    return lax.dot_general(a.astype(BF16), b.astype(BF16), dims, preferred_element_type=F32)


def _dot_split(a, b, dims):
    a_hi, b_hi = a.astype(BF16), b.astype(BF16)
    a_lo = (a - a_hi.astype(F32)).astype(BF16)
    b_lo = (b - b_hi.astype(F32)).astype(BF16)
    dot = functools.partial(lax.dot_general, dimension_numbers=dims, preferred_element_type=F32)
    return dot(a_hi, b_hi) + (dot(a_hi, b_lo) + dot(a_lo, b_hi))


def _matmul_family(dot):
    @jax.custom_vjp
    def nn(a, b):
        return dot(a, b, _NN)

    @jax.custom_vjp
    def nt(a, b):
        return dot(a, b, _NT)

    @jax.custom_vjp
    def tn(a, b):
        return dot(a, b, _TN)

    nn.defvjp(lambda a, b: (nn(a, b), (a, b)), lambda r, g: (nt(g, r[1]), tn(r[0], g)))
    nt.defvjp(lambda a, b: (nt(a, b), (a, b)), lambda r, g: (nn(g, r[1]), tn(g, r[0])))
    tn.defvjp(lambda a, b: (tn(a, b), (a, b)), lambda r, g: (nt(r[1], g), nn(r[0], g)))
    return nn, nt, tn


mm, mm_nt, mm_tn = _matmul_family(_dot_bf16)
mms, mms_nt, mms_tn = _matmul_family(_dot_split)


def _shift_rows(x, k, down):
    n = x.shape[0]
    rows = lax.broadcasted_iota(jnp.int32, x.shape, 0)
    if down:
        return jnp.where(rows >= k, pltpu.roll(x, k, 0), 0.0)
    return jnp.where(rows < n - k, pltpu.roll(x, n - k, 0), 0.0)


@functools.partial(jax.custom_vjp, nondiff_argnums=(1,))
def delay(x, k):
    return _shift_rows(x, k, True) if k else x


delay.defvjp(lambda x, k: (delay(x, k), None), lambda k, _, g: ((_shift_rows(g, k, False) if k else g),))


def _silu(x):
    return x * jax.nn.sigmoid(x)


def _rms(x, g):
    return x * lax.rsqrt(jnp.mean(x * x, axis=-1, keepdims=True) + EPS) * g


def _l2n(t):
    return t * lax.rsqrt(jnp.sum(t * t, axis=-1, keepdims=True) + EPS)


def _causal_conv(x, taps):
    k_taps = len(taps)
    y = delay(x, k_taps - 1) * taps[0]
    for k in range(1, k_taps):
        y = y + delay(x, k_taps - 1 - k) * taps[k]
    return y


def _pool_block(u, w, scale, group):
    sums, acc, width = [], u, 1
    while width < POOL_WINDOWS[-1]:
        acc = acc + delay(acc, width)
        width *= 2
        sums.append(acc)
    picked = sums[-1]
    for i in range(len(POOL_WINDOWS) - 2, -1, -1):
        picked = jnp.where(group == i, sums[i], picked)
    rows = lax.broadcasted_iota(jnp.int32, u.shape, 0)
    count = jnp.minimum(rows + 1, jnp.left_shift(2, group)).astype(F32)
    return mm(picked / count - u, w) * scale


def _unit_lower_inverse(l_mat):
    n = l_mat.shape[0]
    eye = (lax.broadcasted_iota(jnp.int32, (n, n), 0) == lax.broadcasted_iota(jnp.int32, (n, n), 1)).astype(F32)
    m1 = -l_mat
    m2 = mms(m1, m1)
    m4 = mms(m2, m2)
    m8 = mms(m4, m4)
    m16 = mms(m8, m8)
    m32 = mms(m16, m16)
    low = mms(eye + m1, eye + m2)
    mid = mms(eye + m4, eye + m8)
    high = mms(eye + m16, eye + m32)
    return mms(mms(low, mid), high)


@jax.custom_vjp
def _known_inverse(l_mat, inv):
    return inv


_known_inverse.defvjp(lambda l_mat, inv: (inv, inv),
                      lambda inv, g: (-mms_tn(inv, mms_nt(g, inv)), jnp.zeros_like(inv)))


def _ut_chain(k, v, beta, gc_c, gc_r, inv=None):
    r = k.shape[0]
    kn = _l2n(k)
    row = lax.broadcasted_iota(jnp.int32, (r, r), 0)
    col = lax.broadcasted_iota(jnp.int32, (r, r), 1)
    strict = (row // DN_CHUNK == col // DN_CHUNK) & (row > col)
    decay = jnp.exp(jnp.where(strict, gc_c - gc_r, -1e30))
    l_mat = jnp.where(strict, beta * mm_nt(kn, kn) * decay, 0.0)
    inv = _unit_lower_inverse(l_mat) if inv is None else _known_inverse(l_mat, inv)
    return mms(inv, v * beta), mms(inv, kn * (beta * jnp.exp(gc_c))), inv


def _rec_chain(q, k, gc_c, gc_r, u, w, state):
    ch, dh = q.shape
    qn = _l2n(q) * (dh ** -0.5)
    kn = _l2n(k)
    row = lax.broadcasted_iota(jnp.int32, (ch, ch), 0)
    col = lax.broadcasted_iota(jnp.int32, (ch, ch), 1)
    decay = jnp.exp(jnp.where(row >= col, gc_c - gc_r, -1e30))
    attn = mm_nt(qn, kn) * decay
    is_last = lax.broadcasted_iota(jnp.int32, gc_r.shape, 1) == ch - 1
    last = jnp.sum(jnp.where(is_last, gc_r, 0.0), axis=1, keepdims=True)
    v_new = u - mm(w, state)
    out = mm(qn * jnp.exp(gc_c), state) + mm(attn, v_new)
    return out, state * jnp.exp(last) + mm_tn(kn * jnp.exp(last - gc_c), v_new)


def _gated_norm(o, z, g):
    return _rms(o, g) * _silu(z)


def _attn_block(q, k, v):
    s = mm_nt(q, k) * (q.shape[-1] ** -0.5)
    p = jnp.exp(s - lax.stop_gradient(jnp.max(s, axis=-1, keepdims=True)))
    return mm(p / jnp.sum(p, axis=-1, keepdims=True), v)


def _ffn_block(gate_pre, up, taps, bias):
    return _silu(_causal_conv(gate_pre, taps) + bias) * up


def _accumulate(ref, value, first):
    @pl.when(first)
    def _():
        ref[...] = value

    @pl.when(jnp.logical_not(first))
    def _():
        ref[...] += value


def norm_fwd(x, g):
    t, d = x.shape
    tr = _tile(t, 256, 8)

    def body(x_ref, g_ref, h_ref):
        h_ref[...] = _rms(x_ref[...], g_ref[...]).astype(BF16)

    return pl.pallas_call(
        body, name=f"norm_fwd_{t}", grid=(t // tr,),
        in_specs=[pl.BlockSpec((tr, d), lambda i: (i, 0)), pl.BlockSpec((1, d), lambda i: (0, 0))],
        out_specs=pl.BlockSpec((tr, d), lambda i: (i, 0)),
        out_shape=jax.ShapeDtypeStruct((t, d), BF16), compiler_params=_params("parallel"),
    )(x, g.reshape(1, d))


def norm_bwd(x, g, dh, dres):
    t, d = x.shape
    tr = _tile(t, 256, 8)
    has_res = dres is not None

    def body(x_ref, g_ref, dh_ref, *rest):
        rest = list(rest)
        r_ref = rest.pop(0) if has_res else None
        dx_ref, dxb_ref, dg_ref = rest
        _, pull = jax.vjp(_rms, x_ref[...], g_ref[...])
        dx, dg = pull(dh_ref[...])
        if has_res:
            dx = dx + r_ref[...]
        dx_ref[...] = dx
        dxb_ref[...] = dx.astype(BF16)
        _accumulate(dg_ref, dg, pl.program_id(0) == 0)

    row = pl.BlockSpec((tr, d), lambda i: (i, 0))
    vec = pl.BlockSpec((1, d), lambda i: (0, 0))
    dx, dxb, dg = pl.pallas_call(
        body, name=f"norm_bwd_{t}" + ("_res" if has_res else ""), grid=(t // tr,),
        in_specs=[row, vec, row] + ([row] if has_res else []),
        out_specs=[row, row, vec],
        out_shape=[jax.ShapeDtypeStruct((t, d), F32), jax.ShapeDtypeStruct((t, d), BF16),
                   jax.ShapeDtypeStruct((1, d), F32)],
        compiler_params=_params("arbitrary"),
    )(x, g.reshape(1, d), dh, *([dres] if has_res else []))
    return dx, dxb, dg.reshape(d)


def loss_head(x, target, g):
    t, d = x.shape
    tr = _tile(t, 256, 8)

    def body(x_ref, t_ref, g_ref, l_ref, dx_ref, dxb_ref, dg_ref):
        tgt = t_ref[...]

        def block_loss(xv, gv):
            return 0.5 * jnp.sum(jnp.mean(jnp.square(_rms(xv, gv) - tgt), axis=-1))

        val, pull = jax.vjp(block_loss, x_ref[...], g_ref[...])
        dx, dg = pull(jnp.ones((), F32))
        dx_ref[...] = dx
        dxb_ref[...] = dx.astype(BF16)
        first = pl.program_id(0) == 0
        _accumulate(dg_ref, dg, first)
        _accumulate(l_ref, jnp.full((1, LANES), val, F32), first)

    row = pl.BlockSpec((tr, d), lambda i: (i, 0))
    vec = pl.BlockSpec((1, d), lambda i: (0, 0))
    loss, dx, dxb, dg = pl.pallas_call(
        body, name="loss_head", grid=(t // tr,),
        in_specs=[row, row, vec],
        out_specs=[pl.BlockSpec((1, LANES), lambda i: (0, 0)), row, row, vec],
        out_shape=[jax.ShapeDtypeStruct((1, LANES), F32), jax.ShapeDtypeStruct((t, d), F32),
                   jax.ShapeDtypeStruct((t, d), BF16), jax.ShapeDtypeStruct((1, d), F32)],
        compiler_params=_params("arbitrary"),
    )(x, target, g.reshape(1, d))
    return loss[0, 0], dx, dxb, dg.reshape(d)


def pool_fwd(proj, w_pool, scale, b, s):
    n_g, grp = w_pool.shape[0], w_pool.shape[-1]

    def body(u_ref, w_ref, s_ref, y_ref):
        y_ref[...] = _pool_block(u_ref[...], w_ref[...], s_ref[...], pl.program_id(1)).astype(BF16)

    blk = pl.BlockSpec((s, grp), lambda i, j: (i, j))
    return pl.pallas_call(
        body, name="pool_fwd", grid=(b, n_g),
        in_specs=[blk, pl.BlockSpec((None, grp, grp), lambda i, j: (j, 0, 0)),
                  pl.BlockSpec((1, grp), lambda i, j: (0, j))],
        out_specs=blk,
        out_shape=jax.ShapeDtypeStruct((b * s, n_g * grp), BF16), compiler_params=_params("parallel", "parallel"),
    )(proj, w_pool, scale.reshape(1, -1))


def pool_bwd(proj, w_pool, scale, dmixed, b, s):
    n_g, grp = w_pool.shape[0], w_pool.shape[-1]

    def body(u_ref, w_ref, s_ref, dy_ref, du_ref, dw_ref, ds_ref):
        group = pl.program_id(0)
        _, pull = jax.vjp(lambda u, w, sc: _pool_block(u, w, sc, group), u_ref[...], w_ref[...].astype(F32),
                          s_ref[...])
        du, dw, ds = pull(dy_ref[...])
        du_ref[...] = du.astype(BF16)
        first = pl.program_id(1) == 0
        _accumulate(dw_ref, dw, first)
        _accumulate(ds_ref, ds, first)

    blk = pl.BlockSpec((s, grp), lambda j, i: (i, j))
    w_spec = pl.BlockSpec((None, grp, grp), lambda j, i: (j, 0, 0))
    s_spec = pl.BlockSpec((1, grp), lambda j, i: (0, j))
    du, dw, ds = pl.pallas_call(
        body, name="pool_bwd", grid=(n_g, b),
        in_specs=[blk, w_spec, s_spec, blk],
        out_specs=[blk, w_spec, s_spec],
        out_shape=[jax.ShapeDtypeStruct((b * s, n_g * grp), BF16), jax.ShapeDtypeStruct(w_pool.shape, F32),
                   jax.ShapeDtypeStruct((1, n_g * grp), F32)],
        compiler_params=_params("arbitrary", "arbitrary"),
    )(proj, w_pool, scale.reshape(1, -1), dmixed)
    return du, dw, ds.reshape(-1)


def conv_silu_fwd(proj, conv_w, part, col0, width, b, s):
    x_off, w_off = (col0 + part * width) // LANES, part * width // LANES

    def body(x_ref, w_ref, y_ref):
        taps = [w_ref[k:k + 1, :] for k in range(DN_TAPS)]
        y_ref[...] = _silu(_causal_conv(x_ref[...], taps))

    return pl.pallas_call(
        body, name=f"conv_silu_fwd_{part}", grid=(width // LANES, b),
        in_specs=[pl.BlockSpec((s, LANES), lambda j, i: (i, x_off + j)),
                  pl.BlockSpec((DN_TAPS, LANES), lambda j, i: (0, w_off + j))],
        out_specs=pl.BlockSpec((s, LANES), lambda j, i: (i, j)),
        out_shape=jax.ShapeDtypeStruct((b * s, width), F32), compiler_params=_params("parallel", "parallel"),
    )(proj, conv_w)


def conv_silu_bwd(proj, conv_w, dact, part, col0, width, b, s):
    x_off, w_off = (col0 + part * width) // LANES, part * width // LANES

    def body(x_ref, w_ref, dy_ref, dx_ref, dw_ref):
        taps = [w_ref[k:k + 1, :] for k in range(DN_TAPS)]
        _, pull = jax.vjp(lambda x, *tp: _silu(_causal_conv(x, tp)), x_ref[...], *taps)
        dx, *dtaps = pull(dy_ref[...])
        dx_ref[...] = dx.astype(BF16)
        first = pl.program_id(1) == 0
        for k in range(DN_TAPS):
            _accumulate(dw_ref.at[k:k + 1, :], dtaps[k], first)

    out_blk = pl.BlockSpec((s, LANES), lambda j, i: (i, j))
    return pl.pallas_call(
        body, name=f"conv_silu_bwd_{part}", grid=(width // LANES, b),
        in_specs=[pl.BlockSpec((s, LANES), lambda j, i: (i, x_off + j)),
                  pl.BlockSpec((DN_TAPS, LANES), lambda j, i: (0, w_off + j)), out_blk],
        out_specs=[out_blk, pl.BlockSpec((DN_TAPS, LANES), lambda j, i: (0, j))],
        out_shape=[jax.ShapeDtypeStruct((b * s, width), BF16), jax.ShapeDtypeStruct((DN_TAPS, width), F32)],
        compiler_params=_params("arbitrary", "arbitrary"),
    )(proj, conv_w, dact)


def _ut_specs(b, s, heads, width):
    r = min(UT_ROWS, s)
    ns = s // r
    tok = pl.BlockSpec((r, width), lambda i, n: (i * ns + n, 0))
    col = pl.BlockSpec((None, heads, r, 1), lambda i, n: (i, 0, n, 0))
    row = pl.BlockSpec((None, heads, None, 1, r), lambda i, n: (i, 0, n, 0, 0))
    return r, ns, tok, col, row


def ut_fwd(act_k, act_v, beta_col, gc_col, gc_row, b, s, ride=None):
    width = act_k.shape[1]
    heads = width // DN_HEAD_DIM
    r, ns, tok, col, row = _ut_specs(b, s, heads, width)
    inv_spec = pl.BlockSpec((r, heads * r), lambda i, n: (i * ns + n, 0))

    n_ride_in = len(ride.arrays) if ride else 0

    def body(k_ref, v_ref, beta_ref, gcc_ref, gcr_ref, *rest):
        u_ref, w_ref, inv_ref = rest[n_ride_in:n_ride_in + 3]
        if ride:
            ride_in, ride_out, sems = rest[:n_ride_in], rest[n_ride_in + 3:-3], rest[-3:]
            first, last = _first_and_last_step((b, ns))
            pl.when(first)(lambda: ride.start(ride_in, ride_out, *sems))
        for h in range(heads):
            sl = slice(h * DN_HEAD_DIM, (h + 1) * DN_HEAD_DIM)
            u, w, inv = _ut_chain(k_ref[:, sl], v_ref[:, sl], beta_ref[h], gcc_ref[h], gcr_ref[h])
            u_ref[:, sl] = u
            w_ref[:, sl] = w
            inv_ref[:, h * r:(h + 1) * r] = inv
        if ride:
            pl.when(last)(lambda: ride.finish(ride_in, ride_out, *sems))

    out = jax.ShapeDtypeStruct((b * s, width), F32)
    outs = pl.pallas_call(
        body, name="ut_fwd" + ("_with_" + ride.tag if ride else ""), grid=(b, ns),
        in_specs=[tok, tok, col, col, row] + (ride.specs_in if ride else []),
        out_specs=[tok, tok, inv_spec] + (ride.specs_out if ride else []),
        out_shape=[out, out, jax.ShapeDtypeStruct((b * s, heads * r), F32)] + (ride.out_shapes if ride else []),
        scratch_shapes=ride.scratch if ride else [],
        compiler_params=_params("arbitrary", "arbitrary") if ride else _params("parallel", "parallel"),
    )(act_k, act_v, beta_col, gc_col, gc_row, *(ride.arrays if ride else []))
    return (*outs[:3], outs[3:]) if ride else outs


def ut_bwd(act_k, act_v, beta_col, gc_col, gc_row, inv, du, dw, dk_more, b, s):
    width = act_k.shape[1]
    heads = width // DN_HEAD_DIM
    r, ns, tok, col, row = _ut_specs(b, s, heads, width)
    inv_spec = pl.BlockSpec((r, heads * r), lambda i, n: (i * ns + n, 0))

    def body(k_ref, v_ref, beta_ref, gcc_ref, gcr_ref, inv_ref, du_ref, dw_ref, dkm_ref,
             dk_ref, dv_ref, dbeta_ref, dgcc_ref, dgcr_ref):
        for h in range(heads):
            sl = slice(h * DN_HEAD_DIM, (h + 1) * DN_HEAD_DIM)
            inv = inv_ref[:, h * r:(h + 1) * r]
            _, pull = jax.vjp(lambda *a: _ut_chain(*a, inv=inv)[:2], k_ref[:, sl], v_ref[:, sl], beta_ref[h],
                              gcc_ref[h], gcr_ref[h])
            dk, dv, dbeta, dgcc, dgcr = pull((du_ref[:, sl], dw_ref[:, sl]))
            dk_ref[:, sl] = dk + dkm_ref[:, sl]
            dv_ref[:, sl] = dv
            dbeta_ref[h] = dbeta
            dgcc_ref[h] = dgcc
            dgcr_ref[h] = dgcr

    out = jax.ShapeDtypeStruct((b * s, width), F32)
    return pl.pallas_call(
        body, name="ut_bwd", grid=(b, ns), in_specs=[tok, tok, col, col, row, inv_spec, tok, tok, tok],
        out_specs=[tok, tok, col, col, row],
        out_shape=[out, out, jax.ShapeDtypeStruct(beta_col.shape, F32), jax.ShapeDtypeStruct(gc_col.shape, F32),
                   jax.ShapeDtypeStruct(gc_row.shape, F32)],
        compiler_params=_params("parallel", "parallel"),
    )(act_k, act_v, beta_col, gc_col, gc_row, inv, du, dw, dk_more)


def _rec_specs(b, n, heads, width, chunk_of):
    ch = DN_CHUNK
    tok = pl.BlockSpec((b, None, ch, width), lambda i: (0, chunk_of(i), 0, 0))
    col = pl.BlockSpec((b, heads, None, ch, 1), lambda i: (0, 0, chunk_of(i), 0, 0))
    row = pl.BlockSpec((b, heads, None, 1, ch), lambda i: (0, 0, chunk_of(i), 0, 0))
    st = pl.BlockSpec((None, b, heads, DN_HEAD_DIM, DN_HEAD_DIM), lambda i: (chunk_of(i), 0, 0, 0, 0))
    return tok, col, row, st


def rec_fwd(act_q, act_k, gc_col, gc_row, u, w, b, s):
    width = act_q.shape[1]
    heads, n = width // DN_HEAD_DIM, s // DN_CHUNK
    tok, col, row, st = _rec_specs(b, n, heads, width, lambda i: i)
    shape4 = (b, n, DN_CHUNK, width)

    def body(q_ref, k_ref, gcc_ref, gcr_ref, u_ref, w_ref, o_ref, st_ref, state):
        @pl.when(pl.program_id(0) == 0)
        def _():
            state[...] = jnp.zeros_like(state)

        for i in range(b):
            for h in range(heads):
                sl = slice(h * DN_HEAD_DIM, (h + 1) * DN_HEAD_DIM)
                s_in = state[i, h]
                st_ref[i, h] = s_in
                o, s_out = _rec_chain(q_ref[i, :, sl], k_ref[i, :, sl], gcc_ref[i, h], gcr_ref[i, h],
                                      u_ref[i, :, sl], w_ref[i, :, sl], s_in)
                o_ref[i, :, sl] = o
                state[i, h] = s_out

    o, states = pl.pallas_call(
        body, name="rec_fwd", grid=(n,), in_specs=[tok, tok, col, row, tok, tok], out_specs=[tok, st],
        out_shape=[jax.ShapeDtypeStruct(shape4, F32),
                   jax.ShapeDtypeStruct((n, b, heads, DN_HEAD_DIM, DN_HEAD_DIM), F32)],
        scratch_shapes=[pltpu.VMEM((b, heads, DN_HEAD_DIM, DN_HEAD_DIM), F32)],
        compiler_params=_params("arbitrary"),
    )(act_q.reshape(shape4), act_k.reshape(shape4), gc_col.reshape(b, heads, n, DN_CHUNK, 1), gc_row,
      u.reshape(shape4), w.reshape(shape4))
    return o.reshape(b * s, width), states


def rec_bwd(act_q, act_k, gc_col, gc_row, u, w, states, do, b, s):
    width = act_q.shape[1]
    heads, n = width // DN_HEAD_DIM, s // DN_CHUNK
    tok, col, row, st = _rec_specs(b, n, heads, width, lambda i: n - 1 - i)
    shape4 = (b, n, DN_CHUNK, width)

    def body(q_ref, k_ref, gcc_ref, gcr_ref, u_ref, w_ref, st_ref, do_ref,
             dq_ref, dk_ref, du_ref, dw_ref, dgcc_ref, dgcr_ref, dstate):
        @pl.when(pl.program_id(0) == 0)
        def _():
            dstate[...] = jnp.zeros_like(dstate)

        for i in range(b):
            for h in range(heads):
                sl = slice(h * DN_HEAD_DIM, (h + 1) * DN_HEAD_DIM)
                _, pull = jax.vjp(_rec_chain, q_ref[i, :, sl], k_ref[i, :, sl], gcc_ref[i, h], gcr_ref[i, h],
                                  u_ref[i, :, sl], w_ref[i, :, sl], st_ref[i, h])
                dq, dk, dgcc, dgcr, du, dw, ds = pull((do_ref[i, :, sl], dstate[i, h]))
                dq_ref[i, :, sl] = dq
                dk_ref[i, :, sl] = dk
                du_ref[i, :, sl] = du
                dw_ref[i, :, sl] = dw
                dgcc_ref[i, h] = dgcc
                dgcr_ref[i, h] = dgcr
                dstate[i, h] = ds

    tok_out = jax.ShapeDtypeStruct(shape4, F32)
    dq, dk, du, dw, dgcc, dgcr = pl.pallas_call(
        body, name="rec_bwd", grid=(n,), in_specs=[tok, tok, col, row, tok, tok, st, tok],
        out_specs=[tok, tok, tok, tok, col, row],
        out_shape=[tok_out, tok_out, tok_out, tok_out,
                   jax.ShapeDtypeStruct((b, heads, n, DN_CHUNK, 1), F32), jax.ShapeDtypeStruct(gc_row.shape, F32)],
        scratch_shapes=[pltpu.VMEM((b, heads, DN_HEAD_DIM, DN_HEAD_DIM), F32)],
        compiler_params=_params("arbitrary"),
    )(act_q.reshape(shape4), act_k.reshape(shape4), gc_col.reshape(b, heads, n, DN_CHUNK, 1), gc_row,
      u.reshape(shape4), w.reshape(shape4), states, do.reshape(shape4))
    flat = lambda a: a.reshape(b * s, width)
    return flat(dq), flat(dk), flat(du), flat(dw), dgcc.reshape(b, heads, s, 1), dgcr


def dn_norm_fwd(o, proj, g, z_col0):
    t, width = o.shape
    heads = width // DN_HEAD_DIM
    tr = _tile(t, 512, 8)
    z_off = z_col0 // LANES

    def body(o_ref, z_ref, g_ref, y_ref):
        y_ref[...] = _gated_norm(o_ref[...], z_ref[...], g_ref[...]).astype(BF16)

    blk = pl.BlockSpec((tr, DN_HEAD_DIM), lambda i, h: (i, h))
    return pl.pallas_call(
        body, name="dn_norm_fwd", grid=(t // tr, heads),
        in_specs=[blk, pl.BlockSpec((tr, DN_HEAD_DIM), lambda i, h: (i, z_off + h)),
                  pl.BlockSpec((1, DN_HEAD_DIM), lambda i, h: (0, 0))],
        out_specs=blk, out_shape=jax.ShapeDtypeStruct((t, width), BF16),
        compiler_params=_params("parallel", "parallel"),
    )(o, proj, g.reshape(1, -1))


def dn_norm_bwd(o, proj, g, dmixed, z_col0, dy_col0):
    t, width = o.shape
    heads = width // DN_HEAD_DIM
    tr = _tile(t, 512, 8)
    z_off, dy_off = z_col0 // LANES, dy_col0 // LANES

    def body(o_ref, z_ref, g_ref, dy_ref, do_ref, dz_ref, dg_ref):
        _, pull = jax.vjp(_gated_norm, o_ref[...], z_ref[...], g_ref[...])
        do, dz, dg = pull(dy_ref[...])
        do_ref[...] = do
        dz_ref[...] = dz.astype(BF16)
        _accumulate(dg_ref, dg, (pl.program_id(0) == 0) & (pl.program_id(1) == 0))

    blk = pl.BlockSpec((tr, DN_HEAD_DIM), lambda i, h: (i, h))
    vec = pl.BlockSpec((1, DN_HEAD_DIM), lambda i, h: (0, 0))
    do, dz, dg = pl.pallas_call(
        body, name="dn_norm_bwd", grid=(t // tr, heads),
        in_specs=[blk, pl.BlockSpec((tr, DN_HEAD_DIM), lambda i, h: (i, z_off + h)), vec,
                  pl.BlockSpec((tr, DN_HEAD_DIM), lambda i, h: (i, dy_off + h))],
        out_specs=[blk, blk, vec],
        out_shape=[jax.ShapeDtypeStruct((t, width), F32), jax.ShapeDtypeStruct((t, width), BF16),
                   jax.ShapeDtypeStruct((1, DN_HEAD_DIM), F32)],
        compiler_params=_params("arbitrary", "arbitrary"),
    )(o, proj, g.reshape(1, -1), dmixed)
    return do, dz, dg.reshape(-1)


def _attn_specs(b, s, mem_len, d):
    hd = d // XA_HEADS
    tq = _tile(s, 512, 8)
    nq = s // tq
    q_spec = pl.BlockSpec((tq, hd), lambda i, h, j: (i * nq + j, h))
    k_spec = pl.BlockSpec((mem_len, hd), lambda i, h, j: (i, h))
    v_spec = pl.BlockSpec((mem_len, hd), lambda i, h, j: (i, XA_HEADS + h))
    return nq, q_spec, k_spec, v_spec


def attn_fwd(q, kv, b, s):
    d = q.shape[1]
    nq, q_spec, k_spec, v_spec = _attn_specs(b, s, kv.shape[0] // b, d)

    def body(q_ref, k_ref, v_ref, o_ref):
        o_ref[...] = _attn_block(q_ref[...], k_ref[...], v_ref[...]).astype(BF16)

    return pl.pallas_call(
        body, name="attn_fwd", grid=(b, XA_HEADS, nq), in_specs=[q_spec, k_spec, v_spec], out_specs=q_spec,
        out_shape=jax.ShapeDtypeStruct(q.shape, BF16), compiler_params=_params("parallel", "parallel", "parallel"),
    )(q, kv, kv)


def attn_bwd(q, kv, do, b, s):
    d = q.shape[1]
    rows = kv.shape[0]
    nq, q_spec, k_spec, v_spec = _attn_specs(b, s, rows // b, d)

    def body(q_ref, k_ref, v_ref, do_ref, dq_ref, dk_ref, dv_ref):
        _, pull = jax.vjp(_attn_block, q_ref[...], k_ref[...], v_ref[...])
        dq, dk, dv = pull(do_ref[...])
        dq_ref[...] = dq.astype(BF16)
        first = pl.program_id(2) == 0
        _accumulate(dk_ref, dk, first)
        _accumulate(dv_ref, dv, first)

    kv_out = jax.ShapeDtypeStruct((rows, d), F32)
    return pl.pallas_call(
        body, name="attn_bwd", grid=(b, XA_HEADS, nq), in_specs=[q_spec, k_spec, v_spec, q_spec],
        out_specs=[q_spec, k_spec, k_spec], out_shape=[jax.ShapeDtypeStruct(q.shape, BF16), kv_out, kv_out],
        compiler_params=_params("parallel", "parallel", "arbitrary"),
    )(q, kv, kv, do)


def ffn_fwd(gate_pre, up, conv_w, conv_b, b, s):
    taps_n, f = conv_w.shape

    def body(g_ref, u_ref, w_ref, b_ref, y_ref):
        taps = [w_ref[k:k + 1, :] for k in range(taps_n)]
        y_ref[...] = _ffn_block(g_ref[...], u_ref[...], taps, b_ref[...]).astype(BF16)

    blk = pl.BlockSpec((s, LANES), lambda j, i: (i, j))
    return pl.pallas_call(
        body, name="ffn_fwd", grid=(f // LANES, b),
        in_specs=[blk, blk, pl.BlockSpec((taps_n, LANES), lambda j, i: (0, j)),
                  pl.BlockSpec((1, LANES), lambda j, i: (0, j))],
        out_specs=blk,
        out_shape=jax.ShapeDtypeStruct((b * s, f), BF16), compiler_params=_params("parallel", "parallel"),
    )(gate_pre, up, conv_w, conv_b.reshape(1, f))


def ffn_bwd(gate_pre, up, conv_w, conv_b, dact, b, s):
    taps_n, f = conv_w.shape

    def body(g_ref, u_ref, w_ref, b_ref, dy_ref, dg_ref, du_ref, dw_ref, db_ref):
        taps = [w_ref[k:k + 1, :] for k in range(taps_n)]
        _, pull = jax.vjp(lambda gt, up_, bias, *tp: _ffn_block(gt, up_, tp, bias), g_ref[...], u_ref[...],
                          b_ref[...], *taps)
        dgate, dup, dbias, *dtaps = pull(dy_ref[...])
        dg_ref[...] = dgate.astype(BF16)
        du_ref[...] = dup.astype(BF16)
        first = pl.program_id(1) == 0
        _accumulate(db_ref, dbias, first)
        for k in range(taps_n):
            _accumulate(dw_ref.at[k:k + 1, :], dtaps[k], first)

    blk = pl.BlockSpec((s, LANES), lambda j, i: (i, j))
    w_spec = pl.BlockSpec((taps_n, LANES), lambda j, i: (0, j))
    b_spec = pl.BlockSpec((1, LANES), lambda j, i: (0, j))
    half = jax.ShapeDtypeStruct(gate_pre.shape, BF16)
    dgate, dup, dw, db = pl.pallas_call(
        body, name="ffn_bwd", grid=(f // LANES, b),
        in_specs=[blk, blk, w_spec, b_spec, blk],
        out_specs=[blk, blk, w_spec, b_spec],
        out_shape=[half, half, jax.ShapeDtypeStruct((taps_n, f), F32), jax.ShapeDtypeStruct((1, f), F32)],
        compiler_params=_params("arbitrary", "arbitrary"),
    )(gate_pre, up, conv_w, conv_b.reshape(1, f), dact)
    return dgate, dup, dw, db.reshape(f)


def gate_arrays(logits, a_log, dt_bias, b, s, heads):
    n, r = s // DN_CHUNK, min(UT_ROWS, s)
    lg = logits.reshape(b, s, -1)
    beta = jax.nn.sigmoid(lg[..., :heads])
    g = -jnp.exp(a_log) * jax.nn.softplus(lg[..., heads:2 * heads] + dt_bias)
    gc = jnp.cumsum(g.reshape(b, n, DN_CHUNK, heads), axis=2).transpose(0, 3, 1, 2)
    return (beta.transpose(0, 2, 1)[..., None], gc.reshape(b, heads, s, 1), gc[:, :, :, None, :],
            gc.reshape(b, heads, s // r, 1, r))


class Riders:
    def __init__(self, make_ride, groups, arrays, tag):
        self.groups, self.results = groups, {}
        self.rides = {host: make_ride([arrays[n] for n in members], f"{tag}_on_{host}")
                      for host, members in groups.items()}

    def run(self, host, call):
        value, outs = call(self.rides[host])
        self.results.update(zip(self.groups[host], outs))
        return value


def _hosted_matmul(riders, host, *args, **kwargs):
    if riders is None:
        return matmul(*args, **kwargs)
    return riders.run(host, lambda ride: matmul(*args, ride=ride, **kwargs))


FORWARD_HOSTS = {'in_proj': ('w_in',), 'ut': ('w_gate', 'w_up'), 'gate': ('w_xkv',),
                 'up': ('w_mix_out', 'w_xq', 'w_xo', 'w_pool', 'dn_conv_w', 'ffn_conv_w'), 'down': ('w_down',)}
BACKWARD_HOSTS = {'down_wgrad': ('w_down',), 'gate_dgrad': ('w_xkv',), 'up_dgrad': ('w_in',),
                  'gate_wgrad': ('w_gate',), 'up_wgrad': ('w_up',),
                  'in_dgrad': ('w_mix_out', 'w_xq', 'w_xo', 'w_pool', 'dn_conv_w', 'ffn_conv_w')}


def forward_layer(x, mem_hb, p, b, s, riders=None):
    d = x.shape[1]
    pw = d // 2
    dn = d - pw
    heads = dn // DN_HEAD_DIM
    sv = {'x0': x}
    sv['h1'] = h1 = norm_fwd(x, p['mix_norm_g'])
    sv['proj'] = proj = _hosted_matmul(riders, 'in_proj', h1, p['w_in_main'], 'nn', F32)
    logits = matmul(h1, p['w_in_logits'], 'nn', F32)
    y_pool = pool_fwd(proj, p['w_pool'], p['pool_scale'], b, s)
    gates, sv['gates_pull'] = jax.vjp(lambda lg, al, dtb: gate_arrays(lg, al, dtb, b, s, heads), logits,
                                      p['dn_a_log'], p['dn_dt_bias'])
    sv['gates'] = beta_col, gc_col, gc_row, gc_row_ut = gates
    sv['act'] = aq, ak, av = [conv_silu_fwd(proj, p['dn_conv_w'], part, pw, dn, b, s) for part in range(3)]
    if riders is None:
        u, w, sv['inv'] = ut_fwd(ak, av, beta_col, gc_col, gc_row_ut, b, s)
    else:
        def hosted(ride):
            *own, outs = ut_fwd(ak, av, beta_col, gc_col, gc_row_ut, b, s, ride=ride)
            return own, outs
        u, w, sv['inv'] = riders.run('ut', hosted)
    sv['u'], sv['w'] = u, w
    sv['o_dn'], sv['states'] = o_dn, _ = rec_fwd(aq, ak, gc_col, gc_row, u, w, b, s)
    y_dn = dn_norm_fwd(o_dn, proj, p['dn_norm_g'], pw + 3 * dn)
    sv['mixed'] = mixed = jnp.concatenate([y_pool, y_dn], axis=1)
    sv['x1'] = x1 = matmul(mixed, p['w_mix_out'], 'nn', F32, res=x)

    sv['h2'] = h2 = norm_fwd(x1, p['xa_norm_g'])
    sv['q'] = q = matmul(h2, p['w_xq'], 'nn', F32)
    sv['kv'] = kv = matmul(mem_hb, p['w_xkv'], 'nn', F32)
    sv['o_at'] = o_at = attn_fwd(q, kv, b, s)
    sv['x2'] = x2 = matmul(o_at, p['w_xo'], 'nn', F32, res=x1)

    sv['h3'] = h3 = norm_fwd(x2, p['ffn_norm_g'])
    sv['gate_pre'] = gate_pre = _hosted_matmul(riders, 'gate', h3, p['w_gate'], 'nn', F32)
    sv['up'] = up = _hosted_matmul(riders, 'up', h3, p['w_up'], 'nn', F32)
    sv['a_ffn'] = a_ffn = ffn_fwd(gate_pre, up, p['ffn_conv_w'], p['ffn_conv_b'], b, s)
    return _hosted_matmul(riders, 'down', a_ffn, p['w_down'], 'nn', F32, res=x2), sv


def backward_layer(dx, dxb, sv, mem_hb, p, b, s, riders=None):
    d = dx.shape[1]
    pw = d // 2
    dn = d - pw
    g = {}
    da = matmul(dxb, p['w_down'], 'nt', F32)
    g['w_down'] = _hosted_matmul(riders, 'down_wgrad', sv['a_ffn'], dxb, 'tn', BF16)
    dgate, dup, g['ffn_conv_w'], g['ffn_conv_b'] = ffn_bwd(sv['gate_pre'], sv['up'], p['ffn_conv_w'],
                                                           p['ffn_conv_b'], da, b, s)
    dh = _hosted_matmul(riders, 'gate_dgrad', dgate, p['w_gate'], 'nt', F32)
    dh = _hosted_matmul(riders, 'up_dgrad', dup, p['w_up'], 'nt', F32, res=dh)
    g['w_gate'] = _hosted_matmul(riders, 'gate_wgrad', sv['h3'], dgate, 'tn', BF16)
    g['w_up'] = _hosted_matmul(riders, 'up_wgrad', sv['h3'], dup, 'tn', BF16)
    dx, dxb, g['ffn_norm_g'] = norm_bwd(sv['x2'], p['ffn_norm_g'], dh, dx)

    do = matmul(dxb, p['w_xo'], 'nt', F32)
    g['w_xo'] = matmul(sv['o_at'], dxb, 'tn', BF16)
    dq, dk, dv = attn_bwd(sv['q'], sv['kv'], do, b, s)
    dkv = jnp.concatenate([dk, dv], axis=1).astype(BF16)
    dh = matmul(dq, p['w_xq'], 'nt', F32)
    g['w_xq'] = matmul(sv['h2'], dq, 'tn', BF16)
    g['w_xkv'] = matmul(mem_hb, dkv, 'tn', BF16)
    dmem_h = matmul(dkv, p['w_xkv'], 'nt', F32)
    dx, dxb, g['xa_norm_g'] = norm_bwd(sv['x1'], p['xa_norm_g'], dh, dx)

    dmixed = matmul(dxb, p['w_mix_out'], 'nt', F32)
    g['w_mix_out'] = matmul(sv['mixed'], dxb, 'tn', BF16)
    proj = sv['proj']
    du_pool, g['w_pool'], g['pool_scale'] = pool_bwd(proj, p['w_pool'], p['pool_scale'], dmixed, b, s)
    do_dn, dz, g['dn_norm_g'] = dn_norm_bwd(sv['o_dn'], proj, p['dn_norm_g'], dmixed, pw + 3 * dn, pw)
    beta_col, gc_col, gc_row, gc_row_ut = sv['gates']
    aq, ak, av = sv['act']
    daq, dak_rec, du, dw, dgcc_rec, dgcr = rec_bwd(aq, ak, gc_col, gc_row, sv['u'], sv['w'], sv['states'],
                                                   do_dn, b, s)
    dak, dav, dbeta, dgcc_ut, dgcr_ut = ut_bwd(ak, av, beta_col, gc_col, gc_row_ut, sv['inv'], du, dw, dak_rec, b, s)
    dparts, dtaps = zip(*[conv_silu_bwd(proj, p['dn_conv_w'], dact, part, pw, dn, b, s)
                          for part, dact in enumerate((daq, dak, dav))])
    g['dn_conv_w'] = jnp.concatenate(dtaps, axis=1)
    dlogits, g['dn_a_log'], g['dn_dt_bias'] = sv['gates_pull']((dbeta, dgcc_rec + dgcc_ut, dgcr, dgcr_ut))
    dproj = jnp.concatenate([du_pool, *dparts, dz], axis=1)
    dlogits = dlogits.astype(BF16)
    dh = _hosted_matmul(riders, 'in_dgrad', dproj, p['w_in_main'], 'nt', F32,
                        res=matmul(dlogits, p['w_in_logits'], 'nt', F32))
    g['w_in_main'] = matmul(sv['h1'], dproj, 'tn', BF16)
    g['w_in_logits'] = matmul(sv['h1'], dlogits, 'tn', BF16)
    dx, dxb, g['mix_norm_g'] = norm_bwd(sv['x0'], p['mix_norm_g'], dh, dx)
    return dx, dxb, dmem_h, g


def _to_full(gathered, axis):
    moved = jnp.moveaxis(gathered, 0, axis)
    shape = list(gathered.shape[1:])
    shape[axis] *= N_DEV
    return moved.reshape(shape)


def _to_parts(full, axis):
    shape = list(full.shape)
    shape[axis:axis + 1] = [N_DEV, shape[axis] // N_DEV]
    return jnp.moveaxis(full.reshape(shape), axis, 0)


def _as_pack_rows(a, cols):
    rows = -(-a.shape[0] // (8 * cols)) * 8
    return jnp.pad(a, (0, rows * cols - a.shape[0])).reshape(rows, cols)


def kernel(x, mem, mix_norm_g, w_in, w_pool, pool_scale, dn_conv_w, dn_a_log, dn_dt_bias, dn_norm_g, w_mix_out, xa_norm_g, mem_norm_g, w_xq, w_xkv, w_xo, ffn_norm_g, w_gate, w_up, ffn_conv_w, ffn_conv_b, w_down, final_norm_g, loss_target, m_mix_norm_g, m_w_in, m_w_pool, m_pool_scale, m_dn_conv_w, m_dn_a_log, m_dn_dt_bias, m_dn_norm_g, m_w_mix_out, m_xa_norm_g, m_mem_norm_g, m_w_xq, m_w_xkv, m_w_xo, m_ffn_norm_g, m_w_gate, m_w_up, m_ffn_conv_w, m_ffn_conv_b, m_w_down, m_final_norm_g, v_mix_norm_g, v_w_in, v_w_pool, v_pool_scale, v_dn_conv_w, v_dn_a_log, v_dn_dt_bias, v_dn_norm_g, v_w_mix_out, v_xa_norm_g, v_mem_norm_g, v_w_xq, v_w_xkv, v_w_xo, v_ffn_norm_g, v_w_gate, v_w_up, v_ffn_conv_w, v_ffn_conv_b, v_w_down, v_final_norm_g):
    given = dict(locals())
    w = {n: given[n] for n in WEIGHTS}
    mom = {n: given['m_' + n] for n in WEIGHTS}
    var = {n: given['v_' + n] for n in WEIGHTS}
    b, s, d = x.shape
    depth = w_in.shape[0]
    main = 5 * (d // 2)
    n_logits = w_in.shape[-1] * N_DEV - main
    core = jnp.reshape(lax.axis_index("c"), (1,)).astype(jnp.int32)

    names = MATRICES + TAPS

    def shards(l):
        return {n: w[n][l] if n in TAPS else w[n][l].astype(BF16) for n in names}

    def layer_weights(l, gathered):
        full = {n: _to_full(gathered[n], SHARD_AXIS[n] - 1) for n in names}
        p = {n: full[n] for n in names if n != 'w_in'}
        p['w_in_main'] = full['w_in'][:, :main]
        p['w_in_logits'] = jnp.pad(full['w_in'][:, main:], ((0, 0), (0, LANES - n_logits)))
        for n in PER_LAYER_REPLICATED:
            p[n] = w[n][l]
        return p

    mem2 = mem.reshape(-1, d)
    mem_hb = norm_fwd(mem2, mem_norm_g)
    xc, saved, layers = x.reshape(b * s, d), [], []
    gathered = dict(zip(names, all_gather([shards(0)[n] for n in names], "layer0")))
    for l in range(depth):
        layers.append(layer_weights(l, gathered))
        riders = Riders(gather_ride, FORWARD_HOSTS, shards(l + 1), f"layer{l + 1}") if l + 1 < depth else None
        xc, sv = forward_layer(xc, mem_hb, layers[l], b, s, riders)
        saved.append(sv)
        gathered = riders.results if riders else None
    loss, dx, dxb, g_final = loss_head(xc, loss_target.reshape(b * s, d), final_norm_g)

    def as_layers(a, cols):
        return a.reshape(depth, -1, cols)

    results = {n: None for n in names}

    def update(l, got):
        for n in names:
            cols = w[n].shape[-1]
            results[n] = adamw(got[n], as_layers(w[n], cols), as_layers(mom[n], cols), as_layers(var[n], cols),
                               layer=l, prev=results[n])

    g_layers, dmem_h, chip_sums = [None] * depth, None, None
    for l in reversed(range(depth)):
        riders = Riders(chip_exchange_ride, BACKWARD_HOSTS, chip_sums, f"layer{l + 1}") if chip_sums else None
        dx, dxb, dm, g = backward_layer(dx, dxb, saved[l], mem_hb, layers[l], b, s, riders)
        if riders:
            update(l + 1, riders.results)
        g_layers[l] = g
        dmem_h = dm if dmem_h is None else dmem_h + dm
        g_full = {n: g[n] for n in names if n not in ('w_in', 'w_pool')}
        g_full['w_pool'] = g['w_pool'].astype(BF16)
        g_full['w_in'] = jnp.concatenate([g['w_in_main'], g['w_in_logits'][:, :n_logits]], axis=1)
        parts = []
        for n in names:
            by_dev = _to_parts(g_full[n], SHARD_AXIS[n] - 1)
            parts.append(by_dev.reshape(N_DEV, -1, by_dev.shape[-1]))
        from_sibling = sibling_swap(parts, f"layer{l}")
        chip_sums = {n: chip_sum(a, f, core) for n, a, f in zip(names, parts, from_sibling)}
    update(0, dict(zip(names, chip_exchange_ride([chip_sums[n] for n in names], "layer0").alone())))
    _, _, g_mem = norm_bwd(mem2, mem_norm_g, dmem_h, None)
    grad_x = dx.reshape(b, s, d)

    grad, delta, new_m, new_v = {}, {}, {}, {}
    for n in names:
        grad[n], delta[n], new_m[n], new_v[n] = [o.reshape(w[n].shape) for o in results[n]]

    g_small = {n: jnp.stack([g[n] for g in g_layers]) for n in PER_LAYER_REPLICATED}
    g_small['mem_norm_g'], g_small['final_norm_g'] = g_mem, g_final
    sizes = [w[n].size for n in REPLICATED]

    def pack(parts, first):
        flat = jnp.concatenate([jnp.reshape(first, (1,))] + [parts[n].reshape(-1) for n in REPLICATED])
        return _as_pack_rows(flat, LANES)

    zero = jnp.zeros((), F32)
    everyone, = all_gather([pack(g_small, loss)], "replicated")
    outs = adamw(everyone, pack(w, zero)[None], pack(mom, zero)[None], pack(var, zero)[None])
    flat_outs = [o.reshape(-1) for o in outs]
    loss_total = flat_outs[0][0]
    offset = 1
    for n, size in zip(REPLICATED, sizes):
        grad[n], delta[n], new_m[n], new_v[n] = [o[offset:offset + size].reshape(w[n].shape) for o in flat_outs]
        offset += size

    return (loss_total, grad_x, *[grad[n] for n in WEIGHTS], *[delta[n] for n in WEIGHTS],
            *[new_m[n] for n in WEIGHTS], *[new_v[n] for n in WEIGHTS])
```

```python
import functools

import jax
import jax.numpy as jnp
from jax import lax
from jax.experimental import pallas as pl
from jax.experimental.pallas import tpu as pltpu

F32 = jnp.float32
BF16 = jnp.bfloat16
MESH = pl.DeviceIdType.MESH

N_DEV = 8
EPS = 1e-6
POOL_WINDOWS = (2, 4, 8, 16)
DN_HEAD_DIM = 128
DN_CHUNK = 64
DN_TAPS = 4
XA_HEADS = 4
ADAM_LR = 0.001
ADAM_B1 = 0.9
ADAM_B2 = 0.999
ADAM_EPS = 1e-08
ADAM_WD = 0.01
ADAM_STEP = 10

LANES = 128
VMEM_LIMIT = 48 * 1024 * 1024
UT_ROWS = 256
FFN_COLS = 256

WEIGHTS = ['mix_norm_g', 'w_in', 'w_pool', 'pool_scale', 'dn_conv_w', 'dn_a_log', 'dn_dt_bias', 'dn_norm_g',
           'w_mix_out', 'xa_norm_g', 'mem_norm_g', 'w_xq', 'w_xkv', 'w_xo', 'ffn_norm_g', 'w_gate', 'w_up',
           'ffn_conv_w', 'ffn_conv_b', 'w_down', 'final_norm_g']
SHARD_AXIS = {'w_in': 2, 'w_pool': 2, 'dn_conv_w': 2, 'w_mix_out': 1, 'w_xq': 1, 'w_xkv': 2, 'w_xo': 1,
              'w_gate': 2, 'w_up': 2, 'ffn_conv_w': 2, 'w_down': 1}
MATRICES = ('w_in', 'w_pool', 'w_mix_out', 'w_xq', 'w_xkv', 'w_xo', 'w_gate', 'w_up', 'w_down')
TAPS = ('dn_conv_w', 'ffn_conv_w')
REPLICATED = [n for n in WEIGHTS if n not in SHARD_AXIS]
PER_LAYER_REPLICATED = ('mix_norm_g', 'pool_scale', 'dn_a_log', 'dn_dt_bias', 'dn_norm_g', 'xa_norm_g',
                        'ffn_norm_g', 'ffn_conv_b')


def _params(*semantics):
    return pltpu.CompilerParams(dimension_semantics=semantics, vmem_limit_bytes=VMEM_LIMIT)


def _tile(dim, pref, unit=LANES):
    if dim <= pref:
        return dim
    t = (pref // unit) * unit
    while t >= unit:
        if dim % t == 0:
            return t
        t -= unit
    return dim


class Ride:
    def __init__(self, tag, arrays, out_shapes, n_sems, n_local, start, finish):
        self.tag, self.arrays, self.out_shapes = tag, list(arrays), list(out_shapes)
        self.start, self.finish = start, finish
        self.scratch = [pltpu.SemaphoreType.DMA((n_sems,)), pltpu.SemaphoreType.DMA((n_sems,)),
                        pltpu.SemaphoreType.DMA((n_local,))]

    @property
    def specs_in(self):
        return [pl.BlockSpec(memory_space=pl.ANY)] * len(self.arrays)

    @property
    def specs_out(self):
        return [pl.BlockSpec(memory_space=pl.ANY)] * len(self.out_shapes)

    def alone(self):
        n = len(self.arrays)

        def body(*refs):
            ins, outs, sems = refs[:n], refs[n:n + len(self.out_shapes)], refs[n + len(self.out_shapes):]
            self.start(ins, outs, *sems)
            self.finish(ins, outs, *sems)

        return pl.pallas_call(body, name=self.tag, out_shape=self.out_shapes, in_specs=self.specs_in,
                              out_specs=self.specs_out, scratch_shapes=self.scratch)(*self.arrays)


def _first_and_last_step(grid):
    ids = [pl.program_id(axis) for axis in range(len(grid))]
    first = functools.reduce(jnp.logical_and, [i == 0 for i in ids])
    last = functools.reduce(jnp.logical_and, [i == n - 1 for i, n in zip(ids, grid)])
    return first, last


def matmul(a, b, mode, out_dtype, res=None, ride=None):
    assert a.dtype == BF16 and b.dtype == BF16, (a.dtype, b.dtype)
    if mode == 'nn':
        (m, c), (c2, n) = a.shape, b.shape
    elif mode == 'nt':
        (m, c), (n, c2) = a.shape, b.shape
    else:
        (c, m), (c2, n) = a.shape, b.shape
    assert c == c2, (mode, a.shape, b.shape)
    tm, tn, tc = _tile(m, 1024), _tile(n, 1408 if mode == 'tn' else 512), _tile(c, 2048)
    nc = c // tc
    if mode == 'nn':
        a_spec = pl.BlockSpec((tm, tc), lambda i, j, k: (i, k))
        b_spec = pl.BlockSpec((tc, tn), lambda i, j, k: (k, j))
        dims = (((1,), (0,)), ((), ()))
    elif mode == 'nt':
        a_spec = pl.BlockSpec((tm, tc), lambda i, j, k: (i, k))
        b_spec = pl.BlockSpec((tn, tc), lambda i, j, k: (j, k))
        dims = (((1,), (1,)), ((), ()))
    else:
        a_spec = pl.BlockSpec((tc, tm), lambda i, j, k: (k, i))
        b_spec = pl.BlockSpec((tc, tn), lambda i, j, k: (k, j))
        dims = (((0,), (0,)), ((), ()))
    out_spec = pl.BlockSpec((tm, tn), lambda i, j, k: (i, j))
    has_res = res is not None
    grid = (m // tm, n // tn, nc)
    n_ride_in = len(ride.arrays) if ride else 0
    n_ride_out = len(ride.out_shapes) if ride else 0

    def body(a_ref, b_ref, *rest):
        rest = list(rest)
        r_ref = rest.pop(0) if has_res else None
        ride_in = [rest.pop(0) for _ in range(n_ride_in)]
        o_ref = rest.pop(0)
        ride_out = [rest.pop(0) for _ in range(n_ride_out)]
        acc_ref = rest.pop(0) if nc > 1 else None
        if ride:
            first, last = _first_and_last_step(grid)
            pl.when(first)(lambda: ride.start(ride_in, ride_out, *rest))
        prod = lax.dot_general(a_ref[...], b_ref[...], dims, preferred_element_type=F32)

        def finish(total):
            if has_res:
                total = total + r_ref[...]
            o_ref[...] = total.astype(o_ref.dtype)

        if nc == 1:
            finish(prod)
        else:
            k = pl.program_id(2)

            @pl.when(k == 0)
            def _():
                acc_ref[...] = prod

            @pl.when(k > 0)
            def _():
                acc_ref[...] += prod

            @pl.when(k == nc - 1)
            def _():
                finish(acc_ref[...])

        if ride:
            pl.when(last)(lambda: ride.finish(ride_in, ride_out, *rest))

    outs = pl.pallas_call(
        body,
        name=f"mm_{mode}_{m}x{c}x{n}" + ("_res" if has_res else "") + ("_with_" + ride.tag if ride else ""),
        grid=grid,
        in_specs=[a_spec, b_spec] + ([out_spec] if has_res else []) + (ride.specs_in if ride else []),
        out_specs=[out_spec] + (ride.specs_out if ride else []),
        out_shape=[jax.ShapeDtypeStruct((m, n), out_dtype)] + (ride.out_shapes if ride else []),
        scratch_shapes=([] if nc == 1 else [pltpu.VMEM((tm, tn), F32)]) + (ride.scratch if ride else []),
        compiler_params=(_params("arbitrary", "arbitrary", "arbitrary") if ride
                         else _params("parallel", "parallel", "arbitrary")),
    )(a, b, *([res] if has_res else []), *(ride.arrays if ride else []))
    return (outs[0], outs[1:]) if ride else outs[0]


def _position():
    return lax.axis_index("x"), lax.axis_index("y"), lax.axis_index("c")


def _flip(v, bit):
    return 1 - v if bit else v


def _dev_index(px, py, pc):
    return 4 * px + 2 * py + pc


def gather_ride(shards, tag):
    n = len(shards)

    def plan(x_refs, out_refs, send_sems, recv_sems, local_sems):
        x, y, c = _position()
        me = (x, y, c)

        def copy(t, k, block, to, own=False):
            slot = out_refs[t].at[_dev_index(*block)]
            return pltpu.make_async_remote_copy(
                src_ref=x_refs[t] if own else slot, dst_ref=slot, send_sem=send_sems.at[7 * t + k],
                recv_sem=recv_sems.at[7 * t + k], device_id=to, device_id_type=MESH)

        def local(t):
            return pltpu.make_async_copy(x_refs[t], out_refs[t].at[_dev_index(*me)], local_sems.at[t])

        return me, (x, y, 1 - c), [(1 - x, y), (x, 1 - y), (1 - x, 1 - y)], copy, local

    def start(x_refs, out_refs, *sems):
        me, sibling, chips, copy, local = plan(x_refs, out_refs, *sems)
        for t in range(n):
            local(t).start()
            copy(t, 0, me, sibling, own=True).start()
            for j, chip in enumerate(chips):
                copy(t, 1 + j, me, (*chip, me[2]), own=True).start()

    def finish(x_refs, out_refs, *sems):
        me, sibling, chips, copy, local = plan(x_refs, out_refs, *sems)
        c = me[2]
        for j, chip in enumerate(chips):
            for t in range(n):
                copy(t, 1 + j, (*chip, c), me).wait_recv()
                copy(t, 4 + j, (*chip, c), sibling).start()
        for t in range(n):
            copy(t, 0, sibling, me).wait_recv()
            for j, chip in enumerate(chips):
                copy(t, 4 + j, (*chip, 1 - c), me).wait_recv()
        for t in range(n):
            copy(t, 0, me, sibling, own=True).wait_send()
            for j, chip in enumerate(chips):
                copy(t, 1 + j, me, (*chip, c), own=True).wait_send()
                copy(t, 4 + j, (*chip, c), sibling).wait_send()
            local(t).wait()

    return Ride("all_gather_" + tag, shards, [jax.ShapeDtypeStruct((N_DEV,) + a.shape, a.dtype) for a in shards],
                7 * n, n, start, finish)


def all_gather(shards, tag):
    return gather_ride(shards, tag).alone()


def sibling_swap_ride(parts, tag):
    n = len(parts)

    def plan(p_refs, out_refs, send_sems, recv_sems, _):
        x, y, c = _position()
        return [pltpu.make_async_remote_copy(
            src_ref=p_refs[t].at[2 * k + (1 - c)], dst_ref=out_refs[t].at[k], send_sem=send_sems.at[4 * t + k],
            recv_sem=recv_sems.at[4 * t + k], device_id=(x, y, 1 - c), device_id_type=MESH)
            for t in range(n) for k in range(N_DEV // 2)]

    def start(p_refs, out_refs, *sems):
        for cp in plan(p_refs, out_refs, *sems):
            cp.start()

    def finish(p_refs, out_refs, *sems):
        for cp in plan(p_refs, out_refs, *sems):
            cp.wait()

    return Ride("sibling_swap_" + tag, parts,
                [jax.ShapeDtypeStruct((N_DEV // 2,) + a.shape[1:], a.dtype) for a in parts], 4 * n, 1, start, finish)


def chip_sum(parts, from_sibling, core):
    _, r, c = parts.shape
    unit = 16 if parts.dtype == BF16 else 8
    tr = _tile(r, max(unit, (512 * 1024 // c) // unit * unit), unit)

    def body(core_ref, a_ref, b_ref, o_ref):
        o_ref[...] = (a_ref[...].astype(F32) + b_ref[...].astype(F32)).astype(o_ref.dtype)

    blk = pl.BlockSpec((None, tr, c), lambda k, i, core_ref: (k, i, 0))
    return pl.pallas_call(
        body, name=f"chip_sum_{r}x{c}_{parts.dtype.name}",
        grid_spec=pltpu.PrefetchScalarGridSpec(
            num_scalar_prefetch=1, grid=(N_DEV // 2, r // tr),
            in_specs=[pl.BlockSpec((None, tr, c), lambda k, i, core_ref: (2 * k + core_ref[0], i, 0)), blk],
            out_specs=blk),
        out_shape=jax.ShapeDtypeStruct(from_sibling.shape, from_sibling.dtype),
        compiler_params=_params("parallel", "parallel"),
    )(core, parts, from_sibling)


def chip_exchange_ride(parts, tag):
    n = len(parts)

    def plan(p_refs, out_refs, send_sems, recv_sems, local_sems):
        x, y, c = _position()
        my_chip = 2 * x + y
        local = [pltpu.make_async_copy(p_refs[t].at[my_chip], out_refs[t].at[my_chip], local_sems.at[t])
                 for t in range(n)]
        copies = []
        for t in range(n):
            for k in (1, 2, 3):
                px, py = _flip(x, k & 2), _flip(y, k & 1)
                copies.append(pltpu.make_async_remote_copy(
                    src_ref=p_refs[t].at[2 * px + py], dst_ref=out_refs[t].at[my_chip],
                    send_sem=send_sems.at[3 * t + k - 1], recv_sem=recv_sems.at[3 * t + k - 1],
                    device_id=(px, py, c), device_id_type=MESH))
        return local, copies

    def start(p_refs, out_refs, *sems):
        local, copies = plan(p_refs, out_refs, *sems)
        for cp in local + copies:
            cp.start()

    def finish(p_refs, out_refs, *sems):
        local, copies = plan(p_refs, out_refs, *sems)
        for cp in copies:
            cp.wait_recv()
        for cp in copies:
            cp.wait_send()
        for cp in local:
            cp.wait()

    return Ride("chip_exchange_" + tag, parts, [jax.ShapeDtypeStruct(a.shape, a.dtype) for a in parts],
                3 * n, n, start, finish)


def adamw(parts, w, m, v, layer=0, prev=None):
    n_parts, r, c = parts.shape
    depth = w.shape[0]
    unit = 16 if parts.dtype == BF16 else 8
    tr = _tile(r, max(unit, (256 * 1024 // c) // unit * unit), unit)

    def body(p_ref, w_ref, m_ref, v_ref, *rest):
        g_ref, d_ref, nm_ref, nv_ref = rest[-4:]
        g = p_ref[0].astype(F32)
        for j in range(1, n_parts):
            g = g + p_ref[j].astype(F32)
        nm = ADAM_B1 * m_ref[...] + (1.0 - ADAM_B1) * g
        nv = ADAM_B2 * v_ref[...] + (1.0 - ADAM_B2) * jnp.square(g)
        m_hat = nm / (1.0 - ADAM_B1 ** ADAM_STEP)
        v_hat = nv / (1.0 - ADAM_B2 ** ADAM_STEP)
        g_ref[...] = g
        d_ref[...] = -ADAM_LR * (m_hat / (jnp.sqrt(v_hat) + ADAM_EPS) + ADAM_WD * w_ref[...])
        nm_ref[...] = nm
        nv_ref[...] = nv

    spec = pl.BlockSpec((None, tr, c), lambda i: (layer, i, 0))
    out = jax.ShapeDtypeStruct((depth, r, c), F32)
    carried = [] if prev is None else list(prev)
    return pl.pallas_call(
        body,
        name=f"adamw_{n_parts}x{r}x{c}_{parts.dtype.name}_layer{layer}",
        grid=(r // tr,),
        in_specs=[pl.BlockSpec((n_parts, tr, c), lambda i: (0, i, 0)), spec, spec, spec]
        + [pl.BlockSpec(memory_space=pl.ANY)] * len(carried),
        out_specs=[spec, spec, spec, spec],
        out_shape=[out, out, out, out],
        input_output_aliases={4 + k: k for k in range(len(carried))},
        compiler_params=_params("parallel"),
    )(parts, w, m, v, *carried)


_NN = (((1,), (0,)), ((), ()))
_NT = (((1,), (1,)), ((), ()))
_TN = (((0,), (0,)), ((), ()))


def _dot_bf16(a, b, dims):
    return lax.dot_general(a.astype(BF16), b.astype(BF16), dims, preferred_element_type=F32)


def _dot_split(a, b, dims):
    a_hi, b_hi = a.astype(BF16), b.astype(BF16)
    a_lo = (a - a_hi.astype(F32)).astype(BF16)
    b_lo = (b - b_hi.astype(F32)).astype(BF16)
    dot = functools.partial(lax.dot_general, dimension_numbers=dims, preferred_element_type=F32)
    return dot(a_hi, b_hi) + (dot(a_hi, b_lo) + dot(a_lo, b_hi))


def _matmul_family(dot):
    @jax.custom_vjp
    def nn(a, b):
        return dot(a, b, _NN)

    @jax.custom_vjp
    def nt(a, b):
        return dot(a, b, _NT)

    @jax.custom_vjp
    def tn(a, b):
        return dot(a, b, _TN)

    nn.defvjp(lambda a, b: (nn(a, b), (a, b)), lambda r, g: (nt(g, r[1]), tn(r[0], g)))
    nt.defvjp(lambda a, b: (nt(a, b), (a, b)), lambda r, g: (nn(g, r[1]), tn(g, r[0])))
    tn.defvjp(lambda a, b: (tn(a, b), (a, b)), lambda r, g: (nt(r[1], g), nn(r[0], g)))
    return nn, nt, tn


mm, mm_nt, mm_tn = _matmul_family(_dot_bf16)
mms, mms_nt, mms_tn = _matmul_family(_dot_split)


def _shift_rows(x, k, down):
    n = x.shape[0]
    rows = lax.broadcasted_iota(jnp.int32, x.shape, 0)
    if down:
        return jnp.where(rows >= k, pltpu.roll(x, k, 0), 0.0)
    return jnp.where(rows < n - k, pltpu.roll(x, n - k, 0), 0.0)


@functools.partial(jax.custom_vjp, nondiff_argnums=(1,))
def delay(x, k):
    return _shift_rows(x, k, True) if k else x


delay.defvjp(lambda x, k: (delay(x, k), None), lambda k, _, g: ((_shift_rows(g, k, False) if k else g),))


def _silu(x):
    return x * jax.nn.sigmoid(x)


def _rms(x, g):
    return x * lax.rsqrt(jnp.mean(x * x, axis=-1, keepdims=True) + EPS) * g


def _l2n(t):
    return t * lax.rsqrt(jnp.sum(t * t, axis=-1, keepdims=True) + EPS)


def _causal_conv(x, taps):
    k_taps = len(taps)
    y = delay(x, k_taps - 1) * taps[0]
    for k in range(1, k_taps):
        y = y + delay(x, k_taps - 1 - k) * taps[k]
    return y


def _pool_block(u, w, scale, group):
    sums, acc, width = [], u, 1
    while width < POOL_WINDOWS[-1]:
        acc = acc + delay(acc, width)
        width *= 2
        sums.append(acc)
    picked = sums[-1]
    for i in range(len(POOL_WINDOWS) - 2, -1, -1):
        picked = jnp.where(group == i, sums[i], picked)
    rows = lax.broadcasted_iota(jnp.int32, u.shape, 0)
    count = jnp.minimum(rows + 1, jnp.left_shift(2, group)).astype(F32)
    return mm(picked / count - u, w) * scale


def _unit_lower_inverse(l_mat):
    n = l_mat.shape[0]
    eye = (lax.broadcasted_iota(jnp.int32, (n, n), 0) == lax.broadcasted_iota(jnp.int32, (n, n), 1)).astype(F32)
    m1 = -l_mat
    m2 = mms(m1, m1)
    m4 = mms(m2, m2)
    m8 = mms(m4, m4)
    m16 = mms(m8, m8)
    m32 = mms(m16, m16)
    low = mms(eye + m1, eye + m2)
    mid = mms(eye + m4, eye + m8)
    high = mms(eye + m16, eye + m32)
    return mms(mms(low, mid), high)


@jax.custom_vjp
def _known_inverse(l_mat, inv):
    return inv


_known_inverse.defvjp(lambda l_mat, inv: (inv, inv),
                      lambda inv, g: (-mms_tn(inv, mms_nt(g, inv)), jnp.zeros_like(inv)))


def _ut_chain(k, v, beta, gc_c, gc_r, inv=None):
    r = k.shape[0]
    kn = _l2n(k)
    row = lax.broadcasted_iota(jnp.int32, (r, r), 0)
    col = lax.broadcasted_iota(jnp.int32, (r, r), 1)
    strict = (row // DN_CHUNK == col // DN_CHUNK) & (row > col)
    decay = jnp.exp(jnp.where(strict, gc_c - gc_r, -1e30))
    l_mat = jnp.where(strict, beta * mm_nt(kn, kn) * decay, 0.0)
    inv = _unit_lower_inverse(l_mat) if inv is None else _known_inverse(l_mat, inv)
    return mms(inv, v * beta), mms(inv, kn * (beta * jnp.exp(gc_c))), inv


def _rec_chain(q, k, gc_c, gc_r, u, w, state):
    ch, dh = q.shape
    qn = _l2n(q) * (dh ** -0.5)
    kn = _l2n(k)
    row = lax.broadcasted_iota(jnp.int32, (ch, ch), 0)
    col = lax.broadcasted_iota(jnp.int32, (ch, ch), 1)
    decay = jnp.exp(jnp.where(row >= col, gc_c - gc_r, -1e30))
    attn = mm_nt(qn, kn) * decay
    is_last = lax.broadcasted_iota(jnp.int32, gc_r.shape, 1) == ch - 1
    last = jnp.sum(jnp.where(is_last, gc_r, 0.0), axis=1, keepdims=True)
    v_new = u - mm(w, state)
    out = mm(qn * jnp.exp(gc_c), state) + mm(attn, v_new)
    return out, state * jnp.exp(last) + mm_tn(kn * jnp.exp(last - gc_c), v_new)


def _gated_norm(o, z, g):
    return _rms(o, g) * _silu(z)


def _attn_block(q, k, v):
    s = mm_nt(q, k) * (q.shape[-1] ** -0.5)
    p = jnp.exp(s - lax.stop_gradient(jnp.max(s, axis=-1, keepdims=True)))
    return mm(p / jnp.sum(p, axis=-1, keepdims=True), v)


def _ffn_block(gate_pre, up, taps, bias):
    return _silu(_causal_conv(gate_pre, taps) + bias) * up


def _accumulate(ref, value, first):
    @pl.when(first)
    def _():
        ref[...] = value

    @pl.when(jnp.logical_not(first))
    def _():
        ref[...] += value


def norm_fwd(x, g):
    t, d = x.shape
    tr = _tile(t, 256, 8)

    def body(x_ref, g_ref, h_ref):
        h_ref[...] = _rms(x_ref[...], g_ref[...]).astype(BF16)

    return pl.pallas_call(
        body, name=f"norm_fwd_{t}", grid=(t // tr,),
        in_specs=[pl.BlockSpec((tr, d), lambda i: (i, 0)), pl.BlockSpec((1, d), lambda i: (0, 0))],
        out_specs=pl.BlockSpec((tr, d), lambda i: (i, 0)),
        out_shape=jax.ShapeDtypeStruct((t, d), BF16), compiler_params=_params("parallel"),
    )(x, g.reshape(1, d))


def norm_bwd(x, g, dh, dres):
    t, d = x.shape
    tr = _tile(t, 256, 8)
    has_res = dres is not None

    def body(x_ref, g_ref, dh_ref, *rest):
        rest = list(rest)
        r_ref = rest.pop(0) if has_res else None
        dx_ref, dxb_ref, dg_ref = rest
        _, pull = jax.vjp(_rms, x_ref[...], g_ref[...])
        dx, dg = pull(dh_ref[...])
        if has_res:
            dx = dx + r_ref[...]
        dx_ref[...] = dx
        dxb_ref[...] = dx.astype(BF16)
        _accumulate(dg_ref, dg, pl.program_id(0) == 0)

    row = pl.BlockSpec((tr, d), lambda i: (i, 0))
    vec = pl.BlockSpec((1, d), lambda i: (0, 0))
    dx, dxb, dg = pl.pallas_call(
        body, name=f"norm_bwd_{t}" + ("_res" if has_res else ""), grid=(t // tr,),
        in_specs=[row, vec, row] + ([row] if has_res else []),
        out_specs=[row, row, vec],
        out_shape=[jax.ShapeDtypeStruct((t, d), F32), jax.ShapeDtypeStruct((t, d), BF16),
                   jax.ShapeDtypeStruct((1, d), F32)],
        compiler_params=_params("arbitrary"),
    )(x, g.reshape(1, d), dh, *([dres] if has_res else []))
    return dx, dxb, dg.reshape(d)


def loss_head(x, target, g):
    t, d = x.shape
    tr = _tile(t, 256, 8)

    def body(x_ref, t_ref, g_ref, l_ref, dx_ref, dxb_ref, dg_ref):
        tgt = t_ref[...]

        def block_loss(xv, gv):
            return 0.5 * jnp.sum(jnp.mean(jnp.square(_rms(xv, gv) - tgt), axis=-1))

        val, pull = jax.vjp(block_loss, x_ref[...], g_ref[...])
        dx, dg = pull(jnp.ones((), F32))
        dx_ref[...] = dx
        dxb_ref[...] = dx.astype(BF16)
        first = pl.program_id(0) == 0
        _accumulate(dg_ref, dg, first)
        _accumulate(l_ref, jnp.full((1, LANES), val, F32), first)

    row = pl.BlockSpec((tr, d), lambda i: (i, 0))
    vec = pl.BlockSpec((1, d), lambda i: (0, 0))
    loss, dx, dxb, dg = pl.pallas_call(
        body, name="loss_head", grid=(t // tr,),
        in_specs=[row, row, vec],
        out_specs=[pl.BlockSpec((1, LANES), lambda i: (0, 0)), row, row, vec],
        out_shape=[jax.ShapeDtypeStruct((1, LANES), F32), jax.ShapeDtypeStruct((t, d), F32),
                   jax.ShapeDtypeStruct((t, d), BF16), jax.ShapeDtypeStruct((1, d), F32)],
        compiler_params=_params("arbitrary"),
    )(x, target, g.reshape(1, d))
    return loss[0, 0], dx, dxb, dg.reshape(d)


def pool_fwd(proj, w_pool, scale, b, s):
    n_g, grp = w_pool.shape[0], w_pool.shape[-1]

    def body(u_ref, w_ref, s_ref, y_ref):
        y_ref[...] = _pool_block(u_ref[...], w_ref[...], s_ref[...], pl.program_id(1)).astype(BF16)

    blk = pl.BlockSpec((s, grp), lambda i, j: (i, j))
    return pl.pallas_call(
        body, name="pool_fwd", grid=(b, n_g),
        in_specs=[blk, pl.BlockSpec((None, grp, grp), lambda i, j: (j, 0, 0)),
                  pl.BlockSpec((1, grp), lambda i, j: (0, j))],
        out_specs=blk,
        out_shape=jax.ShapeDtypeStruct((b * s, n_g * grp), BF16), compiler_params=_params("parallel", "parallel"),
    )(proj, w_pool, scale.reshape(1, -1))


def pool_bwd(proj, w_pool, scale, dmixed, b, s):
    n_g, grp = w_pool.shape[0], w_pool.shape[-1]

    def body(u_ref, w_ref, s_ref, dy_ref, du_ref, dw_ref, ds_ref):
        group = pl.program_id(0)
        _, pull = jax.vjp(lambda u, w, sc: _pool_block(u, w, sc, group), u_ref[...], w_ref[...].astype(F32),
                          s_ref[...])
        du, dw, ds = pull(dy_ref[...])
        du_ref[...] = du.astype(BF16)
        first = pl.program_id(1) == 0
        _accumulate(dw_ref, dw, first)
        _accumulate(ds_ref, ds, first)

    blk = pl.BlockSpec((s, grp), lambda j, i: (i, j))
    w_spec = pl.BlockSpec((None, grp, grp), lambda j, i: (j, 0, 0))
    s_spec = pl.BlockSpec((1, grp), lambda j, i: (0, j))
    du, dw, ds = pl.pallas_call(
        body, name="pool_bwd", grid=(n_g, b),
        in_specs=[blk, w_spec, s_spec, blk],
        out_specs=[blk, w_spec, s_spec],
        out_shape=[jax.ShapeDtypeStruct((b * s, n_g * grp), BF16), jax.ShapeDtypeStruct(w_pool.shape, F32),
                   jax.ShapeDtypeStruct((1, n_g * grp), F32)],
        compiler_params=_params("arbitrary", "arbitrary"),
    )(proj, w_pool, scale.reshape(1, -1), dmixed)
    return du, dw, ds.reshape(-1)


def conv_silu_fwd(proj, conv_w, part, col0, width, b, s):
    x_off, w_off = (col0 + part * width) // LANES, part * width // LANES

    def body(x_ref, w_ref, y_ref):
        taps = [w_ref[k:k + 1, :] for k in range(DN_TAPS)]
        y_ref[...] = _silu(_causal_conv(x_ref[...], taps))

    return pl.pallas_call(
        body, name=f"conv_silu_fwd_{part}", grid=(width // LANES, b),
        in_specs=[pl.BlockSpec((s, LANES), lambda j, i: (i, x_off + j)),
                  pl.BlockSpec((DN_TAPS, LANES), lambda j, i: (0, w_off + j))],
        out_specs=pl.BlockSpec((s, LANES), lambda j, i: (i, j)),
        out_shape=jax.ShapeDtypeStruct((b * s, width), F32), compiler_params=_params("parallel", "parallel"),
    )(proj, conv_w)


def conv_silu_bwd(proj, conv_w, dact, part, col0, width, b, s):
    x_off, w_off = (col0 + part * width) // LANES, part * width // LANES

    def body(x_ref, w_ref, dy_ref, dx_ref, dw_ref):
        taps = [w_ref[k:k + 1, :] for k in range(DN_TAPS)]
        _, pull = jax.vjp(lambda x, *tp: _silu(_causal_conv(x, tp)), x_ref[...], *taps)
        dx, *dtaps = pull(dy_ref[...])
        dx_ref[...] = dx.astype(BF16)
        first = pl.program_id(1) == 0
        for k in range(DN_TAPS):
            _accumulate(dw_ref.at[k:k + 1, :], dtaps[k], first)

    out_blk = pl.BlockSpec((s, LANES), lambda j, i: (i, j))
    return pl.pallas_call(
        body, name=f"conv_silu_bwd_{part}", grid=(width // LANES, b),
        in_specs=[pl.BlockSpec((s, LANES), lambda j, i: (i, x_off + j)),
                  pl.BlockSpec((DN_TAPS, LANES), lambda j, i: (0, w_off + j)), out_blk],
        out_specs=[out_blk, pl.BlockSpec((DN_TAPS, LANES), lambda j, i: (0, j))],
        out_shape=[jax.ShapeDtypeStruct((b * s, width), BF16), jax.ShapeDtypeStruct((DN_TAPS, width), F32)],
        compiler_params=_params("arbitrary", "arbitrary"),
    )(proj, conv_w, dact)


def _ut_specs(b, s, heads, width):
    r = min(UT_ROWS, s)
    ns = s // r
    tok = pl.BlockSpec((r, width), lambda i, n: (i * ns + n, 0))
    col = pl.BlockSpec((None, heads, r, 1), lambda i, n: (i, 0, n, 0))
    row = pl.BlockSpec((None, heads, None, 1, r), lambda i, n: (i, 0, n, 0, 0))
    return r, ns, tok, col, row


def ut_fwd(act_k, act_v, beta_col, gc_col, gc_row, b, s, ride=None):
    width = act_k.shape[1]
    heads = width // DN_HEAD_DIM
    r, ns, tok, col, row = _ut_specs(b, s, heads, width)
    inv_spec = pl.BlockSpec((r, heads * r), lambda i, n: (i * ns + n, 0))

    n_ride_in = len(ride.arrays) if ride else 0

    def body(k_ref, v_ref, beta_ref, gcc_ref, gcr_ref, *rest):
        u_ref, w_ref, inv_ref = rest[n_ride_in:n_ride_in + 3]
        if ride:
            ride_in, ride_out, sems = rest[:n_ride_in], rest[n_ride_in + 3:-3], rest[-3:]
            first, last = _first_and_last_step((b, ns))
            pl.when(first)(lambda: ride.start(ride_in, ride_out, *sems))
        for h in range(heads):
            sl = slice(h * DN_HEAD_DIM, (h + 1) * DN_HEAD_DIM)
            u, w, inv = _ut_chain(k_ref[:, sl], v_ref[:, sl], beta_ref[h], gcc_ref[h], gcr_ref[h])
            u_ref[:, sl] = u
            w_ref[:, sl] = w
            inv_ref[:, h * r:(h + 1) * r] = inv
        if ride:
            pl.when(last)(lambda: ride.finish(ride_in, ride_out, *sems))

    out = jax.ShapeDtypeStruct((b * s, width), F32)
    outs = pl.pallas_call(
        body, name="ut_fwd" + ("_with_" + ride.tag if ride else ""), grid=(b, ns),
        in_specs=[tok, tok, col, col, row] + (ride.specs_in if ride else []),
        out_specs=[tok, tok, inv_spec] + (ride.specs_out if ride else []),
        out_shape=[out, out, jax.ShapeDtypeStruct((b * s, heads * r), F32)] + (ride.out_shapes if ride else []),
        scratch_shapes=ride.scratch if ride else [],
        compiler_params=_params("arbitrary", "arbitrary") if ride else _params("parallel", "parallel"),
    )(act_k, act_v, beta_col, gc_col, gc_row, *(ride.arrays if ride else []))
    return (*outs[:3], outs[3:]) if ride else outs


def ut_bwd(act_k, act_v, beta_col, gc_col, gc_row, inv, du, dw, dk_more, b, s):
    width = act_k.shape[1]
    heads = width // DN_HEAD_DIM
    r, ns, tok, col, row = _ut_specs(b, s, heads, width)
    inv_spec = pl.BlockSpec((r, heads * r), lambda i, n: (i * ns + n, 0))

    def body(k_ref, v_ref, beta_ref, gcc_ref, gcr_ref, inv_ref, du_ref, dw_ref, dkm_ref,
             dk_ref, dv_ref, dbeta_ref, dgcc_ref, dgcr_ref):
        for h in range(heads):
            sl = slice(h * DN_HEAD_DIM, (h + 1) * DN_HEAD_DIM)
            inv = inv_ref[:, h * r:(h + 1) * r]
            _, pull = jax.vjp(lambda *a: _ut_chain(*a, inv=inv)[:2], k_ref[:, sl], v_ref[:, sl], beta_ref[h],
                              gcc_ref[h], gcr_ref[h])
            dk, dv, dbeta, dgcc, dgcr = pull((du_ref[:, sl], dw_ref[:, sl]))
            dk_ref[:, sl] = dk + dkm_ref[:, sl]
            dv_ref[:, sl] = dv
            dbeta_ref[h] = dbeta
            dgcc_ref[h] = dgcc
            dgcr_ref[h] = dgcr

    out = jax.ShapeDtypeStruct((b * s, width), F32)
    return pl.pallas_call(
        body, name="ut_bwd", grid=(b, ns), in_specs=[tok, tok, col, col, row, inv_spec, tok, tok, tok],
        out_specs=[tok, tok, col, col, row],
        out_shape=[out, out, jax.ShapeDtypeStruct(beta_col.shape, F32), jax.ShapeDtypeStruct(gc_col.shape, F32),
                   jax.ShapeDtypeStruct(gc_row.shape, F32)],
        compiler_params=_params("parallel", "parallel"),
    )(act_k, act_v, beta_col, gc_col, gc_row, inv, du, dw, dk_more)


def _rec_specs(b, n, heads, width, chunk_of):
    ch = DN_CHUNK
    tok = pl.BlockSpec((b, None, ch, width), lambda i: (0, chunk_of(i), 0, 0))
    col = pl.BlockSpec((b, heads, None, ch, 1), lambda i: (0, 0, chunk_of(i), 0, 0))
    row = pl.BlockSpec((b, heads, None, 1, ch), lambda i: (0, 0, chunk_of(i), 0, 0))
    st = pl.BlockSpec((None, b, heads, DN_HEAD_DIM, DN_HEAD_DIM), lambda i: (chunk_of(i), 0, 0, 0, 0))
    return tok, col, row, st


def rec_fwd(act_q, act_k, gc_col, gc_row, u, w, b, s):
    width = act_q.shape[1]
    heads, n = width // DN_HEAD_DIM, s // DN_CHUNK
    tok, col, row, st = _rec_specs(b, n, heads, width, lambda i: i)
    shape4 = (b, n, DN_CHUNK, width)

    def body(q_ref, k_ref, gcc_ref, gcr_ref, u_ref, w_ref, o_ref, st_ref, state):
        @pl.when(pl.program_id(0) == 0)
        def _():
            state[...] = jnp.zeros_like(state)

        for i in range(b):
            for h in range(heads):
                sl = slice(h * DN_HEAD_DIM, (h + 1) * DN_HEAD_DIM)
                s_in = state[i, h]
                st_ref[i, h] = s_in
                o, s_out = _rec_chain(q_ref[i, :, sl], k_ref[i, :, sl], gcc_ref[i, h], gcr_ref[i, h],
                                      u_ref[i, :, sl], w_ref[i, :, sl], s_in)
                o_ref[i, :, sl] = o
                state[i, h] = s_out

    o, states = pl.pallas_call(
        body, name="rec_fwd", grid=(n,), in_specs=[tok, tok, col, row, tok, tok], out_specs=[tok, st],
        out_shape=[jax.ShapeDtypeStruct(shape4, F32),
                   jax.ShapeDtypeStruct((n, b, heads, DN_HEAD_DIM, DN_HEAD_DIM), F32)],
        scratch_shapes=[pltpu.VMEM((b, heads, DN_HEAD_DIM, DN_HEAD_DIM), F32)],
        compiler_params=_params("arbitrary"),
    )(act_q.reshape(shape4), act_k.reshape(shape4), gc_col.reshape(b, heads, n, DN_CHUNK, 1), gc_row,
      u.reshape(shape4), w.reshape(shape4))
    return o.reshape(b * s, width), states


def rec_bwd(act_q, act_k, gc_col, gc_row, u, w, states, do, b, s):
    width = act_q.shape[1]
    heads, n = width // DN_HEAD_DIM, s // DN_CHUNK
    tok, col, row, st = _rec_specs(b, n, heads, width, lambda i: n - 1 - i)
    shape4 = (b, n, DN_CHUNK, width)

    def body(q_ref, k_ref, gcc_ref, gcr_ref, u_ref, w_ref, st_ref, do_ref,
             dq_ref, dk_ref, du_ref, dw_ref, dgcc_ref, dgcr_ref, dstate):
        @pl.when(pl.program_id(0) == 0)
        def _():
            dstate[...] = jnp.zeros_like(dstate)

        for i in range(b):
            for h in range(heads):
                sl = slice(h * DN_HEAD_DIM, (h + 1) * DN_HEAD_DIM)
                _, pull = jax.vjp(_rec_chain, q_ref[i, :, sl], k_ref[i, :, sl], gcc_ref[i, h], gcr_ref[i, h],
                                  u_ref[i, :, sl], w_ref[i, :, sl], st_ref[i, h])
                dq, dk, dgcc, dgcr, du, dw, ds = pull((do_ref[i, :, sl], dstate[i, h]))
                dq_ref[i, :, sl] = dq
                dk_ref[i, :, sl] = dk
                du_ref[i, :, sl] = du
                dw_ref[i, :, sl] = dw
                dgcc_ref[i, h] = dgcc
                dgcr_ref[i, h] = dgcr
                dstate[i, h] = ds

    tok_out = jax.ShapeDtypeStruct(shape4, F32)
    dq, dk, du, dw, dgcc, dgcr = pl.pallas_call(
        body, name="rec_bwd", grid=(n,), in_specs=[tok, tok, col, row, tok, tok, st, tok],
        out_specs=[tok, tok, tok, tok, col, row],
        out_shape=[tok_out, tok_out, tok_out, tok_out,
                   jax.ShapeDtypeStruct((b, heads, n, DN_CHUNK, 1), F32), jax.ShapeDtypeStruct(gc_row.shape, F32)],
        scratch_shapes=[pltpu.VMEM((b, heads, DN_HEAD_DIM, DN_HEAD_DIM), F32)],
        compiler_params=_params("arbitrary"),
    )(act_q.reshape(shape4), act_k.reshape(shape4), gc_col.reshape(b, heads, n, DN_CHUNK, 1), gc_row,
      u.reshape(shape4), w.reshape(shape4), states, do.reshape(shape4))
    flat = lambda a: a.reshape(b * s, width)
    return flat(dq), flat(dk), flat(du), flat(dw), dgcc.reshape(b, heads, s, 1), dgcr


def dn_norm_fwd(o, proj, g, z_col0):
    t, width = o.shape
    heads = width // DN_HEAD_DIM
    tr = _tile(t, 512, 8)
    z_off = z_col0 // LANES

    def body(o_ref, z_ref, g_ref, y_ref):
        y_ref[...] = _gated_norm(o_ref[...], z_ref[...], g_ref[...]).astype(BF16)

    blk = pl.BlockSpec((tr, DN_HEAD_DIM), lambda i, h: (i, h))
    return pl.pallas_call(
        body, name="dn_norm_fwd", grid=(t // tr, heads),
        in_specs=[blk, pl.BlockSpec((tr, DN_HEAD_DIM), lambda i, h: (i, z_off + h)),
                  pl.BlockSpec((1, DN_HEAD_DIM), lambda i, h: (0, 0))],
        out_specs=blk, out_shape=jax.ShapeDtypeStruct((t, width), BF16),
        compiler_params=_params("parallel", "parallel"),
    )(o, proj, g.reshape(1, -1))


def dn_norm_bwd(o, proj, g, dmixed, z_col0, dy_col0):
    t, width = o.shape
    heads = width // DN_HEAD_DIM
    tr = _tile(t, 512, 8)
    z_off, dy_off = z_col0 // LANES, dy_col0 // LANES

    def body(o_ref, z_ref, g_ref, dy_ref, do_ref, dz_ref, dg_ref):
        _, pull = jax.vjp(_gated_norm, o_ref[...], z_ref[...], g_ref[...])
        do, dz, dg = pull(dy_ref[...])
        do_ref[...] = do
        dz_ref[...] = dz.astype(BF16)
        _accumulate(dg_ref, dg, (pl.program_id(0) == 0) & (pl.program_id(1) == 0))

    blk = pl.BlockSpec((tr, DN_HEAD_DIM), lambda i, h: (i, h))
    vec = pl.BlockSpec((1, DN_HEAD_DIM), lambda i, h: (0, 0))
    do, dz, dg = pl.pallas_call(
        body, name="dn_norm_bwd", grid=(t // tr, heads),
        in_specs=[blk, pl.BlockSpec((tr, DN_HEAD_DIM), lambda i, h: (i, z_off + h)), vec,
                  pl.BlockSpec((tr, DN_HEAD_DIM), lambda i, h: (i, dy_off + h))],
        out_specs=[blk, blk, vec],
        out_shape=[jax.ShapeDtypeStruct((t, width), F32), jax.ShapeDtypeStruct((t, width), BF16),
                   jax.ShapeDtypeStruct((1, DN_HEAD_DIM), F32)],
        compiler_params=_params("arbitrary", "arbitrary"),
    )(o, proj, g.reshape(1, -1), dmixed)
    return do, dz, dg.reshape(-1)


def _attn_specs(b, s, mem_len, d):
    hd = d // XA_HEADS
    tq = _tile(s, 512, 8)
    nq = s // tq
    q_spec = pl.BlockSpec((tq, hd), lambda i, h, j: (i * nq + j, h))
    k_spec = pl.BlockSpec((mem_len, hd), lambda i, h, j: (i, h))
    v_spec = pl.BlockSpec((mem_len, hd), lambda i, h, j: (i, XA_HEADS + h))
    return nq, q_spec, k_spec, v_spec


def attn_fwd(q, kv, b, s):
    d = q.shape[1]
    nq, q_spec, k_spec, v_spec = _attn_specs(b, s, kv.shape[0] // b, d)

    def body(q_ref, k_ref, v_ref, o_ref):
        o_ref[...] = _attn_block(q_ref[...], k_ref[...], v_ref[...]).astype(BF16)

    return pl.pallas_call(
        body, name="attn_fwd", grid=(b, XA_HEADS, nq), in_specs=[q_spec, k_spec, v_spec], out_specs=q_spec,
        out_shape=jax.ShapeDtypeStruct(q.shape, BF16), compiler_params=_params("parallel", "parallel", "parallel"),
    )(q, kv, kv)


def attn_bwd(q, kv, do, b, s):
    d = q.shape[1]
    rows = kv.shape[0]
    nq, q_spec, k_spec, v_spec = _attn_specs(b, s, rows // b, d)

    def body(q_ref, k_ref, v_ref, do_ref, dq_ref, dk_ref, dv_ref):
        _, pull = jax.vjp(_attn_block, q_ref[...], k_ref[...], v_ref[...])
        dq, dk, dv = pull(do_ref[...])
        dq_ref[...] = dq.astype(BF16)
        first = pl.program_id(2) == 0
        _accumulate(dk_ref, dk, first)
        _accumulate(dv_ref, dv, first)

    kv_out = jax.ShapeDtypeStruct((rows, d), F32)
    return pl.pallas_call(
        body, name="attn_bwd", grid=(b, XA_HEADS, nq), in_specs=[q_spec, k_spec, v_spec, q_spec],
        out_specs=[q_spec, k_spec, k_spec], out_shape=[jax.ShapeDtypeStruct(q.shape, BF16), kv_out, kv_out],
        compiler_params=_params("parallel", "parallel", "arbitrary"),
    )(q, kv, kv, do)


def ffn_fwd(gate_pre, up, conv_w, conv_b, b, s):
    taps_n, f = conv_w.shape

    def body(g_ref, u_ref, w_ref, b_ref, y_ref):
        taps = [w_ref[k:k + 1, :] for k in range(taps_n)]
        y_ref[...] = _ffn_block(g_ref[...], u_ref[...], taps, b_ref[...]).astype(BF16)

    blk = pl.BlockSpec((s, FFN_COLS), lambda j, i: (i, j))
    return pl.pallas_call(
        body, name="ffn_fwd", grid=(f // FFN_COLS, b),
        in_specs=[blk, blk, pl.BlockSpec((taps_n, FFN_COLS), lambda j, i: (0, j)),
                  pl.BlockSpec((1, FFN_COLS), lambda j, i: (0, j))],
        out_specs=blk,
        out_shape=jax.ShapeDtypeStruct((b * s, f), BF16), compiler_params=_params("parallel", "parallel"),
    )(gate_pre, up, conv_w, conv_b.reshape(1, f))


def ffn_bwd(gate_pre, up, conv_w, conv_b, dact, b, s):
    taps_n, f = conv_w.shape

    def body(g_ref, u_ref, w_ref, b_ref, dy_ref, dg_ref, du_ref, dw_ref, db_ref):
        taps = [w_ref[k:k + 1, :] for k in range(taps_n)]
        _, pull = jax.vjp(lambda gt, up_, bias, *tp: _ffn_block(gt, up_, tp, bias), g_ref[...], u_ref[...],
                          b_ref[...], *taps)
        dgate, dup, dbias, *dtaps = pull(dy_ref[...])
        dg_ref[...] = dgate.astype(BF16)
        du_ref[...] = dup.astype(BF16)
        first = pl.program_id(1) == 0
        _accumulate(db_ref, dbias, first)
        for k in range(taps_n):
            _accumulate(dw_ref.at[k:k + 1, :], dtaps[k], first)

    blk = pl.BlockSpec((s, FFN_COLS), lambda j, i: (i, j))
    w_spec = pl.BlockSpec((taps_n, FFN_COLS), lambda j, i: (0, j))
    b_spec = pl.BlockSpec((1, FFN_COLS), lambda j, i: (0, j))
    half = jax.ShapeDtypeStruct(gate_pre.shape, BF16)
    dgate, dup, dw, db = pl.pallas_call(
        body, name="ffn_bwd", grid=(f // FFN_COLS, b),
        in_specs=[blk, blk, w_spec, b_spec, blk],
        out_specs=[blk, blk, w_spec, b_spec],
        out_shape=[half, half, jax.ShapeDtypeStruct((taps_n, f), F32), jax.ShapeDtypeStruct((1, f), F32)],
        compiler_params=_params("arbitrary", "arbitrary"),
    )(gate_pre, up, conv_w, conv_b.reshape(1, f), dact)
    return dgate, dup, dw, db.reshape(f)


def gate_arrays(logits, a_log, dt_bias, b, s, heads):
    n, r = s // DN_CHUNK, min(UT_ROWS, s)
    lg = logits.reshape(b, s, -1)
    beta = jax.nn.sigmoid(lg[..., :heads])
    g = -jnp.exp(a_log) * jax.nn.softplus(lg[..., heads:2 * heads] + dt_bias)
    gc = jnp.cumsum(g.reshape(b, n, DN_CHUNK, heads), axis=2).transpose(0, 3, 1, 2)
    return (beta.transpose(0, 2, 1)[..., None], gc.reshape(b, heads, s, 1), gc[:, :, :, None, :],
            gc.reshape(b, heads, s // r, 1, r))


class Riders:
    def __init__(self, make_ride, groups, arrays, tag):
        self.groups, self.results = groups, {}
        self.rides = {host: make_ride([arrays[n] for n in members], f"{tag}_on_{host}")
                      for host, members in groups.items()}

    def run(self, host, call):
        value, outs = call(self.rides[host])
        self.results.update(zip(self.groups[host], outs))
        return value


def _hosted_matmul(riders, host, *args, **kwargs):
    if riders is None:
        return matmul(*args, **kwargs)
    return riders.run(host, lambda ride: matmul(*args, ride=ride, **kwargs))


FORWARD_HOSTS = {'in_proj': ('w_in',), 'ut': ('w_gate', 'w_up'), 'gate': ('w_xkv',),
                 'up': ('w_mix_out', 'w_xq', 'w_xo', 'w_pool', 'dn_conv_w', 'ffn_conv_w'), 'down': ('w_down',)}
BACKWARD_HOSTS = {'down_wgrad': ('w_down',), 'gate_dgrad': ('w_xkv',), 'up_dgrad': ('w_in',),
                  'gate_wgrad': ('w_gate',), 'up_wgrad': ('w_up',),
                  'in_dgrad': ('w_mix_out', 'w_xq', 'w_xo', 'w_pool', 'dn_conv_w', 'ffn_conv_w')}


def forward_layer(x, mem_hb, p, b, s, riders=None):
    d = x.shape[1]
    pw = d // 2
    dn = d - pw
    heads = dn // DN_HEAD_DIM
    sv = {'x0': x}
    sv['h1'] = h1 = norm_fwd(x, p['mix_norm_g'])
    sv['proj'] = proj = _hosted_matmul(riders, 'in_proj', h1, p['w_in_main'], 'nn', F32)
    logits = matmul(h1, p['w_in_logits'], 'nn', F32)
    y_pool = pool_fwd(proj, p['w_pool'], p['pool_scale'], b, s)
    gates, sv['gates_pull'] = jax.vjp(lambda lg, al, dtb: gate_arrays(lg, al, dtb, b, s, heads), logits,
                                      p['dn_a_log'], p['dn_dt_bias'])
    sv['gates'] = beta_col, gc_col, gc_row, gc_row_ut = gates
    sv['act'] = aq, ak, av = [conv_silu_fwd(proj, p['dn_conv_w'], part, pw, dn, b, s) for part in range(3)]
    if riders is None:
        u, w, sv['inv'] = ut_fwd(ak, av, beta_col, gc_col, gc_row_ut, b, s)
    else:
        def hosted(ride):
            *own, outs = ut_fwd(ak, av, beta_col, gc_col, gc_row_ut, b, s, ride=ride)
            return own, outs
        u, w, sv['inv'] = riders.run('ut', hosted)
    sv['u'], sv['w'] = u, w
    sv['o_dn'], sv['states'] = o_dn, _ = rec_fwd(aq, ak, gc_col, gc_row, u, w, b, s)
    y_dn = dn_norm_fwd(o_dn, proj, p['dn_norm_g'], pw + 3 * dn)
    sv['mixed'] = mixed = jnp.concatenate([y_pool, y_dn], axis=1)
    sv['x1'] = x1 = matmul(mixed, p['w_mix_out'], 'nn', F32, res=x)

    sv['h2'] = h2 = norm_fwd(x1, p['xa_norm_g'])
    sv['q'] = q = matmul(h2, p['w_xq'], 'nn', F32)
    sv['kv'] = kv = matmul(mem_hb, p['w_xkv'], 'nn', F32)
    sv['o_at'] = o_at = attn_fwd(q, kv, b, s)
    sv['x2'] = x2 = matmul(o_at, p['w_xo'], 'nn', F32, res=x1)

    sv['h3'] = h3 = norm_fwd(x2, p['ffn_norm_g'])
    sv['gate_pre'] = gate_pre = _hosted_matmul(riders, 'gate', h3, p['w_gate'], 'nn', F32)
    sv['up'] = up = _hosted_matmul(riders, 'up', h3, p['w_up'], 'nn', F32)
    sv['a_ffn'] = a_ffn = ffn_fwd(gate_pre, up, p['ffn_conv_w'], p['ffn_conv_b'], b, s)
    return _hosted_matmul(riders, 'down', a_ffn, p['w_down'], 'nn', F32, res=x2), sv


def backward_layer(dx, dxb, sv, mem_hb, p, b, s, above=None):
    d = dx.shape[1]
    pw = d // 2
    dn = d - pw
    g = {}
    if above is None:
        riders = None
        da = matmul(dxb, p['w_down'], 'nt', F32)
    else:
        names, parts, core, tag = above
        da, from_sibling = matmul(dxb, p['w_down'], 'nt', F32, ride=sibling_swap_ride(parts, tag))
        chip_sums = {n: chip_sum(a, f, core) for n, a, f in zip(names, parts, from_sibling)}
        riders = Riders(chip_exchange_ride, BACKWARD_HOSTS, chip_sums, tag)
    g['w_down'] = _hosted_matmul(riders, 'down_wgrad', sv['a_ffn'], dxb, 'tn', BF16)
    dgate, dup, g['ffn_conv_w'], g['ffn_conv_b'] = ffn_bwd(sv['gate_pre'], sv['up'], p['ffn_conv_w'],
                                                           p['ffn_conv_b'], da, b, s)
    dh = _hosted_matmul(riders, 'gate_dgrad', dgate, p['w_gate'], 'nt', F32)
    dh = _hosted_matmul(riders, 'up_dgrad', dup, p['w_up'], 'nt', F32, res=dh)
    g['w_gate'] = _hosted_matmul(riders, 'gate_wgrad', sv['h3'], dgate, 'tn', BF16)
    g['w_up'] = _hosted_matmul(riders, 'up_wgrad', sv['h3'], dup, 'tn', BF16)
    dx, dxb, g['ffn_norm_g'] = norm_bwd(sv['x2'], p['ffn_norm_g'], dh, dx)

    do = matmul(dxb, p['w_xo'], 'nt', F32)
    g['w_xo'] = matmul(sv['o_at'], dxb, 'tn', BF16)
    dq, dk, dv = attn_bwd(sv['q'], sv['kv'], do, b, s)
    dkv = jnp.concatenate([dk, dv], axis=1).astype(BF16)
    dh = matmul(dq, p['w_xq'], 'nt', F32)
    g['w_xq'] = matmul(sv['h2'], dq, 'tn', BF16)
    g['w_xkv'] = matmul(mem_hb, dkv, 'tn', BF16)
    dmem_h = matmul(dkv, p['w_xkv'], 'nt', F32)
    dx, dxb, g['xa_norm_g'] = norm_bwd(sv['x1'], p['xa_norm_g'], dh, dx)

    dmixed = matmul(dxb, p['w_mix_out'], 'nt', F32)
    g['w_mix_out'] = matmul(sv['mixed'], dxb, 'tn', BF16)
    proj = sv['proj']
    du_pool, g['w_pool'], g['pool_scale'] = pool_bwd(proj, p['w_pool'], p['pool_scale'], dmixed, b, s)
    do_dn, dz, g['dn_norm_g'] = dn_norm_bwd(sv['o_dn'], proj, p['dn_norm_g'], dmixed, pw + 3 * dn, pw)
    beta_col, gc_col, gc_row, gc_row_ut = sv['gates']
    aq, ak, av = sv['act']
    daq, dak_rec, du, dw, dgcc_rec, dgcr = rec_bwd(aq, ak, gc_col, gc_row, sv['u'], sv['w'], sv['states'],
                                                   do_dn, b, s)
    dak, dav, dbeta, dgcc_ut, dgcr_ut = ut_bwd(ak, av, beta_col, gc_col, gc_row_ut, sv['inv'], du, dw, dak_rec, b, s)
    dparts, dtaps = zip(*[conv_silu_bwd(proj, p['dn_conv_w'], dact, part, pw, dn, b, s)
                          for part, dact in enumerate((daq, dak, dav))])
    g['dn_conv_w'] = jnp.concatenate(dtaps, axis=1)
    dlogits, g['dn_a_log'], g['dn_dt_bias'] = sv['gates_pull']((dbeta, dgcc_rec + dgcc_ut, dgcr, dgcr_ut))
    dproj = jnp.concatenate([du_pool, *dparts, dz], axis=1)
    dlogits = dlogits.astype(BF16)
    dh = _hosted_matmul(riders, 'in_dgrad', dproj, p['w_in_main'], 'nt', F32,
                        res=matmul(dlogits, p['w_in_logits'], 'nt', F32))
    g['w_in_main'] = matmul(sv['h1'], dproj, 'tn', BF16)
    g['w_in_logits'] = matmul(sv['h1'], dlogits, 'tn', BF16)
    dx, dxb, g['mix_norm_g'] = norm_bwd(sv['x0'], p['mix_norm_g'], dh, dx)
    return dx, dxb, dmem_h, g, (riders.results if riders else None)


def _to_full(gathered, axis):
    moved = jnp.moveaxis(gathered, 0, axis)
    shape = list(gathered.shape[1:])
    shape[axis] *= N_DEV
    return moved.reshape(shape)


def _to_parts(full, axis):
    shape = list(full.shape)
    shape[axis:axis + 1] = [N_DEV, shape[axis] // N_DEV]
    return jnp.moveaxis(full.reshape(shape), axis, 0)


def _as_pack_rows(a, cols):
    rows = -(-a.shape[0] // (8 * cols)) * 8
    return jnp.pad(a, (0, rows * cols - a.shape[0])).reshape(rows, cols)


def kernel(x, mem, mix_norm_g, w_in, w_pool, pool_scale, dn_conv_w, dn_a_log, dn_dt_bias, dn_norm_g, w_mix_out, xa_norm_g, mem_norm_g, w_xq, w_xkv, w_xo, ffn_norm_g, w_gate, w_up, ffn_conv_w, ffn_conv_b, w_down, final_norm_g, loss_target, m_mix_norm_g, m_w_in, m_w_pool, m_pool_scale, m_dn_conv_w, m_dn_a_log, m_dn_dt_bias, m_dn_norm_g, m_w_mix_out, m_xa_norm_g, m_mem_norm_g, m_w_xq, m_w_xkv, m_w_xo, m_ffn_norm_g, m_w_gate, m_w_up, m_ffn_conv_w, m_ffn_conv_b, m_w_down, m_final_norm_g, v_mix_norm_g, v_w_in, v_w_pool, v_pool_scale, v_dn_conv_w, v_dn_a_log, v_dn_dt_bias, v_dn_norm_g, v_w_mix_out, v_xa_norm_g, v_mem_norm_g, v_w_xq, v_w_xkv, v_w_xo, v_ffn_norm_g, v_w_gate, v_w_up, v_ffn_conv_w, v_ffn_conv_b, v_w_down, v_final_norm_g):
    given = dict(locals())
    w = {n: given[n] for n in WEIGHTS}
    mom = {n: given['m_' + n] for n in WEIGHTS}
    var = {n: given['v_' + n] for n in WEIGHTS}
    b, s, d = x.shape
    depth = w_in.shape[0]
    main = 5 * (d // 2)
    n_logits = w_in.shape[-1] * N_DEV - main
    core = jnp.reshape(lax.axis_index("c"), (1,)).astype(jnp.int32)

    names = MATRICES + TAPS

    def shards(l):
        return {n: w[n][l] if n in TAPS else w[n][l].astype(BF16) for n in names}

    def layer_weights(l, gathered):
        full = {n: _to_full(gathered[n], SHARD_AXIS[n] - 1) for n in names}
        p = {n: full[n] for n in names if n != 'w_in'}
        p['w_in_main'] = full['w_in'][:, :main]
        p['w_in_logits'] = jnp.pad(full['w_in'][:, main:], ((0, 0), (0, LANES - n_logits)))
        for n in PER_LAYER_REPLICATED:
            p[n] = w[n][l]
        return p

    mem2 = mem.reshape(-1, d)
    mem_hb = norm_fwd(mem2, mem_norm_g)
    xc, saved, layers = x.reshape(b * s, d), [], []
    gathered = dict(zip(names, all_gather([shards(0)[n] for n in names], "layer0")))
    for l in range(depth):
        layers.append(layer_weights(l, gathered))
        riders = Riders(gather_ride, FORWARD_HOSTS, shards(l + 1), f"layer{l + 1}") if l + 1 < depth else None
        xc, sv = forward_layer(xc, mem_hb, layers[l], b, s, riders)
        saved.append(sv)
        gathered = riders.results if riders else None
    loss, dx, dxb, g_final = loss_head(xc, loss_target.reshape(b * s, d), final_norm_g)

    def as_layers(a, cols):
        return a.reshape(depth, -1, cols)

    results = {n: None for n in names}

    def update(l, got):
        for n in names:
            cols = w[n].shape[-1]
            results[n] = adamw(got[n], as_layers(w[n], cols), as_layers(mom[n], cols), as_layers(var[n], cols),
                               layer=l, prev=results[n])

    g_layers, dmem_h, above = [None] * depth, None, None
    for l in reversed(range(depth)):
        dx, dxb, dm, g, reduced = backward_layer(dx, dxb, saved[l], mem_hb, layers[l], b, s, above)
        if reduced:
            update(l + 1, reduced)
        g_layers[l] = g
        dmem_h = dm if dmem_h is None else dmem_h + dm
        g_full = {n: g[n] for n in names if n not in ('w_in', 'w_pool')}
        g_full['w_pool'] = g['w_pool'].astype(BF16)
        g_full['w_in'] = jnp.concatenate([g['w_in_main'], g['w_in_logits'][:, :n_logits]], axis=1)
        parts = []
        for n in names:
            by_dev = _to_parts(g_full[n], SHARD_AXIS[n] - 1)
            parts.append(by_dev.reshape(N_DEV, -1, by_dev.shape[-1]))
        above = (names, parts, core, f"layer{l}")
    from_sibling = sibling_swap_ride(parts, "layer0").alone()
    chip_sums = [chip_sum(a, f, core) for a, f in zip(parts, from_sibling)]
    update(0, dict(zip(names, chip_exchange_ride(chip_sums, "layer0").alone())))
    _, _, g_mem = norm_bwd(mem2, mem_norm_g, dmem_h, None)
    grad_x = dx.reshape(b, s, d)

    grad, delta, new_m, new_v = {}, {}, {}, {}
    for n in names:
        grad[n], delta[n], new_m[n], new_v[n] = [o.reshape(w[n].shape) for o in results[n]]

    g_small = {n: jnp.stack([g[n] for g in g_layers]) for n in PER_LAYER_REPLICATED}
    g_small['mem_norm_g'], g_small['final_norm_g'] = g_mem, g_final
    sizes = [w[n].size for n in REPLICATED]

    def pack(parts, first):
        flat = jnp.concatenate([jnp.reshape(first, (1,))] + [parts[n].reshape(-1) for n in REPLICATED])
        return _as_pack_rows(flat, LANES)

    zero = jnp.zeros((), F32)
    everyone, = all_gather([pack(g_small, loss)], "replicated")
    outs = adamw(everyone, pack(w, zero)[None], pack(mom, zero)[None], pack(var, zero)[None])
    flat_outs = [o.reshape(-1) for o in outs]
    loss_total = flat_outs[0][0]
    offset = 1
    for n, size in zip(REPLICATED, sizes):
        grad[n], delta[n], new_m[n], new_v[n] = [o[offset:offset + size].reshape(w[n].shape) for o in flat_outs]
        offset += size

    return (loss_total, grad_x, *[grad[n] for n in WEIGHTS], *[delta[n] for n in WEIGHTS],
            *[new_m[n] for n in WEIGHTS], *[new_v[n] for n in WEIGHTS])
```

```python
import functools

import jax
import jax.numpy as jnp
from jax import lax
from jax.experimental import pallas as pl
from jax.experimental.pallas import tpu as pltpu

F32 = jnp.float32
BF16 = jnp.bfloat16
MESH = pl.DeviceIdType.MESH

N_DEV = 8
EPS = 1e-6
POOL_WINDOWS = (2, 4, 8, 16)
DN_HEAD_DIM = 128
DN_CHUNK = 64
DN_TAPS = 4
XA_HEADS = 4
ADAM_LR = 0.001
ADAM_B1 = 0.9
ADAM_B2 = 0.999
ADAM_EPS = 1e-08
ADAM_WD = 0.01
ADAM_STEP = 10

LANES = 128
VMEM_LIMIT = 48 * 1024 * 1024
UT_ROWS = 256
FFN_COLS = 256

WEIGHTS = ['mix_norm_g', 'w_in', 'w_pool', 'pool_scale', 'dn_conv_w', 'dn_a_log', 'dn_dt_bias', 'dn_norm_g',
           'w_mix_out', 'xa_norm_g', 'mem_norm_g', 'w_xq', 'w_xkv', 'w_xo', 'ffn_norm_g', 'w_gate', 'w_up',
           'ffn_conv_w', 'ffn_conv_b', 'w_down', 'final_norm_g']
SHARD_AXIS = {'w_in': 2, 'w_pool': 2, 'dn_conv_w': 2, 'w_mix_out': 1, 'w_xq': 1, 'w_xkv': 2, 'w_xo': 1,
              'w_gate': 2, 'w_up': 2, 'ffn_conv_w': 2, 'w_down': 1}
MATRICES = ('w_in', 'w_pool', 'w_mix_out', 'w_xq', 'w_xkv', 'w_xo', 'w_gate', 'w_up', 'w_down')
TAPS = ('dn_conv_w', 'ffn_conv_w')
REPLICATED = [n for n in WEIGHTS if n not in SHARD_AXIS]
PER_LAYER_REPLICATED = ('mix_norm_g', 'pool_scale', 'dn_a_log', 'dn_dt_bias', 'dn_norm_g', 'xa_norm_g',
                        'ffn_norm_g', 'ffn_conv_b')


def _params(*semantics):
    return pltpu.CompilerParams(dimension_semantics=semantics, vmem_limit_bytes=VMEM_LIMIT)


def _tile(dim, pref, unit=LANES):
    if dim <= pref:
        return dim
    t = (pref // unit) * unit
    while t >= unit:
        if dim % t == 0:
            return t
        t -= unit
    return dim


class Ride:
    def __init__(self, tag, arrays, out_shapes, n_sems, n_local, start, finish):
        self.tag, self.arrays, self.out_shapes = tag, list(arrays), list(out_shapes)
        self.start, self.finish = start, finish
        self.scratch = [pltpu.SemaphoreType.DMA((n_sems,)), pltpu.SemaphoreType.DMA((n_sems,)),
                        pltpu.SemaphoreType.DMA((n_local,))]

    @property
    def specs_in(self):
        return [pl.BlockSpec(memory_space=pl.ANY)] * len(self.arrays)

    @property
    def specs_out(self):
        return [pl.BlockSpec(memory_space=pl.ANY)] * len(self.out_shapes)

    def alone(self):
        n = len(self.arrays)

        def body(*refs):
            ins, outs, sems = refs[:n], refs[n:n + len(self.out_shapes)], refs[n + len(self.out_shapes):]
            self.start(ins, outs, *sems)
            self.finish(ins, outs, *sems)

        return pl.pallas_call(body, name=self.tag, out_shape=self.out_shapes, in_specs=self.specs_in,
                              out_specs=self.specs_out, scratch_shapes=self.scratch)(*self.arrays)


def _first_and_last_step(grid):
    ids = [pl.program_id(axis) for axis in range(len(grid))]
    first = functools.reduce(jnp.logical_and, [i == 0 for i in ids])
    last = functools.reduce(jnp.logical_and, [i == n - 1 for i, n in zip(ids, grid)])
    return first, last


def matmul(a, b, mode, out_dtype, res=None, ride=None):
    assert a.dtype == BF16 and b.dtype == BF16, (a.dtype, b.dtype)
    if mode == 'nn':
        (m, c), (c2, n) = a.shape, b.shape
    elif mode == 'nt':
        (m, c), (n, c2) = a.shape, b.shape
    else:
        (c, m), (c2, n) = a.shape, b.shape
    assert c == c2, (mode, a.shape, b.shape)
    tm, tn = (_tile(m, 2048), _tile(n, 512)) if mode == 'nt' else (_tile(m, 1024), _tile(n, 1408))
    tc = _tile(c, 2048)
    nc = c // tc
    if mode == 'nn':
        a_spec = pl.BlockSpec((tm, tc), lambda i, j, k: (i, k))
        b_spec = pl.BlockSpec((tc, tn), lambda i, j, k: (k, j))
        dims = (((1,), (0,)), ((), ()))
    elif mode == 'nt':
        a_spec = pl.BlockSpec((tm, tc), lambda i, j, k: (i, k))
        b_spec = pl.BlockSpec((tn, tc), lambda i, j, k: (j, k))
        dims = (((1,), (1,)), ((), ()))
    else:
        a_spec = pl.BlockSpec((tc, tm), lambda i, j, k: (k, i))
        b_spec = pl.BlockSpec((tc, tn), lambda i, j, k: (k, j))
        dims = (((0,), (0,)), ((), ()))
    out_spec = pl.BlockSpec((tm, tn), lambda i, j, k: (i, j))
    has_res = res is not None
    grid = (m // tm, n // tn, nc)
    n_ride_in = len(ride.arrays) if ride else 0
    n_ride_out = len(ride.out_shapes) if ride else 0

    def body(a_ref, b_ref, *rest):
        rest = list(rest)
        r_ref = rest.pop(0) if has_res else None
        ride_in = [rest.pop(0) for _ in range(n_ride_in)]
        o_ref = rest.pop(0)
        ride_out = [rest.pop(0) for _ in range(n_ride_out)]
        acc_ref = rest.pop(0) if nc > 1 else None
        if ride:
            first, last = _first_and_last_step(grid)
            pl.when(first)(lambda: ride.start(ride_in, ride_out, *rest))
        prod = lax.dot_general(a_ref[...], b_ref[...], dims, preferred_element_type=F32)

        def finish(total):
            if has_res:
                total = total + r_ref[...]
            o_ref[...] = total.astype(o_ref.dtype)

        if nc == 1:
            finish(prod)
        else:
            k = pl.program_id(2)

            @pl.when(k == 0)
            def _():
                acc_ref[...] = prod

            @pl.when(k > 0)
            def _():
                acc_ref[...] += prod

            @pl.when(k == nc - 1)
            def _():
                finish(acc_ref[...])

        if ride:
            pl.when(last)(lambda: ride.finish(ride_in, ride_out, *rest))

    outs = pl.pallas_call(
        body,
        name=f"mm_{mode}_{m}x{c}x{n}" + ("_res" if has_res else "") + ("_with_" + ride.tag if ride else ""),
        grid=grid,
        in_specs=[a_spec, b_spec] + ([out_spec] if has_res else []) + (ride.specs_in if ride else []),
        out_specs=[out_spec] + (ride.specs_out if ride else []),
        out_shape=[jax.ShapeDtypeStruct((m, n), out_dtype)] + (ride.out_shapes if ride else []),
        scratch_shapes=([] if nc == 1 else [pltpu.VMEM((tm, tn), F32)]) + (ride.scratch if ride else []),
        compiler_params=(_params("arbitrary", "arbitrary", "arbitrary") if ride
                         else _params("parallel", "parallel", "arbitrary")),
    )(a, b, *([res] if has_res else []), *(ride.arrays if ride else []))
    return (outs[0], outs[1:]) if ride else outs[0]


def _position():
    return lax.axis_index("x"), lax.axis_index("y"), lax.axis_index("c")


def _flip(v, bit):
    return 1 - v if bit else v


def _dev_index(px, py, pc):
    return 4 * px + 2 * py + pc


def gather_ride(shards, tag):
    n = len(shards)

    def plan(x_refs, out_refs, send_sems, recv_sems, local_sems):
        x, y, c = _position()
        me = (x, y, c)

        def copy(t, k, block, to, own=False):
            slot = out_refs[t].at[_dev_index(*block)]
            return pltpu.make_async_remote_copy(
                src_ref=x_refs[t] if own else slot, dst_ref=slot, send_sem=send_sems.at[7 * t + k],
                recv_sem=recv_sems.at[7 * t + k], device_id=to, device_id_type=MESH)

        def local(t):
            return pltpu.make_async_copy(x_refs[t], out_refs[t].at[_dev_index(*me)], local_sems.at[t])

        return me, (x, y, 1 - c), [(1 - x, y), (x, 1 - y), (1 - x, 1 - y)], copy, local

    def start(x_refs, out_refs, *sems):
        me, sibling, chips, copy, local = plan(x_refs, out_refs, *sems)
        for t in range(n):
            local(t).start()
            copy(t, 0, me, sibling, own=True).start()
            for j, chip in enumerate(chips):
                copy(t, 1 + j, me, (*chip, me[2]), own=True).start()

    def finish(x_refs, out_refs, *sems):
        me, sibling, chips, copy, local = plan(x_refs, out_refs, *sems)
        c = me[2]
        for j, chip in enumerate(chips):
            for t in range(n):
                copy(t, 1 + j, (*chip, c), me).wait_recv()
                copy(t, 4 + j, (*chip, c), sibling).start()
        for t in range(n):
            copy(t, 0, sibling, me).wait_recv()
            for j, chip in enumerate(chips):
                copy(t, 4 + j, (*chip, 1 - c), me).wait_recv()
        for t in range(n):
            copy(t, 0, me, sibling, own=True).wait_send()
            for j, chip in enumerate(chips):
                copy(t, 1 + j, me, (*chip, c), own=True).wait_send()
                copy(t, 4 + j, (*chip, c), sibling).wait_send()
            local(t).wait()

    return Ride("all_gather_" + tag, shards, [jax.ShapeDtypeStruct((N_DEV,) + a.shape, a.dtype) for a in shards],
                7 * n, n, start, finish)


def all_gather(shards, tag):
    return gather_ride(shards, tag).alone()


def sibling_swap_ride(parts, tag):
    n = len(parts)

    def plan(p_refs, out_refs, send_sems, recv_sems, _):
        x, y, c = _position()
        return [pltpu.make_async_remote_copy(
            src_ref=p_refs[t].at[2 * k + (1 - c)], dst_ref=out_refs[t].at[k], send_sem=send_sems.at[4 * t + k],
            recv_sem=recv_sems.at[4 * t + k], device_id=(x, y, 1 - c), device_id_type=MESH)
            for t in range(n) for k in range(N_DEV // 2)]

    def start(p_refs, out_refs, *sems):
        for cp in plan(p_refs, out_refs, *sems):
            cp.start()

    def finish(p_refs, out_refs, *sems):
        for cp in plan(p_refs, out_refs, *sems):
            cp.wait()

    return Ride("sibling_swap_" + tag, parts,
                [jax.ShapeDtypeStruct((N_DEV // 2,) + a.shape[1:], a.dtype) for a in parts], 4 * n, 1, start, finish)


def chip_sum(parts, from_sibling, core):
    _, r, c = parts.shape
    unit = 16 if parts.dtype == BF16 else 8
    tr = _tile(r, max(unit, (512 * 1024 // c) // unit * unit), unit)

    def body(core_ref, a_ref, b_ref, o_ref):
        o_ref[...] = (a_ref[...].astype(F32) + b_ref[...].astype(F32)).astype(o_ref.dtype)

    blk = pl.BlockSpec((None, tr, c), lambda k, i, core_ref: (k, i, 0))
    return pl.pallas_call(
        body, name=f"chip_sum_{r}x{c}_{parts.dtype.name}",
        grid_spec=pltpu.PrefetchScalarGridSpec(
            num_scalar_prefetch=1, grid=(N_DEV // 2, r // tr),
            in_specs=[pl.BlockSpec((None, tr, c), lambda k, i, core_ref: (2 * k + core_ref[0], i, 0)), blk],
            out_specs=blk),
        out_shape=jax.ShapeDtypeStruct(from_sibling.shape, from_sibling.dtype),
        compiler_params=_params("parallel", "parallel"),
    )(core, parts, from_sibling)


def chip_exchange_ride(parts, tag):
    n = len(parts)

    def plan(p_refs, out_refs, send_sems, recv_sems, local_sems):
        x, y, c = _position()
        my_chip = 2 * x + y
        local = [pltpu.make_async_copy(p_refs[t].at[my_chip], out_refs[t].at[my_chip], local_sems.at[t])
                 for t in range(n)]
        copies = []
        for t in range(n):
            for k in (1, 2, 3):
                px, py = _flip(x, k & 2), _flip(y, k & 1)
                copies.append(pltpu.make_async_remote_copy(
                    src_ref=p_refs[t].at[2 * px + py], dst_ref=out_refs[t].at[my_chip],
                    send_sem=send_sems.at[3 * t + k - 1], recv_sem=recv_sems.at[3 * t + k - 1],
                    device_id=(px, py, c), device_id_type=MESH))
        return local, copies

    def start(p_refs, out_refs, *sems):
        local, copies = plan(p_refs, out_refs, *sems)
        for cp in local + copies:
            cp.start()

    def finish(p_refs, out_refs, *sems):
        local, copies = plan(p_refs, out_refs, *sems)
        for cp in copies:
            cp.wait_recv()
        for cp in copies:
            cp.wait_send()
        for cp in local:
            cp.wait()

    return Ride("chip_exchange_" + tag, parts, [jax.ShapeDtypeStruct(a.shape, a.dtype) for a in parts],
                3 * n, n, start, finish)


def adamw(parts, w, m, v, layer=0, prev=None):
    n_parts, r, c = parts.shape
    depth = w.shape[0]
    unit = 16 if parts.dtype == BF16 else 8
    tr = _tile(r, max(unit, (256 * 1024 // c) // unit * unit), unit)

    def body(p_ref, w_ref, m_ref, v_ref, *rest):
        g_ref, d_ref, nm_ref, nv_ref = rest[-4:]
        g = p_ref[0].astype(F32)
        for j in range(1, n_parts):
            g = g + p_ref[j].astype(F32)
        nm = ADAM_B1 * m_ref[...] + (1.0 - ADAM_B1) * g
        nv = ADAM_B2 * v_ref[...] + (1.0 - ADAM_B2) * jnp.square(g)
        m_hat = nm / (1.0 - ADAM_B1 ** ADAM_STEP)
        v_hat = nv / (1.0 - ADAM_B2 ** ADAM_STEP)
        g_ref[...] = g
        d_ref[...] = -ADAM_LR * (m_hat / (jnp.sqrt(v_hat) + ADAM_EPS) + ADAM_WD * w_ref[...])
        nm_ref[...] = nm
        nv_ref[...] = nv

    spec = pl.BlockSpec((None, tr, c), lambda i: (layer, i, 0))
    out = jax.ShapeDtypeStruct((depth, r, c), F32)
    carried = [] if prev is None else list(prev)
    return pl.pallas_call(
        body,
        name=f"adamw_{n_parts}x{r}x{c}_{parts.dtype.name}_layer{layer}",
        grid=(r // tr,),
        in_specs=[pl.BlockSpec((n_parts, tr, c), lambda i: (0, i, 0)), spec, spec, spec]
        + [pl.BlockSpec(memory_space=pl.ANY)] * len(carried),
        out_specs=[spec, spec, spec, spec],
        out_shape=[out, out, out, out],
        input_output_aliases={4 + k: k for k in range(len(carried))},
        compiler_params=_params("parallel"),
    )(parts, w, m, v, *carried)


_NN = (((1,), (0,)), ((), ()))
_NT = (((1,), (1,)), ((), ()))
_TN = (((0,), (0,)), ((), ()))


def _dot_bf16(a, b, dims):
    return lax.dot_general(a.astype(BF16), b.astype(BF16), dims, preferred_element_type=F32)


def _dot_split(a, b, dims):
    a_hi, b_hi = a.astype(BF16), b.astype(BF16)
    a_lo = (a - a_hi.astype(F32)).astype(BF16)
    b_lo = (b - b_hi.astype(F32)).astype(BF16)
    dot = functools.partial(lax.dot_general, dimension_numbers=dims, preferred_element_type=F32)
    return dot(a_hi, b_hi) + (dot(a_hi, b_lo) + dot(a_lo, b_hi))


def _matmul_family(dot):
    @jax.custom_vjp
    def nn(a, b):
        return dot(a, b, _NN)

    @jax.custom_vjp
    def nt(a, b):
        return dot(a, b, _NT)

    @jax.custom_vjp
    def tn(a, b):
        return dot(a, b, _TN)

    nn.defvjp(lambda a, b: (nn(a, b), (a, b)), lambda r, g: (nt(g, r[1]), tn(r[0], g)))
    nt.defvjp(lambda a, b: (nt(a, b), (a, b)), lambda r, g: (nn(g, r[1]), tn(g, r[0])))
    tn.defvjp(lambda a, b: (tn(a, b), (a, b)), lambda r, g: (nt(r[1], g), nn(r[0], g)))
    return nn, nt, tn


mm, mm_nt, mm_tn = _matmul_family(_dot_bf16)
mms, mms_nt, mms_tn = _matmul_family(_dot_split)


def _shift_rows(x, k, down):
    n = x.shape[0]
    rows = lax.broadcasted_iota(jnp.int32, x.shape, 0)
    if down:
        return jnp.where(rows >= k, pltpu.roll(x, k, 0), 0.0)
    return jnp.where(rows < n - k, pltpu.roll(x, n - k, 0), 0.0)


@functools.partial(jax.custom_vjp, nondiff_argnums=(1,))
def delay(x, k):
    return _shift_rows(x, k, True) if k else x


delay.defvjp(lambda x, k: (delay(x, k), None), lambda k, _, g: ((_shift_rows(g, k, False) if k else g),))


def _silu(x):
    return x * jax.nn.sigmoid(x)


def _rms(x, g):
    return x * lax.rsqrt(jnp.mean(x * x, axis=-1, keepdims=True) + EPS) * g


def _l2n(t):
    return t * lax.rsqrt(jnp.sum(t * t, axis=-1, keepdims=True) + EPS)


def _causal_conv(x, taps):
    k_taps = len(taps)
    y = delay(x, k_taps - 1) * taps[0]
    for k in range(1, k_taps):
        y = y + delay(x, k_taps - 1 - k) * taps[k]
    return y


def _pool_block(u, w, scale, group):
    sums, acc, width = [], u, 1
    while width < POOL_WINDOWS[-1]:
        acc = acc + delay(acc, width)
        width *= 2
        sums.append(acc)
    picked = sums[-1]
    for i in range(len(POOL_WINDOWS) - 2, -1, -1):
        picked = jnp.where(group == i, sums[i], picked)
    rows = lax.broadcasted_iota(jnp.int32, u.shape, 0)
    count = jnp.minimum(rows + 1, jnp.left_shift(2, group)).astype(F32)
    return mm(picked / count - u, w) * scale


def _unit_lower_inverse(l_mat):
    n = l_mat.shape[0]
    eye = (lax.broadcasted_iota(jnp.int32, (n, n), 0) == lax.broadcasted_iota(jnp.int32, (n, n), 1)).astype(F32)
    m1 = -l_mat
    m2 = mms(m1, m1)
    m4 = mms(m2, m2)
    m8 = mms(m4, m4)
    m16 = mms(m8, m8)
    m32 = mms(m16, m16)
    low = mms(eye + m1, eye + m2)
    mid = mms(eye + m4, eye + m8)
    high = mms(eye + m16, eye + m32)
    return mms(mms(low, mid), high)


@jax.custom_vjp
def _known_inverse(l_mat, inv):
    return inv


_known_inverse.defvjp(lambda l_mat, inv: (inv, inv),
                      lambda inv, g: (-mms_tn(inv, mms_nt(g, inv)), jnp.zeros_like(inv)))


def _ut_chain(k, v, beta, gc_c, gc_r, inv=None):
    r = k.shape[0]
    kn = _l2n(k)
    row = lax.broadcasted_iota(jnp.int32, (r, r), 0)
    col = lax.broadcasted_iota(jnp.int32, (r, r), 1)
    strict = (row // DN_CHUNK == col // DN_CHUNK) & (row > col)
    decay = jnp.exp(jnp.where(strict, gc_c - gc_r, -1e30))
    l_mat = jnp.where(strict, beta * mm_nt(kn, kn) * decay, 0.0)
    inv = _unit_lower_inverse(l_mat) if inv is None else _known_inverse(l_mat, inv)
    return mms(inv, v * beta), mms(inv, kn * (beta * jnp.exp(gc_c))), inv


def _rec_chain(q, k, gc_c, gc_r, u, w, state):
    ch, dh = q.shape
    qn = _l2n(q) * (dh ** -0.5)
    kn = _l2n(k)
    row = lax.broadcasted_iota(jnp.int32, (ch, ch), 0)
    col = lax.broadcasted_iota(jnp.int32, (ch, ch), 1)
    decay = jnp.exp(jnp.where(row >= col, gc_c - gc_r, -1e30))
    attn = mm_nt(qn, kn) * decay
    is_last = lax.broadcasted_iota(jnp.int32, gc_r.shape, 1) == ch - 1
    last = jnp.sum(jnp.where(is_last, gc_r, 0.0), axis=1, keepdims=True)
    v_new = u - mm(w, state)
    out = mm(qn * jnp.exp(gc_c), state) + mm(attn, v_new)
    return out, state * jnp.exp(last) + mm_tn(kn * jnp.exp(last - gc_c), v_new)


def _gated_norm(o, z, g):
    return _rms(o, g) * _silu(z)


def _attn_block(q, k, v):
    s = mm_nt(q, k) * (q.shape[-1] ** -0.5)
    p = jnp.exp(s - lax.stop_gradient(jnp.max(s, axis=-1, keepdims=True)))
    return mm(p / jnp.sum(p, axis=-1, keepdims=True), v)


def _ffn_block(gate_pre, up, taps, bias):
    return _silu(_causal_conv(gate_pre, taps) + bias) * up


def _accumulate(ref, value, first):
    @pl.when(first)
    def _():
        ref[...] = value

    @pl.when(jnp.logical_not(first))
    def _():
        ref[...] += value


def norm_fwd(x, g):
    t, d = x.shape
    tr = _tile(t, 256, 8)

    def body(x_ref, g_ref, h_ref):
        h_ref[...] = _rms(x_ref[...], g_ref[...]).astype(BF16)

    return pl.pallas_call(
        body, name=f"norm_fwd_{t}", grid=(t // tr,),
        in_specs=[pl.BlockSpec((tr, d), lambda i: (i, 0)), pl.BlockSpec((1, d), lambda i: (0, 0))],
        out_specs=pl.BlockSpec((tr, d), lambda i: (i, 0)),
        out_shape=jax.ShapeDtypeStruct((t, d), BF16), compiler_params=_params("parallel"),
    )(x, g.reshape(1, d))


def norm_bwd(x, g, dh, dres):
    t, d = x.shape
    tr = _tile(t, 256, 8)
    has_res = dres is not None

    def body(x_ref, g_ref, dh_ref, *rest):
        rest = list(rest)
        r_ref = rest.pop(0) if has_res else None
        dx_ref, dxb_ref, dg_ref = rest
        _, pull = jax.vjp(_rms, x_ref[...], g_ref[...])
        dx, dg = pull(dh_ref[...])
        if has_res:
            dx = dx + r_ref[...]
        dx_ref[...] = dx
        dxb_ref[...] = dx.astype(BF16)
        _accumulate(dg_ref, dg, pl.program_id(0) == 0)

    row = pl.BlockSpec((tr, d), lambda i: (i, 0))
    vec = pl.BlockSpec((1, d), lambda i: (0, 0))
    dx, dxb, dg = pl.pallas_call(
        body, name=f"norm_bwd_{t}" + ("_res" if has_res else ""), grid=(t // tr,),
        in_specs=[row, vec, row] + ([row] if has_res else []),
        out_specs=[row, row, vec],
        out_shape=[jax.ShapeDtypeStruct((t, d), F32), jax.ShapeDtypeStruct((t, d), BF16),
                   jax.ShapeDtypeStruct((1, d), F32)],
        compiler_params=_params("arbitrary"),
    )(x, g.reshape(1, d), dh, *([dres] if has_res else []))
    return dx, dxb, dg.reshape(d)


def loss_head(x, target, g):
    t, d = x.shape
    tr = _tile(t, 256, 8)

    def body(x_ref, t_ref, g_ref, l_ref, dx_ref, dxb_ref, dg_ref):
        tgt = t_ref[...]

        def block_loss(xv, gv):
            return 0.5 * jnp.sum(jnp.mean(jnp.square(_rms(xv, gv) - tgt), axis=-1))

        val, pull = jax.vjp(block_loss, x_ref[...], g_ref[...])
        dx, dg = pull(jnp.ones((), F32))
        dx_ref[...] = dx
        dxb_ref[...] = dx.astype(BF16)
        first = pl.program_id(0) == 0
        _accumulate(dg_ref, dg, first)
        _accumulate(l_ref, jnp.full((1, LANES), val, F32), first)

    row = pl.BlockSpec((tr, d), lambda i: (i, 0))
    vec = pl.BlockSpec((1, d), lambda i: (0, 0))
    loss, dx, dxb, dg = pl.pallas_call(
        body, name="loss_head", grid=(t // tr,),
        in_specs=[row, row, vec],
        out_specs=[pl.BlockSpec((1, LANES), lambda i: (0, 0)), row, row, vec],
        out_shape=[jax.ShapeDtypeStruct((1, LANES), F32), jax.ShapeDtypeStruct((t, d), F32),
                   jax.ShapeDtypeStruct((t, d), BF16), jax.ShapeDtypeStruct((1, d), F32)],
        compiler_params=_params("arbitrary"),
    )(x, target, g.reshape(1, d))
    return loss[0, 0], dx, dxb, dg.reshape(d)


def pool_fwd(proj, w_pool, scale, b, s):
    n_g, grp = w_pool.shape[0], w_pool.shape[-1]

    def body(u_ref, w_ref, s_ref, y_ref):
        y_ref[...] = _pool_block(u_ref[...], w_ref[...], s_ref[...], pl.program_id(1)).astype(BF16)

    blk = pl.BlockSpec((s, grp), lambda i, j: (i, j))
    return pl.pallas_call(
        body, name="pool_fwd", grid=(b, n_g),
        in_specs=[blk, pl.BlockSpec((None, grp, grp), lambda i, j: (j, 0, 0)),
                  pl.BlockSpec((1, grp), lambda i, j: (0, j))],
        out_specs=blk,
        out_shape=jax.ShapeDtypeStruct((b * s, n_g * grp), BF16), compiler_params=_params("parallel", "parallel"),
    )(proj, w_pool, scale.reshape(1, -1))


def pool_bwd(proj, w_pool, scale, dmixed, b, s):
    n_g, grp = w_pool.shape[0], w_pool.shape[-1]

    def body(u_ref, w_ref, s_ref, dy_ref, du_ref, dw_ref, ds_ref):
        group = pl.program_id(0)
        _, pull = jax.vjp(lambda u, w, sc: _pool_block(u, w, sc, group), u_ref[...], w_ref[...].astype(F32),
                          s_ref[...])
        du, dw, ds = pull(dy_ref[...])
        du_ref[...] = du.astype(BF16)
        first = pl.program_id(1) == 0
        _accumulate(dw_ref, dw, first)
        _accumulate(ds_ref, ds, first)

    blk = pl.BlockSpec((s, grp), lambda j, i: (i, j))
    w_spec = pl.BlockSpec((None, grp, grp), lambda j, i: (j, 0, 0))
    s_spec = pl.BlockSpec((1, grp), lambda j, i: (0, j))
    du, dw, ds = pl.pallas_call(
        body, name="pool_bwd", grid=(n_g, b),
        in_specs=[blk, w_spec, s_spec, blk],
        out_specs=[blk, w_spec, s_spec],
        out_shape=[jax.ShapeDtypeStruct((b * s, n_g * grp), BF16), jax.ShapeDtypeStruct(w_pool.shape, F32),
                   jax.ShapeDtypeStruct((1, n_g * grp), F32)],
        compiler_params=_params("arbitrary", "arbitrary"),
    )(proj, w_pool, scale.reshape(1, -1), dmixed)
    return du, dw, ds.reshape(-1)


def conv_silu_fwd(proj, conv_w, part, col0, width, b, s):
    x_off, w_off = (col0 + part * width) // FFN_COLS, part * width // FFN_COLS

    def body(x_ref, w_ref, y_ref):
        taps = [w_ref[k:k + 1, :] for k in range(DN_TAPS)]
        y_ref[...] = _silu(_causal_conv(x_ref[...], taps))

    return pl.pallas_call(
        body, name=f"conv_silu_fwd_{part}", grid=(width // FFN_COLS, b),
        in_specs=[pl.BlockSpec((s, FFN_COLS), lambda j, i: (i, x_off + j)),
                  pl.BlockSpec((DN_TAPS, FFN_COLS), lambda j, i: (0, w_off + j))],
        out_specs=pl.BlockSpec((s, FFN_COLS), lambda j, i: (i, j)),
        out_shape=jax.ShapeDtypeStruct((b * s, width), F32), compiler_params=_params("parallel", "parallel"),
    )(proj, conv_w)


def conv_silu_bwd(proj, conv_w, dact, part, col0, width, b, s):
    x_off, w_off = (col0 + part * width) // FFN_COLS, part * width // FFN_COLS

    def body(x_ref, w_ref, dy_ref, dx_ref, dw_ref):
        taps = [w_ref[k:k + 1, :] for k in range(DN_TAPS)]
        _, pull = jax.vjp(lambda x, *tp: _silu(_causal_conv(x, tp)), x_ref[...], *taps)
        dx, *dtaps = pull(dy_ref[...])
        dx_ref[...] = dx.astype(BF16)
        first = pl.program_id(1) == 0
        for k in range(DN_TAPS):
            _accumulate(dw_ref.at[k:k + 1, :], dtaps[k], first)

    out_blk = pl.BlockSpec((s, FFN_COLS), lambda j, i: (i, j))
    return pl.pallas_call(
        body, name=f"conv_silu_bwd_{part}", grid=(width // FFN_COLS, b),
        in_specs=[pl.BlockSpec((s, FFN_COLS), lambda j, i: (i, x_off + j)),
                  pl.BlockSpec((DN_TAPS, FFN_COLS), lambda j, i: (0, w_off + j)), out_blk],
        out_specs=[out_blk, pl.BlockSpec((DN_TAPS, FFN_COLS), lambda j, i: (0, j))],
        out_shape=[jax.ShapeDtypeStruct((b * s, width), BF16), jax.ShapeDtypeStruct((DN_TAPS, width), F32)],
        compiler_params=_params("arbitrary", "arbitrary"),
    )(proj, conv_w, dact)


def _ut_specs(b, s, heads, width):
    r = min(UT_ROWS, s)
    ns = s // r
    tok = pl.BlockSpec((r, width), lambda i, n: (i * ns + n, 0))
    col = pl.BlockSpec((None, heads, r, 1), lambda i, n: (i, 0, n, 0))
    row = pl.BlockSpec((None, heads, None, 1, r), lambda i, n: (i, 0, n, 0, 0))
    return r, ns, tok, col, row


def ut_fwd(act_k, act_v, beta_col, gc_col, gc_row, b, s, ride=None):
    width = act_k.shape[1]
    heads = width // DN_HEAD_DIM
    r, ns, tok, col, row = _ut_specs(b, s, heads, width)
    inv_spec = pl.BlockSpec((r, heads * r), lambda i, n: (i * ns + n, 0))

    n_ride_in = len(ride.arrays) if ride else 0

    def body(k_ref, v_ref, beta_ref, gcc_ref, gcr_ref, *rest):
        u_ref, w_ref, inv_ref = rest[n_ride_in:n_ride_in + 3]
        if ride:
            ride_in, ride_out, sems = rest[:n_ride_in], rest[n_ride_in + 3:-3], rest[-3:]
            first, last = _first_and_last_step((b, ns))
            pl.when(first)(lambda: ride.start(ride_in, ride_out, *sems))
        for h in range(heads):
            sl = slice(h * DN_HEAD_DIM, (h + 1) * DN_HEAD_DIM)
            u, w, inv = _ut_chain(k_ref[:, sl], v_ref[:, sl], beta_ref[h], gcc_ref[h], gcr_ref[h])
            u_ref[:, sl] = u
            w_ref[:, sl] = w
            inv_ref[:, h * r:(h + 1) * r] = inv
        if ride:
            pl.when(last)(lambda: ride.finish(ride_in, ride_out, *sems))

    out = jax.ShapeDtypeStruct((b * s, width), F32)
    outs = pl.pallas_call(
        body, name="ut_fwd" + ("_with_" + ride.tag if ride else ""), grid=(b, ns),
        in_specs=[tok, tok, col, col, row] + (ride.specs_in if ride else []),
        out_specs=[tok, tok, inv_spec] + (ride.specs_out if ride else []),
        out_shape=[out, out, jax.ShapeDtypeStruct((b * s, heads * r), F32)] + (ride.out_shapes if ride else []),
        scratch_shapes=ride.scratch if ride else [],
        compiler_params=_params("arbitrary", "arbitrary") if ride else _params("parallel", "parallel"),
    )(act_k, act_v, beta_col, gc_col, gc_row, *(ride.arrays if ride else []))
    return (*outs[:3], outs[3:]) if ride else outs


def ut_bwd(act_k, act_v, beta_col, gc_col, gc_row, inv, du, dw, dk_more, b, s):
    width = act_k.shape[1]
    heads = width // DN_HEAD_DIM
    r, ns, tok, col, row = _ut_specs(b, s, heads, width)
    inv_spec = pl.BlockSpec((r, heads * r), lambda i, n: (i * ns + n, 0))

    def body(k_ref, v_ref, beta_ref, gcc_ref, gcr_ref, inv_ref, du_ref, dw_ref, dkm_ref,
             dk_ref, dv_ref, dbeta_ref, dgcc_ref, dgcr_ref):
        for h in range(heads):
            sl = slice(h * DN_HEAD_DIM, (h + 1) * DN_HEAD_DIM)
            inv = inv_ref[:, h * r:(h + 1) * r]
            _, pull = jax.vjp(lambda *a: _ut_chain(*a, inv=inv)[:2], k_ref[:, sl], v_ref[:, sl], beta_ref[h],
                              gcc_ref[h], gcr_ref[h])
            dk, dv, dbeta, dgcc, dgcr = pull((du_ref[:, sl], dw_ref[:, sl]))
            dk_ref[:, sl] = dk + dkm_ref[:, sl]
            dv_ref[:, sl] = dv
            dbeta_ref[h] = dbeta
            dgcc_ref[h] = dgcc
            dgcr_ref[h] = dgcr

    out = jax.ShapeDtypeStruct((b * s, width), F32)
    return pl.pallas_call(
        body, name="ut_bwd", grid=(b, ns), in_specs=[tok, tok, col, col, row, inv_spec, tok, tok, tok],
        out_specs=[tok, tok, col, col, row],
        out_shape=[out, out, jax.ShapeDtypeStruct(beta_col.shape, F32), jax.ShapeDtypeStruct(gc_col.shape, F32),
                   jax.ShapeDtypeStruct(gc_row.shape, F32)],
        compiler_params=_params("parallel", "parallel"),
    )(act_k, act_v, beta_col, gc_col, gc_row, inv, du, dw, dk_more)


def _rec_specs(b, n, heads, width, chunk_of):
    ch = DN_CHUNK
    tok = pl.BlockSpec((b, None, ch, width), lambda i: (0, chunk_of(i), 0, 0))
    col = pl.BlockSpec((b, heads, None, ch, 1), lambda i: (0, 0, chunk_of(i), 0, 0))
    row = pl.BlockSpec((b, heads, None, 1, ch), lambda i: (0, 0, chunk_of(i), 0, 0))
    st = pl.BlockSpec((None, b, heads, DN_HEAD_DIM, DN_HEAD_DIM), lambda i: (chunk_of(i), 0, 0, 0, 0))
    return tok, col, row, st


def rec_fwd(act_q, act_k, gc_col, gc_row, u, w, b, s):
    width = act_q.shape[1]
    heads, n = width // DN_HEAD_DIM, s // DN_CHUNK
    tok, col, row, st = _rec_specs(b, n, heads, width, lambda i: i)
    shape4 = (b, n, DN_CHUNK, width)

    def body(q_ref, k_ref, gcc_ref, gcr_ref, u_ref, w_ref, o_ref, st_ref, state):
        @pl.when(pl.program_id(0) == 0)
        def _():
            state[...] = jnp.zeros_like(state)

        for i in range(b):
            for h in range(heads):
                sl = slice(h * DN_HEAD_DIM, (h + 1) * DN_HEAD_DIM)
                s_in = state[i, h]
                st_ref[i, h] = s_in
                o, s_out = _rec_chain(q_ref[i, :, sl], k_ref[i, :, sl], gcc_ref[i, h], gcr_ref[i, h],
                                      u_ref[i, :, sl], w_ref[i, :, sl], s_in)
                o_ref[i, :, sl] = o
                state[i, h] = s_out

    o, states = pl.pallas_call(
        body, name="rec_fwd", grid=(n,), in_specs=[tok, tok, col, row, tok, tok], out_specs=[tok, st],
        out_shape=[jax.ShapeDtypeStruct(shape4, F32),
                   jax.ShapeDtypeStruct((n, b, heads, DN_HEAD_DIM, DN_HEAD_DIM), F32)],
        scratch_shapes=[pltpu.VMEM((b, heads, DN_HEAD_DIM, DN_HEAD_DIM), F32)],
        compiler_params=_params("arbitrary"),
    )(act_q.reshape(shape4), act_k.reshape(shape4), gc_col.reshape(b, heads, n, DN_CHUNK, 1), gc_row,
      u.reshape(shape4), w.reshape(shape4))
    return o.reshape(b * s, width), states


def rec_bwd(act_q, act_k, gc_col, gc_row, u, w, states, do, b, s):
    width = act_q.shape[1]
    heads, n = width // DN_HEAD_DIM, s // DN_CHUNK
    tok, col, row, st = _rec_specs(b, n, heads, width, lambda i: n - 1 - i)
    shape4 = (b, n, DN_CHUNK, width)

    def body(q_ref, k_ref, gcc_ref, gcr_ref, u_ref, w_ref, st_ref, do_ref,
             dq_ref, dk_ref, du_ref, dw_ref, dgcc_ref, dgcr_ref, dstate):
        @pl.when(pl.program_id(0) == 0)
        def _():
            dstate[...] = jnp.zeros_like(dstate)

        for i in range(b):
            for h in range(heads):
                sl = slice(h * DN_HEAD_DIM, (h + 1) * DN_HEAD_DIM)
                _, pull = jax.vjp(_rec_chain, q_ref[i, :, sl], k_ref[i, :, sl], gcc_ref[i, h], gcr_ref[i, h],
                                  u_ref[i, :, sl], w_ref[i, :, sl], st_ref[i, h])
                dq, dk, dgcc, dgcr, du, dw, ds = pull((do_ref[i, :, sl], dstate[i, h]))
                dq_ref[i, :, sl] = dq
                dk_ref[i, :, sl] = dk
                du_ref[i, :, sl] = du
                dw_ref[i, :, sl] = dw
                dgcc_ref[i, h] = dgcc
                dgcr_ref[i, h] = dgcr
                dstate[i, h] = ds

    tok_out = jax.ShapeDtypeStruct(shape4, F32)
    dq, dk, du, dw, dgcc, dgcr = pl.pallas_call(
        body, name="rec_bwd", grid=(n,), in_specs=[tok, tok, col, row, tok, tok, st, tok],
        out_specs=[tok, tok, tok, tok, col, row],
        out_shape=[tok_out, tok_out, tok_out, tok_out,
                   jax.ShapeDtypeStruct((b, heads, n, DN_CHUNK, 1), F32), jax.ShapeDtypeStruct(gc_row.shape, F32)],
        scratch_shapes=[pltpu.VMEM((b, heads, DN_HEAD_DIM, DN_HEAD_DIM), F32)],
        compiler_params=_params("arbitrary"),
    )(act_q.reshape(shape4), act_k.reshape(shape4), gc_col.reshape(b, heads, n, DN_CHUNK, 1), gc_row,
      u.reshape(shape4), w.reshape(shape4), states, do.reshape(shape4))
    flat = lambda a: a.reshape(b * s, width)
    return flat(dq), flat(dk), flat(du), flat(dw), dgcc.reshape(b, heads, s, 1), dgcr


def dn_norm_fwd(o, proj, g, z_col0):
    t, width = o.shape
    heads = width // DN_HEAD_DIM
    tr = _tile(t, 512, 8)
    z_off = z_col0 // LANES

    def body(o_ref, z_ref, g_ref, y_ref):
        y_ref[...] = _gated_norm(o_ref[...], z_ref[...], g_ref[...]).astype(BF16)

    blk = pl.BlockSpec((tr, DN_HEAD_DIM), lambda i, h: (i, h))
    return pl.pallas_call(
        body, name="dn_norm_fwd", grid=(t // tr, heads),
        in_specs=[blk, pl.BlockSpec((tr, DN_HEAD_DIM), lambda i, h: (i, z_off + h)),
                  pl.BlockSpec((1, DN_HEAD_DIM), lambda i, h: (0, 0))],
        out_specs=blk, out_shape=jax.ShapeDtypeStruct((t, width), BF16),
        compiler_params=_params("parallel", "parallel"),
    )(o, proj, g.reshape(1, -1))


def dn_norm_bwd(o, proj, g, dmixed, z_col0, dy_col0):
    t, width = o.shape
    heads = width // DN_HEAD_DIM
    tr = _tile(t, 512, 8)
    z_off, dy_off = z_col0 // LANES, dy_col0 // LANES

    def body(o_ref, z_ref, g_ref, dy_ref, do_ref, dz_ref, dg_ref):
        _, pull = jax.vjp(_gated_norm, o_ref[...], z_ref[...], g_ref[...])
        do, dz, dg = pull(dy_ref[...])
        do_ref[...] = do
        dz_ref[...] = dz.astype(BF16)
        _accumulate(dg_ref, dg, (pl.program_id(0) == 0) & (pl.program_id(1) == 0))

    blk = pl.BlockSpec((tr, DN_HEAD_DIM), lambda i, h: (i, h))
    vec = pl.BlockSpec((1, DN_HEAD_DIM), lambda i, h: (0, 0))
    do, dz, dg = pl.pallas_call(
        body, name="dn_norm_bwd", grid=(t // tr, heads),
        in_specs=[blk, pl.BlockSpec((tr, DN_HEAD_DIM), lambda i, h: (i, z_off + h)), vec,
                  pl.BlockSpec((tr, DN_HEAD_DIM), lambda i, h: (i, dy_off + h))],
        out_specs=[blk, blk, vec],
        out_shape=[jax.ShapeDtypeStruct((t, width), F32), jax.ShapeDtypeStruct((t, width), BF16),
                   jax.ShapeDtypeStruct((1, DN_HEAD_DIM), F32)],
        compiler_params=_params("arbitrary", "arbitrary"),
    )(o, proj, g.reshape(1, -1), dmixed)
    return do, dz, dg.reshape(-1)


def _attn_specs(b, s, mem_len, d):
    hd = d // XA_HEADS
    tq = _tile(s, 512, 8)
    nq = s // tq
    q_spec = pl.BlockSpec((tq, hd), lambda i, h, j: (i * nq + j, h))
    k_spec = pl.BlockSpec((mem_len, hd), lambda i, h, j: (i, h))
    v_spec = pl.BlockSpec((mem_len, hd), lambda i, h, j: (i, XA_HEADS + h))
    return nq, q_spec, k_spec, v_spec


def attn_fwd(q, kv, b, s):
    d = q.shape[1]
    nq, q_spec, k_spec, v_spec = _attn_specs(b, s, kv.shape[0] // b, d)

    def body(q_ref, k_ref, v_ref, o_ref):
        o_ref[...] = _attn_block(q_ref[...], k_ref[...], v_ref[...]).astype(BF16)

    return pl.pallas_call(
        body, name="attn_fwd", grid=(b, XA_HEADS, nq), in_specs=[q_spec, k_spec, v_spec], out_specs=q_spec,
        out_shape=jax.ShapeDtypeStruct(q.shape, BF16), compiler_params=_params("parallel", "parallel", "parallel"),
    )(q, kv, kv)


def attn_bwd(q, kv, do, b, s):
    d = q.shape[1]
    rows = kv.shape[0]
    nq, q_spec, k_spec, v_spec = _attn_specs(b, s, rows // b, d)

    def body(q_ref, k_ref, v_ref, do_ref, dq_ref, dk_ref, dv_ref):
        _, pull = jax.vjp(_attn_block, q_ref[...], k_ref[...], v_ref[...])
        dq, dk, dv = pull(do_ref[...])
        dq_ref[...] = dq.astype(BF16)
        first = pl.program_id(2) == 0
        _accumulate(dk_ref, dk, first)
        _accumulate(dv_ref, dv, first)

    kv_out = jax.ShapeDtypeStruct((rows, d), F32)
    return pl.pallas_call(
        body, name="attn_bwd", grid=(b, XA_HEADS, nq), in_specs=[q_spec, k_spec, v_spec, q_spec],
        out_specs=[q_spec, k_spec, k_spec], out_shape=[jax.ShapeDtypeStruct(q.shape, BF16), kv_out, kv_out],
        compiler_params=_params("parallel", "parallel", "arbitrary"),
    )(q, kv, kv, do)


def ffn_fwd(gate_pre, up, conv_w, conv_b, b, s):
    taps_n, f = conv_w.shape

    def body(g_ref, u_ref, w_ref, b_ref, y_ref):
        taps = [w_ref[k:k + 1, :] for k in range(taps_n)]
        y_ref[...] = _ffn_block(g_ref[...], u_ref[...], taps, b_ref[...]).astype(BF16)

    blk = pl.BlockSpec((s, FFN_COLS), lambda j, i: (i, j))
    return pl.pallas_call(
        body, name="ffn_fwd", grid=(f // FFN_COLS, b),
        in_specs=[blk, blk, pl.BlockSpec((taps_n, FFN_COLS), lambda j, i: (0, j)),
                  pl.BlockSpec((1, FFN_COLS), lambda j, i: (0, j))],
        out_specs=blk,
        out_shape=jax.ShapeDtypeStruct((b * s, f), BF16), compiler_params=_params("parallel", "parallel"),
    )(gate_pre, up, conv_w, conv_b.reshape(1, f))


def ffn_bwd(gate_pre, up, conv_w, conv_b, dact, b, s):
    taps_n, f = conv_w.shape

    def body(g_ref, u_ref, w_ref, b_ref, dy_ref, dg_ref, du_ref, dw_ref, db_ref):
        taps = [w_ref[k:k + 1, :] for k in range(taps_n)]
        _, pull = jax.vjp(lambda gt, up_, bias, *tp: _ffn_block(gt, up_, tp, bias), g_ref[...], u_ref[...],
                          b_ref[...], *taps)
        dgate, dup, dbias, *dtaps = pull(dy_ref[...])
        dg_ref[...] = dgate.astype(BF16)
        du_ref[...] = dup.astype(BF16)
        first = pl.program_id(1) == 0
        _accumulate(db_ref, dbias, first)
        for k in range(taps_n):
            _accumulate(dw_ref.at[k:k + 1, :], dtaps[k], first)

    blk = pl.BlockSpec((s, FFN_COLS), lambda j, i: (i, j))
    w_spec = pl.BlockSpec((taps_n, FFN_COLS), lambda j, i: (0, j))
    b_spec = pl.BlockSpec((1, FFN_COLS), lambda j, i: (0, j))
    half = jax.ShapeDtypeStruct(gate_pre.shape, BF16)
    dgate, dup, dw, db = pl.pallas_call(
        body, name="ffn_bwd", grid=(f // FFN_COLS, b),
        in_specs=[blk, blk, w_spec, b_spec, blk],
        out_specs=[blk, blk, w_spec, b_spec],
        out_shape=[half, half, jax.ShapeDtypeStruct((taps_n, f), F32), jax.ShapeDtypeStruct((1, f), F32)],
        compiler_params=_params("arbitrary", "arbitrary"),
    )(gate_pre, up, conv_w, conv_b.reshape(1, f), dact)
    return dgate, dup, dw, db.reshape(f)


def gate_arrays(logits, a_log, dt_bias, b, s, heads):
    n, r = s // DN_CHUNK, min(UT_ROWS, s)
    lg = logits.reshape(b, s, -1)
    beta = jax.nn.sigmoid(lg[..., :heads])
    g = -jnp.exp(a_log) * jax.nn.softplus(lg[..., heads:2 * heads] + dt_bias)
    gc = jnp.cumsum(g.reshape(b, n, DN_CHUNK, heads), axis=2).transpose(0, 3, 1, 2)
    return (beta.transpose(0, 2, 1)[..., None], gc.reshape(b, heads, s, 1), gc[:, :, :, None, :],
            gc.reshape(b, heads, s // r, 1, r))


class Riders:
    def __init__(self, make_ride, groups, arrays, tag):
        self.groups, self.results = groups, {}
        self.rides = {host: make_ride([arrays[n] for n in members], f"{tag}_on_{host}")
                      for host, members in groups.items()}

    def run(self, host, call):
        value, outs = call(self.rides[host])
        self.results.update(zip(self.groups[host], outs))
        return value


def _hosted_matmul(riders, host, *args, **kwargs):
    if riders is None:
        return matmul(*args, **kwargs)
    return riders.run(host, lambda ride: matmul(*args, ride=ride, **kwargs))


FORWARD_HOSTS = {'in_proj': ('w_in',), 'ut': ('w_gate', 'w_up'), 'gate': ('w_xkv',),
                 'up': ('w_mix_out', 'w_xq', 'w_xo', 'w_pool', 'dn_conv_w', 'ffn_conv_w'), 'down': ('w_down',)}
BACKWARD_HOSTS = {'down_wgrad': ('w_down',), 'gate_dgrad': ('w_xkv',), 'up_dgrad': ('w_in',),
                  'gate_wgrad': ('w_gate',), 'up_wgrad': ('w_up',),
                  'in_dgrad': ('w_mix_out', 'w_xq', 'w_xo', 'w_pool', 'dn_conv_w', 'ffn_conv_w')}


def forward_layer(x, mem_hb, p, b, s, riders=None):
    d = x.shape[1]
    pw = d // 2
    dn = d - pw
    heads = dn // DN_HEAD_DIM
    sv = {'x0': x}
    sv['h1'] = h1 = norm_fwd(x, p['mix_norm_g'])
    sv['proj'] = proj = _hosted_matmul(riders, 'in_proj', h1, p['w_in_main'], 'nn', F32)
    logits = matmul(h1, p['w_in_logits'], 'nn', F32)
    y_pool = pool_fwd(proj, p['w_pool'], p['pool_scale'], b, s)
    gates, sv['gates_pull'] = jax.vjp(lambda lg, al, dtb: gate_arrays(lg, al, dtb, b, s, heads), logits,
                                      p['dn_a_log'], p['dn_dt_bias'])
    sv['gates'] = beta_col, gc_col, gc_row, gc_row_ut = gates
    sv['act'] = aq, ak, av = [conv_silu_fwd(proj, p['dn_conv_w'], part, pw, dn, b, s) for part in range(3)]
    if riders is None:
        u, w, sv['inv'] = ut_fwd(ak, av, beta_col, gc_col, gc_row_ut, b, s)
    else:
        def hosted(ride):
            *own, outs = ut_fwd(ak, av, beta_col, gc_col, gc_row_ut, b, s, ride=ride)
            return own, outs
        u, w, sv['inv'] = riders.run('ut', hosted)
    sv['u'], sv['w'] = u, w
    sv['o_dn'], sv['states'] = o_dn, _ = rec_fwd(aq, ak, gc_col, gc_row, u, w, b, s)
    y_dn = dn_norm_fwd(o_dn, proj, p['dn_norm_g'], pw + 3 * dn)
    sv['mixed'] = mixed = jnp.concatenate([y_pool, y_dn], axis=1)
    sv['x1'] = x1 = matmul(mixed, p['w_mix_out'], 'nn', F32, res=x)

    sv['h2'] = h2 = norm_fwd(x1, p['xa_norm_g'])
    sv['q'] = q = matmul(h2, p['w_xq'], 'nn', F32)
    sv['kv'] = kv = matmul(mem_hb, p['w_xkv'], 'nn', F32)
    sv['o_at'] = o_at = attn_fwd(q, kv, b, s)
    sv['x2'] = x2 = matmul(o_at, p['w_xo'], 'nn', F32, res=x1)

    sv['h3'] = h3 = norm_fwd(x2, p['ffn_norm_g'])
    sv['gate_pre'] = gate_pre = _hosted_matmul(riders, 'gate', h3, p['w_gate'], 'nn', F32)
    sv['up'] = up = _hosted_matmul(riders, 'up', h3, p['w_up'], 'nn', F32)
    sv['a_ffn'] = a_ffn = ffn_fwd(gate_pre, up, p['ffn_conv_w'], p['ffn_conv_b'], b, s)
    return _hosted_matmul(riders, 'down', a_ffn, p['w_down'], 'nn', F32, res=x2), sv


def backward_layer(dx, dxb, sv, mem_hb, p, b, s, above=None):
    d = dx.shape[1]
    pw = d // 2
    dn = d - pw
    g = {}
    if above is None:
        riders = None
        da = matmul(dxb, p['w_down'], 'nt', F32)
    else:
        names, parts, core, tag = above
        da, from_sibling = matmul(dxb, p['w_down'], 'nt', F32, ride=sibling_swap_ride(parts, tag))
        chip_sums = {n: chip_sum(a, f, core) for n, a, f in zip(names, parts, from_sibling)}
        riders = Riders(chip_exchange_ride, BACKWARD_HOSTS, chip_sums, tag)
    g['w_down'] = _hosted_matmul(riders, 'down_wgrad', sv['a_ffn'], dxb, 'tn', BF16)
    dgate, dup, g['ffn_conv_w'], g['ffn_conv_b'] = ffn_bwd(sv['gate_pre'], sv['up'], p['ffn_conv_w'],
                                                           p['ffn_conv_b'], da, b, s)
    dh = _hosted_matmul(riders, 'gate_dgrad', dgate, p['w_gate'], 'nt', F32)
    dh = _hosted_matmul(riders, 'up_dgrad', dup, p['w_up'], 'nt', F32, res=dh)
    g['w_gate'] = _hosted_matmul(riders, 'gate_wgrad', sv['h3'], dgate, 'tn', BF16)
    g['w_up'] = _hosted_matmul(riders, 'up_wgrad', sv['h3'], dup, 'tn', BF16)
    dx, dxb, g['ffn_norm_g'] = norm_bwd(sv['x2'], p['ffn_norm_g'], dh, dx)

    do = matmul(dxb, p['w_xo'], 'nt', F32)
    g['w_xo'] = matmul(sv['o_at'], dxb, 'tn', BF16)
    dq, dk, dv = attn_bwd(sv['q'], sv['kv'], do, b, s)
    dkv = jnp.concatenate([dk, dv], axis=1).astype(BF16)
    dh = matmul(dq, p['w_xq'], 'nt', F32)
    g['w_xq'] = matmul(sv['h2'], dq, 'tn', BF16)
    g['w_xkv'] = matmul(mem_hb, dkv, 'tn', BF16)
    dmem_h = matmul(dkv, p['w_xkv'], 'nt', F32)
    dx, dxb, g['xa_norm_g'] = norm_bwd(sv['x1'], p['xa_norm_g'], dh, dx)

    dmixed = matmul(dxb, p['w_mix_out'], 'nt', F32)
    g['w_mix_out'] = matmul(sv['mixed'], dxb, 'tn', BF16)
    proj = sv['proj']
    du_pool, g['w_pool'], g['pool_scale'] = pool_bwd(proj, p['w_pool'], p['pool_scale'], dmixed, b, s)
    do_dn, dz, g['dn_norm_g'] = dn_norm_bwd(sv['o_dn'], proj, p['dn_norm_g'], dmixed, pw + 3 * dn, pw)
    beta_col, gc_col, gc_row, gc_row_ut = sv['gates']
    aq, ak, av = sv['act']
    daq, dak_rec, du, dw, dgcc_rec, dgcr = rec_bwd(aq, ak, gc_col, gc_row, sv['u'], sv['w'], sv['states'],
                                                   do_dn, b, s)
    dak, dav, dbeta, dgcc_ut, dgcr_ut = ut_bwd(ak, av, beta_col, gc_col, gc_row_ut, sv['inv'], du, dw, dak_rec, b, s)
    dparts, dtaps = zip(*[conv_silu_bwd(proj, p['dn_conv_w'], dact, part, pw, dn, b, s)
                          for part, dact in enumerate((daq, dak, dav))])
    g['dn_conv_w'] = jnp.concatenate(dtaps, axis=1)
    dlogits, g['dn_a_log'], g['dn_dt_bias'] = sv['gates_pull']((dbeta, dgcc_rec + dgcc_ut, dgcr, dgcr_ut))
    dproj = jnp.concatenate([du_pool, *dparts, dz], axis=1)
    dlogits = dlogits.astype(BF16)
    dh = _hosted_matmul(riders, 'in_dgrad', dproj, p['w_in_main'], 'nt', F32,
                        res=matmul(dlogits, p['w_in_logits'], 'nt', F32))
    g['w_in_main'] = matmul(sv['h1'], dproj, 'tn', BF16)
    g['w_in_logits'] = matmul(sv['h1'], dlogits, 'tn', BF16)
    dx, dxb, g['mix_norm_g'] = norm_bwd(sv['x0'], p['mix_norm_g'], dh, dx)
    return dx, dxb, dmem_h, g, (riders.results if riders else None)


def _to_full(gathered, axis):
    moved = jnp.moveaxis(gathered, 0, axis)
    shape = list(gathered.shape[1:])
    shape[axis] *= N_DEV
    return moved.reshape(shape)


def _to_parts(full, axis):
    shape = list(full.shape)
    shape[axis:axis + 1] = [N_DEV, shape[axis] // N_DEV]
    return jnp.moveaxis(full.reshape(shape), axis, 0)


def _as_pack_rows(a, cols):
    rows = -(-a.shape[0] // (8 * cols)) * 8
    return jnp.pad(a, (0, rows * cols - a.shape[0])).reshape(rows, cols)


def kernel(x, mem, mix_norm_g, w_in, w_pool, pool_scale, dn_conv_w, dn_a_log, dn_dt_bias, dn_norm_g, w_mix_out, xa_norm_g, mem_norm_g, w_xq, w_xkv, w_xo, ffn_norm_g, w_gate, w_up, ffn_conv_w, ffn_conv_b, w_down, final_norm_g, loss_target, m_mix_norm_g, m_w_in, m_w_pool, m_pool_scale, m_dn_conv_w, m_dn_a_log, m_dn_dt_bias, m_dn_norm_g, m_w_mix_out, m_xa_norm_g, m_mem_norm_g, m_w_xq, m_w_xkv, m_w_xo, m_ffn_norm_g, m_w_gate, m_w_up, m_ffn_conv_w, m_ffn_conv_b, m_w_down, m_final_norm_g, v_mix_norm_g, v_w_in, v_w_pool, v_pool_scale, v_dn_conv_w, v_dn_a_log, v_dn_dt_bias, v_dn_norm_g, v_w_mix_out, v_xa_norm_g, v_mem_norm_g, v_w_xq, v_w_xkv, v_w_xo, v_ffn_norm_g, v_w_gate, v_w_up, v_ffn_conv_w, v_ffn_conv_b, v_w_down, v_final_norm_g):
    given = dict(locals())
    w = {n: given[n] for n in WEIGHTS}
    mom = {n: given['m_' + n] for n in WEIGHTS}
    var = {n: given['v_' + n] for n in WEIGHTS}
    b, s, d = x.shape
    depth = w_in.shape[0]
    main = 5 * (d // 2)
    n_logits = w_in.shape[-1] * N_DEV - main
    core = jnp.reshape(lax.axis_index("c"), (1,)).astype(jnp.int32)

    names = MATRICES + TAPS

    def shards(l):
        return {n: w[n][l] if n in TAPS else w[n][l].astype(BF16) for n in names}

    def layer_weights(l, gathered):
        full = {n: _to_full(gathered[n], SHARD_AXIS[n] - 1) for n in names}
        p = {n: full[n] for n in names if n != 'w_in'}
        p['w_in_main'] = full['w_in'][:, :main]
        p['w_in_logits'] = jnp.pad(full['w_in'][:, main:], ((0, 0), (0, LANES - n_logits)))
        for n in PER_LAYER_REPLICATED:
            p[n] = w[n][l]
        return p

    mem2 = mem.reshape(-1, d)
    mem_hb = norm_fwd(mem2, mem_norm_g)
    xc, saved, layers = x.reshape(b * s, d), [], []
    gathered = dict(zip(names, all_gather([shards(0)[n] for n in names], "layer0")))
    for l in range(depth):
        layers.append(layer_weights(l, gathered))
        riders = Riders(gather_ride, FORWARD_HOSTS, shards(l + 1), f"layer{l + 1}") if l + 1 < depth else None
        xc, sv = forward_layer(xc, mem_hb, layers[l], b, s, riders)
        saved.append(sv)
        gathered = riders.results if riders else None
    loss, dx, dxb, g_final = loss_head(xc, loss_target.reshape(b * s, d), final_norm_g)

    def as_layers(a, cols):
        return a.reshape(depth, -1, cols)

    results = {n: None for n in names}

    def update(l, got):
        for n in names:
            cols = w[n].shape[-1]
            results[n] = adamw(got[n], as_layers(w[n], cols), as_layers(mom[n], cols), as_layers(var[n], cols),
                               layer=l, prev=results[n])

    g_layers, dmem_h, above = [None] * depth, None, None
    for l in reversed(range(depth)):
        dx, dxb, dm, g, reduced = backward_layer(dx, dxb, saved[l], mem_hb, layers[l], b, s, above)
        if reduced:
            update(l + 1, reduced)
        g_layers[l] = g
        dmem_h = dm if dmem_h is None else dmem_h + dm
        g_full = {n: g[n] for n in names if n not in ('w_in', 'w_pool')}
        g_full['w_pool'] = g['w_pool'].astype(BF16)
        g_full['w_in'] = jnp.concatenate([g['w_in_main'], g['w_in_logits'][:, :n_logits]], axis=1)
        parts = []
        for n in names:
            by_dev = _to_parts(g_full[n], SHARD_AXIS[n] - 1)
            parts.append(by_dev.reshape(N_DEV, -1, by_dev.shape[-1]))
        above = (names, parts, core, f"layer{l}")
    from_sibling = sibling_swap_ride(parts, "layer0").alone()
    chip_sums = [chip_sum(a, f, core) for a, f in zip(parts, from_sibling)]
    update(0, dict(zip(names, chip_exchange_ride(chip_sums, "layer0").alone())))
    _, _, g_mem = norm_bwd(mem2, mem_norm_g, dmem_h, None)
    grad_x = dx.reshape(b, s, d)

    grad, delta, new_m, new_v = {}, {}, {}, {}
    for n in names:
        grad[n], delta[n], new_m[n], new_v[n] = [o.reshape(w[n].shape) for o in results[n]]

    g_small = {n: jnp.stack([g[n] for g in g_layers]) for n in PER_LAYER_REPLICATED}
    g_small['mem_norm_g'], g_small['final_norm_g'] = g_mem, g_final
    sizes = [w[n].size for n in REPLICATED]

    def pack(parts, first):
        flat = jnp.concatenate([jnp.reshape(first, (1,))] + [parts[n].reshape(-1) for n in REPLICATED])
        return _as_pack_rows(flat, LANES)

    zero = jnp.zeros((), F32)
    everyone, = all_gather([pack(g_small, loss)], "replicated")
    outs = adamw(everyone, pack(w, zero)[None], pack(mom, zero)[None], pack(var, zero)[None])
    flat_outs = [o.reshape(-1) for o in outs]
    loss_total = flat_outs[0][0]
    offset = 1
    for n, size in zip(REPLICATED, sizes):
        grad[n], delta[n], new_m[n], new_v[n] = [o[offset:offset + size].reshape(w[n].shape) for o in flat_outs]
        offset += size

    return (loss_total, grad_x, *[grad[n] for n in WEIGHTS], *[delta[n] for n in WEIGHTS],
            *[new_m[n] for n in WEIGHTS], *[new_v[n] for n in WEIGHTS])
```

```python
import functools

import jax
import jax.numpy as jnp
from jax import lax
from jax.experimental import pallas as pl
from jax.experimental.pallas import tpu as pltpu

F32 = jnp.float32
BF16 = jnp.bfloat16
MESH = pl.DeviceIdType.MESH

N_DEV = 8
EPS = 1e-6
POOL_WINDOWS = (2, 4, 8, 16)
DN_HEAD_DIM = 128
DN_CHUNK = 64
DN_TAPS = 4
XA_HEADS = 4
ADAM_LR = 0.001
ADAM_B1 = 0.9
ADAM_B2 = 0.999
ADAM_EPS = 1e-08
ADAM_WD = 0.01
ADAM_STEP = 10

LANES = 128
VMEM_LIMIT = 48 * 1024 * 1024
UT_ROWS = 256
FFN_COLS = 256

WEIGHTS = ['mix_norm_g', 'w_in', 'w_pool', 'pool_scale', 'dn_conv_w', 'dn_a_log', 'dn_dt_bias', 'dn_norm_g',
           'w_mix_out', 'xa_norm_g', 'mem_norm_g', 'w_xq', 'w_xkv', 'w_xo', 'ffn_norm_g', 'w_gate', 'w_up',
           'ffn_conv_w', 'ffn_conv_b', 'w_down', 'final_norm_g']
SHARD_AXIS = {'w_in': 2, 'w_pool': 2, 'dn_conv_w': 2, 'w_mix_out': 1, 'w_xq': 1, 'w_xkv': 2, 'w_xo': 1,
              'w_gate': 2, 'w_up': 2, 'ffn_conv_w': 2, 'w_down': 1}
MATRICES = ('w_in', 'w_pool', 'w_mix_out', 'w_xq', 'w_xkv', 'w_xo', 'w_gate', 'w_up', 'w_down')
TAPS = ('dn_conv_w', 'ffn_conv_w')
REPLICATED = [n for n in WEIGHTS if n not in SHARD_AXIS]
PER_LAYER_REPLICATED = ('mix_norm_g', 'pool_scale', 'dn_a_log', 'dn_dt_bias', 'dn_norm_g', 'xa_norm_g',
                        'ffn_norm_g', 'ffn_conv_b')


def _params(*semantics):
    return pltpu.CompilerParams(dimension_semantics=semantics, vmem_limit_bytes=VMEM_LIMIT)


def _tile(dim, pref, unit=LANES):
    if dim <= pref:
        return dim
    t = (pref // unit) * unit
    while t >= unit:
        if dim % t == 0:
            return t
        t -= unit
    return dim


class Ride:
    def __init__(self, tag, arrays, out_shapes, n_sems, n_local, start, finish):
        self.tag, self.arrays, self.out_shapes = tag, list(arrays), list(out_shapes)
        self.start, self.finish = start, finish
        self.scratch = [pltpu.SemaphoreType.DMA((n_sems,)), pltpu.SemaphoreType.DMA((n_sems,)),
                        pltpu.SemaphoreType.DMA((n_local,))]

    @property
    def specs_in(self):
        return [pl.BlockSpec(memory_space=pl.ANY)] * len(self.arrays)

    @property
    def specs_out(self):
        return [pl.BlockSpec(memory_space=pl.ANY)] * len(self.out_shapes)

    def alone(self):
        n = len(self.arrays)

        def body(*refs):
            ins, outs, sems = refs[:n], refs[n:n + len(self.out_shapes)], refs[n + len(self.out_shapes):]
            self.start(ins, outs, *sems)
            self.finish(ins, outs, *sems)

        return pl.pallas_call(body, name=self.tag, out_shape=self.out_shapes, in_specs=self.specs_in,
                              out_specs=self.specs_out, scratch_shapes=self.scratch)(*self.arrays)


def _first_and_last_step(grid):
    ids = [pl.program_id(axis) for axis in range(len(grid))]
    first = functools.reduce(jnp.logical_and, [i == 0 for i in ids])
    last = functools.reduce(jnp.logical_and, [i == n - 1 for i, n in zip(ids, grid)])
    return first, last


def matmul(a, b, mode, out_dtype, res=None, ride=None):
    assert a.dtype == BF16 and b.dtype == BF16, (a.dtype, b.dtype)
    if mode == 'nn':
        (m, c), (c2, n) = a.shape, b.shape
    elif mode == 'nt':
        (m, c), (n, c2) = a.shape, b.shape
    else:
        (c, m), (c2, n) = a.shape, b.shape
    assert c == c2, (mode, a.shape, b.shape)
    tm, tn = (_tile(m, 2048), _tile(n, 512)) if mode == 'nt' else (_tile(m, 1024), _tile(n, 1408))
    tc = _tile(c, 2048)
    nc = c // tc
    if mode == 'nn':
        a_spec = pl.BlockSpec((tm, tc), lambda i, j, k: (i, k))
        b_spec = pl.BlockSpec((tc, tn), lambda i, j, k: (k, j))
        dims = (((1,), (0,)), ((), ()))
    elif mode == 'nt':
        a_spec = pl.BlockSpec((tm, tc), lambda i, j, k: (i, k))
        b_spec = pl.BlockSpec((tn, tc), lambda i, j, k: (j, k))
        dims = (((1,), (1,)), ((), ()))
    else:
        a_spec = pl.BlockSpec((tc, tm), lambda i, j, k: (k, i))
        b_spec = pl.BlockSpec((tc, tn), lambda i, j, k: (k, j))
        dims = (((0,), (0,)), ((), ()))
    out_spec = pl.BlockSpec((tm, tn), lambda i, j, k: (i, j))
    has_res = res is not None
    grid = (m // tm, n // tn, nc)
    n_ride_in = len(ride.arrays) if ride else 0
    n_ride_out = len(ride.out_shapes) if ride else 0

    def body(a_ref, b_ref, *rest):
        rest = list(rest)
        r_ref = rest.pop(0) if has_res else None
        ride_in = [rest.pop(0) for _ in range(n_ride_in)]
        o_ref = rest.pop(0)
        ride_out = [rest.pop(0) for _ in range(n_ride_out)]
        acc_ref = rest.pop(0) if nc > 1 else None
        if ride:
            first, last = _first_and_last_step(grid)
            pl.when(first)(lambda: ride.start(ride_in, ride_out, *rest))
        prod = lax.dot_general(a_ref[...], b_ref[...], dims, preferred_element_type=F32)

        def finish(total):
            if has_res:
                total = total + r_ref[...]
            o_ref[...] = total.astype(o_ref.dtype)

        if nc == 1:
            finish(prod)
        else:
            k = pl.program_id(2)

            @pl.when(k == 0)
            def _():
                acc_ref[...] = prod

            @pl.when(k > 0)
            def _():
                acc_ref[...] += prod

            @pl.when(k == nc - 1)
            def _():
                finish(acc_ref[...])

        if ride:
            pl.when(last)(lambda: ride.finish(ride_in, ride_out, *rest))

    outs = pl.pallas_call(
        body,
        name=f"mm_{mode}_{m}x{c}x{n}" + ("_res" if has_res else "") + ("_with_" + ride.tag if ride else ""),
        grid=grid,
        in_specs=[a_spec, b_spec] + ([out_spec] if has_res else []) + (ride.specs_in if ride else []),
        out_specs=[out_spec] + (ride.specs_out if ride else []),
        out_shape=[jax.ShapeDtypeStruct((m, n), out_dtype)] + (ride.out_shapes if ride else []),
        scratch_shapes=([] if nc == 1 else [pltpu.VMEM((tm, tn), F32)]) + (ride.scratch if ride else []),
        compiler_params=(_params("arbitrary", "arbitrary", "arbitrary") if ride
                         else _params("parallel", "parallel", "arbitrary")),
    )(a, b, *([res] if has_res else []), *(ride.arrays if ride else []))
    return (outs[0], outs[1:]) if ride else outs[0]


def _position():
    return lax.axis_index("x"), lax.axis_index("y"), lax.axis_index("c")


def _flip(v, bit):
    return 1 - v if bit else v


def _dev_index(px, py, pc):
    return 4 * px + 2 * py + pc


def gather_ride(shards, tag):
    n = len(shards)

    def plan(x_refs, out_refs, send_sems, recv_sems, local_sems):
        x, y, c = _position()
        me = (x, y, c)

        def copy(t, k, block, to, own=False):
            slot = out_refs[t].at[_dev_index(*block)]
            return pltpu.make_async_remote_copy(
                src_ref=x_refs[t] if own else slot, dst_ref=slot, send_sem=send_sems.at[7 * t + k],
                recv_sem=recv_sems.at[7 * t + k], device_id=to, device_id_type=MESH)

        def local(t):
            return pltpu.make_async_copy(x_refs[t], out_refs[t].at[_dev_index(*me)], local_sems.at[t])

        return me, (x, y, 1 - c), [(1 - x, y), (x, 1 - y), (1 - x, 1 - y)], copy, local

    def start(x_refs, out_refs, *sems):
        me, sibling, chips, copy, local = plan(x_refs, out_refs, *sems)
        for t in range(n):
            local(t).start()
            copy(t, 0, me, sibling, own=True).start()
            for j, chip in enumerate(chips):
                copy(t, 1 + j, me, (*chip, me[2]), own=True).start()

    def finish(x_refs, out_refs, *sems):
        me, sibling, chips, copy, local = plan(x_refs, out_refs, *sems)
        c = me[2]
        for j, chip in enumerate(chips):
            for t in range(n):
                copy(t, 1 + j, (*chip, c), me).wait_recv()
                copy(t, 4 + j, (*chip, c), sibling).start()
        for t in range(n):
            copy(t, 0, sibling, me).wait_recv()
            for j, chip in enumerate(chips):
                copy(t, 4 + j, (*chip, 1 - c), me).wait_recv()
        for t in range(n):
            copy(t, 0, me, sibling, own=True).wait_send()
            for j, chip in enumerate(chips):
                copy(t, 1 + j, me, (*chip, c), own=True).wait_send()
                copy(t, 4 + j, (*chip, c), sibling).wait_send()
            local(t).wait()

    return Ride("all_gather_" + tag, shards, [jax.ShapeDtypeStruct((N_DEV,) + a.shape, a.dtype) for a in shards],
                7 * n, n, start, finish)


def all_gather(shards, tag):
    return gather_ride(shards, tag).alone()


def sibling_swap_ride(parts, tag):
    n = len(parts)

    def plan(p_refs, out_refs, send_sems, recv_sems, _):
        x, y, c = _position()
        return [pltpu.make_async_remote_copy(
            src_ref=p_refs[t].at[2 * k + (1 - c)], dst_ref=out_refs[t].at[k], send_sem=send_sems.at[4 * t + k],
            recv_sem=recv_sems.at[4 * t + k], device_id=(x, y, 1 - c), device_id_type=MESH)
            for t in range(n) for k in range(N_DEV // 2)]

    def start(p_refs, out_refs, *sems):
        for cp in plan(p_refs, out_refs, *sems):
            cp.start()

    def finish(p_refs, out_refs, *sems):
        for cp in plan(p_refs, out_refs, *sems):
            cp.wait()

    return Ride("sibling_swap_" + tag, parts,
                [jax.ShapeDtypeStruct((N_DEV // 2,) + a.shape[1:], a.dtype) for a in parts], 4 * n, 1, start, finish)


def chip_sum(parts, from_sibling, core):
    _, r, c = parts.shape
    unit = 16 if parts.dtype == BF16 else 8
    tr = _tile(r, max(unit, (512 * 1024 // c) // unit * unit), unit)

    def body(core_ref, a_ref, b_ref, o_ref):
        o_ref[...] = (a_ref[...].astype(F32) + b_ref[...].astype(F32)).astype(o_ref.dtype)

    blk = pl.BlockSpec((None, tr, c), lambda k, i, core_ref: (k, i, 0))
    return pl.pallas_call(
        body, name=f"chip_sum_{r}x{c}_{parts.dtype.name}",
        grid_spec=pltpu.PrefetchScalarGridSpec(
            num_scalar_prefetch=1, grid=(N_DEV // 2, r // tr),
            in_specs=[pl.BlockSpec((None, tr, c), lambda k, i, core_ref: (2 * k + core_ref[0], i, 0)), blk],
            out_specs=blk),
        out_shape=jax.ShapeDtypeStruct(from_sibling.shape, from_sibling.dtype),
        compiler_params=_params("parallel", "parallel"),
    )(core, parts, from_sibling)


def chip_exchange_ride(parts, tag):
    n = len(parts)

    def plan(p_refs, out_refs, send_sems, recv_sems, local_sems):
        x, y, c = _position()
        my_chip = 2 * x + y
        local = [pltpu.make_async_copy(p_refs[t].at[my_chip], out_refs[t].at[my_chip], local_sems.at[t])
                 for t in range(n)]
        copies = []
        for t in range(n):
            for k in (1, 2, 3):
                px, py = _flip(x, k & 2), _flip(y, k & 1)
                copies.append(pltpu.make_async_remote_copy(
                    src_ref=p_refs[t].at[2 * px + py], dst_ref=out_refs[t].at[my_chip],
                    send_sem=send_sems.at[3 * t + k - 1], recv_sem=recv_sems.at[3 * t + k - 1],
                    device_id=(px, py, c), device_id_type=MESH))
        return local, copies

    def start(p_refs, out_refs, *sems):
        local, copies = plan(p_refs, out_refs, *sems)
        for cp in local + copies:
            cp.start()

    def finish(p_refs, out_refs, *sems):
        local, copies = plan(p_refs, out_refs, *sems)
        for cp in copies:
            cp.wait_recv()
        for cp in copies:
            cp.wait_send()
        for cp in local:
            cp.wait()

    return Ride("chip_exchange_" + tag, parts, [jax.ShapeDtypeStruct(a.shape, a.dtype) for a in parts],
                3 * n, n, start, finish)


def adamw(parts, w, m, v, layer=0, prev=None):
    n_parts, r, c = parts.shape
    depth = w.shape[0]
    unit = 16 if parts.dtype == BF16 else 8
    tr = _tile(r, max(unit, (256 * 1024 // c) // unit * unit), unit)

    def body(p_ref, w_ref, m_ref, v_ref, *rest):
        g_ref, d_ref, nm_ref, nv_ref = rest[-4:]
        g = p_ref[0].astype(F32)
        for j in range(1, n_parts):
            g = g + p_ref[j].astype(F32)
        nm = ADAM_B1 * m_ref[...] + (1.0 - ADAM_B1) * g
        nv = ADAM_B2 * v_ref[...] + (1.0 - ADAM_B2) * jnp.square(g)
        m_hat = nm / (1.0 - ADAM_B1 ** ADAM_STEP)
        v_hat = nv / (1.0 - ADAM_B2 ** ADAM_STEP)
        g_ref[...] = g
        d_ref[...] = -ADAM_LR * (m_hat / (jnp.sqrt(v_hat) + ADAM_EPS) + ADAM_WD * w_ref[...])
        nm_ref[...] = nm
        nv_ref[...] = nv

    spec = pl.BlockSpec((None, tr, c), lambda i: (layer, i, 0))
    out = jax.ShapeDtypeStruct((depth, r, c), F32)
    carried = [] if prev is None else list(prev)
    return pl.pallas_call(
        body,
        name=f"adamw_{n_parts}x{r}x{c}_{parts.dtype.name}_layer{layer}",
        grid=(r // tr,),
        in_specs=[pl.BlockSpec((n_parts, tr, c), lambda i: (0, i, 0)), spec, spec, spec]
        + [pl.BlockSpec(memory_space=pl.ANY)] * len(carried),
        out_specs=[spec, spec, spec, spec],
        out_shape=[out, out, out, out],
        input_output_aliases={4 + k: k for k in range(len(carried))},
        compiler_params=_params("parallel"),
    )(parts, w, m, v, *carried)


_NN = (((1,), (0,)), ((), ()))
_NT = (((1,), (1,)), ((), ()))
_TN = (((0,), (0,)), ((), ()))


def _dot_bf16(a, b, dims):
    return lax.dot_general(a.astype(BF16), b.astype(BF16), dims, preferred_element_type=F32)


def _dot_split(a, b, dims):
    a_hi, b_hi = a.astype(BF16), b.astype(BF16)
    a_lo = (a - a_hi.astype(F32)).astype(BF16)
    b_lo = (b - b_hi.astype(F32)).astype(BF16)
    dot = functools.partial(lax.dot_general, dimension_numbers=dims, preferred_element_type=F32)
    return dot(a_hi, b_hi) + (dot(a_hi, b_lo) + dot(a_lo, b_hi))


def _matmul_family(dot):
    @jax.custom_vjp
    def nn(a, b):
        return dot(a, b, _NN)

    @jax.custom_vjp
    def nt(a, b):
        return dot(a, b, _NT)

    @jax.custom_vjp
    def tn(a, b):
        return dot(a, b, _TN)

    nn.defvjp(lambda a, b: (nn(a, b), (a, b)), lambda r, g: (nt(g, r[1]), tn(r[0], g)))
    nt.defvjp(lambda a, b: (nt(a, b), (a, b)), lambda r, g: (nn(g, r[1]), tn(g, r[0])))
    tn.defvjp(lambda a, b: (tn(a, b), (a, b)), lambda r, g: (nt(r[1], g), nn(r[0], g)))
    return nn, nt, tn


mm, mm_nt, mm_tn = _matmul_family(_dot_bf16)
mms, mms_nt, mms_tn = _matmul_family(_dot_split)


def _shift_rows(x, k, down):
    n = x.shape[0]
    rows = lax.broadcasted_iota(jnp.int32, x.shape, 0)
    if down:
        return jnp.where(rows >= k, pltpu.roll(x, k, 0), 0.0)
    return jnp.where(rows < n - k, pltpu.roll(x, n - k, 0), 0.0)


@functools.partial(jax.custom_vjp, nondiff_argnums=(1,))
def delay(x, k):
    return _shift_rows(x, k, True) if k else x


delay.defvjp(lambda x, k: (delay(x, k), None), lambda k, _, g: ((_shift_rows(g, k, False) if k else g),))


def _silu(x):
    return x * jax.nn.sigmoid(x)


def _rms(x, g):
    return x * lax.rsqrt(jnp.mean(x * x, axis=-1, keepdims=True) + EPS) * g


def _l2n(t):
    return t * lax.rsqrt(jnp.sum(t * t, axis=-1, keepdims=True) + EPS)


def _causal_conv(x, taps):
    k_taps = len(taps)
    y = delay(x, k_taps - 1) * taps[0]
    for k in range(1, k_taps):
        y = y + delay(x, k_taps - 1 - k) * taps[k]
    return y


def _pool_block(u, w, scale, group):
    sums, acc, width = [], u, 1
    while width < POOL_WINDOWS[-1]:
        acc = acc + delay(acc, width)
        width *= 2
        sums.append(acc)
    picked = sums[-1]
    for i in range(len(POOL_WINDOWS) - 2, -1, -1):
        picked = jnp.where(group == i, sums[i], picked)
    rows = lax.broadcasted_iota(jnp.int32, u.shape, 0)
    count = jnp.minimum(rows + 1, jnp.left_shift(2, group)).astype(F32)
    return mm(picked / count - u, w) * scale


def _unit_lower_inverse(l_mat):
    n = l_mat.shape[0]
    eye = (lax.broadcasted_iota(jnp.int32, (n, n), 0) == lax.broadcasted_iota(jnp.int32, (n, n), 1)).astype(F32)
    m1 = -l_mat
    m2 = mms(m1, m1)
    m4 = mms(m2, m2)
    m8 = mms(m4, m4)
    m16 = mms(m8, m8)
    m32 = mms(m16, m16)
    low = mms(eye + m1, eye + m2)
    mid = mms(eye + m4, eye + m8)
    high = mms(eye + m16, eye + m32)
    return mms(mms(low, mid), high)


@jax.custom_vjp
def _known_inverse(l_mat, inv):
    return inv


_known_inverse.defvjp(lambda l_mat, inv: (inv, inv),
                      lambda inv, g: (-mms_tn(inv, mms_nt(g, inv)), jnp.zeros_like(inv)))


def _ut_chain(k, v, beta, gc_c, gc_r, inv=None):
    r = k.shape[0]
    kn = _l2n(k)
    row = lax.broadcasted_iota(jnp.int32, (r, r), 0)
    col = lax.broadcasted_iota(jnp.int32, (r, r), 1)
    strict = (row // DN_CHUNK == col // DN_CHUNK) & (row > col)
    decay = jnp.exp(jnp.where(strict, gc_c - gc_r, -1e30))
    l_mat = jnp.where(strict, beta * mm_nt(kn, kn) * decay, 0.0)
    inv = _unit_lower_inverse(l_mat) if inv is None else _known_inverse(l_mat, inv)
    return mms(inv, v * beta), mms(inv, kn * (beta * jnp.exp(gc_c))), inv


def _rec_chain(q, k, gc_c, gc_r, u, w, state):
    ch, dh = q.shape
    qn = _l2n(q) * (dh ** -0.5)
    kn = _l2n(k)
    row = lax.broadcasted_iota(jnp.int32, (ch, ch), 0)
    col = lax.broadcasted_iota(jnp.int32, (ch, ch), 1)
    decay = jnp.exp(jnp.where(row >= col, gc_c - gc_r, -1e30))
    attn = mm_nt(qn, kn) * decay
    is_last = lax.broadcasted_iota(jnp.int32, gc_r.shape, 1) == ch - 1
    last = jnp.sum(jnp.where(is_last, gc_r, 0.0), axis=1, keepdims=True)
    v_new = u - mm(w, state)
    out = mm(qn * jnp.exp(gc_c), state) + mm(attn, v_new)
    return out, state * jnp.exp(last) + mm_tn(kn * jnp.exp(last - gc_c), v_new)


def _gated_norm(o, z, g):
    return _rms(o, g) * _silu(z)


def _attn_block(q, k, v):
    s = mm_nt(q, k) * (q.shape[-1] ** -0.5)
    p = jnp.exp(s - lax.stop_gradient(jnp.max(s, axis=-1, keepdims=True)))
    return mm(p / jnp.sum(p, axis=-1, keepdims=True), v)


def _ffn_block(gate_pre, up, taps, bias):
    return _silu(_causal_conv(gate_pre, taps) + bias) * up


def _accumulate(ref, value, first):
    @pl.when(first)
    def _():
        ref[...] = value

    @pl.when(jnp.logical_not(first))
    def _():
        ref[...] += value


def norm_fwd(x, g):
    t, d = x.shape
    tr = _tile(t, 256, 8)

    def body(x_ref, g_ref, h_ref):
        h_ref[...] = _rms(x_ref[...], g_ref[...]).astype(BF16)

    return pl.pallas_call(
        body, name=f"norm_fwd_{t}", grid=(t // tr,),
        in_specs=[pl.BlockSpec((tr, d), lambda i: (i, 0)), pl.BlockSpec((1, d), lambda i: (0, 0))],
        out_specs=pl.BlockSpec((tr, d), lambda i: (i, 0)),
        out_shape=jax.ShapeDtypeStruct((t, d), BF16), compiler_params=_params("parallel"),
    )(x, g.reshape(1, d))


def norm_bwd(x, g, dh, dres):
    t, d = x.shape
    tr = _tile(t, 256, 8)
    has_res = dres is not None

    def body(x_ref, g_ref, dh_ref, *rest):
        rest = list(rest)
        r_ref = rest.pop(0) if has_res else None
        dx_ref, dxb_ref, dg_ref = rest
        _, pull = jax.vjp(_rms, x_ref[...], g_ref[...])
        dx, dg = pull(dh_ref[...])
        if has_res:
            dx = dx + r_ref[...]
        dx_ref[...] = dx
        dxb_ref[...] = dx.astype(BF16)
        _accumulate(dg_ref, dg, pl.program_id(0) == 0)

    row = pl.BlockSpec((tr, d), lambda i: (i, 0))
    vec = pl.BlockSpec((1, d), lambda i: (0, 0))
    dx, dxb, dg = pl.pallas_call(
        body, name=f"norm_bwd_{t}" + ("_res" if has_res else ""), grid=(t // tr,),
        in_specs=[row, vec, row] + ([row] if has_res else []),
        out_specs=[row, row, vec],
        out_shape=[jax.ShapeDtypeStruct((t, d), F32), jax.ShapeDtypeStruct((t, d), BF16),
                   jax.ShapeDtypeStruct((1, d), F32)],
        compiler_params=_params("arbitrary"),
    )(x, g.reshape(1, d), dh, *([dres] if has_res else []))
    return dx, dxb, dg.reshape(d)


def loss_head(x, target, g):
    t, d = x.shape
    tr = _tile(t, 256, 8)

    def body(x_ref, t_ref, g_ref, l_ref, dx_ref, dxb_ref, dg_ref):
        tgt = t_ref[...]

        def block_loss(xv, gv):
            return 0.5 * jnp.sum(jnp.mean(jnp.square(_rms(xv, gv) - tgt), axis=-1))

        val, pull = jax.vjp(block_loss, x_ref[...], g_ref[...])
        dx, dg = pull(jnp.ones((), F32))
        dx_ref[...] = dx
        dxb_ref[...] = dx.astype(BF16)
        first = pl.program_id(0) == 0
        _accumulate(dg_ref, dg, first)
        _accumulate(l_ref, jnp.full((1, LANES), val, F32), first)

    row = pl.BlockSpec((tr, d), lambda i: (i, 0))
    vec = pl.BlockSpec((1, d), lambda i: (0, 0))
    loss, dx, dxb, dg = pl.pallas_call(
        body, name="loss_head", grid=(t // tr,),
        in_specs=[row, row, vec],
        out_specs=[pl.BlockSpec((1, LANES), lambda i: (0, 0)), row, row, vec],
        out_shape=[jax.ShapeDtypeStruct((1, LANES), F32), jax.ShapeDtypeStruct((t, d), F32),
                   jax.ShapeDtypeStruct((t, d), BF16), jax.ShapeDtypeStruct((1, d), F32)],
        compiler_params=_params("arbitrary"),
    )(x, target, g.reshape(1, d))
    return loss[0, 0], dx, dxb, dg.reshape(d)


def pool_fwd(proj, w_pool, scale, b, s):
    n_g, grp = w_pool.shape[0], w_pool.shape[-1]

    def body(u_ref, w_ref, s_ref, y_ref):
        y_ref[...] = _pool_block(u_ref[...], w_ref[...], s_ref[...], pl.program_id(1)).astype(BF16)

    blk = pl.BlockSpec((s, grp), lambda i, j: (i, j))
    return pl.pallas_call(
        body, name="pool_fwd", grid=(b, n_g),
        in_specs=[blk, pl.BlockSpec((None, grp, grp), lambda i, j: (j, 0, 0)),
                  pl.BlockSpec((1, grp), lambda i, j: (0, j))],
        out_specs=blk,
        out_shape=jax.ShapeDtypeStruct((b * s, n_g * grp), BF16), compiler_params=_params("parallel", "parallel"),
    )(proj, w_pool, scale.reshape(1, -1))


def pool_bwd(proj, w_pool, scale, dmixed, b, s):
    n_g, grp = w_pool.shape[0], w_pool.shape[-1]

    def body(u_ref, w_ref, s_ref, dy_ref, du_ref, dw_ref, ds_ref):
        group = pl.program_id(0)
        _, pull = jax.vjp(lambda u, w, sc: _pool_block(u, w, sc, group), u_ref[...], w_ref[...].astype(F32),
                          s_ref[...])
        du, dw, ds = pull(dy_ref[...])
        du_ref[...] = du.astype(BF16)
        first = pl.program_id(1) == 0
        _accumulate(dw_ref, dw, first)
        _accumulate(ds_ref, ds, first)

    blk = pl.BlockSpec((s, grp), lambda j, i: (i, j))
    w_spec = pl.BlockSpec((None, grp, grp), lambda j, i: (j, 0, 0))
    s_spec = pl.BlockSpec((1, grp), lambda j, i: (0, j))
    du, dw, ds = pl.pallas_call(
        body, name="pool_bwd", grid=(n_g, b),
        in_specs=[blk, w_spec, s_spec, blk],
        out_specs=[blk, w_spec, s_spec],
        out_shape=[jax.ShapeDtypeStruct((b * s, n_g * grp), BF16), jax.ShapeDtypeStruct(w_pool.shape, F32),
                   jax.ShapeDtypeStruct((1, n_g * grp), F32)],
        compiler_params=_params("arbitrary", "arbitrary"),
    )(proj, w_pool, scale.reshape(1, -1), dmixed)
    return du, dw, ds.reshape(-1)


def conv_silu_fwd(proj, conv_w, part, col0, width, b, s):
    x_off, w_off = (col0 + part * width) // FFN_COLS, part * width // FFN_COLS

    def body(x_ref, w_ref, y_ref):
        taps = [w_ref[k:k + 1, :] for k in range(DN_TAPS)]
        y_ref[...] = _silu(_causal_conv(x_ref[...], taps))

    return pl.pallas_call(
        body, name=f"conv_silu_fwd_{part}", grid=(width // FFN_COLS, b),
        in_specs=[pl.BlockSpec((s, FFN_COLS), lambda j, i: (i, x_off + j)),
                  pl.BlockSpec((DN_TAPS, FFN_COLS), lambda j, i: (0, w_off + j))],
        out_specs=pl.BlockSpec((s, FFN_COLS), lambda j, i: (i, j)),
        out_shape=jax.ShapeDtypeStruct((b * s, width), F32), compiler_params=_params("parallel", "parallel"),
    )(proj, conv_w)


def conv_silu_bwd(proj, conv_w, dact, part, col0, width, b, s):
    x_off, w_off = (col0 + part * width) // FFN_COLS, part * width // FFN_COLS

    def body(x_ref, w_ref, dy_ref, dx_ref, dw_ref):
        taps = [w_ref[k:k + 1, :] for k in range(DN_TAPS)]
        _, pull = jax.vjp(lambda x, *tp: _silu(_causal_conv(x, tp)), x_ref[...], *taps)
        dx, *dtaps = pull(dy_ref[...])
        dx_ref[...] = dx.astype(BF16)
        first = pl.program_id(1) == 0
        for k in range(DN_TAPS):
            _accumulate(dw_ref.at[k:k + 1, :], dtaps[k], first)

    out_blk = pl.BlockSpec((s, FFN_COLS), lambda j, i: (i, j))
    return pl.pallas_call(
        body, name=f"conv_silu_bwd_{part}", grid=(width // FFN_COLS, b),
        in_specs=[pl.BlockSpec((s, FFN_COLS), lambda j, i: (i, x_off + j)),
                  pl.BlockSpec((DN_TAPS, FFN_COLS), lambda j, i: (0, w_off + j)), out_blk],
        out_specs=[out_blk, pl.BlockSpec((DN_TAPS, FFN_COLS), lambda j, i: (0, j))],
        out_shape=[jax.ShapeDtypeStruct((b * s, width), BF16), jax.ShapeDtypeStruct((DN_TAPS, width), F32)],
        compiler_params=_params("arbitrary", "arbitrary"),
    )(proj, conv_w, dact)


def _ut_specs(b, s, heads, width):
    r = min(UT_ROWS, s)
    ns = s // r
    tok = pl.BlockSpec((r, width), lambda i, n: (i * ns + n, 0))
    col = pl.BlockSpec((None, heads, r, 1), lambda i, n: (i, 0, n, 0))
    row = pl.BlockSpec((None, heads, None, 1, r), lambda i, n: (i, 0, n, 0, 0))
    return r, ns, tok, col, row


def ut_fwd(act_k, act_v, beta_col, gc_col, gc_row, b, s, ride=None):
    width = act_k.shape[1]
    heads = width // DN_HEAD_DIM
    r, ns, tok, col, row = _ut_specs(b, s, heads, width)
    inv_spec = pl.BlockSpec((r, heads * r), lambda i, n: (i * ns + n, 0))

    n_ride_in = len(ride.arrays) if ride else 0

    def body(k_ref, v_ref, beta_ref, gcc_ref, gcr_ref, *rest):
        u_ref, w_ref, inv_ref = rest[n_ride_in:n_ride_in + 3]
        if ride:
            ride_in, ride_out, sems = rest[:n_ride_in], rest[n_ride_in + 3:-3], rest[-3:]
            first, last = _first_and_last_step((b, ns))
            pl.when(first)(lambda: ride.start(ride_in, ride_out, *sems))
        for h in range(heads):
            sl = slice(h * DN_HEAD_DIM, (h + 1) * DN_HEAD_DIM)
            u, w, inv = _ut_chain(k_ref[:, sl], v_ref[:, sl], beta_ref[h], gcc_ref[h], gcr_ref[h])
            u_ref[:, sl] = u
            w_ref[:, sl] = w
            inv_ref[:, h * r:(h + 1) * r] = inv
        if ride:
            pl.when(last)(lambda: ride.finish(ride_in, ride_out, *sems))

    out = jax.ShapeDtypeStruct((b * s, width), F32)
    outs = pl.pallas_call(
        body, name="ut_fwd" + ("_with_" + ride.tag if ride else ""), grid=(b, ns),
        in_specs=[tok, tok, col, col, row] + (ride.specs_in if ride else []),
        out_specs=[tok, tok, inv_spec] + (ride.specs_out if ride else []),
        out_shape=[out, out, jax.ShapeDtypeStruct((b * s, heads * r), F32)] + (ride.out_shapes if ride else []),
        scratch_shapes=ride.scratch if ride else [],
        compiler_params=_params("arbitrary", "arbitrary") if ride else _params("parallel", "parallel"),
    )(act_k, act_v, beta_col, gc_col, gc_row, *(ride.arrays if ride else []))
    return (*outs[:3], outs[3:]) if ride else outs


def ut_bwd(act_k, act_v, beta_col, gc_col, gc_row, inv, du, dw, dk_more, b, s):
    width = act_k.shape[1]
    heads = width // DN_HEAD_DIM
    r, ns, tok, col, row = _ut_specs(b, s, heads, width)
    inv_spec = pl.BlockSpec((r, heads * r), lambda i, n: (i * ns + n, 0))

    def body(k_ref, v_ref, beta_ref, gcc_ref, gcr_ref, inv_ref, du_ref, dw_ref, dkm_ref,
             dk_ref, dv_ref, dbeta_ref, dgcc_ref, dgcr_ref):
        for h in range(heads):
            sl = slice(h * DN_HEAD_DIM, (h + 1) * DN_HEAD_DIM)
            inv = inv_ref[:, h * r:(h + 1) * r]
            _, pull = jax.vjp(lambda *a: _ut_chain(*a, inv=inv)[:2], k_ref[:, sl], v_ref[:, sl], beta_ref[h],
                              gcc_ref[h], gcr_ref[h])
            dk, dv, dbeta, dgcc, dgcr = pull((du_ref[:, sl], dw_ref[:, sl]))
            dk_ref[:, sl] = dk + dkm_ref[:, sl]
            dv_ref[:, sl] = dv
            dbeta_ref[h] = dbeta
            dgcc_ref[h] = dgcc
            dgcr_ref[h] = dgcr

    out = jax.ShapeDtypeStruct((b * s, width), F32)
    return pl.pallas_call(
        body, name="ut_bwd", grid=(b, ns), in_specs=[tok, tok, col, col, row, inv_spec, tok, tok, tok],
        out_specs=[tok, tok, col, col, row],
        out_shape=[out, out, jax.ShapeDtypeStruct(beta_col.shape, F32), jax.ShapeDtypeStruct(gc_col.shape, F32),
                   jax.ShapeDtypeStruct(gc_row.shape, F32)],
        compiler_params=_params("parallel", "parallel"),
    )(act_k, act_v, beta_col, gc_col, gc_row, inv, du, dw, dk_more)


def _rec_specs(b, n, heads, width, chunk_of):
    ch = DN_CHUNK
    tok = pl.BlockSpec((b, None, ch, width), lambda i: (0, chunk_of(i), 0, 0))
    col = pl.BlockSpec((b, heads, None, ch, 1), lambda i: (0, 0, chunk_of(i), 0, 0))
    row = pl.BlockSpec((b, heads, None, 1, ch), lambda i: (0, 0, chunk_of(i), 0, 0))
    st = pl.BlockSpec((None, b, heads, DN_HEAD_DIM, DN_HEAD_DIM), lambda i: (chunk_of(i), 0, 0, 0, 0))
    return tok, col, row, st


def rec_fwd(act_q, act_k, gc_col, gc_row, u, w, b, s):
    width = act_q.shape[1]
    heads, n = width // DN_HEAD_DIM, s // DN_CHUNK
    tok, col, row, st = _rec_specs(b, n, heads, width, lambda i: i)
    shape4 = (b, n, DN_CHUNK, width)

    def body(q_ref, k_ref, gcc_ref, gcr_ref, u_ref, w_ref, o_ref, st_ref, state):
        @pl.when(pl.program_id(0) == 0)
        def _():
            state[...] = jnp.zeros_like(state)

        for i in range(b):
            for h in range(heads):
                sl = slice(h * DN_HEAD_DIM, (h + 1) * DN_HEAD_DIM)
                s_in = state[i, h]
                st_ref[i, h] = s_in
                o, s_out = _rec_chain(q_ref[i, :, sl], k_ref[i, :, sl], gcc_ref[i, h], gcr_ref[i, h],
                                      u_ref[i, :, sl], w_ref[i, :, sl], s_in)
                o_ref[i, :, sl] = o
                state[i, h] = s_out

    o, states = pl.pallas_call(
        body, name="rec_fwd", grid=(n,), in_specs=[tok, tok, col, row, tok, tok], out_specs=[tok, st],
        out_shape=[jax.ShapeDtypeStruct(shape4, F32),
                   jax.ShapeDtypeStruct((n, b, heads, DN_HEAD_DIM, DN_HEAD_DIM), F32)],
        scratch_shapes=[pltpu.VMEM((b, heads, DN_HEAD_DIM, DN_HEAD_DIM), F32)],
        compiler_params=_params("arbitrary"),
    )(act_q.reshape(shape4), act_k.reshape(shape4), gc_col.reshape(b, heads, n, DN_CHUNK, 1), gc_row,
      u.reshape(shape4), w.reshape(shape4))
    return o.reshape(b * s, width), states


def rec_bwd(act_q, act_k, gc_col, gc_row, u, w, states, do, b, s):
    width = act_q.shape[1]
    heads, n = width // DN_HEAD_DIM, s // DN_CHUNK
    tok, col, row, st = _rec_specs(b, n, heads, width, lambda i: n - 1 - i)
    shape4 = (b, n, DN_CHUNK, width)

    def body(q_ref, k_ref, gcc_ref, gcr_ref, u_ref, w_ref, st_ref, do_ref,
             dq_ref, dk_ref, du_ref, dw_ref, dgcc_ref, dgcr_ref, dstate):
        @pl.when(pl.program_id(0) == 0)
        def _():
            dstate[...] = jnp.zeros_like(dstate)

        for i in range(b):
            for h in range(heads):
                sl = slice(h * DN_HEAD_DIM, (h + 1) * DN_HEAD_DIM)
                _, pull = jax.vjp(_rec_chain, q_ref[i, :, sl], k_ref[i, :, sl], gcc_ref[i, h], gcr_ref[i, h],
                                  u_ref[i, :, sl], w_ref[i, :, sl], st_ref[i, h])
                dq, dk, dgcc, dgcr, du, dw, ds = pull((do_ref[i, :, sl], dstate[i, h]))
                dq_ref[i, :, sl] = dq
                dk_ref[i, :, sl] = dk
                du_ref[i, :, sl] = du
                dw_ref[i, :, sl] = dw
                dgcc_ref[i, h] = dgcc
                dgcr_ref[i, h] = dgcr
                dstate[i, h] = ds

    tok_out = jax.ShapeDtypeStruct(shape4, F32)
    dq, dk, du, dw, dgcc, dgcr = pl.pallas_call(
        body, name="rec_bwd", grid=(n,), in_specs=[tok, tok, col, row, tok, tok, st, tok],
        out_specs=[tok, tok, tok, tok, col, row],
        out_shape=[tok_out, tok_out, tok_out, tok_out,
                   jax.ShapeDtypeStruct((b, heads, n, DN_CHUNK, 1), F32), jax.ShapeDtypeStruct(gc_row.shape, F32)],
        scratch_shapes=[pltpu.VMEM((b, heads, DN_HEAD_DIM, DN_HEAD_DIM), F32)],
        compiler_params=_params("arbitrary"),
    )(act_q.reshape(shape4), act_k.reshape(shape4), gc_col.reshape(b, heads, n, DN_CHUNK, 1), gc_row,
      u.reshape(shape4), w.reshape(shape4), states, do.reshape(shape4))
    flat = lambda a: a.reshape(b * s, width)
    return flat(dq), flat(dk), flat(du), flat(dw), dgcc.reshape(b, heads, s, 1), dgcr


def dn_norm_fwd(o, proj, g, z_col0):
    t, width = o.shape
    heads = width // DN_HEAD_DIM
    tr = _tile(t, 512, 8)
    z_off = z_col0 // width

    def body(o_ref, z_ref, g_ref, y_ref):
        for h in range(heads):
            sl = slice(h * DN_HEAD_DIM, (h + 1) * DN_HEAD_DIM)
            y_ref[:, sl] = _gated_norm(o_ref[:, sl], z_ref[:, sl], g_ref[...]).astype(BF16)

    blk = pl.BlockSpec((tr, width), lambda i: (i, 0))
    return pl.pallas_call(
        body, name="dn_norm_fwd", grid=(t // tr,),
        in_specs=[blk, pl.BlockSpec((tr, width), lambda i: (i, z_off)),
                  pl.BlockSpec((1, DN_HEAD_DIM), lambda i: (0, 0))],
        out_specs=blk, out_shape=jax.ShapeDtypeStruct((t, width), BF16),
        compiler_params=_params("parallel"),
    )(o, proj, g.reshape(1, -1))


def dn_norm_bwd(o, proj, g, dmixed, z_col0, dy_col0):
    t, width = o.shape
    heads = width // DN_HEAD_DIM
    tr = _tile(t, 512, 8)
    z_off, dy_off = z_col0 // width, dy_col0 // width

    def body(o_ref, z_ref, g_ref, dy_ref, do_ref, dz_ref, dg_ref):
        dg_sum = None
        for h in range(heads):
            sl = slice(h * DN_HEAD_DIM, (h + 1) * DN_HEAD_DIM)
            _, pull = jax.vjp(_gated_norm, o_ref[:, sl], z_ref[:, sl], g_ref[...])
            do, dz, dg = pull(dy_ref[:, sl])
            do_ref[:, sl] = do
            dz_ref[:, sl] = dz.astype(BF16)
            dg_sum = dg if dg_sum is None else dg_sum + dg
        _accumulate(dg_ref, dg_sum, pl.program_id(0) == 0)

    blk = pl.BlockSpec((tr, width), lambda i: (i, 0))
    vec = pl.BlockSpec((1, DN_HEAD_DIM), lambda i: (0, 0))
    do, dz, dg = pl.pallas_call(
        body, name="dn_norm_bwd", grid=(t // tr,),
        in_specs=[blk, pl.BlockSpec((tr, width), lambda i: (i, z_off)), vec,
                  pl.BlockSpec((tr, width), lambda i: (i, dy_off))],
        out_specs=[blk, blk, vec],
        out_shape=[jax.ShapeDtypeStruct((t, width), F32), jax.ShapeDtypeStruct((t, width), BF16),
                   jax.ShapeDtypeStruct((1, DN_HEAD_DIM), F32)],
        compiler_params=_params("arbitrary"),
    )(o, proj, g.reshape(1, -1), dmixed)
    return do, dz, dg.reshape(-1)


def _attn_specs(b, s, mem_len, d):
    hd = d // XA_HEADS
    tq = _tile(s, 512, 8)
    nq = s // tq
    q_spec = pl.BlockSpec((tq, hd), lambda i, h, j: (i * nq + j, h))
    k_spec = pl.BlockSpec((mem_len, hd), lambda i, h, j: (i, h))
    v_spec = pl.BlockSpec((mem_len, hd), lambda i, h, j: (i, XA_HEADS + h))
    return nq, q_spec, k_spec, v_spec


def attn_fwd(q, kv, b, s):
    d = q.shape[1]
    nq, q_spec, k_spec, v_spec = _attn_specs(b, s, kv.shape[0] // b, d)

    def body(q_ref, k_ref, v_ref, o_ref):
        o_ref[...] = _attn_block(q_ref[...], k_ref[...], v_ref[...]).astype(BF16)

    return pl.pallas_call(
        body, name="attn_fwd", grid=(b, XA_HEADS, nq), in_specs=[q_spec, k_spec, v_spec], out_specs=q_spec,
        out_shape=jax.ShapeDtypeStruct(q.shape, BF16), compiler_params=_params("parallel", "parallel", "parallel"),
    )(q, kv, kv)


def attn_bwd(q, kv, do, b, s):
    d = q.shape[1]
    rows = kv.shape[0]
    nq, q_spec, k_spec, v_spec = _attn_specs(b, s, rows // b, d)

    def body(q_ref, k_ref, v_ref, do_ref, dq_ref, dk_ref, dv_ref):
        _, pull = jax.vjp(_attn_block, q_ref[...], k_ref[...], v_ref[...])
        dq, dk, dv = pull(do_ref[...])
        dq_ref[...] = dq.astype(BF16)
        first = pl.program_id(2) == 0
        _accumulate(dk_ref, dk, first)
        _accumulate(dv_ref, dv, first)

    kv_out = jax.ShapeDtypeStruct((rows, d), F32)
    return pl.pallas_call(
        body, name="attn_bwd", grid=(b, XA_HEADS, nq), in_specs=[q_spec, k_spec, v_spec, q_spec],
        out_specs=[q_spec, k_spec, k_spec], out_shape=[jax.ShapeDtypeStruct(q.shape, BF16), kv_out, kv_out],
        compiler_params=_params("parallel", "parallel", "arbitrary"),
    )(q, kv, kv, do)


def ffn_fwd(gate_pre, up, conv_w, conv_b, b, s):
    taps_n, f = conv_w.shape

    def body(g_ref, u_ref, w_ref, b_ref, y_ref):
        taps = [w_ref[k:k + 1, :] for k in range(taps_n)]
        y_ref[...] = _ffn_block(g_ref[...], u_ref[...], taps, b_ref[...]).astype(BF16)

    blk = pl.BlockSpec((s, FFN_COLS), lambda j, i: (i, j))
    return pl.pallas_call(
        body, name="ffn_fwd", grid=(f // FFN_COLS, b),
        in_specs=[blk, blk, pl.BlockSpec((taps_n, FFN_COLS), lambda j, i: (0, j)),
                  pl.BlockSpec((1, FFN_COLS), lambda j, i: (0, j))],
        out_specs=blk,
        out_shape=jax.ShapeDtypeStruct((b * s, f), BF16), compiler_params=_params("parallel", "parallel"),
    )(gate_pre, up, conv_w, conv_b.reshape(1, f))


def ffn_bwd(gate_pre, up, conv_w, conv_b, dact, b, s):
    taps_n, f = conv_w.shape

    def body(g_ref, u_ref, w_ref, b_ref, dy_ref, dg_ref, du_ref, dw_ref, db_ref):
        taps = [w_ref[k:k + 1, :] for k in range(taps_n)]
        _, pull = jax.vjp(lambda gt, up_, bias, *tp: _ffn_block(gt, up_, tp, bias), g_ref[...], u_ref[...],
                          b_ref[...], *taps)
        dgate, dup, dbias, *dtaps = pull(dy_ref[...])
        dg_ref[...] = dgate.astype(BF16)
        du_ref[...] = dup.astype(BF16)
        first = pl.program_id(1) == 0
        _accumulate(db_ref, dbias, first)
        for k in range(taps_n):
            _accumulate(dw_ref.at[k:k + 1, :], dtaps[k], first)

    blk = pl.BlockSpec((s, FFN_COLS), lambda j, i: (i, j))
    w_spec = pl.BlockSpec((taps_n, FFN_COLS), lambda j, i: (0, j))
    b_spec = pl.BlockSpec((1, FFN_COLS), lambda j, i: (0, j))
    half = jax.ShapeDtypeStruct(gate_pre.shape, BF16)
    dgate, dup, dw, db = pl.pallas_call(
        body, name="ffn_bwd", grid=(f // FFN_COLS, b),
        in_specs=[blk, blk, w_spec, b_spec, blk],
        out_specs=[blk, blk, w_spec, b_spec],
        out_shape=[half, half, jax.ShapeDtypeStruct((taps_n, f), F32), jax.ShapeDtypeStruct((1, f), F32)],
        compiler_params=_params("arbitrary", "arbitrary"),
    )(gate_pre, up, conv_w, conv_b.reshape(1, f), dact)
    return dgate, dup, dw, db.reshape(f)


def gate_arrays(logits, a_log, dt_bias, b, s, heads):
    n, r = s // DN_CHUNK, min(UT_ROWS, s)
    lg = logits.reshape(b, s, -1)
    beta = jax.nn.sigmoid(lg[..., :heads])
    g = -jnp.exp(a_log) * jax.nn.softplus(lg[..., heads:2 * heads] + dt_bias)
    gc = jnp.cumsum(g.reshape(b, n, DN_CHUNK, heads), axis=2).transpose(0, 3, 1, 2)
    return (beta.transpose(0, 2, 1)[..., None], gc.reshape(b, heads, s, 1), gc[:, :, :, None, :],
            gc.reshape(b, heads, s // r, 1, r))


class Riders:
    def __init__(self, make_ride, groups, arrays, tag):
        self.groups, self.results = groups, {}
        self.rides = {host: make_ride([arrays[n] for n in members], f"{tag}_on_{host}")
                      for host, members in groups.items()}

    def run(self, host, call):
        value, outs = call(self.rides[host])
        self.results.update(zip(self.groups[host], outs))
        return value


def _hosted_matmul(riders, host, *args, **kwargs):
    if riders is None:
        return matmul(*args, **kwargs)
    return riders.run(host, lambda ride: matmul(*args, ride=ride, **kwargs))


FORWARD_HOSTS = {'in_proj': ('w_in',), 'ut': ('w_gate', 'w_up'), 'gate': ('w_xkv',),
                 'up': ('w_mix_out', 'w_xq', 'w_xo', 'w_pool', 'dn_conv_w', 'ffn_conv_w'), 'down': ('w_down',)}
BACKWARD_HOSTS = {'down_wgrad': ('w_down',), 'gate_dgrad': ('w_xkv',), 'up_dgrad': ('w_in',),
                  'gate_wgrad': ('w_gate',), 'up_wgrad': ('w_up',),
                  'in_dgrad': ('w_mix_out', 'w_xq', 'w_xo', 'w_pool', 'dn_conv_w', 'ffn_conv_w')}


def forward_layer(x, mem_hb, p, b, s, riders=None):
    d = x.shape[1]
    pw = d // 2
    dn = d - pw
    heads = dn // DN_HEAD_DIM
    sv = {'x0': x}
    sv['h1'] = h1 = norm_fwd(x, p['mix_norm_g'])
    sv['proj'] = proj = _hosted_matmul(riders, 'in_proj', h1, p['w_in_main'], 'nn', F32)
    logits = matmul(h1, p['w_in_logits'], 'nn', F32)
    y_pool = pool_fwd(proj, p['w_pool'], p['pool_scale'], b, s)
    gates, sv['gates_pull'] = jax.vjp(lambda lg, al, dtb: gate_arrays(lg, al, dtb, b, s, heads), logits,
                                      p['dn_a_log'], p['dn_dt_bias'])
    sv['gates'] = beta_col, gc_col, gc_row, gc_row_ut = gates
    sv['act'] = aq, ak, av = [conv_silu_fwd(proj, p['dn_conv_w'], part, pw, dn, b, s) for part in range(3)]
    if riders is None:
        u, w, sv['inv'] = ut_fwd(ak, av, beta_col, gc_col, gc_row_ut, b, s)
    else:
        def hosted(ride):
            *own, outs = ut_fwd(ak, av, beta_col, gc_col, gc_row_ut, b, s, ride=ride)
            return own, outs
        u, w, sv['inv'] = riders.run('ut', hosted)
    sv['u'], sv['w'] = u, w
    sv['o_dn'], sv['states'] = o_dn, _ = rec_fwd(aq, ak, gc_col, gc_row, u, w, b, s)
    y_dn = dn_norm_fwd(o_dn, proj, p['dn_norm_g'], pw + 3 * dn)
    sv['mixed'] = mixed = jnp.concatenate([y_pool, y_dn], axis=1)
    sv['x1'] = x1 = matmul(mixed, p['w_mix_out'], 'nn', F32, res=x)

    sv['h2'] = h2 = norm_fwd(x1, p['xa_norm_g'])
    sv['q'] = q = matmul(h2, p['w_xq'], 'nn', F32)
    sv['kv'] = kv = matmul(mem_hb, p['w_xkv'], 'nn', F32)
    sv['o_at'] = o_at = attn_fwd(q, kv, b, s)
    sv['x2'] = x2 = matmul(o_at, p['w_xo'], 'nn', F32, res=x1)

    sv['h3'] = h3 = norm_fwd(x2, p['ffn_norm_g'])
    sv['gate_pre'] = gate_pre = _hosted_matmul(riders, 'gate', h3, p['w_gate'], 'nn', F32)
    sv['up'] = up = _hosted_matmul(riders, 'up', h3, p['w_up'], 'nn', F32)
    sv['a_ffn'] = a_ffn = ffn_fwd(gate_pre, up, p['ffn_conv_w'], p['ffn_conv_b'], b, s)
    return _hosted_matmul(riders, 'down', a_ffn, p['w_down'], 'nn', F32, res=x2), sv


def backward_layer(dx, dxb, sv, mem_hb, p, b, s, above=None):
    d = dx.shape[1]
    pw = d // 2
    dn = d - pw
    g = {}
    if above is None:
        riders = None
        da = matmul(dxb, p['w_down'], 'nt', F32)
    else:
        names, parts, core, tag = above
        da, from_sibling = matmul(dxb, p['w_down'], 'nt', F32, ride=sibling_swap_ride(parts, tag))
        chip_sums = {n: chip_sum(a, f, core) for n, a, f in zip(names, parts, from_sibling)}
        riders = Riders(chip_exchange_ride, BACKWARD_HOSTS, chip_sums, tag)
    g['w_down'] = _hosted_matmul(riders, 'down_wgrad', sv['a_ffn'], dxb, 'tn', BF16)
    dgate, dup, g['ffn_conv_w'], g['ffn_conv_b'] = ffn_bwd(sv['gate_pre'], sv['up'], p['ffn_conv_w'],
                                                           p['ffn_conv_b'], da, b, s)
    dh = _hosted_matmul(riders, 'gate_dgrad', dgate, p['w_gate'], 'nt', F32)
    dh = _hosted_matmul(riders, 'up_dgrad', dup, p['w_up'], 'nt', F32, res=dh)
    g['w_gate'] = _hosted_matmul(riders, 'gate_wgrad', sv['h3'], dgate, 'tn', BF16)
    g['w_up'] = _hosted_matmul(riders, 'up_wgrad', sv['h3'], dup, 'tn', BF16)
    dx, dxb, g['ffn_norm_g'] = norm_bwd(sv['x2'], p['ffn_norm_g'], dh, dx)

    do = matmul(dxb, p['w_xo'], 'nt', F32)
    g['w_xo'] = matmul(sv['o_at'], dxb, 'tn', BF16)
    dq, dk, dv = attn_bwd(sv['q'], sv['kv'], do, b, s)
    dkv = jnp.concatenate([dk, dv], axis=1).astype(BF16)
    dh = matmul(dq, p['w_xq'], 'nt', F32)
    g['w_xq'] = matmul(sv['h2'], dq, 'tn', BF16)
    g['w_xkv'] = matmul(mem_hb, dkv, 'tn', BF16)
    dmem_h = matmul(dkv, p['w_xkv'], 'nt', F32)
    dx, dxb, g['xa_norm_g'] = norm_bwd(sv['x1'], p['xa_norm_g'], dh, dx)

    dmixed = matmul(dxb, p['w_mix_out'], 'nt', F32)
    g['w_mix_out'] = matmul(sv['mixed'], dxb, 'tn', BF16)
    proj = sv['proj']
    du_pool, g['w_pool'], g['pool_scale'] = pool_bwd(proj, p['w_pool'], p['pool_scale'], dmixed, b, s)
    do_dn, dz, g['dn_norm_g'] = dn_norm_bwd(sv['o_dn'], proj, p['dn_norm_g'], dmixed, pw + 3 * dn, pw)
    beta_col, gc_col, gc_row, gc_row_ut = sv['gates']
    aq, ak, av = sv['act']
    daq, dak_rec, du, dw, dgcc_rec, dgcr = rec_bwd(aq, ak, gc_col, gc_row, sv['u'], sv['w'], sv['states'],
                                                   do_dn, b, s)
    dak, dav, dbeta, dgcc_ut, dgcr_ut = ut_bwd(ak, av, beta_col, gc_col, gc_row_ut, sv['inv'], du, dw, dak_rec, b, s)
    dparts, dtaps = zip(*[conv_silu_bwd(proj, p['dn_conv_w'], dact, part, pw, dn, b, s)
                          for part, dact in enumerate((daq, dak, dav))])
    g['dn_conv_w'] = jnp.concatenate(dtaps, axis=1)
    dlogits, g['dn_a_log'], g['dn_dt_bias'] = sv['gates_pull']((dbeta, dgcc_rec + dgcc_ut, dgcr, dgcr_ut))
    dproj = jnp.concatenate([du_pool, *dparts, dz], axis=1)
    dlogits = dlogits.astype(BF16)
    dh = _hosted_matmul(riders, 'in_dgrad', dproj, p['w_in_main'], 'nt', F32,
                        res=matmul(dlogits, p['w_in_logits'], 'nt', F32))
    g['w_in_main'] = matmul(sv['h1'], dproj, 'tn', BF16)
    g['w_in_logits'] = matmul(sv['h1'], dlogits, 'tn', BF16)
    dx, dxb, g['mix_norm_g'] = norm_bwd(sv['x0'], p['mix_norm_g'], dh, dx)
    return dx, dxb, dmem_h, g, (riders.results if riders else None)


def _to_full(gathered, axis):
    moved = jnp.moveaxis(gathered, 0, axis)
    shape = list(gathered.shape[1:])
    shape[axis] *= N_DEV
    return moved.reshape(shape)


def _to_parts(full, axis):
    shape = list(full.shape)
    shape[axis:axis + 1] = [N_DEV, shape[axis] // N_DEV]
    return jnp.moveaxis(full.reshape(shape), axis, 0)


def _as_pack_rows(a, cols):
    rows = -(-a.shape[0] // (8 * cols)) * 8
    return jnp.pad(a, (0, rows * cols - a.shape[0])).reshape(rows, cols)


def kernel(x, mem, mix_norm_g, w_in, w_pool, pool_scale, dn_conv_w, dn_a_log, dn_dt_bias, dn_norm_g, w_mix_out, xa_norm_g, mem_norm_g, w_xq, w_xkv, w_xo, ffn_norm_g, w_gate, w_up, ffn_conv_w, ffn_conv_b, w_down, final_norm_g, loss_target, m_mix_norm_g, m_w_in, m_w_pool, m_pool_scale, m_dn_conv_w, m_dn_a_log, m_dn_dt_bias, m_dn_norm_g, m_w_mix_out, m_xa_norm_g, m_mem_norm_g, m_w_xq, m_w_xkv, m_w_xo, m_ffn_norm_g, m_w_gate, m_w_up, m_ffn_conv_w, m_ffn_conv_b, m_w_down, m_final_norm_g, v_mix_norm_g, v_w_in, v_w_pool, v_pool_scale, v_dn_conv_w, v_dn_a_log, v_dn_dt_bias, v_dn_norm_g, v_w_mix_out, v_xa_norm_g, v_mem_norm_g, v_w_xq, v_w_xkv, v_w_xo, v_ffn_norm_g, v_w_gate, v_w_up, v_ffn_conv_w, v_ffn_conv_b, v_w_down, v_final_norm_g):
    given = dict(locals())
    w = {n: given[n] for n in WEIGHTS}
    mom = {n: given['m_' + n] for n in WEIGHTS}
    var = {n: given['v_' + n] for n in WEIGHTS}
    b, s, d = x.shape
    depth = w_in.shape[0]
    main = 5 * (d // 2)
    n_logits = w_in.shape[-1] * N_DEV - main
    core = jnp.reshape(lax.axis_index("c"), (1,)).astype(jnp.int32)

    names = MATRICES + TAPS

    def shards(l):
        return {n: w[n][l] if n in TAPS else w[n][l].astype(BF16) for n in names}

    def layer_weights(l, gathered):
        full = {n: _to_full(gathered[n], SHARD_AXIS[n] - 1) for n in names}
        p = {n: full[n] for n in names if n != 'w_in'}
        p['w_in_main'] = full['w_in'][:, :main]
        p['w_in_logits'] = jnp.pad(full['w_in'][:, main:], ((0, 0), (0, LANES - n_logits)))
        for n in PER_LAYER_REPLICATED:
            p[n] = w[n][l]
        return p

    mem2 = mem.reshape(-1, d)
    mem_hb = norm_fwd(mem2, mem_norm_g)
    xc, saved, layers = x.reshape(b * s, d), [], []
    gathered = dict(zip(names, all_gather([shards(0)[n] for n in names], "layer0")))
    for l in range(depth):
        layers.append(layer_weights(l, gathered))
        riders = Riders(gather_ride, FORWARD_HOSTS, shards(l + 1), f"layer{l + 1}") if l + 1 < depth else None
        xc, sv = forward_layer(xc, mem_hb, layers[l], b, s, riders)
        saved.append(sv)
        gathered = riders.results if riders else None
    loss, dx, dxb, g_final = loss_head(xc, loss_target.reshape(b * s, d), final_norm_g)

    def as_layers(a, cols):
        return a.reshape(depth, -1, cols)

    results = {n: None for n in names}

    def update(l, got):
        for n in names:
            cols = w[n].shape[-1]
            results[n] = adamw(got[n], as_layers(w[n], cols), as_layers(mom[n], cols), as_layers(var[n], cols),
                               layer=l, prev=results[n])

    g_layers, dmem_h, above = [None] * depth, None, None
    for l in reversed(range(depth)):
        dx, dxb, dm, g, reduced = backward_layer(dx, dxb, saved[l], mem_hb, layers[l], b, s, above)
        if reduced:
            update(l + 1, reduced)
        g_layers[l] = g
        dmem_h = dm if dmem_h is None else dmem_h + dm
        g_full = {n: g[n] for n in names if n not in ('w_in', 'w_pool')}
        g_full['w_pool'] = g['w_pool'].astype(BF16)
        g_full['w_in'] = jnp.concatenate([g['w_in_main'], g['w_in_logits'][:, :n_logits]], axis=1)
        parts = []
        for n in names:
            by_dev = _to_parts(g_full[n], SHARD_AXIS[n] - 1)
            parts.append(by_dev.reshape(N_DEV, -1, by_dev.shape[-1]))
        above = (names, parts, core, f"layer{l}")
    from_sibling = sibling_swap_ride(parts, "layer0").alone()
    chip_sums = [chip_sum(a, f, core) for a, f in zip(parts, from_sibling)]
    update(0, dict(zip(names, chip_exchange_ride(chip_sums, "layer0").alone())))
    _, _, g_mem = norm_bwd(mem2, mem_norm_g, dmem_h, None)
    grad_x = dx.reshape(b, s, d)

    grad, delta, new_m, new_v = {}, {}, {}, {}
    for n in names:
        grad[n], delta[n], new_m[n], new_v[n] = [o.reshape(w[n].shape) for o in results[n]]

    g_small = {n: jnp.stack([g[n] for g in g_layers]) for n in PER_LAYER_REPLICATED}
    g_small['mem_norm_g'], g_small['final_norm_g'] = g_mem, g_final
    sizes = [w[n].size for n in REPLICATED]

    def pack(parts, first):
        flat = jnp.concatenate([jnp.reshape(first, (1,))] + [parts[n].reshape(-1) for n in REPLICATED])
        return _as_pack_rows(flat, LANES)

    zero = jnp.zeros((), F32)
    everyone, = all_gather([pack(g_small, loss)], "replicated")
    outs = adamw(everyone, pack(w, zero)[None], pack(mom, zero)[None], pack(var, zero)[None])
    flat_outs = [o.reshape(-1) for o in outs]
    loss_total = flat_outs[0][0]
    offset = 1
    for n, size in zip(REPLICATED, sizes):
        grad[n], delta[n], new_m[n], new_v[n] = [o[offset:offset + size].reshape(w[n].shape) for o in flat_outs]
        offset += size

    return (loss_total, grad_x, *[grad[n] for n in WEIGHTS], *[delta[n] for n in WEIGHTS],
            *[new_m[n] for n in WEIGHTS], *[new_v[n] for n in WEIGHTS])
```

```python
import functools

import jax
import jax.numpy as jnp
from jax import lax
from jax.experimental import pallas as pl
from jax.experimental.pallas import tpu as pltpu

F32 = jnp.float32
BF16 = jnp.bfloat16
MESH = pl.DeviceIdType.MESH

N_DEV = 8
EPS = 1e-6
POOL_WINDOWS = (2, 4, 8, 16)
DN_HEAD_DIM = 128
DN_CHUNK = 64
DN_TAPS = 4
XA_HEADS = 4
ADAM_LR = 0.001
ADAM_B1 = 0.9
ADAM_B2 = 0.999
ADAM_EPS = 1e-08
ADAM_WD = 0.01
ADAM_STEP = 10

LANES = 128
VMEM_LIMIT = 48 * 1024 * 1024
UT_ROWS = 256
FFN_COLS = 256

WEIGHTS = ['mix_norm_g', 'w_in', 'w_pool', 'pool_scale', 'dn_conv_w', 'dn_a_log', 'dn_dt_bias', 'dn_norm_g',
           'w_mix_out', 'xa_norm_g', 'mem_norm_g', 'w_xq', 'w_xkv', 'w_xo', 'ffn_norm_g', 'w_gate', 'w_up',
           'ffn_conv_w', 'ffn_conv_b', 'w_down', 'final_norm_g']
SHARD_AXIS = {'w_in': 2, 'w_pool': 2, 'dn_conv_w': 2, 'w_mix_out': 1, 'w_xq': 1, 'w_xkv': 2, 'w_xo': 1,
              'w_gate': 2, 'w_up': 2, 'ffn_conv_w': 2, 'w_down': 1}
MATRICES = ('w_in', 'w_pool', 'w_mix_out', 'w_xq', 'w_xkv', 'w_xo', 'w_gate', 'w_up', 'w_down')
TAPS = ('dn_conv_w', 'ffn_conv_w')
REPLICATED = [n for n in WEIGHTS if n not in SHARD_AXIS]
PER_LAYER_REPLICATED = ('mix_norm_g', 'pool_scale', 'dn_a_log', 'dn_dt_bias', 'dn_norm_g', 'xa_norm_g',
                        'ffn_norm_g', 'ffn_conv_b')


def _params(*semantics):
    return pltpu.CompilerParams(dimension_semantics=semantics, vmem_limit_bytes=VMEM_LIMIT)


def _tile(dim, pref, unit=LANES):
    if dim <= pref:
        return dim
    t = (pref // unit) * unit
    while t >= unit:
        if dim % t == 0:
            return t
        t -= unit
    return dim


class Ride:
    def __init__(self, tag, arrays, out_shapes, n_sems, n_local, start, finish):
        self.tag, self.arrays, self.out_shapes = tag, list(arrays), list(out_shapes)
        self.start, self.finish = start, finish
        self.scratch = [pltpu.SemaphoreType.DMA((n_sems,)), pltpu.SemaphoreType.DMA((n_sems,)),
                        pltpu.SemaphoreType.DMA((n_local,))]

    @property
    def specs_in(self):
        return [pl.BlockSpec(memory_space=pl.ANY)] * len(self.arrays)

    @property
    def specs_out(self):
        return [pl.BlockSpec(memory_space=pl.ANY)] * len(self.out_shapes)

    def alone(self):
        n = len(self.arrays)

        def body(*refs):
            ins, outs, sems = refs[:n], refs[n:n + len(self.out_shapes)], refs[n + len(self.out_shapes):]
            self.start(ins, outs, *sems)
            self.finish(ins, outs, *sems)

        return pl.pallas_call(body, name=self.tag, out_shape=self.out_shapes, in_specs=self.specs_in,
                              out_specs=self.specs_out, scratch_shapes=self.scratch)(*self.arrays)


def _first_and_last_step(grid):
    ids = [pl.program_id(axis) for axis in range(len(grid))]
    first = functools.reduce(jnp.logical_and, [i == 0 for i in ids])
    last = functools.reduce(jnp.logical_and, [i == n - 1 for i, n in zip(ids, grid)])
    return first, last


def matmul(a, b, mode, out_dtype, res=None, ride=None):
    assert a.dtype == BF16 and b.dtype == BF16, (a.dtype, b.dtype)
    if mode == 'nn':
        (m, c), (c2, n) = a.shape, b.shape
    elif mode == 'nt':
        (m, c), (n, c2) = a.shape, b.shape
    else:
        (c, m), (c2, n) = a.shape, b.shape
    assert c == c2, (mode, a.shape, b.shape)
    tm, tn = (_tile(m, 2048), _tile(n, 512)) if mode == 'nt' else (_tile(m, 1024), _tile(n, 1408))
    tc = _tile(c, 2048)
    nc = c // tc
    if mode == 'nn':
        a_spec = pl.BlockSpec((tm, tc), lambda i, j, k: (i, k))
        b_spec = pl.BlockSpec((tc, tn), lambda i, j, k: (k, j))
        dims = (((1,), (0,)), ((), ()))
    elif mode == 'nt':
        a_spec = pl.BlockSpec((tm, tc), lambda i, j, k: (i, k))
        b_spec = pl.BlockSpec((tn, tc), lambda i, j, k: (j, k))
        dims = (((1,), (1,)), ((), ()))
    else:
        a_spec = pl.BlockSpec((tc, tm), lambda i, j, k: (k, i))
        b_spec = pl.BlockSpec((tc, tn), lambda i, j, k: (k, j))
        dims = (((0,), (0,)), ((), ()))
    out_spec = pl.BlockSpec((tm, tn), lambda i, j, k: (i, j))
    has_res = res is not None
    grid = (m // tm, n // tn, nc)
    n_ride_in = len(ride.arrays) if ride else 0
    n_ride_out = len(ride.out_shapes) if ride else 0

    def body(a_ref, b_ref, *rest):
        rest = list(rest)
        r_ref = rest.pop(0) if has_res else None
        ride_in = [rest.pop(0) for _ in range(n_ride_in)]
        o_ref = rest.pop(0)
        ride_out = [rest.pop(0) for _ in range(n_ride_out)]
        acc_ref = rest.pop(0) if nc > 1 else None
        if ride:
            first, last = _first_and_last_step(grid)
            pl.when(first)(lambda: ride.start(ride_in, ride_out, *rest))
        prod = lax.dot_general(a_ref[...], b_ref[...], dims, preferred_element_type=F32)

        def finish(total):
            if has_res:
                total = total + r_ref[...]
            o_ref[...] = total.astype(o_ref.dtype)

        if nc == 1:
            finish(prod)
        else:
            k = pl.program_id(2)

            @pl.when(k == 0)
            def _():
                acc_ref[...] = prod

            @pl.when(k > 0)
            def _():
                acc_ref[...] += prod

            @pl.when(k == nc - 1)
            def _():
                finish(acc_ref[...])

        if ride:
            pl.when(last)(lambda: ride.finish(ride_in, ride_out, *rest))

    outs = pl.pallas_call(
        body,
        name=f"mm_{mode}_{m}x{c}x{n}" + ("_res" if has_res else "") + ("_with_" + ride.tag if ride else ""),
        grid=grid,
        in_specs=[a_spec, b_spec] + ([out_spec] if has_res else []) + (ride.specs_in if ride else []),
        out_specs=[out_spec] + (ride.specs_out if ride else []),
        out_shape=[jax.ShapeDtypeStruct((m, n), out_dtype)] + (ride.out_shapes if ride else []),
        scratch_shapes=([] if nc == 1 else [pltpu.VMEM((tm, tn), F32)]) + (ride.scratch if ride else []),
        compiler_params=(_params("arbitrary", "arbitrary", "arbitrary") if ride
                         else _params("parallel", "parallel", "arbitrary")),
    )(a, b, *([res] if has_res else []), *(ride.arrays if ride else []))
    return (outs[0], outs[1:]) if ride else outs[0]


def _position():
    return lax.axis_index("x"), lax.axis_index("y"), lax.axis_index("c")


def _flip(v, bit):
    return 1 - v if bit else v


def _dev_index(px, py, pc):
    return 4 * px + 2 * py + pc


def gather_ride(shards, tag):
    n = len(shards)

    def plan(x_refs, out_refs, send_sems, recv_sems, local_sems):
        x, y, c = _position()
        me = (x, y, c)

        def copy(t, k, block, to, own=False):
            slot = out_refs[t].at[_dev_index(*block)]
            return pltpu.make_async_remote_copy(
                src_ref=x_refs[t] if own else slot, dst_ref=slot, send_sem=send_sems.at[7 * t + k],
                recv_sem=recv_sems.at[7 * t + k], device_id=to, device_id_type=MESH)

        def local(t):
            return pltpu.make_async_copy(x_refs[t], out_refs[t].at[_dev_index(*me)], local_sems.at[t])

        return me, (x, y, 1 - c), [(1 - x, y), (x, 1 - y), (1 - x, 1 - y)], copy, local

    def start(x_refs, out_refs, *sems):
        me, sibling, chips, copy, local = plan(x_refs, out_refs, *sems)
        for t in range(n):
            local(t).start()
            copy(t, 0, me, sibling, own=True).start()
            for j, chip in enumerate(chips):
                copy(t, 1 + j, me, (*chip, me[2]), own=True).start()

    def finish(x_refs, out_refs, *sems):
        me, sibling, chips, copy, local = plan(x_refs, out_refs, *sems)
        c = me[2]
        for j, chip in enumerate(chips):
            for t in range(n):
                copy(t, 1 + j, (*chip, c), me).wait_recv()
                copy(t, 4 + j, (*chip, c), sibling).start()
        for t in range(n):
            copy(t, 0, sibling, me).wait_recv()
            for j, chip in enumerate(chips):
                copy(t, 4 + j, (*chip, 1 - c), me).wait_recv()
        for t in range(n):
            copy(t, 0, me, sibling, own=True).wait_send()
            for j, chip in enumerate(chips):
                copy(t, 1 + j, me, (*chip, c), own=True).wait_send()
                copy(t, 4 + j, (*chip, c), sibling).wait_send()
            local(t).wait()

    return Ride("all_gather_" + tag, shards, [jax.ShapeDtypeStruct((N_DEV,) + a.shape, a.dtype) for a in shards],
                7 * n, n, start, finish)


def all_gather(shards, tag):
    return gather_ride(shards, tag).alone()


def sibling_swap_ride(parts, tag):
    n = len(parts)

    def plan(p_refs, out_refs, send_sems, recv_sems, _):
        x, y, c = _position()
        return [pltpu.make_async_remote_copy(
            src_ref=p_refs[t].at[2 * k + (1 - c)], dst_ref=out_refs[t].at[k], send_sem=send_sems.at[4 * t + k],
            recv_sem=recv_sems.at[4 * t + k], device_id=(x, y, 1 - c), device_id_type=MESH)
            for t in range(n) for k in range(N_DEV // 2)]

    def start(p_refs, out_refs, *sems):
        for cp in plan(p_refs, out_refs, *sems):
            cp.start()

    def finish(p_refs, out_refs, *sems):
        for cp in plan(p_refs, out_refs, *sems):
            cp.wait()

    return Ride("sibling_swap_" + tag, parts,
                [jax.ShapeDtypeStruct((N_DEV // 2,) + a.shape[1:], a.dtype) for a in parts], 4 * n, 1, start, finish)


def chip_sum(parts, from_sibling, core):
    _, r, c = parts.shape
    unit = 16 if parts.dtype == BF16 else 8
    tr = _tile(r, max(unit, (512 * 1024 // c) // unit * unit), unit)

    def body(core_ref, a_ref, b_ref, o_ref):
        o_ref[...] = (a_ref[...].astype(F32) + b_ref[...].astype(F32)).astype(o_ref.dtype)

    blk = pl.BlockSpec((None, tr, c), lambda k, i, core_ref: (k, i, 0))
    return pl.pallas_call(
        body, name=f"chip_sum_{r}x{c}_{parts.dtype.name}",
        grid_spec=pltpu.PrefetchScalarGridSpec(
            num_scalar_prefetch=1, grid=(N_DEV // 2, r // tr),
            in_specs=[pl.BlockSpec((None, tr, c), lambda k, i, core_ref: (2 * k + core_ref[0], i, 0)), blk],
            out_specs=blk),
        out_shape=jax.ShapeDtypeStruct(from_sibling.shape, from_sibling.dtype),
        compiler_params=_params("parallel", "parallel"),
    )(core, parts, from_sibling)


def chip_exchange_ride(parts, tag):
    n = len(parts)

    def plan(p_refs, out_refs, send_sems, recv_sems, local_sems):
        x, y, c = _position()
        my_chip = 2 * x + y
        local = [pltpu.make_async_copy(p_refs[t].at[my_chip], out_refs[t].at[my_chip], local_sems.at[t])
                 for t in range(n)]
        copies = []
        for t in range(n):
            for k in (1, 2, 3):
                px, py = _flip(x, k & 2), _flip(y, k & 1)
                copies.append(pltpu.make_async_remote_copy(
                    src_ref=p_refs[t].at[2 * px + py], dst_ref=out_refs[t].at[my_chip],
                    send_sem=send_sems.at[3 * t + k - 1], recv_sem=recv_sems.at[3 * t + k - 1],
                    device_id=(px, py, c), device_id_type=MESH))
        return local, copies

    def start(p_refs, out_refs, *sems):
        local, copies = plan(p_refs, out_refs, *sems)
        for cp in local + copies:
            cp.start()

    def finish(p_refs, out_refs, *sems):
        local, copies = plan(p_refs, out_refs, *sems)
        for cp in copies:
            cp.wait_recv()
        for cp in copies:
            cp.wait_send()
        for cp in local:
            cp.wait()

    return Ride("chip_exchange_" + tag, parts, [jax.ShapeDtypeStruct(a.shape, a.dtype) for a in parts],
                3 * n, n, start, finish)


def adamw(parts, w, m, v, layer=0, prev=None):
    n_parts, r, c = parts.shape
    depth = w.shape[0]
    unit = 16 if parts.dtype == BF16 else 8
    tr = _tile(r, max(unit, (256 * 1024 // c) // unit * unit), unit)

    def body(p_ref, w_ref, m_ref, v_ref, *rest):
        g_ref, d_ref, nm_ref, nv_ref = rest[-4:]
        g = p_ref[0].astype(F32)
        for j in range(1, n_parts):
            g = g + p_ref[j].astype(F32)
        nm = ADAM_B1 * m_ref[...] + (1.0 - ADAM_B1) * g
        nv = ADAM_B2 * v_ref[...] + (1.0 - ADAM_B2) * jnp.square(g)
        m_hat = nm / (1.0 - ADAM_B1 ** ADAM_STEP)
        v_hat = nv / (1.0 - ADAM_B2 ** ADAM_STEP)
        g_ref[...] = g
        d_ref[...] = -ADAM_LR * (m_hat / (jnp.sqrt(v_hat) + ADAM_EPS) + ADAM_WD * w_ref[...])
        nm_ref[...] = nm
        nv_ref[...] = nv

    spec = pl.BlockSpec((None, tr, c), lambda i: (layer, i, 0))
    out = jax.ShapeDtypeStruct((depth, r, c), F32)
    carried = [] if prev is None else list(prev)
    return pl.pallas_call(
        body,
        name=f"adamw_{n_parts}x{r}x{c}_{parts.dtype.name}_layer{layer}",
        grid=(r // tr,),
        in_specs=[pl.BlockSpec((n_parts, tr, c), lambda i: (0, i, 0)), spec, spec, spec]
        + [pl.BlockSpec(memory_space=pl.ANY)] * len(carried),
        out_specs=[spec, spec, spec, spec],
        out_shape=[out, out, out, out],
        input_output_aliases={4 + k: k for k in range(len(carried))},
        compiler_params=_params("parallel"),
    )(parts, w, m, v, *carried)


_NN = (((1,), (0,)), ((), ()))
_NT = (((1,), (1,)), ((), ()))
_TN = (((0,), (0,)), ((), ()))


def _dot_bf16(a, b, dims):
    return lax.dot_general(a.astype(BF16), b.astype(BF16), dims, preferred_element_type=F32)


def _dot_split(a, b, dims):
    a_hi, b_hi = a.astype(BF16), b.astype(BF16)
    a_lo = (a - a_hi.astype(F32)).astype(BF16)
    b_lo = (b - b_hi.astype(F32)).astype(BF16)
    dot = functools.partial(lax.dot_general, dimension_numbers=dims, preferred_element_type=F32)
    return dot(a_hi, b_hi) + (dot(a_hi, b_lo) + dot(a_lo, b_hi))


def _matmul_family(dot):
    @jax.custom_vjp
    def nn(a, b):
        return dot(a, b, _NN)

    @jax.custom_vjp
    def nt(a, b):
        return dot(a, b, _NT)

    @jax.custom_vjp
    def tn(a, b):
        return dot(a, b, _TN)

    nn.defvjp(lambda a, b: (nn(a, b), (a, b)), lambda r, g: (nt(g, r[1]), tn(r[0], g)))
    nt.defvjp(lambda a, b: (nt(a, b), (a, b)), lambda r, g: (nn(g, r[1]), tn(g, r[0])))
    tn.defvjp(lambda a, b: (tn(a, b), (a, b)), lambda r, g: (nt(r[1], g), nn(r[0], g)))
    return nn, nt, tn


mm, mm_nt, mm_tn = _matmul_family(_dot_bf16)
mms, mms_nt, mms_tn = _matmul_family(_dot_split)


def _shift_rows(x, k, down):
    n = x.shape[0]
    rows = lax.broadcasted_iota(jnp.int32, x.shape, 0)
    if down:
        return jnp.where(rows >= k, pltpu.roll(x, k, 0), 0.0)
    return jnp.where(rows < n - k, pltpu.roll(x, n - k, 0), 0.0)


@functools.partial(jax.custom_vjp, nondiff_argnums=(1,))
def delay(x, k):
    return _shift_rows(x, k, True) if k else x


delay.defvjp(lambda x, k: (delay(x, k), None), lambda k, _, g: ((_shift_rows(g, k, False) if k else g),))


def _silu(x):
    return x * jax.nn.sigmoid(x)


def _rms(x, g):
    return x * lax.rsqrt(jnp.mean(x * x, axis=-1, keepdims=True) + EPS) * g


def _l2n(t):
    return t * lax.rsqrt(jnp.sum(t * t, axis=-1, keepdims=True) + EPS)


def _causal_conv(x, taps):
    k_taps = len(taps)
    y = delay(x, k_taps - 1) * taps[0]
    for k in range(1, k_taps):
        y = y + delay(x, k_taps - 1 - k) * taps[k]
    return y


def _pool_block(u, w, scale, group):
    sums, acc, width = [], u, 1
    while width < POOL_WINDOWS[-1]:
        acc = acc + delay(acc, width)
        width *= 2
        sums.append(acc)
    picked = sums[-1]
    for i in range(len(POOL_WINDOWS) - 2, -1, -1):
        picked = jnp.where(group == i, sums[i], picked)
    rows = lax.broadcasted_iota(jnp.int32, u.shape, 0)
    count = jnp.minimum(rows + 1, jnp.left_shift(2, group)).astype(F32)
    return mm(picked / count - u, w) * scale


def _unit_lower_inverse(l_mat):
    n = l_mat.shape[0]
    eye = (lax.broadcasted_iota(jnp.int32, (n, n), 0) == lax.broadcasted_iota(jnp.int32, (n, n), 1)).astype(F32)
    m1 = -l_mat
    m2 = mms(m1, m1)
    m4 = mms(m2, m2)
    m8 = mms(m4, m4)
    m16 = mms(m8, m8)
    m32 = mms(m16, m16)
    low = mms(eye + m1, eye + m2)
    mid = mms(eye + m4, eye + m8)
    high = mms(eye + m16, eye + m32)
    return mms(mms(low, mid), high)


@jax.custom_vjp
def _known_inverse(l_mat, inv):
    return inv


_known_inverse.defvjp(lambda l_mat, inv: (inv, inv),
                      lambda inv, g: (-mms_tn(inv, mms_nt(g, inv)), jnp.zeros_like(inv)))


def _ut_chain(k, v, beta, gc_c, gc_r, inv=None):
    r = k.shape[0]
    kn = _l2n(k)
    row = lax.broadcasted_iota(jnp.int32, (r, r), 0)
    col = lax.broadcasted_iota(jnp.int32, (r, r), 1)
    strict = (row // DN_CHUNK == col // DN_CHUNK) & (row > col)
    decay = jnp.exp(jnp.where(strict, gc_c - gc_r, -1e30))
    l_mat = jnp.where(strict, beta * mm_nt(kn, kn) * decay, 0.0)
    inv = _unit_lower_inverse(l_mat) if inv is None else _known_inverse(l_mat, inv)
    return mms(inv, v * beta), mms(inv, kn * (beta * jnp.exp(gc_c))), inv


def _rec_chain(q, k, gc_c, gc_r, u, w, state):
    ch, dh = q.shape
    qn = _l2n(q) * (dh ** -0.5)
    kn = _l2n(k)
    row = lax.broadcasted_iota(jnp.int32, (ch, ch), 0)
    col = lax.broadcasted_iota(jnp.int32, (ch, ch), 1)
    decay = jnp.exp(jnp.where(row >= col, gc_c - gc_r, -1e30))
    attn = mm_nt(qn, kn) * decay
    is_last = lax.broadcasted_iota(jnp.int32, gc_r.shape, 1) == ch - 1
    last = jnp.sum(jnp.where(is_last, gc_r, 0.0), axis=1, keepdims=True)
    v_new = u - mm(w, state)
    out = mm(qn * jnp.exp(gc_c), state) + mm(attn, v_new)
    return out, state * jnp.exp(last) + mm_tn(kn * jnp.exp(last - gc_c), v_new)


def _gated_norm(o, z, g):
    return _rms(o, g) * _silu(z)


def _attn_block(q, k, v):
    s = mm_nt(q, k) * (q.shape[-1] ** -0.5)
    p = jnp.exp(s - lax.stop_gradient(jnp.max(s, axis=-1, keepdims=True)))
    return mm(p / jnp.sum(p, axis=-1, keepdims=True), v)


def _ffn_block(gate_pre, up, taps, bias):
    return _silu(_causal_conv(gate_pre, taps) + bias) * up


def _accumulate(ref, value, first):
    @pl.when(first)
    def _():
        ref[...] = value

    @pl.when(jnp.logical_not(first))
    def _():
        ref[...] += value


def norm_fwd(x, g):
    t, d = x.shape
    tr = _tile(t, 256, 8)

    def body(x_ref, g_ref, h_ref):
        h_ref[...] = _rms(x_ref[...], g_ref[...]).astype(BF16)

    return pl.pallas_call(
        body, name=f"norm_fwd_{t}", grid=(t // tr,),
        in_specs=[pl.BlockSpec((tr, d), lambda i: (i, 0)), pl.BlockSpec((1, d), lambda i: (0, 0))],
        out_specs=pl.BlockSpec((tr, d), lambda i: (i, 0)),
        out_shape=jax.ShapeDtypeStruct((t, d), BF16), compiler_params=_params("parallel"),
    )(x, g.reshape(1, d))


def norm_bwd(x, g, dh, dres):
    t, d = x.shape
    tr = _tile(t, 256, 8)
    has_res = dres is not None

    def body(x_ref, g_ref, dh_ref, *rest):
        rest = list(rest)
        r_ref = rest.pop(0) if has_res else None
        dx_ref, dxb_ref, dg_ref = rest
        _, pull = jax.vjp(_rms, x_ref[...], g_ref[...])
        dx, dg = pull(dh_ref[...])
        if has_res:
            dx = dx + r_ref[...]
        dx_ref[...] = dx
        dxb_ref[...] = dx.astype(BF16)
        _accumulate(dg_ref, dg, pl.program_id(0) == 0)

    row = pl.BlockSpec((tr, d), lambda i: (i, 0))
    vec = pl.BlockSpec((1, d), lambda i: (0, 0))
    dx, dxb, dg = pl.pallas_call(
        body, name=f"norm_bwd_{t}" + ("_res" if has_res else ""), grid=(t // tr,),
        in_specs=[row, vec, row] + ([row] if has_res else []),
        out_specs=[row, row, vec],
        out_shape=[jax.ShapeDtypeStruct((t, d), F32), jax.ShapeDtypeStruct((t, d), BF16),
                   jax.ShapeDtypeStruct((1, d), F32)],
        compiler_params=_params("arbitrary"),
    )(x, g.reshape(1, d), dh, *([dres] if has_res else []))
    return dx, dxb, dg.reshape(d)


def loss_head(x, target, g):
    t, d = x.shape
    tr = _tile(t, 256, 8)

    def body(x_ref, t_ref, g_ref, l_ref, dx_ref, dxb_ref, dg_ref):
        tgt = t_ref[...]

        def block_loss(xv, gv):
            return 0.5 * jnp.sum(jnp.mean(jnp.square(_rms(xv, gv) - tgt), axis=-1))

        val, pull = jax.vjp(block_loss, x_ref[...], g_ref[...])
        dx, dg = pull(jnp.ones((), F32))
        dx_ref[...] = dx
        dxb_ref[...] = dx.astype(BF16)
        first = pl.program_id(0) == 0
        _accumulate(dg_ref, dg, first)
        _accumulate(l_ref, jnp.full((1, LANES), val, F32), first)

    row = pl.BlockSpec((tr, d), lambda i: (i, 0))
    vec = pl.BlockSpec((1, d), lambda i: (0, 0))
    loss, dx, dxb, dg = pl.pallas_call(
        body, name="loss_head", grid=(t // tr,),
        in_specs=[row, row, vec],
        out_specs=[pl.BlockSpec((1, LANES), lambda i: (0, 0)), row, row, vec],
        out_shape=[jax.ShapeDtypeStruct((1, LANES), F32), jax.ShapeDtypeStruct((t, d), F32),
                   jax.ShapeDtypeStruct((t, d), BF16), jax.ShapeDtypeStruct((1, d), F32)],
        compiler_params=_params("arbitrary"),
    )(x, target, g.reshape(1, d))
    return loss[0, 0], dx, dxb, dg.reshape(d)


def pool_fwd(proj, w_pool, scale, b, s):
    n_g, grp = w_pool.shape[0], w_pool.shape[-1]

    def body(u_ref, w_ref, s_ref, y_ref):
        y_ref[...] = _pool_block(u_ref[...], w_ref[...], s_ref[...], pl.program_id(1)).astype(BF16)

    blk = pl.BlockSpec((s, grp), lambda i, j: (i, j))
    return pl.pallas_call(
        body, name="pool_fwd", grid=(b, n_g),
        in_specs=[blk, pl.BlockSpec((None, grp, grp), lambda i, j: (j, 0, 0)),
                  pl.BlockSpec((1, grp), lambda i, j: (0, j))],
        out_specs=blk,
        out_shape=jax.ShapeDtypeStruct((b * s, n_g * grp), BF16), compiler_params=_params("parallel", "parallel"),
    )(proj, w_pool, scale.reshape(1, -1))


def pool_bwd(proj, w_pool, scale, dmixed, b, s):
    n_g, grp = w_pool.shape[0], w_pool.shape[-1]

    def body(u_ref, w_ref, s_ref, dy_ref, du_ref, dw_ref, ds_ref):
        group = pl.program_id(0)
        _, pull = jax.vjp(lambda u, w, sc: _pool_block(u, w, sc, group), u_ref[...], w_ref[...].astype(F32),
                          s_ref[...])
        du, dw, ds = pull(dy_ref[...])
        du_ref[...] = du.astype(BF16)
        first = pl.program_id(1) == 0
        _accumulate(dw_ref, dw, first)
        _accumulate(ds_ref, ds, first)

    blk = pl.BlockSpec((s, grp), lambda j, i: (i, j))
    w_spec = pl.BlockSpec((None, grp, grp), lambda j, i: (j, 0, 0))
    s_spec = pl.BlockSpec((1, grp), lambda j, i: (0, j))
    du, dw, ds = pl.pallas_call(
        body, name="pool_bwd", grid=(n_g, b),
        in_specs=[blk, w_spec, s_spec, blk],
        out_specs=[blk, w_spec, s_spec],
        out_shape=[jax.ShapeDtypeStruct((b * s, n_g * grp), BF16), jax.ShapeDtypeStruct(w_pool.shape, F32),
                   jax.ShapeDtypeStruct((1, n_g * grp), F32)],
        compiler_params=_params("arbitrary", "arbitrary"),
    )(proj, w_pool, scale.reshape(1, -1), dmixed)
    return du, dw, ds.reshape(-1)


def conv_silu_fwd(proj, conv_w, part, col0, width, b, s):
    x_off, w_off = (col0 + part * width) // FFN_COLS, part * width // FFN_COLS

    def body(x_ref, w_ref, y_ref):
        taps = [w_ref[k:k + 1, :] for k in range(DN_TAPS)]
        y_ref[...] = _silu(_causal_conv(x_ref[...], taps))

    return pl.pallas_call(
        body, name=f"conv_silu_fwd_{part}", grid=(width // FFN_COLS, b),
        in_specs=[pl.BlockSpec((s, FFN_COLS), lambda j, i: (i, x_off + j)),
                  pl.BlockSpec((DN_TAPS, FFN_COLS), lambda j, i: (0, w_off + j))],
        out_specs=pl.BlockSpec((s, FFN_COLS), lambda j, i: (i, j)),
        out_shape=jax.ShapeDtypeStruct((b * s, width), F32), compiler_params=_params("parallel", "parallel"),
    )(proj, conv_w)


def conv_silu_bwd(proj, conv_w, dact, part, col0, width, b, s):
    x_off, w_off = (col0 + part * width) // FFN_COLS, part * width // FFN_COLS

    def body(x_ref, w_ref, dy_ref, dx_ref, dw_ref):
        taps = [w_ref[k:k + 1, :] for k in range(DN_TAPS)]
        _, pull = jax.vjp(lambda x, *tp: _silu(_causal_conv(x, tp)), x_ref[...], *taps)
        dx, *dtaps = pull(dy_ref[...])
        dx_ref[...] = dx.astype(BF16)
        first = pl.program_id(1) == 0
        for k in range(DN_TAPS):
            _accumulate(dw_ref.at[k:k + 1, :], dtaps[k], first)

    out_blk = pl.BlockSpec((s, FFN_COLS), lambda j, i: (i, j))
    return pl.pallas_call(
        body, name=f"conv_silu_bwd_{part}", grid=(width // FFN_COLS, b),
        in_specs=[pl.BlockSpec((s, FFN_COLS), lambda j, i: (i, x_off + j)),
                  pl.BlockSpec((DN_TAPS, FFN_COLS), lambda j, i: (0, w_off + j)), out_blk],
        out_specs=[out_blk, pl.BlockSpec((DN_TAPS, FFN_COLS), lambda j, i: (0, j))],
        out_shape=[jax.ShapeDtypeStruct((b * s, width), BF16), jax.ShapeDtypeStruct((DN_TAPS, width), F32)],
        compiler_params=_params("arbitrary", "arbitrary"),
    )(proj, conv_w, dact)


def _ut_specs(b, s, heads, width):
    r = min(UT_ROWS, s)
    ns = s // r
    tok = pl.BlockSpec((r, width), lambda i, n: (i * ns + n, 0))
    col = pl.BlockSpec((None, heads, r, 1), lambda i, n: (i, 0, n, 0))
    row = pl.BlockSpec((None, heads, None, 1, r), lambda i, n: (i, 0, n, 0, 0))
    return r, ns, tok, col, row


def ut_fwd(act_k, act_v, beta_col, gc_col, gc_row, b, s, ride=None):
    width = act_k.shape[1]
    heads = width // DN_HEAD_DIM
    r, ns, tok, col, row = _ut_specs(b, s, heads, width)
    inv_spec = pl.BlockSpec((r, heads * r), lambda i, n: (i * ns + n, 0))

    n_ride_in = len(ride.arrays) if ride else 0

    def body(k_ref, v_ref, beta_ref, gcc_ref, gcr_ref, *rest):
        u_ref, w_ref, inv_ref = rest[n_ride_in:n_ride_in + 3]
        if ride:
            ride_in, ride_out, sems = rest[:n_ride_in], rest[n_ride_in + 3:-3], rest[-3:]
            first, last = _first_and_last_step((b, ns))
            pl.when(first)(lambda: ride.start(ride_in, ride_out, *sems))
        for h in range(heads):
            sl = slice(h * DN_HEAD_DIM, (h + 1) * DN_HEAD_DIM)
            u, w, inv = _ut_chain(k_ref[:, sl], v_ref[:, sl], beta_ref[h], gcc_ref[h], gcr_ref[h])
            u_ref[:, sl] = u
            w_ref[:, sl] = w
            inv_ref[:, h * r:(h + 1) * r] = inv
        if ride:
            pl.when(last)(lambda: ride.finish(ride_in, ride_out, *sems))

    out = jax.ShapeDtypeStruct((b * s, width), F32)
    outs = pl.pallas_call(
        body, name="ut_fwd" + ("_with_" + ride.tag if ride else ""), grid=(b, ns),
        in_specs=[tok, tok, col, col, row] + (ride.specs_in if ride else []),
        out_specs=[tok, tok, inv_spec] + (ride.specs_out if ride else []),
        out_shape=[out, out, jax.ShapeDtypeStruct((b * s, heads * r), F32)] + (ride.out_shapes if ride else []),
        scratch_shapes=ride.scratch if ride else [],
        compiler_params=_params("arbitrary", "arbitrary") if ride else _params("parallel", "parallel"),
    )(act_k, act_v, beta_col, gc_col, gc_row, *(ride.arrays if ride else []))
    return (*outs[:3], outs[3:]) if ride else outs


def ut_bwd(act_k, act_v, beta_col, gc_col, gc_row, inv, du, dw, dk_more, b, s):
    width = act_k.shape[1]
    heads = width // DN_HEAD_DIM
    r, ns, tok, col, row = _ut_specs(b, s, heads, width)
    inv_spec = pl.BlockSpec((r, heads * r), lambda i, n: (i * ns + n, 0))

    def body(k_ref, v_ref, beta_ref, gcc_ref, gcr_ref, inv_ref, du_ref, dw_ref, dkm_ref,
             dk_ref, dv_ref, dbeta_ref, dgcc_ref, dgcr_ref):
        for h in range(heads):
            sl = slice(h * DN_HEAD_DIM, (h + 1) * DN_HEAD_DIM)
            inv = inv_ref[:, h * r:(h + 1) * r]
            _, pull = jax.vjp(lambda *a: _ut_chain(*a, inv=inv)[:2], k_ref[:, sl], v_ref[:, sl], beta_ref[h],
                              gcc_ref[h], gcr_ref[h])
            dk, dv, dbeta, dgcc, dgcr = pull((du_ref[:, sl], dw_ref[:, sl]))
            dk_ref[:, sl] = dk + dkm_ref[:, sl]
            dv_ref[:, sl] = dv
            dbeta_ref[h] = dbeta
            dgcc_ref[h] = dgcc
            dgcr_ref[h] = dgcr

    out = jax.ShapeDtypeStruct((b * s, width), F32)
    return pl.pallas_call(
        body, name="ut_bwd", grid=(b, ns), in_specs=[tok, tok, col, col, row, inv_spec, tok, tok, tok],
        out_specs=[tok, tok, col, col, row],
        out_shape=[out, out, jax.ShapeDtypeStruct(beta_col.shape, F32), jax.ShapeDtypeStruct(gc_col.shape, F32),
                   jax.ShapeDtypeStruct(gc_row.shape, F32)],
        compiler_params=_params("parallel", "parallel"),
    )(act_k, act_v, beta_col, gc_col, gc_row, inv, du, dw, dk_more)


def _rec_specs(b, n, heads, width, chunk_of):
    ch = DN_CHUNK
    tok = pl.BlockSpec((b, None, ch, width), lambda i: (0, chunk_of(i), 0, 0))
    col = pl.BlockSpec((b, heads, None, ch, 1), lambda i: (0, 0, chunk_of(i), 0, 0))
    row = pl.BlockSpec((b, heads, None, 1, ch), lambda i: (0, 0, chunk_of(i), 0, 0))
    st = pl.BlockSpec((None, b, heads, DN_HEAD_DIM, DN_HEAD_DIM), lambda i: (chunk_of(i), 0, 0, 0, 0))
    return tok, col, row, st


def rec_fwd(act_q, act_k, gc_col, gc_row, u, w, b, s):
    width = act_q.shape[1]
    heads, n = width // DN_HEAD_DIM, s // DN_CHUNK
    tok, col, row, st = _rec_specs(b, n, heads, width, lambda i: i)
    shape4 = (b, n, DN_CHUNK, width)

    def body(q_ref, k_ref, gcc_ref, gcr_ref, u_ref, w_ref, o_ref, st_ref, state):
        @pl.when(pl.program_id(0) == 0)
        def _():
            state[...] = jnp.zeros_like(state)

        for i in range(b):
            for h in range(heads):
                sl = slice(h * DN_HEAD_DIM, (h + 1) * DN_HEAD_DIM)
                s_in = state[i, h]
                st_ref[i, h] = s_in
                o, s_out = _rec_chain(q_ref[i, :, sl], k_ref[i, :, sl], gcc_ref[i, h], gcr_ref[i, h],
                                      u_ref[i, :, sl], w_ref[i, :, sl], s_in)
                o_ref[i, :, sl] = o
                state[i, h] = s_out

    o, states = pl.pallas_call(
        body, name="rec_fwd", grid=(n,), in_specs=[tok, tok, col, row, tok, tok], out_specs=[tok, st],
        out_shape=[jax.ShapeDtypeStruct(shape4, F32),
                   jax.ShapeDtypeStruct((n, b, heads, DN_HEAD_DIM, DN_HEAD_DIM), F32)],
        scratch_shapes=[pltpu.VMEM((b, heads, DN_HEAD_DIM, DN_HEAD_DIM), F32)],
        compiler_params=_params("arbitrary"),
    )(act_q.reshape(shape4), act_k.reshape(shape4), gc_col.reshape(b, heads, n, DN_CHUNK, 1), gc_row,
      u.reshape(shape4), w.reshape(shape4))
    return o.reshape(b * s, width), states


def rec_bwd(act_q, act_k, gc_col, gc_row, u, w, states, do, b, s):
    width = act_q.shape[1]
    heads, n = width // DN_HEAD_DIM, s // DN_CHUNK
    tok, col, row, st = _rec_specs(b, n, heads, width, lambda i: n - 1 - i)
    shape4 = (b, n, DN_CHUNK, width)

    def body(q_ref, k_ref, gcc_ref, gcr_ref, u_ref, w_ref, st_ref, do_ref,
             dq_ref, dk_ref, du_ref, dw_ref, dgcc_ref, dgcr_ref, dstate):
        @pl.when(pl.program_id(0) == 0)
        def _():
            dstate[...] = jnp.zeros_like(dstate)

        for i in range(b):
            for h in range(heads):
                sl = slice(h * DN_HEAD_DIM, (h + 1) * DN_HEAD_DIM)
                _, pull = jax.vjp(_rec_chain, q_ref[i, :, sl], k_ref[i, :, sl], gcc_ref[i, h], gcr_ref[i, h],
                                  u_ref[i, :, sl], w_ref[i, :, sl], st_ref[i, h])
                dq, dk, dgcc, dgcr, du, dw, ds = pull((do_ref[i, :, sl], dstate[i, h]))
                dq_ref[i, :, sl] = dq
                dk_ref[i, :, sl] = dk
                du_ref[i, :, sl] = du
                dw_ref[i, :, sl] = dw
                dgcc_ref[i, h] = dgcc
                dgcr_ref[i, h] = dgcr
                dstate[i, h] = ds

    tok_out = jax.ShapeDtypeStruct(shape4, F32)
    dq, dk, du, dw, dgcc, dgcr = pl.pallas_call(
        body, name="rec_bwd", grid=(n,), in_specs=[tok, tok, col, row, tok, tok, st, tok],
        out_specs=[tok, tok, tok, tok, col, row],
        out_shape=[tok_out, tok_out, tok_out, tok_out,
                   jax.ShapeDtypeStruct((b, heads, n, DN_CHUNK, 1), F32), jax.ShapeDtypeStruct(gc_row.shape, F32)],
        scratch_shapes=[pltpu.VMEM((b, heads, DN_HEAD_DIM, DN_HEAD_DIM), F32)],
        compiler_params=_params("arbitrary"),
    )(act_q.reshape(shape4), act_k.reshape(shape4), gc_col.reshape(b, heads, n, DN_CHUNK, 1), gc_row,
      u.reshape(shape4), w.reshape(shape4), states, do.reshape(shape4))
    flat = lambda a: a.reshape(b * s, width)
    return flat(dq), flat(dk), flat(du), flat(dw), dgcc.reshape(b, heads, s, 1), dgcr


def dn_norm_fwd(o, proj, g, z_col0):
    t, width = o.shape
    heads = width // DN_HEAD_DIM
    tr = _tile(t, 512, 8)
    z_off = z_col0 // width

    def body(o_ref, z_ref, g_ref, y_ref):
        for h in range(heads):
            sl = slice(h * DN_HEAD_DIM, (h + 1) * DN_HEAD_DIM)
            y_ref[:, sl] = _gated_norm(o_ref[:, sl], z_ref[:, sl], g_ref[...]).astype(BF16)

    blk = pl.BlockSpec((tr, width), lambda i: (i, 0))
    return pl.pallas_call(
        body, name="dn_norm_fwd", grid=(t // tr,),
        in_specs=[blk, pl.BlockSpec((tr, width), lambda i: (i, z_off)),
                  pl.BlockSpec((1, DN_HEAD_DIM), lambda i: (0, 0))],
        out_specs=blk, out_shape=jax.ShapeDtypeStruct((t, width), BF16),
        compiler_params=_params("parallel"),
    )(o, proj, g.reshape(1, -1))


def dn_norm_bwd(o, proj, g, dmixed, z_col0, dy_col0):
    t, width = o.shape
    heads = width // DN_HEAD_DIM
    tr = _tile(t, 512, 8)
    z_off, dy_off = z_col0 // width, dy_col0 // width

    def body(o_ref, z_ref, g_ref, dy_ref, do_ref, dz_ref, dg_ref):
        dg_sum = None
        for h in range(heads):
            sl = slice(h * DN_HEAD_DIM, (h + 1) * DN_HEAD_DIM)
            _, pull = jax.vjp(_gated_norm, o_ref[:, sl], z_ref[:, sl], g_ref[...])
            do, dz, dg = pull(dy_ref[:, sl])
            do_ref[:, sl] = do
            dz_ref[:, sl] = dz.astype(BF16)
            dg_sum = dg if dg_sum is None else dg_sum + dg
        _accumulate(dg_ref, dg_sum, pl.program_id(0) == 0)

    blk = pl.BlockSpec((tr, width), lambda i: (i, 0))
    vec = pl.BlockSpec((1, DN_HEAD_DIM), lambda i: (0, 0))
    do, dz, dg = pl.pallas_call(
        body, name="dn_norm_bwd", grid=(t // tr,),
        in_specs=[blk, pl.BlockSpec((tr, width), lambda i: (i, z_off)), vec,
                  pl.BlockSpec((tr, width), lambda i: (i, dy_off))],
        out_specs=[blk, blk, vec],
        out_shape=[jax.ShapeDtypeStruct((t, width), F32), jax.ShapeDtypeStruct((t, width), BF16),
                   jax.ShapeDtypeStruct((1, DN_HEAD_DIM), F32)],
        compiler_params=_params("arbitrary"),
    )(o, proj, g.reshape(1, -1), dmixed)
    return do, dz, dg.reshape(-1)


def _attn_specs(b, s, mem_len, d):
    hd = d // XA_HEADS
    tq = _tile(s, 512, 8)
    nq = s // tq
    q_spec = pl.BlockSpec((tq, hd), lambda i, h, j: (i * nq + j, h))
    k_spec = pl.BlockSpec((mem_len, hd), lambda i, h, j: (i, h))
    v_spec = pl.BlockSpec((mem_len, hd), lambda i, h, j: (i, XA_HEADS + h))
    return nq, q_spec, k_spec, v_spec


def attn_fwd(q, kv, b, s):
    d = q.shape[1]
    nq, q_spec, k_spec, v_spec = _attn_specs(b, s, kv.shape[0] // b, d)

    def body(q_ref, k_ref, v_ref, o_ref):
        o_ref[...] = _attn_block(q_ref[...], k_ref[...], v_ref[...]).astype(BF16)

    return pl.pallas_call(
        body, name="attn_fwd", grid=(b, XA_HEADS, nq), in_specs=[q_spec, k_spec, v_spec], out_specs=q_spec,
        out_shape=jax.ShapeDtypeStruct(q.shape, BF16), compiler_params=_params("parallel", "parallel", "parallel"),
    )(q, kv, kv)


def attn_bwd(q, kv, do, b, s):
    d = q.shape[1]
    rows = kv.shape[0]
    nq, q_spec, k_spec, v_spec = _attn_specs(b, s, rows // b, d)

    def body(q_ref, k_ref, v_ref, do_ref, dq_ref, dk_ref, dv_ref):
        _, pull = jax.vjp(_attn_block, q_ref[...], k_ref[...], v_ref[...])
        dq, dk, dv = pull(do_ref[...])
        dq_ref[...] = dq.astype(BF16)
        first = pl.program_id(2) == 0
        _accumulate(dk_ref, dk, first)
        _accumulate(dv_ref, dv, first)

    kv_out = jax.ShapeDtypeStruct((rows, d), F32)
    return pl.pallas_call(
        body, name="attn_bwd", grid=(b, XA_HEADS, nq), in_specs=[q_spec, k_spec, v_spec, q_spec],
        out_specs=[q_spec, k_spec, k_spec], out_shape=[jax.ShapeDtypeStruct(q.shape, BF16), kv_out, kv_out],
        compiler_params=_params("parallel", "parallel", "arbitrary"),
    )(q, kv, kv, do)


def ffn_fwd(gate_pre, up, conv_w, conv_b, b, s):
    taps_n, f = conv_w.shape

    def body(g_ref, u_ref, w_ref, b_ref, y_ref):
        taps = [w_ref[k:k + 1, :] for k in range(taps_n)]
        y_ref[...] = _ffn_block(g_ref[...], u_ref[...], taps, b_ref[...]).astype(BF16)

    blk = pl.BlockSpec((s, FFN_COLS), lambda j, i: (i, j))
    return pl.pallas_call(
        body, name="ffn_fwd", grid=(f // FFN_COLS, b),
        in_specs=[blk, blk, pl.BlockSpec((taps_n, FFN_COLS), lambda j, i: (0, j)),
                  pl.BlockSpec((1, FFN_COLS), lambda j, i: (0, j))],
        out_specs=blk,
        out_shape=jax.ShapeDtypeStruct((b * s, f), BF16), compiler_params=_params("parallel", "parallel"),
    )(gate_pre, up, conv_w, conv_b.reshape(1, f))


def ffn_bwd(gate_pre, up, conv_w, conv_b, dact, b, s):
    taps_n, f = conv_w.shape

    def body(g_ref, u_ref, w_ref, b_ref, dy_ref, dg_ref, du_ref, dw_ref, db_ref):
        taps = [w_ref[k:k + 1, :] for k in range(taps_n)]
        _, pull = jax.vjp(lambda gt, up_, bias, *tp: _ffn_block(gt, up_, tp, bias), g_ref[...], u_ref[...],
                          b_ref[...], *taps)
        dgate, dup, dbias, *dtaps = pull(dy_ref[...])
        dg_ref[...] = dgate.astype(BF16)
        du_ref[...] = dup.astype(BF16)
        first = pl.program_id(1) == 0
        _accumulate(db_ref, dbias, first)
        for k in range(taps_n):
            _accumulate(dw_ref.at[k:k + 1, :], dtaps[k], first)

    blk = pl.BlockSpec((s, FFN_COLS), lambda j, i: (i, j))
    w_spec = pl.BlockSpec((taps_n, FFN_COLS), lambda j, i: (0, j))
    b_spec = pl.BlockSpec((1, FFN_COLS), lambda j, i: (0, j))
    half = jax.ShapeDtypeStruct(gate_pre.shape, BF16)
    dgate, dup, dw, db = pl.pallas_call(
        body, name="ffn_bwd", grid=(f // FFN_COLS, b),
        in_specs=[blk, blk, w_spec, b_spec, blk],
        out_specs=[blk, blk, w_spec, b_spec],
        out_shape=[half, half, jax.ShapeDtypeStruct((taps_n, f), F32), jax.ShapeDtypeStruct((1, f), F32)],
        compiler_params=_params("arbitrary", "arbitrary"),
    )(gate_pre, up, conv_w, conv_b.reshape(1, f), dact)
    return dgate, dup, dw, db.reshape(f)


def gate_arrays(logits, a_log, dt_bias, b, s, heads):
    n, r = s // DN_CHUNK, min(UT_ROWS, s)
    lg = logits.reshape(b, s, -1)
    beta = jax.nn.sigmoid(lg[..., :heads])
    g = -jnp.exp(a_log) * jax.nn.softplus(lg[..., heads:2 * heads] + dt_bias)
    gc = jnp.cumsum(g.reshape(b, n, DN_CHUNK, heads), axis=2).transpose(0, 3, 1, 2)
    return (beta.transpose(0, 2, 1)[..., None], gc.reshape(b, heads, s, 1), gc[:, :, :, None, :],
            gc.reshape(b, heads, s // r, 1, r))


class Riders:
    def __init__(self, make_ride, groups, arrays, tag):
        self.groups, self.results = groups, {}
        self.rides = {host: make_ride([arrays[n] for n in members], f"{tag}_on_{host}")
                      for host, members in groups.items()}

    def run(self, host, call):
        value, outs = call(self.rides[host])
        self.results.update(zip(self.groups[host], outs))
        return value


def _hosted_matmul(riders, host, *args, **kwargs):
    if riders is None:
        return matmul(*args, **kwargs)
    return riders.run(host, lambda ride: matmul(*args, ride=ride, **kwargs))


FORWARD_HOSTS = {'in_proj': ('w_in',), 'ut': ('w_gate', 'w_up'), 'gate': ('w_xkv',),
                 'up': ('w_mix_out', 'w_xq', 'w_xo', 'w_pool', 'dn_conv_w', 'ffn_conv_w'), 'down': ('w_down',)}
BACKWARD_HOSTS = {'down_wgrad': ('w_down',), 'gate_dgrad': ('w_xkv',), 'up_dgrad': ('w_in',),
                  'gate_wgrad': ('w_gate',), 'up_wgrad': ('w_up',),
                  'in_dgrad': ('w_mix_out', 'w_xq', 'w_xo', 'w_pool', 'dn_conv_w', 'ffn_conv_w')}
EARLY_HOSTS = {'in_wgrad': ('w_down',), 'mix_dgrad': ('w_xo',), 'mix_wgrad': ('w_xq', 'ffn_conv_w')}


def forward_layer(x, mem_hb, p, b, s, riders=None):
    d = x.shape[1]
    pw = d // 2
    dn = d - pw
    heads = dn // DN_HEAD_DIM
    sv = {'x0': x}
    sv['h1'] = h1 = norm_fwd(x, p['mix_norm_g'])
    sv['proj'] = proj = _hosted_matmul(riders, 'in_proj', h1, p['w_in_main'], 'nn', F32)
    logits = matmul(h1, p['w_in_logits'], 'nn', F32)
    y_pool = pool_fwd(proj, p['w_pool'], p['pool_scale'], b, s)
    gates, sv['gates_pull'] = jax.vjp(lambda lg, al, dtb: gate_arrays(lg, al, dtb, b, s, heads), logits,
                                      p['dn_a_log'], p['dn_dt_bias'])
    sv['gates'] = beta_col, gc_col, gc_row, gc_row_ut = gates
    sv['act'] = aq, ak, av = [conv_silu_fwd(proj, p['dn_conv_w'], part, pw, dn, b, s) for part in range(3)]
    if riders is None:
        u, w, sv['inv'] = ut_fwd(ak, av, beta_col, gc_col, gc_row_ut, b, s)
    else:
        def hosted(ride):
            *own, outs = ut_fwd(ak, av, beta_col, gc_col, gc_row_ut, b, s, ride=ride)
            return own, outs
        u, w, sv['inv'] = riders.run('ut', hosted)
    sv['u'], sv['w'] = u, w
    sv['o_dn'], sv['states'] = o_dn, _ = rec_fwd(aq, ak, gc_col, gc_row, u, w, b, s)
    y_dn = dn_norm_fwd(o_dn, proj, p['dn_norm_g'], pw + 3 * dn)
    sv['mixed'] = mixed = jnp.concatenate([y_pool, y_dn], axis=1)
    sv['x1'] = x1 = matmul(mixed, p['w_mix_out'], 'nn', F32, res=x)

    sv['h2'] = h2 = norm_fwd(x1, p['xa_norm_g'])
    sv['q'] = q = matmul(h2, p['w_xq'], 'nn', F32)
    sv['kv'] = kv = matmul(mem_hb, p['w_xkv'], 'nn', F32)
    sv['o_at'] = o_at = attn_fwd(q, kv, b, s)
    sv['x2'] = x2 = matmul(o_at, p['w_xo'], 'nn', F32, res=x1)

    sv['h3'] = h3 = norm_fwd(x2, p['ffn_norm_g'])
    sv['gate_pre'] = gate_pre = _hosted_matmul(riders, 'gate', h3, p['w_gate'], 'nn', F32)
    sv['up'] = up = _hosted_matmul(riders, 'up', h3, p['w_up'], 'nn', F32)
    sv['a_ffn'] = a_ffn = ffn_fwd(gate_pre, up, p['ffn_conv_w'], p['ffn_conv_b'], b, s)
    return _hosted_matmul(riders, 'down', a_ffn, p['w_down'], 'nn', F32, res=x2), sv


def backward_layer(dx, dxb, sv, mem_hb, p, b, s, above=None, early=None):
    d = dx.shape[1]
    pw = d // 2
    dn = d - pw
    g = {}
    if above is None:
        riders = None
        da = matmul(dxb, p['w_down'], 'nt', F32)
    else:
        names, parts, core, tag = above
        da, from_sibling = matmul(dxb, p['w_down'], 'nt', F32, ride=sibling_swap_ride(parts, tag))
        chip_sums = {n: chip_sum(a, f, core) for n, a, f in zip(names, parts, from_sibling)}
        riders = Riders(chip_exchange_ride, BACKWARD_HOSTS, chip_sums, tag)
    g['w_down'] = _hosted_matmul(riders, 'down_wgrad', sv['a_ffn'], dxb, 'tn', BF16)
    dgate, dup, g['ffn_conv_w'], g['ffn_conv_b'] = ffn_bwd(sv['gate_pre'], sv['up'], p['ffn_conv_w'],
                                                           p['ffn_conv_b'], da, b, s)
    dh = _hosted_matmul(riders, 'gate_dgrad', dgate, p['w_gate'], 'nt', F32)
    dh = _hosted_matmul(riders, 'up_dgrad', dup, p['w_up'], 'nt', F32, res=dh)
    g['w_gate'] = _hosted_matmul(riders, 'gate_wgrad', sv['h3'], dgate, 'tn', BF16)
    g['w_up'] = _hosted_matmul(riders, 'up_wgrad', sv['h3'], dup, 'tn', BF16)
    dx, dxb, g['ffn_norm_g'] = norm_bwd(sv['x2'], p['ffn_norm_g'], dh, dx)

    do = matmul(dxb, p['w_xo'], 'nt', F32)
    g['w_xo'] = matmul(sv['o_at'], dxb, 'tn', BF16)
    dq, dk, dv = attn_bwd(sv['q'], sv['kv'], do, b, s)
    dkv = jnp.concatenate([dk, dv], axis=1).astype(BF16)
    dh = matmul(dq, p['w_xq'], 'nt', F32)
    g['w_xq'] = matmul(sv['h2'], dq, 'tn', BF16)
    g['w_xkv'] = matmul(mem_hb, dkv, 'tn', BF16)
    dmem_h = matmul(dkv, p['w_xkv'], 'nt', F32)
    dx, dxb, g['xa_norm_g'] = norm_bwd(sv['x1'], p['xa_norm_g'], dh, dx)

    early_riders = early(g) if early else None
    dmixed = _hosted_matmul(early_riders, 'mix_dgrad', dxb, p['w_mix_out'], 'nt', F32)
    g['w_mix_out'] = _hosted_matmul(early_riders, 'mix_wgrad', sv['mixed'], dxb, 'tn', BF16)
    proj = sv['proj']
    du_pool, g['w_pool'], g['pool_scale'] = pool_bwd(proj, p['w_pool'], p['pool_scale'], dmixed, b, s)
    do_dn, dz, g['dn_norm_g'] = dn_norm_bwd(sv['o_dn'], proj, p['dn_norm_g'], dmixed, pw + 3 * dn, pw)
    beta_col, gc_col, gc_row, gc_row_ut = sv['gates']
    aq, ak, av = sv['act']
    daq, dak_rec, du, dw, dgcc_rec, dgcr = rec_bwd(aq, ak, gc_col, gc_row, sv['u'], sv['w'], sv['states'],
                                                   do_dn, b, s)
    dak, dav, dbeta, dgcc_ut, dgcr_ut = ut_bwd(ak, av, beta_col, gc_col, gc_row_ut, sv['inv'], du, dw, dak_rec, b, s)
    dparts, dtaps = zip(*[conv_silu_bwd(proj, p['dn_conv_w'], dact, part, pw, dn, b, s)
                          for part, dact in enumerate((daq, dak, dav))])
    g['dn_conv_w'] = jnp.concatenate(dtaps, axis=1)
    dlogits, g['dn_a_log'], g['dn_dt_bias'] = sv['gates_pull']((dbeta, dgcc_rec + dgcc_ut, dgcr, dgcr_ut))
    dproj = jnp.concatenate([du_pool, *dparts, dz], axis=1)
    dlogits = dlogits.astype(BF16)
    dh = _hosted_matmul(riders, 'in_dgrad', dproj, p['w_in_main'], 'nt', F32,
                        res=matmul(dlogits, p['w_in_logits'], 'nt', F32))
    g['w_in_main'] = _hosted_matmul(early_riders, 'in_wgrad', sv['h1'], dproj, 'tn', BF16)
    g['w_in_logits'] = matmul(sv['h1'], dlogits, 'tn', BF16)
    dx, dxb, g['mix_norm_g'] = norm_bwd(sv['x0'], p['mix_norm_g'], dh, dx)
    return (dx, dxb, dmem_h, g, (riders.results if riders else None),
            (early_riders.results if early_riders else None))


def _to_full(gathered, axis):
    moved = jnp.moveaxis(gathered, 0, axis)
    shape = list(gathered.shape[1:])
    shape[axis] *= N_DEV
    return moved.reshape(shape)


def _to_parts(full, axis):
    shape = list(full.shape)
    shape[axis:axis + 1] = [N_DEV, shape[axis] // N_DEV]
    return jnp.moveaxis(full.reshape(shape), axis, 0)


def _as_pack_rows(a, cols):
    rows = -(-a.shape[0] // (8 * cols)) * 8
    return jnp.pad(a, (0, rows * cols - a.shape[0])).reshape(rows, cols)


def kernel(x, mem, mix_norm_g, w_in, w_pool, pool_scale, dn_conv_w, dn_a_log, dn_dt_bias, dn_norm_g, w_mix_out, xa_norm_g, mem_norm_g, w_xq, w_xkv, w_xo, ffn_norm_g, w_gate, w_up, ffn_conv_w, ffn_conv_b, w_down, final_norm_g, loss_target, m_mix_norm_g, m_w_in, m_w_pool, m_pool_scale, m_dn_conv_w, m_dn_a_log, m_dn_dt_bias, m_dn_norm_g, m_w_mix_out, m_xa_norm_g, m_mem_norm_g, m_w_xq, m_w_xkv, m_w_xo, m_ffn_norm_g, m_w_gate, m_w_up, m_ffn_conv_w, m_ffn_conv_b, m_w_down, m_final_norm_g, v_mix_norm_g, v_w_in, v_w_pool, v_pool_scale, v_dn_conv_w, v_dn_a_log, v_dn_dt_bias, v_dn_norm_g, v_w_mix_out, v_xa_norm_g, v_mem_norm_g, v_w_xq, v_w_xkv, v_w_xo, v_ffn_norm_g, v_w_gate, v_w_up, v_ffn_conv_w, v_ffn_conv_b, v_w_down, v_final_norm_g):
    given = dict(locals())
    w = {n: given[n] for n in WEIGHTS}
    mom = {n: given['m_' + n] for n in WEIGHTS}
    var = {n: given['v_' + n] for n in WEIGHTS}
    b, s, d = x.shape
    depth = w_in.shape[0]
    main = 5 * (d // 2)
    n_logits = w_in.shape[-1] * N_DEV - main
    core = jnp.reshape(lax.axis_index("c"), (1,)).astype(jnp.int32)

    names = MATRICES + TAPS

    def shards(l):
        return {n: w[n][l] if n in TAPS else w[n][l].astype(BF16) for n in names}

    def layer_weights(l, gathered):
        full = {n: _to_full(gathered[n], SHARD_AXIS[n] - 1) for n in names}
        p = {n: full[n] for n in names if n != 'w_in'}
        p['w_in_main'] = full['w_in'][:, :main]
        p['w_in_logits'] = jnp.pad(full['w_in'][:, main:], ((0, 0), (0, LANES - n_logits)))
        for n in PER_LAYER_REPLICATED:
            p[n] = w[n][l]
        return p

    mem2 = mem.reshape(-1, d)
    mem_hb = norm_fwd(mem2, mem_norm_g)
    xc, saved, layers = x.reshape(b * s, d), [], []
    gathered = dict(zip(names, all_gather([shards(0)[n] for n in names], "layer0")))
    for l in range(depth):
        layers.append(layer_weights(l, gathered))
        riders = Riders(gather_ride, FORWARD_HOSTS, shards(l + 1), f"layer{l + 1}") if l + 1 < depth else None
        xc, sv = forward_layer(xc, mem_hb, layers[l], b, s, riders)
        saved.append(sv)
        gathered = riders.results if riders else None
    loss, dx, dxb, g_final = loss_head(xc, loss_target.reshape(b * s, d), final_norm_g)

    def as_layers(a, cols):
        return a.reshape(depth, -1, cols)

    results = {n: None for n in names}

    def update(l, got):
        for n in names:
            cols = w[n].shape[-1]
            results[n] = adamw(got[n], as_layers(w[n], cols), as_layers(mom[n], cols), as_layers(var[n], cols),
                               layer=l, prev=results[n])

    def by_device(full, n):
        by_dev = _to_parts(full, SHARD_AXIS[n] - 1)
        return by_dev.reshape(N_DEV, -1, by_dev.shape[-1])

    def chip_sums_alone(group, parts, tag):
        from_sibling = sibling_swap_ride(parts, tag).alone()
        return {n: chip_sum(a, f, core) for n, a, f in zip(group, parts, from_sibling)}

    early_names = tuple(n for members in EARLY_HOSTS.values() for n in members)
    late_names = tuple(n for n in names if n not in early_names)

    def early_reduce(g):
        sums = chip_sums_alone(early_names, [by_device(g[n], n) for n in early_names], "layer0_early")
        return Riders(chip_exchange_ride, EARLY_HOSTS, sums, "layer0_early")

    g_layers, dmem_h, above = [None] * depth, None, None
    for l in reversed(range(depth)):
        dx, dxb, dm, g, reduced, early_got = backward_layer(dx, dxb, saved[l], mem_hb, layers[l], b, s, above,
                                                            early_reduce if l == 0 else None)
        if reduced:
            update(l + 1, reduced)
        g_layers[l] = g
        dmem_h = dm if dmem_h is None else dmem_h + dm
        g_full = {n: g[n] for n in names if n not in ('w_in', 'w_pool')}
        g_full['w_pool'] = g['w_pool'].astype(BF16)
        g_full['w_in'] = jnp.concatenate([g['w_in_main'], g['w_in_logits'][:, :n_logits]], axis=1)
        if l > 0:
            above = (names, [by_device(g_full[n], n) for n in names], core, f"layer{l}")
    late_sums = chip_sums_alone(late_names, [by_device(g_full[n], n) for n in late_names], "layer0")
    late_got = chip_exchange_ride([late_sums[n] for n in late_names], "layer0").alone()
    update(0, {**early_got, **dict(zip(late_names, late_got))})
    _, _, g_mem = norm_bwd(mem2, mem_norm_g, dmem_h, None)
    grad_x = dx.reshape(b, s, d)

    grad, delta, new_m, new_v = {}, {}, {}, {}
    for n in names:
        grad[n], delta[n], new_m[n], new_v[n] = [o.reshape(w[n].shape) for o in results[n]]

    g_small = {n: jnp.stack([g[n] for g in g_layers]) for n in PER_LAYER_REPLICATED}
    g_small['mem_norm_g'], g_small['final_norm_g'] = g_mem, g_final
    sizes = [w[n].size for n in REPLICATED]

    def pack(parts, first):
        flat = jnp.concatenate([jnp.reshape(first, (1,))] + [parts[n].reshape(-1) for n in REPLICATED])
        return _as_pack_rows(flat, LANES)

    zero = jnp.zeros((), F32)
    everyone, = all_gather([pack(g_small, loss)], "replicated")
    outs = adamw(everyone, pack(w, zero)[None], pack(mom, zero)[None], pack(var, zero)[None])
    flat_outs = [o.reshape(-1) for o in outs]
    loss_total = flat_outs[0][0]
    offset = 1
    for n, size in zip(REPLICATED, sizes):
        grad[n], delta[n], new_m[n], new_v[n] = [o[offset:offset + size].reshape(w[n].shape) for o in flat_outs]
        offset += size

    return (loss_total, grad_x, *[grad[n] for n in WEIGHTS], *[delta[n] for n in WEIGHTS],
            *[new_m[n] for n in WEIGHTS], *[new_v[n] for n in WEIGHTS])
```

```python
import functools

import jax
import jax.numpy as jnp
from jax import lax
from jax.experimental import pallas as pl
from jax.experimental.pallas import tpu as pltpu

F32 = jnp.float32
BF16 = jnp.bfloat16
MESH = pl.DeviceIdType.MESH

N_DEV = 8
EPS = 1e-6
POOL_WINDOWS = (2, 4, 8, 16)
DN_HEAD_DIM = 128
DN_CHUNK = 64
DN_TAPS = 4
XA_HEADS = 4
ADAM_LR = 0.001
ADAM_B1 = 0.9
ADAM_B2 = 0.999
ADAM_EPS = 1e-08
ADAM_WD = 0.01
ADAM_STEP = 10

LANES = 128
VMEM_LIMIT = 48 * 1024 * 1024
UT_ROWS = 256
FFN_COLS = 256

WEIGHTS = ['mix_norm_g', 'w_in', 'w_pool', 'pool_scale', 'dn_conv_w', 'dn_a_log', 'dn_dt_bias', 'dn_norm_g',
           'w_mix_out', 'xa_norm_g', 'mem_norm_g', 'w_xq', 'w_xkv', 'w_xo', 'ffn_norm_g', 'w_gate', 'w_up',
           'ffn_conv_w', 'ffn_conv_b', 'w_down', 'final_norm_g']
SHARD_AXIS = {'w_in': 2, 'w_pool': 2, 'dn_conv_w': 2, 'w_mix_out': 1, 'w_xq': 1, 'w_xkv': 2, 'w_xo': 1,
              'w_gate': 2, 'w_up': 2, 'ffn_conv_w': 2, 'w_down': 1}
MATRICES = ('w_in', 'w_pool', 'w_mix_out', 'w_xq', 'w_xkv', 'w_xo', 'w_gate', 'w_up', 'w_down')
TAPS = ('dn_conv_w', 'ffn_conv_w')
REPLICATED = [n for n in WEIGHTS if n not in SHARD_AXIS]
PER_LAYER_REPLICATED = ('mix_norm_g', 'pool_scale', 'dn_a_log', 'dn_dt_bias', 'dn_norm_g', 'xa_norm_g',
                        'ffn_norm_g', 'ffn_conv_b')


def _params(*semantics):
    return pltpu.CompilerParams(dimension_semantics=semantics, vmem_limit_bytes=VMEM_LIMIT)


def _tile(dim, pref, unit=LANES):
    if dim <= pref:
        return dim
    t = (pref // unit) * unit
    while t >= unit:
        if dim % t == 0:
            return t
        t -= unit
    return dim


class Ride:
    def __init__(self, tag, arrays, out_shapes, n_sems, n_local, start, finish):
        self.tag, self.arrays, self.out_shapes = tag, list(arrays), list(out_shapes)
        self.start, self.finish = start, finish
        self.scratch = [pltpu.SemaphoreType.DMA((n_sems,)), pltpu.SemaphoreType.DMA((n_sems,)),
                        pltpu.SemaphoreType.DMA((n_local,))]

    @property
    def specs_in(self):
        return [pl.BlockSpec(memory_space=pl.ANY)] * len(self.arrays)

    @property
    def specs_out(self):
        return [pl.BlockSpec(memory_space=pl.ANY)] * len(self.out_shapes)

    def alone(self):
        n = len(self.arrays)

        def body(*refs):
            ins, outs, sems = refs[:n], refs[n:n + len(self.out_shapes)], refs[n + len(self.out_shapes):]
            self.start(ins, outs, *sems)
            self.finish(ins, outs, *sems)

        return pl.pallas_call(body, name=self.tag, out_shape=self.out_shapes, in_specs=self.specs_in,
                              out_specs=self.specs_out, scratch_shapes=self.scratch)(*self.arrays)


def _first_and_last_step(grid):
    ids = [pl.program_id(axis) for axis in range(len(grid))]
    first = functools.reduce(jnp.logical_and, [i == 0 for i in ids])
    last = functools.reduce(jnp.logical_and, [i == n - 1 for i, n in zip(ids, grid)])
    return first, last


def matmul(a, b, mode, out_dtype, res=None, ride=None):
    assert a.dtype == BF16 and b.dtype == BF16, (a.dtype, b.dtype)
    if mode == 'nn':
        (m, c), (c2, n) = a.shape, b.shape
    elif mode == 'nt':
        (m, c), (n, c2) = a.shape, b.shape
    else:
        (c, m), (c2, n) = a.shape, b.shape
    assert c == c2, (mode, a.shape, b.shape)
    tm, tn = (_tile(m, 2048), _tile(n, 512)) if mode == 'nt' else (_tile(m, 1024), _tile(n, 1408))
    tc = _tile(c, 2048)
    nc = c // tc
    if mode == 'nn':
        a_spec = pl.BlockSpec((tm, tc), lambda i, j, k: (i, k))
        b_spec = pl.BlockSpec((tc, tn), lambda i, j, k: (k, j))
        dims = (((1,), (0,)), ((), ()))
    elif mode == 'nt':
        a_spec = pl.BlockSpec((tm, tc), lambda i, j, k: (i, k))
        b_spec = pl.BlockSpec((tn, tc), lambda i, j, k: (j, k))
        dims = (((1,), (1,)), ((), ()))
    else:
        a_spec = pl.BlockSpec((tc, tm), lambda i, j, k: (k, i))
        b_spec = pl.BlockSpec((tc, tn), lambda i, j, k: (k, j))
        dims = (((0,), (0,)), ((), ()))
    out_spec = pl.BlockSpec((tm, tn), lambda i, j, k: (i, j))
    has_res = res is not None
    grid = (m // tm, n // tn, nc)
    n_ride_in = len(ride.arrays) if ride else 0
    n_ride_out = len(ride.out_shapes) if ride else 0

    def body(a_ref, b_ref, *rest):
        rest = list(rest)
        r_ref = rest.pop(0) if has_res else None
        ride_in = [rest.pop(0) for _ in range(n_ride_in)]
        o_ref = rest.pop(0)
        ride_out = [rest.pop(0) for _ in range(n_ride_out)]
        acc_ref = rest.pop(0) if nc > 1 else None
        if ride:
            first, last = _first_and_last_step(grid)
            pl.when(first)(lambda: ride.start(ride_in, ride_out, *rest))
        prod = lax.dot_general(a_ref[...], b_ref[...], dims, preferred_element_type=F32)

        def finish(total):
            if has_res:
                total = total + r_ref[...]
            o_ref[...] = total.astype(o_ref.dtype)

        if nc == 1:
            finish(prod)
        else:
            k = pl.program_id(2)

            @pl.when(k == 0)
            def _():
                acc_ref[...] = prod

            @pl.when(k > 0)
            def _():
                acc_ref[...] += prod

            @pl.when(k == nc - 1)
            def _():
                finish(acc_ref[...])

        if ride:
            pl.when(last)(lambda: ride.finish(ride_in, ride_out, *rest))

    outs = pl.pallas_call(
        body,
        name=f"mm_{mode}_{m}x{c}x{n}" + ("_res" if has_res else "") + ("_with_" + ride.tag if ride else ""),
        grid=grid,
        in_specs=[a_spec, b_spec] + ([out_spec] if has_res else []) + (ride.specs_in if ride else []),
        out_specs=[out_spec] + (ride.specs_out if ride else []),
        out_shape=[jax.ShapeDtypeStruct((m, n), out_dtype)] + (ride.out_shapes if ride else []),
        scratch_shapes=([] if nc == 1 else [pltpu.VMEM((tm, tn), F32)]) + (ride.scratch if ride else []),
        compiler_params=(_params("arbitrary", "arbitrary", "arbitrary") if ride
                         else _params("parallel", "parallel", "arbitrary")),
    )(a, b, *([res] if has_res else []), *(ride.arrays if ride else []))
    return (outs[0], outs[1:]) if ride else outs[0]


def _position():
    return lax.axis_index("x"), lax.axis_index("y"), lax.axis_index("c")


def _flip(v, bit):
    return 1 - v if bit else v


def _dev_index(px, py, pc):
    return 4 * px + 2 * py + pc


def gather_ride(shards, tag):
    n = len(shards)

    def plan(x_refs, out_refs, send_sems, recv_sems, local_sems):
        x, y, c = _position()
        me = (x, y, c)

        def copy(t, k, block, to, own=False):
            slot = out_refs[t].at[_dev_index(*block)]
            return pltpu.make_async_remote_copy(
                src_ref=x_refs[t] if own else slot, dst_ref=slot, send_sem=send_sems.at[7 * t + k],
                recv_sem=recv_sems.at[7 * t + k], device_id=to, device_id_type=MESH)

        def local(t):
            return pltpu.make_async_copy(x_refs[t], out_refs[t].at[_dev_index(*me)], local_sems.at[t])

        return me, (x, y, 1 - c), [(1 - x, y), (x, 1 - y), (1 - x, 1 - y)], copy, local

    def start(x_refs, out_refs, *sems):
        me, sibling, chips, copy, local = plan(x_refs, out_refs, *sems)
        for t in range(n):
            local(t).start()
            copy(t, 0, me, sibling, own=True).start()
            for j, chip in enumerate(chips):
                copy(t, 1 + j, me, (*chip, me[2]), own=True).start()

    def finish(x_refs, out_refs, *sems):
        me, sibling, chips, copy, local = plan(x_refs, out_refs, *sems)
        c = me[2]
        for j, chip in enumerate(chips):
            for t in range(n):
                copy(t, 1 + j, (*chip, c), me).wait_recv()
                copy(t, 4 + j, (*chip, c), sibling).start()
        for t in range(n):
            copy(t, 0, sibling, me).wait_recv()
            for j, chip in enumerate(chips):
                copy(t, 4 + j, (*chip, 1 - c), me).wait_recv()
        for t in range(n):
            copy(t, 0, me, sibling, own=True).wait_send()
            for j, chip in enumerate(chips):
                copy(t, 1 + j, me, (*chip, c), own=True).wait_send()
                copy(t, 4 + j, (*chip, c), sibling).wait_send()
            local(t).wait()

    return Ride("all_gather_" + tag, shards, [jax.ShapeDtypeStruct((N_DEV,) + a.shape, a.dtype) for a in shards],
                7 * n, n, start, finish)


def all_gather(shards, tag):
    return gather_ride(shards, tag).alone()


def sibling_swap_ride(parts, tag):
    n = len(parts)

    def plan(p_refs, out_refs, send_sems, recv_sems, _):
        x, y, c = _position()
        return [pltpu.make_async_remote_copy(
            src_ref=p_refs[t].at[2 * k + (1 - c)], dst_ref=out_refs[t].at[k], send_sem=send_sems.at[4 * t + k],
            recv_sem=recv_sems.at[4 * t + k], device_id=(x, y, 1 - c), device_id_type=MESH)
            for t in range(n) for k in range(N_DEV // 2)]

    def start(p_refs, out_refs, *sems):
        for cp in plan(p_refs, out_refs, *sems):
            cp.start()

    def finish(p_refs, out_refs, *sems):
        for cp in plan(p_refs, out_refs, *sems):
            cp.wait()

    return Ride("sibling_swap_" + tag, parts,
                [jax.ShapeDtypeStruct((N_DEV // 2,) + a.shape[1:], a.dtype) for a in parts], 4 * n, 1, start, finish)


def chip_sum(parts, from_sibling, core):
    _, r, c = parts.shape
    unit = 16 if parts.dtype == BF16 else 8
    tr = _tile(r, max(unit, (512 * 1024 // c) // unit * unit), unit)

    def body(core_ref, a_ref, b_ref, o_ref):
        o_ref[...] = (a_ref[...].astype(F32) + b_ref[...].astype(F32)).astype(o_ref.dtype)

    blk = pl.BlockSpec((None, tr, c), lambda k, i, core_ref: (k, i, 0))
    return pl.pallas_call(
        body, name=f"chip_sum_{r}x{c}_{parts.dtype.name}",
        grid_spec=pltpu.PrefetchScalarGridSpec(
            num_scalar_prefetch=1, grid=(N_DEV // 2, r // tr),
            in_specs=[pl.BlockSpec((None, tr, c), lambda k, i, core_ref: (2 * k + core_ref[0], i, 0)), blk],
            out_specs=blk),
        out_shape=jax.ShapeDtypeStruct(from_sibling.shape, from_sibling.dtype),
        compiler_params=_params("parallel", "parallel"),
    )(core, parts, from_sibling)


def chip_exchange_ride(parts, tag):
    n = len(parts)

    def plan(p_refs, out_refs, send_sems, recv_sems, local_sems):
        x, y, c = _position()
        my_chip = 2 * x + y
        local = [pltpu.make_async_copy(p_refs[t].at[my_chip], out_refs[t].at[my_chip], local_sems.at[t])
                 for t in range(n)]
        copies = []
        for t in range(n):
            for k in (1, 2, 3):
                px, py = _flip(x, k & 2), _flip(y, k & 1)
                copies.append(pltpu.make_async_remote_copy(
                    src_ref=p_refs[t].at[2 * px + py], dst_ref=out_refs[t].at[my_chip],
                    send_sem=send_sems.at[3 * t + k - 1], recv_sem=recv_sems.at[3 * t + k - 1],
                    device_id=(px, py, c), device_id_type=MESH))
        return local, copies

    def start(p_refs, out_refs, *sems):
        local, copies = plan(p_refs, out_refs, *sems)
        for cp in local + copies:
            cp.start()

    def finish(p_refs, out_refs, *sems):
        local, copies = plan(p_refs, out_refs, *sems)
        for cp in copies:
            cp.wait_recv()
        for cp in copies:
            cp.wait_send()
        for cp in local:
            cp.wait()

    return Ride("chip_exchange_" + tag, parts, [jax.ShapeDtypeStruct(a.shape, a.dtype) for a in parts],
                3 * n, n, start, finish)


def adamw(parts, w, m, v, layer=0, prev=None):
    n_parts, r, c = parts.shape
    depth = w.shape[0]
    unit = 16 if parts.dtype == BF16 else 8
    tr = _tile(r, max(unit, (256 * 1024 // c) // unit * unit), unit)

    def body(p_ref, w_ref, m_ref, v_ref, *rest):
        g_ref, d_ref, nm_ref, nv_ref = rest[-4:]
        g = p_ref[0].astype(F32)
        for j in range(1, n_parts):
            g = g + p_ref[j].astype(F32)
        nm = ADAM_B1 * m_ref[...] + (1.0 - ADAM_B1) * g
        nv = ADAM_B2 * v_ref[...] + (1.0 - ADAM_B2) * jnp.square(g)
        m_hat = nm / (1.0 - ADAM_B1 ** ADAM_STEP)
        v_hat = nv / (1.0 - ADAM_B2 ** ADAM_STEP)
        g_ref[...] = g
        d_ref[...] = -ADAM_LR * (m_hat / (jnp.sqrt(v_hat) + ADAM_EPS) + ADAM_WD * w_ref[...])
        nm_ref[...] = nm
        nv_ref[...] = nv

    spec = pl.BlockSpec((None, tr, c), lambda i: (layer, i, 0))
    out = jax.ShapeDtypeStruct((depth, r, c), F32)
    carried = [] if prev is None else list(prev)
    return pl.pallas_call(
        body,
        name=f"adamw_{n_parts}x{r}x{c}_{parts.dtype.name}_layer{layer}",
        grid=(r // tr,),
        in_specs=[pl.BlockSpec((n_parts, tr, c), lambda i: (0, i, 0)), spec, spec, spec]
        + [pl.BlockSpec(memory_space=pl.ANY)] * len(carried),
        out_specs=[spec, spec, spec, spec],
        out_shape=[out, out, out, out],
        input_output_aliases={4 + k: k for k in range(len(carried))},
        compiler_params=_params("parallel"),
    )(parts, w, m, v, *carried)


_NN = (((1,), (0,)), ((), ()))
_NT = (((1,), (1,)), ((), ()))
_TN = (((0,), (0,)), ((), ()))


def _dot_bf16(a, b, dims):
    return lax.dot_general(a.astype(BF16), b.astype(BF16), dims, preferred_element_type=F32)


def _dot_split(a, b, dims):
    a_hi, b_hi = a.astype(BF16), b.astype(BF16)
    a_lo = (a - a_hi.astype(F32)).astype(BF16)
    b_lo = (b - b_hi.astype(F32)).astype(BF16)
    dot = functools.partial(lax.dot_general, dimension_numbers=dims, preferred_element_type=F32)
    return dot(a_hi, b_hi) + (dot(a_hi, b_lo) + dot(a_lo, b_hi))


def _matmul_family(dot):
    @jax.custom_vjp
    def nn(a, b):
        return dot(a, b, _NN)

    @jax.custom_vjp
    def nt(a, b):
        return dot(a, b, _NT)

    @jax.custom_vjp
    def tn(a, b):
        return dot(a, b, _TN)

    nn.defvjp(lambda a, b: (nn(a, b), (a, b)), lambda r, g: (nt(g, r[1]), tn(r[0], g)))
    nt.defvjp(lambda a, b: (nt(a, b), (a, b)), lambda r, g: (nn(g, r[1]), tn(g, r[0])))
    tn.defvjp(lambda a, b: (tn(a, b), (a, b)), lambda r, g: (nt(r[1], g), nn(r[0], g)))
    return nn, nt, tn


mm, mm_nt, mm_tn = _matmul_family(_dot_bf16)
mms, mms_nt, mms_tn = _matmul_family(_dot_split)


def _shift_rows(x, k, down):
    n = x.shape[0]
    rows = lax.broadcasted_iota(jnp.int32, x.shape, 0)
    if down:
        return jnp.where(rows >= k, pltpu.roll(x, k, 0), 0.0)
    return jnp.where(rows < n - k, pltpu.roll(x, n - k, 0), 0.0)


@functools.partial(jax.custom_vjp, nondiff_argnums=(1,))
def delay(x, k):
    return _shift_rows(x, k, True) if k else x


delay.defvjp(lambda x, k: (delay(x, k), None), lambda k, _, g: ((_shift_rows(g, k, False) if k else g),))


def _silu(x):
    return x * jax.nn.sigmoid(x)


def _rms(x, g):
    return x * lax.rsqrt(jnp.mean(x * x, axis=-1, keepdims=True) + EPS) * g


def _l2n(t):
    return t * lax.rsqrt(jnp.sum(t * t, axis=-1, keepdims=True) + EPS)


def _causal_conv(x, taps):
    k_taps = len(taps)
    y = delay(x, k_taps - 1) * taps[0]
    for k in range(1, k_taps):
        y = y + delay(x, k_taps - 1 - k) * taps[k]
    return y


def _pool_block(u, w, scale, group):
    sums, acc, width = [], u, 1
    while width < POOL_WINDOWS[-1]:
        acc = acc + delay(acc, width)
        width *= 2
        sums.append(acc)
    picked = sums[-1]
    for i in range(len(POOL_WINDOWS) - 2, -1, -1):
        picked = jnp.where(group == i, sums[i], picked)
    rows = lax.broadcasted_iota(jnp.int32, u.shape, 0)
    count = jnp.minimum(rows + 1, jnp.left_shift(2, group)).astype(F32)
    return mm(picked / count - u, w) * scale


def _unit_lower_inverse(l_mat):
    n = l_mat.shape[0]
    eye = (lax.broadcasted_iota(jnp.int32, (n, n), 0) == lax.broadcasted_iota(jnp.int32, (n, n), 1)).astype(F32)
    m1 = -l_mat
    m2 = mms(m1, m1)
    m4 = mms(m2, m2)
    m8 = mms(m4, m4)
    m16 = mms(m8, m8)
    m32 = mms(m16, m16)
    low = mms(eye + m1, eye + m2)
    mid = mms(eye + m4, eye + m8)
    high = mms(eye + m16, eye + m32)
    return mms(mms(low, mid), high)


@jax.custom_vjp
def _known_inverse(l_mat, inv):
    return inv


_known_inverse.defvjp(lambda l_mat, inv: (inv, inv),
                      lambda inv, g: (-mms_tn(inv, mms_nt(g, inv)), jnp.zeros_like(inv)))


def _ut_chain(k, v, beta, gc_c, gc_r, inv=None):
    r = k.shape[0]
    kn = _l2n(k)
    row = lax.broadcasted_iota(jnp.int32, (r, r), 0)
    col = lax.broadcasted_iota(jnp.int32, (r, r), 1)
    strict = (row // DN_CHUNK == col // DN_CHUNK) & (row > col)
    decay = jnp.exp(jnp.where(strict, gc_c - gc_r, -1e30))
    l_mat = jnp.where(strict, beta * mm_nt(kn, kn) * decay, 0.0)
    inv = _unit_lower_inverse(l_mat) if inv is None else _known_inverse(l_mat, inv)
    return mms(inv, v * beta), mms(inv, kn * (beta * jnp.exp(gc_c))), inv


def _rec_chain(q, k, gc_c, gc_r, u, w, state):
    ch, dh = q.shape
    qn = _l2n(q) * (dh ** -0.5)
    kn = _l2n(k)
    row = lax.broadcasted_iota(jnp.int32, (ch, ch), 0)
    col = lax.broadcasted_iota(jnp.int32, (ch, ch), 1)
    decay = jnp.exp(jnp.where(row >= col, gc_c - gc_r, -1e30))
    attn = mm_nt(qn, kn) * decay
    is_last = lax.broadcasted_iota(jnp.int32, gc_r.shape, 1) == ch - 1
    last = jnp.sum(jnp.where(is_last, gc_r, 0.0), axis=1, keepdims=True)
    v_new = u - mm(w, state)
    out = mm(qn * jnp.exp(gc_c), state) + mm(attn, v_new)
    return out, state * jnp.exp(last) + mm_tn(kn * jnp.exp(last - gc_c), v_new)


def _gated_norm(o, z, g):
    return _rms(o, g) * _silu(z)


def _attn_block(q, k, v):
    s = mm_nt(q, k) * (q.shape[-1] ** -0.5)
    p = jnp.exp(s - lax.stop_gradient(jnp.max(s, axis=-1, keepdims=True)))
    return mm(p / jnp.sum(p, axis=-1, keepdims=True), v)


def _ffn_block(gate_pre, up, taps, bias):
    return _silu(_causal_conv(gate_pre, taps) + bias) * up


def _accumulate(ref, value, first):
    @pl.when(first)
    def _():
        ref[...] = value

    @pl.when(jnp.logical_not(first))
    def _():
        ref[...] += value


def norm_fwd(x, g):
    t, d = x.shape
    tr = _tile(t, 256, 8)

    def body(x_ref, g_ref, h_ref):
        h_ref[...] = _rms(x_ref[...], g_ref[...]).astype(BF16)

    return pl.pallas_call(
        body, name=f"norm_fwd_{t}", grid=(t // tr,),
        in_specs=[pl.BlockSpec((tr, d), lambda i: (i, 0)), pl.BlockSpec((1, d), lambda i: (0, 0))],
        out_specs=pl.BlockSpec((tr, d), lambda i: (i, 0)),
        out_shape=jax.ShapeDtypeStruct((t, d), BF16), compiler_params=_params("parallel"),
    )(x, g.reshape(1, d))


def norm_bwd(x, g, dh, dres):
    t, d = x.shape
    tr = _tile(t, 256, 8)
    has_res = dres is not None

    def body(x_ref, g_ref, dh_ref, *rest):
        rest = list(rest)
        r_ref = rest.pop(0) if has_res else None
        dx_ref, dxb_ref, dg_ref = rest
        _, pull = jax.vjp(_rms, x_ref[...], g_ref[...])
        dx, dg = pull(dh_ref[...])
        if has_res:
            dx = dx + r_ref[...]
        dx_ref[...] = dx
        dxb_ref[...] = dx.astype(BF16)
        _accumulate(dg_ref, dg, pl.program_id(0) == 0)

    row = pl.BlockSpec((tr, d), lambda i: (i, 0))
    vec = pl.BlockSpec((1, d), lambda i: (0, 0))
    dx, dxb, dg = pl.pallas_call(
        body, name=f"norm_bwd_{t}" + ("_res" if has_res else ""), grid=(t // tr,),
        in_specs=[row, vec, row] + ([row] if has_res else []),
        out_specs=[row, row, vec],
        out_shape=[jax.ShapeDtypeStruct((t, d), F32), jax.ShapeDtypeStruct((t, d), BF16),
                   jax.ShapeDtypeStruct((1, d), F32)],
        compiler_params=_params("arbitrary"),
    )(x, g.reshape(1, d), dh, *([dres] if has_res else []))
    return dx, dxb, dg.reshape(d)


def loss_head(x, target, g):
    t, d = x.shape
    tr = _tile(t, 256, 8)

    def body(x_ref, t_ref, g_ref, l_ref, dx_ref, dxb_ref, dg_ref):
        tgt = t_ref[...]

        def block_loss(xv, gv):
            return 0.5 * jnp.sum(jnp.mean(jnp.square(_rms(xv, gv) - tgt), axis=-1))

        val, pull = jax.vjp(block_loss, x_ref[...], g_ref[...])
        dx, dg = pull(jnp.ones((), F32))
        dx_ref[...] = dx
        dxb_ref[...] = dx.astype(BF16)
        first = pl.program_id(0) == 0
        _accumulate(dg_ref, dg, first)
        _accumulate(l_ref, jnp.full((1, LANES), val, F32), first)

    row = pl.BlockSpec((tr, d), lambda i: (i, 0))
    vec = pl.BlockSpec((1, d), lambda i: (0, 0))
    loss, dx, dxb, dg = pl.pallas_call(
        body, name="loss_head", grid=(t // tr,),
        in_specs=[row, row, vec],
        out_specs=[pl.BlockSpec((1, LANES), lambda i: (0, 0)), row, row, vec],
        out_shape=[jax.ShapeDtypeStruct((1, LANES), F32), jax.ShapeDtypeStruct((t, d), F32),
                   jax.ShapeDtypeStruct((t, d), BF16), jax.ShapeDtypeStruct((1, d), F32)],
        compiler_params=_params("arbitrary"),
    )(x, target, g.reshape(1, d))
    return loss[0, 0], dx, dxb, dg.reshape(d)


def pool_fwd(proj, w_pool, scale, b, s):
    n_g, grp = w_pool.shape[0], w_pool.shape[-1]

    def body(u_ref, w_ref, s_ref, y_ref):
        y_ref[...] = _pool_block(u_ref[...], w_ref[...], s_ref[...], pl.program_id(1)).astype(BF16)

    blk = pl.BlockSpec((s, grp), lambda i, j: (i, j))
    return pl.pallas_call(
        body, name="pool_fwd", grid=(b, n_g),
        in_specs=[blk, pl.BlockSpec((None, grp, grp), lambda i, j: (j, 0, 0)),
                  pl.BlockSpec((1, grp), lambda i, j: (0, j))],
        out_specs=blk,
        out_shape=jax.ShapeDtypeStruct((b * s, n_g * grp), BF16), compiler_params=_params("parallel", "parallel"),
    )(proj, w_pool, scale.reshape(1, -1))


def pool_bwd(proj, w_pool, scale, dmixed, b, s):
    n_g, grp = w_pool.shape[0], w_pool.shape[-1]

    def body(u_ref, w_ref, s_ref, dy_ref, du_ref, dw_ref, ds_ref):
        group = pl.program_id(0)
        _, pull = jax.vjp(lambda u, w, sc: _pool_block(u, w, sc, group), u_ref[...], w_ref[...].astype(F32),
                          s_ref[...])
        du, dw, ds = pull(dy_ref[...])
        du_ref[...] = du.astype(BF16)
        first = pl.program_id(1) == 0
        _accumulate(dw_ref, dw, first)
        _accumulate(ds_ref, ds, first)

    blk = pl.BlockSpec((s, grp), lambda j, i: (i, j))
    w_spec = pl.BlockSpec((None, grp, grp), lambda j, i: (j, 0, 0))
    s_spec = pl.BlockSpec((1, grp), lambda j, i: (0, j))
    du, dw, ds = pl.pallas_call(
        body, name="pool_bwd", grid=(n_g, b),
        in_specs=[blk, w_spec, s_spec, blk],
        out_specs=[blk, w_spec, s_spec],
        out_shape=[jax.ShapeDtypeStruct((b * s, n_g * grp), BF16), jax.ShapeDtypeStruct(w_pool.shape, F32),
                   jax.ShapeDtypeStruct((1, n_g * grp), F32)],
        compiler_params=_params("arbitrary", "arbitrary"),
    )(proj, w_pool, scale.reshape(1, -1), dmixed)
    return du, dw, ds.reshape(-1)


def conv_silu_fwd(proj, conv_w, part, col0, width, b, s):
    x_off, w_off = (col0 + part * width) // FFN_COLS, part * width // FFN_COLS

    def body(x_ref, w_ref, y_ref):
        taps = [w_ref[k:k + 1, :] for k in range(DN_TAPS)]
        y_ref[...] = _silu(_causal_conv(x_ref[...], taps))

    return pl.pallas_call(
        body, name=f"conv_silu_fwd_{part}", grid=(width // FFN_COLS, b),
        in_specs=[pl.BlockSpec((s, FFN_COLS), lambda j, i: (i, x_off + j)),
                  pl.BlockSpec((DN_TAPS, FFN_COLS), lambda j, i: (0, w_off + j))],
        out_specs=pl.BlockSpec((s, FFN_COLS), lambda j, i: (i, j)),
        out_shape=jax.ShapeDtypeStruct((b * s, width), F32), compiler_params=_params("parallel", "parallel"),
    )(proj, conv_w)


def conv_silu_bwd(proj, conv_w, dact, part, col0, width, b, s):
    x_off, w_off = (col0 + part * width) // FFN_COLS, part * width // FFN_COLS

    def body(x_ref, w_ref, dy_ref, dx_ref, dw_ref):
        taps = [w_ref[k:k + 1, :] for k in range(DN_TAPS)]
        _, pull = jax.vjp(lambda x, *tp: _silu(_causal_conv(x, tp)), x_ref[...], *taps)
        dx, *dtaps = pull(dy_ref[...])
        dx_ref[...] = dx.astype(BF16)
        first = pl.program_id(1) == 0
        for k in range(DN_TAPS):
            _accumulate(dw_ref.at[k:k + 1, :], dtaps[k], first)

    out_blk = pl.BlockSpec((s, FFN_COLS), lambda j, i: (i, j))
    return pl.pallas_call(
        body, name=f"conv_silu_bwd_{part}", grid=(width // FFN_COLS, b),
        in_specs=[pl.BlockSpec((s, FFN_COLS), lambda j, i: (i, x_off + j)),
                  pl.BlockSpec((DN_TAPS, FFN_COLS), lambda j, i: (0, w_off + j)), out_blk],
        out_specs=[out_blk, pl.BlockSpec((DN_TAPS, FFN_COLS), lambda j, i: (0, j))],
        out_shape=[jax.ShapeDtypeStruct((b * s, width), BF16), jax.ShapeDtypeStruct((DN_TAPS, width), F32)],
        compiler_params=_params("arbitrary", "arbitrary"),
    )(proj, conv_w, dact)


def _ut_specs(b, s, heads, width):
    r = min(UT_ROWS, s)
    ns = s // r
    tok = pl.BlockSpec((r, width), lambda i, n: (i * ns + n, 0))
    col = pl.BlockSpec((None, heads, r, 1), lambda i, n: (i, 0, n, 0))
    row = pl.BlockSpec((None, heads, None, 1, r), lambda i, n: (i, 0, n, 0, 0))
    return r, ns, tok, col, row


def ut_fwd(act_k, act_v, beta_col, gc_col, gc_row, b, s, ride=None):
    width = act_k.shape[1]
    heads = width // DN_HEAD_DIM
    r, ns, tok, col, row = _ut_specs(b, s, heads, width)
    inv_spec = pl.BlockSpec((r, heads * r), lambda i, n: (i * ns + n, 0))

    n_ride_in = len(ride.arrays) if ride else 0

    def body(k_ref, v_ref, beta_ref, gcc_ref, gcr_ref, *rest):
        u_ref, w_ref, inv_ref = rest[n_ride_in:n_ride_in + 3]
        if ride:
            ride_in, ride_out, sems = rest[:n_ride_in], rest[n_ride_in + 3:-3], rest[-3:]
            first, last = _first_and_last_step((b, ns))
            pl.when(first)(lambda: ride.start(ride_in, ride_out, *sems))
        for h in range(heads):
            sl = slice(h * DN_HEAD_DIM, (h + 1) * DN_HEAD_DIM)
            u, w, inv = _ut_chain(k_ref[:, sl], v_ref[:, sl], beta_ref[h], gcc_ref[h], gcr_ref[h])
            u_ref[:, sl] = u
            w_ref[:, sl] = w
            inv_ref[:, h * r:(h + 1) * r] = inv
        if ride:
            pl.when(last)(lambda: ride.finish(ride_in, ride_out, *sems))

    out = jax.ShapeDtypeStruct((b * s, width), F32)
    outs = pl.pallas_call(
        body, name="ut_fwd" + ("_with_" + ride.tag if ride else ""), grid=(b, ns),
        in_specs=[tok, tok, col, col, row] + (ride.specs_in if ride else []),
        out_specs=[tok, tok, inv_spec] + (ride.specs_out if ride else []),
        out_shape=[out, out, jax.ShapeDtypeStruct((b * s, heads * r), F32)] + (ride.out_shapes if ride else []),
        scratch_shapes=ride.scratch if ride else [],
        compiler_params=_params("arbitrary", "arbitrary") if ride else _params("parallel", "parallel"),
    )(act_k, act_v, beta_col, gc_col, gc_row, *(ride.arrays if ride else []))
    return (*outs[:3], outs[3:]) if ride else outs


def ut_bwd(act_k, act_v, beta_col, gc_col, gc_row, inv, du, dw, dk_more, b, s):
    width = act_k.shape[1]
    heads = width // DN_HEAD_DIM
    r, ns, tok, col, row = _ut_specs(b, s, heads, width)
    inv_spec = pl.BlockSpec((r, heads * r), lambda i, n: (i * ns + n, 0))

    def body(k_ref, v_ref, beta_ref, gcc_ref, gcr_ref, inv_ref, du_ref, dw_ref, dkm_ref,
             dk_ref, dv_ref, dbeta_ref, dgcc_ref, dgcr_ref):
        for h in range(heads):
            sl = slice(h * DN_HEAD_DIM, (h + 1) * DN_HEAD_DIM)
            inv = inv_ref[:, h * r:(h + 1) * r]
            _, pull = jax.vjp(lambda *a: _ut_chain(*a, inv=inv)[:2], k_ref[:, sl], v_ref[:, sl], beta_ref[h],
                              gcc_ref[h], gcr_ref[h])
            dk, dv, dbeta, dgcc, dgcr = pull((du_ref[:, sl], dw_ref[:, sl]))
            dk_ref[:, sl] = dk + dkm_ref[:, sl]
            dv_ref[:, sl] = dv
            dbeta_ref[h] = dbeta
            dgcc_ref[h] = dgcc
            dgcr_ref[h] = dgcr

    out = jax.ShapeDtypeStruct((b * s, width), F32)
    return pl.pallas_call(
        body, name="ut_bwd", grid=(b, ns), in_specs=[tok, tok, col, col, row, inv_spec, tok, tok, tok],
        out_specs=[tok, tok, col, col, row],
        out_shape=[out, out, jax.ShapeDtypeStruct(beta_col.shape, F32), jax.ShapeDtypeStruct(gc_col.shape, F32),
                   jax.ShapeDtypeStruct(gc_row.shape, F32)],
        compiler_params=_params("parallel", "parallel"),
    )(act_k, act_v, beta_col, gc_col, gc_row, inv, du, dw, dk_more)


def _rec_specs(b, n, heads, width, chunk_of):
    ch = DN_CHUNK
    tok = pl.BlockSpec((b, None, ch, width), lambda i: (0, chunk_of(i), 0, 0))
    col = pl.BlockSpec((b, heads, None, ch, 1), lambda i: (0, 0, chunk_of(i), 0, 0))
    row = pl.BlockSpec((b, heads, None, 1, ch), lambda i: (0, 0, chunk_of(i), 0, 0))
    st = pl.BlockSpec((None, b, heads, DN_HEAD_DIM, DN_HEAD_DIM), lambda i: (chunk_of(i), 0, 0, 0, 0))
    return tok, col, row, st


def rec_fwd(act_q, act_k, gc_col, gc_row, u, w, b, s):
    width = act_q.shape[1]
    heads, n = width // DN_HEAD_DIM, s // DN_CHUNK
    tok, col, row, st = _rec_specs(b, n, heads, width, lambda i: i)
    shape4 = (b, n, DN_CHUNK, width)

    def body(q_ref, k_ref, gcc_ref, gcr_ref, u_ref, w_ref, o_ref, st_ref, state):
        @pl.when(pl.program_id(0) == 0)
        def _():
            state[...] = jnp.zeros_like(state)

        for i in range(b):
            for h in range(heads):
                sl = slice(h * DN_HEAD_DIM, (h + 1) * DN_HEAD_DIM)
                s_in = state[i, h]
                st_ref[i, h] = s_in
                o, s_out = _rec_chain(q_ref[i, :, sl], k_ref[i, :, sl], gcc_ref[i, h], gcr_ref[i, h],
                                      u_ref[i, :, sl], w_ref[i, :, sl], s_in)
                o_ref[i, :, sl] = o
                state[i, h] = s_out

    o, states = pl.pallas_call(
        body, name="rec_fwd", grid=(n,), in_specs=[tok, tok, col, row, tok, tok], out_specs=[tok, st],
        out_shape=[jax.ShapeDtypeStruct(shape4, F32),
                   jax.ShapeDtypeStruct((n, b, heads, DN_HEAD_DIM, DN_HEAD_DIM), F32)],
        scratch_shapes=[pltpu.VMEM((b, heads, DN_HEAD_DIM, DN_HEAD_DIM), F32)],
        compiler_params=_params("arbitrary"),
    )(act_q.reshape(shape4), act_k.reshape(shape4), gc_col.reshape(b, heads, n, DN_CHUNK, 1), gc_row,
      u.reshape(shape4), w.reshape(shape4))
    return o.reshape(b * s, width), states


def rec_bwd(act_q, act_k, gc_col, gc_row, u, w, states, do, b, s):
    width = act_q.shape[1]
    heads, n = width // DN_HEAD_DIM, s // DN_CHUNK
    tok, col, row, st = _rec_specs(b, n, heads, width, lambda i: n - 1 - i)
    shape4 = (b, n, DN_CHUNK, width)

    def body(q_ref, k_ref, gcc_ref, gcr_ref, u_ref, w_ref, st_ref, do_ref,
             dq_ref, dk_ref, du_ref, dw_ref, dgcc_ref, dgcr_ref, dstate):
        @pl.when(pl.program_id(0) == 0)
        def _():
            dstate[...] = jnp.zeros_like(dstate)

        for i in range(b):
            for h in range(heads):
                sl = slice(h * DN_HEAD_DIM, (h + 1) * DN_HEAD_DIM)
                _, pull = jax.vjp(_rec_chain, q_ref[i, :, sl], k_ref[i, :, sl], gcc_ref[i, h], gcr_ref[i, h],
                                  u_ref[i, :, sl], w_ref[i, :, sl], st_ref[i, h])
                dq, dk, dgcc, dgcr, du, dw, ds = pull((do_ref[i, :, sl], dstate[i, h]))
                dq_ref[i, :, sl] = dq
                dk_ref[i, :, sl] = dk
                du_ref[i, :, sl] = du
                dw_ref[i, :, sl] = dw
                dgcc_ref[i, h] = dgcc
                dgcr_ref[i, h] = dgcr
                dstate[i, h] = ds

    tok_out = jax.ShapeDtypeStruct(shape4, F32)
    dq, dk, du, dw, dgcc, dgcr = pl.pallas_call(
        body, name="rec_bwd", grid=(n,), in_specs=[tok, tok, col, row, tok, tok, st, tok],
        out_specs=[tok, tok, tok, tok, col, row],
        out_shape=[tok_out, tok_out, tok_out, tok_out,
                   jax.ShapeDtypeStruct((b, heads, n, DN_CHUNK, 1), F32), jax.ShapeDtypeStruct(gc_row.shape, F32)],
        scratch_shapes=[pltpu.VMEM((b, heads, DN_HEAD_DIM, DN_HEAD_DIM), F32)],
        compiler_params=_params("arbitrary"),
    )(act_q.reshape(shape4), act_k.reshape(shape4), gc_col.reshape(b, heads, n, DN_CHUNK, 1), gc_row,
      u.reshape(shape4), w.reshape(shape4), states, do.reshape(shape4))
    flat = lambda a: a.reshape(b * s, width)
    return flat(dq), flat(dk), flat(du), flat(dw), dgcc.reshape(b, heads, s, 1), dgcr


def dn_norm_fwd(o, proj, g, z_col0):
    t, width = o.shape
    heads = width // DN_HEAD_DIM
    tr = _tile(t, 512, 8)
    z_off = z_col0 // width

    def body(o_ref, z_ref, g_ref, y_ref):
        for h in range(heads):
            sl = slice(h * DN_HEAD_DIM, (h + 1) * DN_HEAD_DIM)
            y_ref[:, sl] = _gated_norm(o_ref[:, sl], z_ref[:, sl], g_ref[...]).astype(BF16)

    blk = pl.BlockSpec((tr, width), lambda i: (i, 0))
    return pl.pallas_call(
        body, name="dn_norm_fwd", grid=(t // tr,),
        in_specs=[blk, pl.BlockSpec((tr, width), lambda i: (i, z_off)),
                  pl.BlockSpec((1, DN_HEAD_DIM), lambda i: (0, 0))],
        out_specs=blk, out_shape=jax.ShapeDtypeStruct((t, width), BF16),
        compiler_params=_params("parallel"),
    )(o, proj, g.reshape(1, -1))


def dn_norm_bwd(o, proj, g, dmixed, z_col0, dy_col0):
    t, width = o.shape
    heads = width // DN_HEAD_DIM
    tr = _tile(t, 512, 8)
    z_off, dy_off = z_col0 // width, dy_col0 // width

    def body(o_ref, z_ref, g_ref, dy_ref, do_ref, dz_ref, dg_ref):
        dg_sum = None
        for h in range(heads):
            sl = slice(h * DN_HEAD_DIM, (h + 1) * DN_HEAD_DIM)
            _, pull = jax.vjp(_gated_norm, o_ref[:, sl], z_ref[:, sl], g_ref[...])
            do, dz, dg = pull(dy_ref[:, sl])
            do_ref[:, sl] = do
            dz_ref[:, sl] = dz.astype(BF16)
            dg_sum = dg if dg_sum is None else dg_sum + dg
        _accumulate(dg_ref, dg_sum, pl.program_id(0) == 0)

    blk = pl.BlockSpec((tr, width), lambda i: (i, 0))
    vec = pl.BlockSpec((1, DN_HEAD_DIM), lambda i: (0, 0))
    do, dz, dg = pl.pallas_call(
        body, name="dn_norm_bwd", grid=(t // tr,),
        in_specs=[blk, pl.BlockSpec((tr, width), lambda i: (i, z_off)), vec,
                  pl.BlockSpec((tr, width), lambda i: (i, dy_off))],
        out_specs=[blk, blk, vec],
        out_shape=[jax.ShapeDtypeStruct((t, width), F32), jax.ShapeDtypeStruct((t, width), BF16),
                   jax.ShapeDtypeStruct((1, DN_HEAD_DIM), F32)],
        compiler_params=_params("arbitrary"),
    )(o, proj, g.reshape(1, -1), dmixed)
    return do, dz, dg.reshape(-1)


def _attn_specs(b, s, mem_len, d):
    hd = d // XA_HEADS
    tq = _tile(s, 1024, 8)
    nq = s // tq
    q_spec = pl.BlockSpec((tq, hd), lambda i, h, j: (i * nq + j, h))
    k_spec = pl.BlockSpec((mem_len, hd), lambda i, h, j: (i, h))
    v_spec = pl.BlockSpec((mem_len, hd), lambda i, h, j: (i, XA_HEADS + h))
    return nq, q_spec, k_spec, v_spec


def attn_fwd(q, kv, b, s):
    d = q.shape[1]
    nq, q_spec, k_spec, v_spec = _attn_specs(b, s, kv.shape[0] // b, d)

    def body(q_ref, k_ref, v_ref, o_ref):
        o_ref[...] = _attn_block(q_ref[...], k_ref[...], v_ref[...]).astype(BF16)

    return pl.pallas_call(
        body, name="attn_fwd", grid=(b, XA_HEADS, nq), in_specs=[q_spec, k_spec, v_spec], out_specs=q_spec,
        out_shape=jax.ShapeDtypeStruct(q.shape, BF16), compiler_params=_params("parallel", "parallel", "parallel"),
    )(q, kv, kv)


def attn_bwd(q, kv, do, b, s):
    d = q.shape[1]
    rows = kv.shape[0]
    nq, q_spec, k_spec, v_spec = _attn_specs(b, s, rows // b, d)

    def body(q_ref, k_ref, v_ref, do_ref, dq_ref, dk_ref, dv_ref):
        _, pull = jax.vjp(_attn_block, q_ref[...], k_ref[...], v_ref[...])
        dq, dk, dv = pull(do_ref[...])
        dq_ref[...] = dq.astype(BF16)
        first = pl.program_id(2) == 0
        _accumulate(dk_ref, dk, first)
        _accumulate(dv_ref, dv, first)

    kv_out = jax.ShapeDtypeStruct((rows, d), F32)
    return pl.pallas_call(
        body, name="attn_bwd", grid=(b, XA_HEADS, nq), in_specs=[q_spec, k_spec, v_spec, q_spec],
        out_specs=[q_spec, k_spec, k_spec], out_shape=[jax.ShapeDtypeStruct(q.shape, BF16), kv_out, kv_out],
        compiler_params=_params("parallel", "parallel", "arbitrary"),
    )(q, kv, kv, do)


def ffn_fwd(gate_pre, up, conv_w, conv_b, b, s):
    taps_n, f = conv_w.shape

    def body(g_ref, u_ref, w_ref, b_ref, y_ref):
        taps = [w_ref[k:k + 1, :] for k in range(taps_n)]
        y_ref[...] = _ffn_block(g_ref[...], u_ref[...], taps, b_ref[...]).astype(BF16)

    blk = pl.BlockSpec((s, FFN_COLS), lambda j, i: (i, j))
    return pl.pallas_call(
        body, name="ffn_fwd", grid=(f // FFN_COLS, b),
        in_specs=[blk, blk, pl.BlockSpec((taps_n, FFN_COLS), lambda j, i: (0, j)),
                  pl.BlockSpec((1, FFN_COLS), lambda j, i: (0, j))],
        out_specs=blk,
        out_shape=jax.ShapeDtypeStruct((b * s, f), BF16), compiler_params=_params("parallel", "parallel"),
    )(gate_pre, up, conv_w, conv_b.reshape(1, f))


def ffn_bwd(gate_pre, up, conv_w, conv_b, dact, b, s):
    taps_n, f = conv_w.shape

    def body(g_ref, u_ref, w_ref, b_ref, dy_ref, dg_ref, du_ref, dw_ref, db_ref):
        taps = [w_ref[k:k + 1, :] for k in range(taps_n)]
        _, pull = jax.vjp(lambda gt, up_, bias, *tp: _ffn_block(gt, up_, tp, bias), g_ref[...], u_ref[...],
                          b_ref[...], *taps)
        dgate, dup, dbias, *dtaps = pull(dy_ref[...])
        dg_ref[...] = dgate.astype(BF16)
        du_ref[...] = dup.astype(BF16)
        first = pl.program_id(1) == 0
        _accumulate(db_ref, dbias, first)
        for k in range(taps_n):
            _accumulate(dw_ref.at[k:k + 1, :], dtaps[k], first)

    blk = pl.BlockSpec((s, FFN_COLS), lambda j, i: (i, j))
    w_spec = pl.BlockSpec((taps_n, FFN_COLS), lambda j, i: (0, j))
    b_spec = pl.BlockSpec((1, FFN_COLS), lambda j, i: (0, j))
    half = jax.ShapeDtypeStruct(gate_pre.shape, BF16)
    dgate, dup, dw, db = pl.pallas_call(
        body, name="ffn_bwd", grid=(f // FFN_COLS, b),
        in_specs=[blk, blk, w_spec, b_spec, blk],
        out_specs=[blk, blk, w_spec, b_spec],
        out_shape=[half, half, jax.ShapeDtypeStruct((taps_n, f), F32), jax.ShapeDtypeStruct((1, f), F32)],
        compiler_params=_params("arbitrary", "arbitrary"),
    )(gate_pre, up, conv_w, conv_b.reshape(1, f), dact)
    return dgate, dup, dw, db.reshape(f)


def gate_arrays(logits, a_log, dt_bias, b, s, heads):
    n, r = s // DN_CHUNK, min(UT_ROWS, s)
    lg = logits.reshape(b, s, -1)
    beta = jax.nn.sigmoid(lg[..., :heads])
    g = -jnp.exp(a_log) * jax.nn.softplus(lg[..., heads:2 * heads] + dt_bias)
    gc = jnp.cumsum(g.reshape(b, n, DN_CHUNK, heads), axis=2).transpose(0, 3, 1, 2)
    return (beta.transpose(0, 2, 1)[..., None], gc.reshape(b, heads, s, 1), gc[:, :, :, None, :],
            gc.reshape(b, heads, s // r, 1, r))


class Riders:
    def __init__(self, make_ride, groups, arrays, tag):
        self.groups, self.results = groups, {}
        self.rides = {host: make_ride([arrays[n] for n in members], f"{tag}_on_{host}")
                      for host, members in groups.items()}

    def run(self, host, call):
        value, outs = call(self.rides[host])
        self.results.update(zip(self.groups[host], outs))
        return value


def _hosted_matmul(riders, host, *args, **kwargs):
    if riders is None:
        return matmul(*args, **kwargs)
    return riders.run(host, lambda ride: matmul(*args, ride=ride, **kwargs))


FORWARD_HOSTS = {'in_proj': ('w_in',), 'ut': ('w_gate', 'w_up'), 'gate': ('w_xkv',),
                 'up': ('w_mix_out', 'w_xq', 'w_xo', 'w_pool', 'dn_conv_w', 'ffn_conv_w'), 'down': ('w_down',)}
BACKWARD_HOSTS = {'down_wgrad': ('w_down',), 'gate_dgrad': ('w_xkv',), 'up_dgrad': ('w_in',),
                  'gate_wgrad': ('w_gate',), 'up_wgrad': ('w_up',),
                  'in_dgrad': ('w_mix_out', 'w_xq', 'w_xo', 'w_pool', 'dn_conv_w', 'ffn_conv_w')}
EARLY_HOSTS = {'in_wgrad': ('w_down',), 'mix_dgrad': ('w_xo',), 'mix_wgrad': ('w_xq', 'ffn_conv_w')}


def forward_layer(x, mem_hb, p, b, s, riders=None):
    d = x.shape[1]
    pw = d // 2
    dn = d - pw
    heads = dn // DN_HEAD_DIM
    sv = {'x0': x}
    sv['h1'] = h1 = norm_fwd(x, p['mix_norm_g'])
    sv['proj'] = proj = _hosted_matmul(riders, 'in_proj', h1, p['w_in_main'], 'nn', F32)
    logits = matmul(h1, p['w_in_logits'], 'nn', F32)
    y_pool = pool_fwd(proj, p['w_pool'], p['pool_scale'], b, s)
    gates, sv['gates_pull'] = jax.vjp(lambda lg, al, dtb: gate_arrays(lg, al, dtb, b, s, heads), logits,
                                      p['dn_a_log'], p['dn_dt_bias'])
    sv['gates'] = beta_col, gc_col, gc_row, gc_row_ut = gates
    sv['act'] = aq, ak, av = [conv_silu_fwd(proj, p['dn_conv_w'], part, pw, dn, b, s) for part in range(3)]
    if riders is None:
        u, w, sv['inv'] = ut_fwd(ak, av, beta_col, gc_col, gc_row_ut, b, s)
    else:
        def hosted(ride):
            *own, outs = ut_fwd(ak, av, beta_col, gc_col, gc_row_ut, b, s, ride=ride)
            return own, outs
        u, w, sv['inv'] = riders.run('ut', hosted)
    sv['u'], sv['w'] = u, w
    sv['o_dn'], sv['states'] = o_dn, _ = rec_fwd(aq, ak, gc_col, gc_row, u, w, b, s)
    y_dn = dn_norm_fwd(o_dn, proj, p['dn_norm_g'], pw + 3 * dn)
    sv['mixed'] = mixed = jnp.concatenate([y_pool, y_dn], axis=1)
    sv['x1'] = x1 = matmul(mixed, p['w_mix_out'], 'nn', F32, res=x)

    sv['h2'] = h2 = norm_fwd(x1, p['xa_norm_g'])
    sv['q'] = q = matmul(h2, p['w_xq'], 'nn', F32)
    sv['kv'] = kv = matmul(mem_hb, p['w_xkv'], 'nn', F32)
    sv['o_at'] = o_at = attn_fwd(q, kv, b, s)
    sv['x2'] = x2 = matmul(o_at, p['w_xo'], 'nn', F32, res=x1)

    sv['h3'] = h3 = norm_fwd(x2, p['ffn_norm_g'])
    sv['gate_pre'] = gate_pre = _hosted_matmul(riders, 'gate', h3, p['w_gate'], 'nn', F32)
    sv['up'] = up = _hosted_matmul(riders, 'up', h3, p['w_up'], 'nn', F32)
    sv['a_ffn'] = a_ffn = ffn_fwd(gate_pre, up, p['ffn_conv_w'], p['ffn_conv_b'], b, s)
    return _hosted_matmul(riders, 'down', a_ffn, p['w_down'], 'nn', F32, res=x2), sv


def backward_layer(dx, dxb, sv, mem_hb, p, b, s, above=None, early=None):
    d = dx.shape[1]
    pw = d // 2
    dn = d - pw
    g = {}
    if above is None:
        riders = None
        da = matmul(dxb, p['w_down'], 'nt', F32)
    else:
        names, parts, core, tag = above
        da, from_sibling = matmul(dxb, p['w_down'], 'nt', F32, ride=sibling_swap_ride(parts, tag))
        chip_sums = {n: chip_sum(a, f, core) for n, a, f in zip(names, parts, from_sibling)}
        riders = Riders(chip_exchange_ride, BACKWARD_HOSTS, chip_sums, tag)
    g['w_down'] = _hosted_matmul(riders, 'down_wgrad', sv['a_ffn'], dxb, 'tn', BF16)
    dgate, dup, g['ffn_conv_w'], g['ffn_conv_b'] = ffn_bwd(sv['gate_pre'], sv['up'], p['ffn_conv_w'],
                                                           p['ffn_conv_b'], da, b, s)
    dh = _hosted_matmul(riders, 'gate_dgrad', dgate, p['w_gate'], 'nt', F32)
    dh = _hosted_matmul(riders, 'up_dgrad', dup, p['w_up'], 'nt', F32, res=dh)
    g['w_gate'] = _hosted_matmul(riders, 'gate_wgrad', sv['h3'], dgate, 'tn', BF16)
    g['w_up'] = _hosted_matmul(riders, 'up_wgrad', sv['h3'], dup, 'tn', BF16)
    dx, dxb, g['ffn_norm_g'] = norm_bwd(sv['x2'], p['ffn_norm_g'], dh, dx)

    do = matmul(dxb, p['w_xo'], 'nt', F32)
    g['w_xo'] = matmul(sv['o_at'], dxb, 'tn', BF16)
    dq, dk, dv = attn_bwd(sv['q'], sv['kv'], do, b, s)
    dkv = jnp.concatenate([dk, dv], axis=1).astype(BF16)
    dh = matmul(dq, p['w_xq'], 'nt', F32)
    g['w_xq'] = matmul(sv['h2'], dq, 'tn', BF16)
    g['w_xkv'] = matmul(mem_hb, dkv, 'tn', BF16)
    dmem_h = matmul(dkv, p['w_xkv'], 'nt', F32)
    dx, dxb, g['xa_norm_g'] = norm_bwd(sv['x1'], p['xa_norm_g'], dh, dx)

    early_riders = early(g) if early else None
    dmixed = _hosted_matmul(early_riders, 'mix_dgrad', dxb, p['w_mix_out'], 'nt', F32)
    g['w_mix_out'] = _hosted_matmul(early_riders, 'mix_wgrad', sv['mixed'], dxb, 'tn', BF16)
    proj = sv['proj']
    du_pool, g['w_pool'], g['pool_scale'] = pool_bwd(proj, p['w_pool'], p['pool_scale'], dmixed, b, s)
    do_dn, dz, g['dn_norm_g'] = dn_norm_bwd(sv['o_dn'], proj, p['dn_norm_g'], dmixed, pw + 3 * dn, pw)
    beta_col, gc_col, gc_row, gc_row_ut = sv['gates']
    aq, ak, av = sv['act']
    daq, dak_rec, du, dw, dgcc_rec, dgcr = rec_bwd(aq, ak, gc_col, gc_row, sv['u'], sv['w'], sv['states'],
                                                   do_dn, b, s)
    dak, dav, dbeta, dgcc_ut, dgcr_ut = ut_bwd(ak, av, beta_col, gc_col, gc_row_ut, sv['inv'], du, dw, dak_rec, b, s)
    dparts, dtaps = zip(*[conv_silu_bwd(proj, p['dn_conv_w'], dact, part, pw, dn, b, s)
                          for part, dact in enumerate((daq, dak, dav))])
    g['dn_conv_w'] = jnp.concatenate(dtaps, axis=1)
    dlogits, g['dn_a_log'], g['dn_dt_bias'] = sv['gates_pull']((dbeta, dgcc_rec + dgcc_ut, dgcr, dgcr_ut))
    dproj = jnp.concatenate([du_pool, *dparts, dz], axis=1)
    dlogits = dlogits.astype(BF16)
    dh = _hosted_matmul(riders, 'in_dgrad', dproj, p['w_in_main'], 'nt', F32,
                        res=matmul(dlogits, p['w_in_logits'], 'nt', F32))
    g['w_in_main'] = _hosted_matmul(early_riders, 'in_wgrad', sv['h1'], dproj, 'tn', BF16)
    g['w_in_logits'] = matmul(sv['h1'], dlogits, 'tn', BF16)
    dx, dxb, g['mix_norm_g'] = norm_bwd(sv['x0'], p['mix_norm_g'], dh, dx)
    return (dx, dxb, dmem_h, g, (riders.results if riders else None),
            (early_riders.results if early_riders else None))


def _to_full(gathered, axis):
    moved = jnp.moveaxis(gathered, 0, axis)
    shape = list(gathered.shape[1:])
    shape[axis] *= N_DEV
    return moved.reshape(shape)


def _to_parts(full, axis):
    shape = list(full.shape)
    shape[axis:axis + 1] = [N_DEV, shape[axis] // N_DEV]
    return jnp.moveaxis(full.reshape(shape), axis, 0)


def _as_pack_rows(a, cols):
    rows = -(-a.shape[0] // (8 * cols)) * 8
    return jnp.pad(a, (0, rows * cols - a.shape[0])).reshape(rows, cols)


def kernel(x, mem, mix_norm_g, w_in, w_pool, pool_scale, dn_conv_w, dn_a_log, dn_dt_bias, dn_norm_g, w_mix_out, xa_norm_g, mem_norm_g, w_xq, w_xkv, w_xo, ffn_norm_g, w_gate, w_up, ffn_conv_w, ffn_conv_b, w_down, final_norm_g, loss_target, m_mix_norm_g, m_w_in, m_w_pool, m_pool_scale, m_dn_conv_w, m_dn_a_log, m_dn_dt_bias, m_dn_norm_g, m_w_mix_out, m_xa_norm_g, m_mem_norm_g, m_w_xq, m_w_xkv, m_w_xo, m_ffn_norm_g, m_w_gate, m_w_up, m_ffn_conv_w, m_ffn_conv_b, m_w_down, m_final_norm_g, v_mix_norm_g, v_w_in, v_w_pool, v_pool_scale, v_dn_conv_w, v_dn_a_log, v_dn_dt_bias, v_dn_norm_g, v_w_mix_out, v_xa_norm_g, v_mem_norm_g, v_w_xq, v_w_xkv, v_w_xo, v_ffn_norm_g, v_w_gate, v_w_up, v_ffn_conv_w, v_ffn_conv_b, v_w_down, v_final_norm_g):
    given = dict(locals())
    w = {n: given[n] for n in WEIGHTS}
    mom = {n: given['m_' + n] for n in WEIGHTS}
    var = {n: given['v_' + n] for n in WEIGHTS}
    b, s, d = x.shape
    depth = w_in.shape[0]
    main = 5 * (d // 2)
    n_logits = w_in.shape[-1] * N_DEV - main
    core = jnp.reshape(lax.axis_index("c"), (1,)).astype(jnp.int32)

    names = MATRICES + TAPS

    def shards(l):
        return {n: w[n][l] if n in TAPS else w[n][l].astype(BF16) for n in names}

    def layer_weights(l, gathered):
        full = {n: _to_full(gathered[n], SHARD_AXIS[n] - 1) for n in names}
        p = {n: full[n] for n in names if n != 'w_in'}
        p['w_in_main'] = full['w_in'][:, :main]
        p['w_in_logits'] = jnp.pad(full['w_in'][:, main:], ((0, 0), (0, LANES - n_logits)))
        for n in PER_LAYER_REPLICATED:
            p[n] = w[n][l]
        return p

    mem2 = mem.reshape(-1, d)
    mem_hb = norm_fwd(mem2, mem_norm_g)
    xc, saved, layers = x.reshape(b * s, d), [], []
    gathered = dict(zip(names, all_gather([shards(0)[n] for n in names], "layer0")))
    for l in range(depth):
        layers.append(layer_weights(l, gathered))
        riders = Riders(gather_ride, FORWARD_HOSTS, shards(l + 1), f"layer{l + 1}") if l + 1 < depth else None
        xc, sv = forward_layer(xc, mem_hb, layers[l], b, s, riders)
        saved.append(sv)
        gathered = riders.results if riders else None
    loss, dx, dxb, g_final = loss_head(xc, loss_target.reshape(b * s, d), final_norm_g)

    def as_layers(a, cols):
        return a.reshape(depth, -1, cols)

    results = {n: None for n in names}

    def update(l, got):
        for n in names:
            cols = w[n].shape[-1]
            results[n] = adamw(got[n], as_layers(w[n], cols), as_layers(mom[n], cols), as_layers(var[n], cols),
                               layer=l, prev=results[n])

    def by_device(full, n):
        by_dev = _to_parts(full, SHARD_AXIS[n] - 1)
        return by_dev.reshape(N_DEV, -1, by_dev.shape[-1])

    def chip_sums_alone(group, parts, tag):
        from_sibling = sibling_swap_ride(parts, tag).alone()
        return {n: chip_sum(a, f, core) for n, a, f in zip(group, parts, from_sibling)}

    early_names = tuple(n for members in EARLY_HOSTS.values() for n in members)
    late_names = tuple(n for n in names if n not in early_names)

    def early_reduce(g):
        sums = chip_sums_alone(early_names, [by_device(g[n], n) for n in early_names], "layer0_early")
        return Riders(chip_exchange_ride, EARLY_HOSTS, sums, "layer0_early")

    g_layers, dmem_h, above = [None] * depth, None, None
    for l in reversed(range(depth)):
        dx, dxb, dm, g, reduced, early_got = backward_layer(dx, dxb, saved[l], mem_hb, layers[l], b, s, above,
                                                            early_reduce if l == 0 else None)
        if reduced:
            update(l + 1, reduced)
        g_layers[l] = g
        dmem_h = dm if dmem_h is None else dmem_h + dm
        g_full = {n: g[n] for n in names if n not in ('w_in', 'w_pool')}
        g_full['w_pool'] = g['w_pool'].astype(BF16)
        g_full['w_in'] = jnp.concatenate([g['w_in_main'], g['w_in_logits'][:, :n_logits]], axis=1)
        if l > 0:
            above = (names, [by_device(g_full[n], n) for n in names], core, f"layer{l}")
    late_sums = chip_sums_alone(late_names, [by_device(g_full[n], n) for n in late_names], "layer0")
    late_got = chip_exchange_ride([late_sums[n] for n in late_names], "layer0").alone()
    update(0, {**early_got, **dict(zip(late_names, late_got))})
    _, _, g_mem = norm_bwd(mem2, mem_norm_g, dmem_h, None)
    grad_x = dx.reshape(b, s, d)

    grad, delta, new_m, new_v = {}, {}, {}, {}
    for n in names:
        grad[n], delta[n], new_m[n], new_v[n] = [o.reshape(w[n].shape) for o in results[n]]

    g_small = {n: jnp.stack([g[n] for g in g_layers]) for n in PER_LAYER_REPLICATED}
    g_small['mem_norm_g'], g_small['final_norm_g'] = g_mem, g_final
    sizes = [w[n].size for n in REPLICATED]

    def pack(parts, first):
        flat = jnp.concatenate([jnp.reshape(first, (1,))] + [parts[n].reshape(-1) for n in REPLICATED])
        return _as_pack_rows(flat, LANES)

    zero = jnp.zeros((), F32)
    everyone, = all_gather([pack(g_small, loss)], "replicated")
    outs = adamw(everyone, pack(w, zero)[None], pack(mom, zero)[None], pack(var, zero)[None])
    flat_outs = [o.reshape(-1) for o in outs]
    loss_total = flat_outs[0][0]
    offset = 1
    for n, size in zip(REPLICATED, sizes):
        grad[n], delta[n], new_m[n], new_v[n] = [o[offset:offset + size].reshape(w[n].shape) for o in flat_outs]
        offset += size

    return (loss_total, grad_x, *[grad[n] for n in WEIGHTS], *[delta[n] for n in WEIGHTS],
            *[new_m[n] for n in WEIGHTS], *[new_v[n] for n in WEIGHTS])
```
